```python
import jax
import jax.numpy as jnp
from jax import lax
import numpy as np


D_MODEL = 1024
BATCH = 2
SEQ = 8192
DEPTH = 2

CHUNK = 64
Q_BLOCK = 128
EPS = 1e-6
POOL_WINDOWS = (2, 4, 8, 16)
POOL_GROUPS = len(POOL_WINDOWS)
POOL_WIDTH = D_MODEL // 2
POOL_GROUP_DIM = POOL_WIDTH // POOL_GROUPS
MLA_HEADS = 8
QK_NOPE_DIM = 64
QK_ROPE_DIM = 32
QK_HEAD_DIM = QK_NOPE_DIM + QK_ROPE_DIM
V_HEAD_DIM = 64
Q_LORA_RANK = 384
KV_LORA_RANK = 256
ROPE_THETA = 10000.0
MLA_WIDTH = MLA_HEADS * V_HEAD_DIM
EVEN_IN_WIDTH = POOL_WIDTH + Q_LORA_RANK + KV_LORA_RANK + QK_ROPE_DIM
EVEN_MIX_WIDTH = POOL_WIDTH + MLA_WIDTH
CONV_DIM = D_MODEL
CONV_WIDTH = 31
N_EXPERTS = 32
TOP_K = 4
D_FF = D_MODEL
SWIGLU_ALPHA = 1.702
SWIGLU_LIMIT = 7.0
MOE_BLOCK = 256
N_EVEN = (DEPTH + 1) // 2
N_ODD = DEPTH // 2

kernel_name = 'hybrid_pool_mla_conformer_moe_adaln'


def rmsnorm(x, g):
    xf = x.astype(jnp.float32)
    y = xf * lax.rsqrt(jnp.mean(xf * xf, axis=-1, keepdims=True) + EPS)
    return y.astype(x.dtype) * g


def layernorm(x, g, b):
    xf = x.astype(jnp.float32)
    mu = jnp.mean(xf, axis=-1, keepdims=True)
    var = jnp.mean(jnp.square(xf - mu), axis=-1, keepdims=True)
    y = (xf - mu) * lax.rsqrt(var + EPS)
    return y.astype(x.dtype) * g + b


def adaln(c, w, b):
    m = jax.nn.silu(c) @ w + b
    shift, scale, gate = jnp.split(m[:, None, :], 3, axis=-1)
    return shift, scale, gate


def rope_tables(positions):
    inv = 1.0 / (ROPE_THETA ** (jnp.arange(0, QK_ROPE_DIM, 2, dtype=jnp.float32) / QK_ROPE_DIM))
    ang = positions.astype(jnp.float32)[..., None] * inv
    return jnp.cos(ang)[:, :, None, :], jnp.sin(ang)[:, :, None, :]


def apply_rope(x, cos, sin):
    xf = x.astype(jnp.float32)
    x1, x2 = jnp.split(xf, 2, axis=-1)
    out = jnp.concatenate([x1 * cos - x2 * sin, x1 * sin + x2 * cos], axis=-1)
    return out.astype(x.dtype)


def multiscale_pool(u, pool_w, pool_scale):
    B, S, _ = u.shape
    ug = u.reshape(B, S, POOL_GROUPS, POOL_GROUP_DIM)
    t = jnp.arange(S)
    outs = []
    for g, w in enumerate(POOL_WINDOWS):
        ui = ug[:, :, g].astype(jnp.float32)
        cs = jnp.cumsum(ui, axis=1)
        lagged = jnp.pad(cs[:, :S - w], ((0, 0), (w, 0), (0, 0)))
        cnt = jnp.minimum(t + 1, w).astype(jnp.float32)[None, :, None]
        pooled = (cs - lagged) / cnt - ui
        outs.append(jnp.einsum('bsc,cd->bsd', pooled.astype(u.dtype), pool_w[g]))
    return jnp.concatenate(outs, axis=-1) * pool_scale


def chunk_causal_attention(q, k, v):
    S = q.shape[1]
    scale = QK_HEAD_DIM ** -0.5
    neg = jnp.finfo(jnp.float32).min
    outs = []
    for s0 in range(0, S, Q_BLOCK):
        end = s0 + Q_BLOCK
        scores = jnp.einsum('bqhd,bkhd->bhqk', q[:, s0:end], k[:, :end],
                            preferred_element_type=jnp.float32) * scale
        q_chunk = (s0 + jnp.arange(Q_BLOCK)) // CHUNK
        k_chunk = jnp.arange(end) // CHUNK
        mask = k_chunk[None, :] <= q_chunk[:, None]
        p = jax.nn.softmax(jnp.where(mask, scores, neg), axis=-1).astype(v.dtype)
        outs.append(jnp.einsum('bhqk,bkhd->bqhd', p, v[:, :end]))
    return jnp.concatenate(outs, axis=1)


def pool_mla_mixer(h, cos, sin, w_in, pool_w, pool_scale, cq_norm_g, w_uq, ckv_norm_g, w_ukv,
                   q_norm_g, k_norm_g, w_out):
    B, S, _ = h.shape
    z = h @ w_in
    i1 = POOL_WIDTH
    i2 = i1 + Q_LORA_RANK
    i3 = i2 + KV_LORA_RANK
    u, c_q, c_kv, k_rope = jnp.split(z, [i1, i2, i3], axis=-1)
    y_pool = multiscale_pool(u, pool_w, pool_scale)
    q = (rmsnorm(c_q, cq_norm_g) @ w_uq).reshape(B, S, MLA_HEADS, QK_HEAD_DIM)
    kv = (rmsnorm(c_kv, ckv_norm_g) @ w_ukv).reshape(B, S, MLA_HEADS, QK_NOPE_DIM + V_HEAD_DIM)
    k_nope, v = kv[..., :QK_NOPE_DIM], kv[..., QK_NOPE_DIM:]
    k_rope_h = jnp.broadcast_to(k_rope[:, :, None, :], (B, S, MLA_HEADS, QK_ROPE_DIM))
    k = jnp.concatenate([k_nope, k_rope_h], axis=-1)
    q = rmsnorm(q, q_norm_g)
    k = rmsnorm(k, k_norm_g)
    q = jnp.concatenate([q[..., :QK_NOPE_DIM], apply_rope(q[..., QK_NOPE_DIM:], cos, sin)], axis=-1)
    k = jnp.concatenate([k[..., :QK_NOPE_DIM], apply_rope(k[..., QK_NOPE_DIM:], cos, sin)], axis=-1)
    y_att = chunk_causal_attention(q, k, v).reshape(B, S, MLA_WIDTH)
    return jnp.concatenate([y_pool, y_att], axis=-1) @ w_out


def conformer_conv(h, pw1_w, pw1_b, dw_w, dw_b, ln_g, ln_b, pw2_w, pw2_b):
    a = h @ pw1_w + pw1_b
    val, gt = jnp.split(a, 2, axis=-1)
    u = val * jax.nn.sigmoid(gt)
    u = lax.conv_general_dilated(u, dw_w[:, None, :], window_strides=(1,),
                                 padding=((CONV_WIDTH - 1, 0),),
                                 dimension_numbers=('NWC', 'WIO', 'NWC'),
                                 feature_group_count=CONV_DIM) + dw_b
    u = jax.nn.silu(layernorm(u, ln_g, ln_b))
    return u @ pw2_w + pw2_b


def clamped_swiglu(hb):
    gate = jnp.minimum(hb[..., ::2], SWIGLU_LIMIT)
    up = jnp.clip(hb[..., 1::2], -SWIGLU_LIMIT, SWIGLU_LIMIT)
    return gate * jax.nn.sigmoid(SWIGLU_ALPHA * gate) * (up + 1.0)


def moe(h, router_w, router_b, w1, b1, w2, b2):
    B, S, D = h.shape
    hf = h.reshape(-1, D)
    N = hf.shape[0]
    logits = (hf @ router_w + router_b).astype(jnp.float32)
    top_v, top_i = lax.top_k(logits, TOP_K)
    gates = jax.nn.softmax(top_v, axis=-1)
    M = N * TOP_K
    flat_e = top_i.reshape(-1)
    order = jnp.argsort(flat_e)
    sorted_e = flat_e[order]
    tok = order // TOP_K
    counts = jnp.bincount(flat_e, length=N_EXPERTS)
    starts = jnp.cumsum(counts) - counts
    padded = (counts + MOE_BLOCK - 1) // MOE_BLOCK * MOE_BLOCK
    pad_ends = jnp.cumsum(padded)
    pad_starts = pad_ends - padded
    dest = pad_starts[sorted_e] + jnp.arange(M) - starts[sorted_e]
    n_blocks = -(-M // MOE_BLOCK) + N_EXPERTS
    P = n_blocks * MOE_BLOCK
    xs = jnp.zeros((P, D), h.dtype).at[dest].set(hf[tok])
    block_e = jnp.minimum(jnp.searchsorted(pad_ends, jnp.arange(n_blocks) * MOE_BLOCK, side='right'),
                          N_EXPERTS - 1)

    def expert_block(args):
        xb, e = args
        return clamped_swiglu(xb @ w1[e] + b1[e]) @ w2[e] + b2[e]

    ys = lax.map(expert_block, (xs.reshape(n_blocks, MOE_BLOCK, D), block_e)).reshape(P, D)
    w_sorted = gates.reshape(-1)[order].astype(h.dtype)
    out = jax.ops.segment_sum(ys[dest] * w_sorted[:, None], tok, num_segments=N)
    return out.reshape(B, S, D)


def setup_inputs(seed: int = 0) -> dict:
    key = jax.random.key(seed)
    ks = jax.random.split(key, 64)
    idx = [0]
    f32 = jnp.float32

    def nk():
        idx[0] += 1
        return ks[idx[0] - 1]

    def nrm(shape, scale):
        return scale * jax.random.normal(nk(), shape, f32)

    def gain(shape):
        return 1.0 + 0.05 * jax.random.normal(nk(), shape, f32)

    L, E, O, D = DEPTH, N_EVEN, N_ODD, D_MODEL
    x = nrm((BATCH, SEQ, D), 1.0)
    c = nrm((BATCH, D), 1.0)
    offsets = jax.random.randint(nk(), (BATCH, 1), 0, 4096, dtype=jnp.int32)
    positions = offsets + jnp.arange(SEQ, dtype=jnp.int32)[None, :]
    return {
        'x': x,
        'c': c,
        'positions': positions,
        'ada_mix_w': nrm((L, D, 3 * D), 0.5 * D ** -0.5),
        'ada_mix_b': nrm((L, 3 * D), 0.02),
        'norm_mix_g': gain((L, D)),
        'w_in': nrm((E, D, EVEN_IN_WIDTH), D ** -0.5),
        'pool_w': nrm((E, POOL_GROUPS, POOL_GROUP_DIM, POOL_GROUP_DIM), POOL_GROUP_DIM ** -0.5),
        'pool_scale': 1.0 + 0.1 * jax.random.normal(nk(), (E, POOL_WIDTH), f32),
        'cq_norm_g': gain((E, Q_LORA_RANK)),
        'w_uq': nrm((E, Q_LORA_RANK, MLA_HEADS * QK_HEAD_DIM), Q_LORA_RANK ** -0.5),
        'ckv_norm_g': gain((E, KV_LORA_RANK)),
        'w_ukv': nrm((E, KV_LORA_RANK, MLA_HEADS * (QK_NOPE_DIM + V_HEAD_DIM)), KV_LORA_RANK ** -0.5),
        'q_norm_g': gain((E, QK_HEAD_DIM)),
        'k_norm_g': gain((E, QK_HEAD_DIM)),
        'w_out': nrm((E, EVEN_MIX_WIDTH, D), EVEN_MIX_WIDTH ** -0.5),
        'conv_pw1_w': nrm((O, D, 2 * CONV_DIM), D ** -0.5),
        'conv_pw1_b': nrm((O, 2 * CONV_DIM), 0.02),
        'conv_dw_w': nrm((O, CONV_WIDTH, CONV_DIM), CONV_WIDTH ** -0.5),
        'conv_dw_b': nrm((O, CONV_DIM), 0.02),
        'conv_ln_g': gain((O, CONV_DIM)),
        'conv_ln_b': nrm((O, CONV_DIM), 0.02),
        'conv_pw2_w': nrm((O, CONV_DIM, D), CONV_DIM ** -0.5),
        'conv_pw2_b': nrm((O, D), 0.02),
        'ada_ffn_w': nrm((L, D, 3 * D), 0.5 * D ** -0.5),
        'ada_ffn_b': nrm((L, 3 * D), 0.02),
        'norm_ffn_g': gain((L, D)),
        'router_w': nrm((L, D, N_EXPERTS), D ** -0.5),
        'router_b': nrm((L, N_EXPERTS), 0.01),
        'moe_w1': nrm((L, N_EXPERTS, D, 2 * D_FF), D ** -0.5),
        'moe_b1': nrm((L, N_EXPERTS, 2 * D_FF), 0.02),
        'moe_w2': nrm((L, N_EXPERTS, D_FF, D), D_FF ** -0.5),
        'moe_b2': nrm((L, N_EXPERTS, D), 0.02),
    }


def reference(x, c, positions, ada_mix_w, ada_mix_b, norm_mix_g, w_in, pool_w, pool_scale,
              cq_norm_g, w_uq, ckv_norm_g, w_ukv, q_norm_g, k_norm_g, w_out,
              conv_pw1_w, conv_pw1_b, conv_dw_w, conv_dw_b, conv_ln_g, conv_ln_b,
              conv_pw2_w, conv_pw2_b, ada_ffn_w, ada_ffn_b, norm_ffn_g, router_w, router_b,
              moe_w1, moe_b1, moe_w2, moe_b2):
    cos, sin = rope_tables(positions)
    for layer in range(DEPTH):
        i = layer // 2
        shift, scale, gate = adaln(c, ada_mix_w[layer], ada_mix_b[layer])
        h = rmsnorm(x, norm_mix_g[layer]) * (1.0 + scale) + shift
        if layer % 2 == 0:
            y = pool_mla_mixer(h, cos, sin, w_in[i], pool_w[i], pool_scale[i], cq_norm_g[i], w_uq[i],
                               ckv_norm_g[i], w_ukv[i], q_norm_g[i], k_norm_g[i], w_out[i])
        else:
            y = conformer_conv(h, conv_pw1_w[i], conv_pw1_b[i], conv_dw_w[i], conv_dw_b[i],
                               conv_ln_g[i], conv_ln_b[i], conv_pw2_w[i], conv_pw2_b[i])
        x = x + gate * y
        shift, scale, gate = adaln(c, ada_ffn_w[layer], ada_ffn_b[layer])
        h = rmsnorm(x, norm_ffn_g[layer]) * (1.0 + scale) + shift
        x = x + gate * moe(h, router_w[layer], router_b[layer], moe_w1[layer], moe_b1[layer],
                           moe_w2[layer], moe_b2[layer])
    return x
```

```python
import functools

import jax
import jax.numpy as jnp
from jax import lax
from jax.experimental import pallas as pl
from jax.experimental.pallas import tpu as pltpu

F32 = jnp.float32
BF16 = jnp.bfloat16
HIGHEST = lax.Precision.HIGHEST

EPS = 1e-6
POOL_WINDOWS = (2, 4, 8, 16)
POOL_GROUP_DIM = 128
POOL_WIDTH = POOL_GROUP_DIM * len(POOL_WINDOWS)
MLA_HEADS = 8
QK_NOPE_DIM = 64
QK_ROPE_DIM = 32
QK_HEAD_DIM = QK_NOPE_DIM + QK_ROPE_DIM
V_HEAD_DIM = 64
Q_LORA_RANK = 384
KV_LORA_RANK = 256
ROPE_THETA = 10000.0
CHUNK = 64
CONV_WIDTH = 31
N_EXPERTS = 32
TOP_K = 4
SWIGLU_ALPHA = 1.702
SWIGLU_LIMIT = 7.0
MOE_BLOCK = 256

LANES = 128
POOL_HALO = 16
CONV_HALO = 32
MASK_VALUE = -1e30
VMEM_LIMIT = 52 * 1024 * 1024

Z_CQ = POOL_WIDTH
Z_CKV = Z_CQ + Q_LORA_RANK
Z_ROPE = Z_CKV + KV_LORA_RANK
Z_WIDTH = Z_ROPE + LANES


def _params(*sem, vmem=None):
    return pltpu.CompilerParams(dimension_semantics=sem, vmem_limit_bytes=vmem or VMEM_LIMIT,
                                disable_bounds_checks=True)


def _adaln_kernel(c_ref, w_ref, b_ref, o_ref):
    c = c_ref[...]
    s = c * jax.nn.sigmoid(c)
    o_ref[0] = jnp.dot(s, w_ref[0], preferred_element_type=F32, precision=HIGHEST) + b_ref[0]


def adaln(c_pad, w, b):
    n_l, d, d3 = w.shape
    tn = 512
    return pl.pallas_call(
        _adaln_kernel,
        grid=(n_l, d3 // tn),
        in_specs=[pl.BlockSpec((8, d), lambda l, j: (0, 0)),
                  pl.BlockSpec((1, d, tn), lambda l, j: (l, 0, j)),
                  pl.BlockSpec((1, 1, tn), lambda l, j: (l, 0, j))],
        out_specs=pl.BlockSpec((1, 8, tn), lambda l, j: (l, 0, j)),
        out_shape=jax.ShapeDtypeStruct((n_l, 8, d3), F32),
        compiler_params=_params("parallel", "parallel"),
        name="adaln",
    )(c_pad, w, b.reshape(n_l, 1, d3))


def _modulated_norm(x, g, scale, shift):
    ms = jnp.mean(x * x, axis=-1, keepdims=True)
    return x * lax.rsqrt(ms + EPS) * g * (1.0 + scale) + shift


def _norm_mod_kernel(x_ref, g_ref, sc_ref, sh_ref, o_ref):
    o_ref[...] = _modulated_norm(x_ref[...], g_ref[...], sc_ref[0], sh_ref[0]).astype(o_ref.dtype)


def _norm_mod_router_kernel(x_ref, g_ref, sc_ref, sh_ref, rw_ref, rb_ref, o_ref, lg_ref):
    h = _modulated_norm(x_ref[...], g_ref[...], sc_ref[0], sh_ref[0])
    o_ref[...] = h
    lg_ref[...] = jnp.dot(h, rw_ref[...], preferred_element_type=F32, precision=HIGHEST) + rb_ref[...]


def norm_mod(x, g, scale, shift, seq, tm=512):
    n, d = x.shape
    per = seq // tm
    vec = pl.BlockSpec((1, 1, d), lambda i: (i // per, 0, 0))
    return pl.pallas_call(
        _norm_mod_kernel,
        grid=(n // tm,),
        in_specs=[pl.BlockSpec((tm, d), lambda i: (i, 0)), pl.BlockSpec((1, d), lambda i: (0, 0)), vec, vec],
        out_specs=pl.BlockSpec((tm, d), lambda i: (i, 0)),
        out_shape=jax.ShapeDtypeStruct((n, d), BF16),
        compiler_params=_params("parallel"),
        name="norm_mod",
    )(x, g, scale, shift)


def norm_mod_router(x, g, scale, shift, rw_pad, rb_pad, seq, tm=512):
    n, d = x.shape
    per = seq // tm
    vec = pl.BlockSpec((1, 1, d), lambda i: (i // per, 0, 0))
    return pl.pallas_call(
        _norm_mod_router_kernel,
        grid=(n // tm,),
        in_specs=[pl.BlockSpec((tm, d), lambda i: (i, 0)), pl.BlockSpec((1, d), lambda i: (0, 0)), vec, vec,
                  pl.BlockSpec((d, LANES), lambda i: (0, 0)), pl.BlockSpec((1, LANES), lambda i: (0, 0))],
        out_specs=[pl.BlockSpec((tm, d), lambda i: (i, 0)), pl.BlockSpec((tm, LANES), lambda i: (i, 0))],
        out_shape=[jax.ShapeDtypeStruct((n, d), F32), jax.ShapeDtypeStruct((n, LANES), F32)],
        compiler_params=_params("parallel"),
        name="norm_mod_router",
    )(x, g, scale, shift, rw_pad, rb_pad)


def _matmul_kernel(a_ref, w_ref, o_ref):
    o_ref[...] = jnp.dot(a_ref[...], w_ref[...], preferred_element_type=F32).astype(o_ref.dtype)


def matmul(a, w, out_dtype=F32, tm=512):
    n, k = a.shape
    m = w.shape[1]
    return pl.pallas_call(
        _matmul_kernel,
        grid=(n // tm,),
        in_specs=[pl.BlockSpec((tm, k), lambda i: (i, 0)), pl.BlockSpec((k, m), lambda i: (0, 0))],
        out_specs=pl.BlockSpec((tm, m), lambda i: (i, 0)),
        out_shape=jax.ShapeDtypeStruct((n, m), out_dtype),
        compiler_params=_params("parallel"),
        name="matmul",
    )(a, w)


def _glu_kernel(a_ref, wv_ref, wg_ref, bv_ref, bg_ref, o_ref):
    a = a_ref[...]
    val = jnp.dot(a, wv_ref[...], preferred_element_type=F32) + bv_ref[...]
    gt = jnp.dot(a, wg_ref[...], preferred_element_type=F32) + bg_ref[...]
    o_ref[...] = val * jax.nn.sigmoid(gt)


def glu_matmul(a, wv, wg, bv, bg, tm=512):
    n, k = a.shape
    m = wv.shape[1]
    full = lambda r, c: pl.BlockSpec((r, c), lambda i: (0, 0))
    return pl.pallas_call(
        _glu_kernel,
        grid=(n // tm,),
        in_specs=[pl.BlockSpec((tm, k), lambda i: (i, 0)), full(k, m), full(k, m), full(1, m), full(1, m)],
        out_specs=pl.BlockSpec((tm, m), lambda i: (i, 0)),
        out_shape=jax.ShapeDtypeStruct((n, m), F32),
        compiler_params=_params("parallel"),
        name="glu_matmul",
    )(a, wv, wg, bv, bg)


def _proj_residual_kernel(*refs, n_in):
    a_refs, w_refs = refs[:n_in], refs[n_in:2 * n_in]
    b_ref, x_ref, gate_ref, o_ref = refs[2 * n_in:]
    acc = b_ref[...]
    for a_ref, w_ref in zip(a_refs, w_refs):
        acc = acc + jnp.dot(a_ref[...], w_ref[...], preferred_element_type=F32)
    o_ref[...] = x_ref[...] + gate_ref[0] * acc


def proj_residual(a_list, w_list, bias, x, gate, seq, tm=512):
    n, d = x.shape
    per = seq // tm
    n_in = len(a_list)
    in_specs = [pl.BlockSpec((tm, a.shape[1]), lambda i: (i, 0)) for a in a_list]
    in_specs += [pl.BlockSpec(w.shape, lambda i: (0, 0)) for w in w_list]
    in_specs += [pl.BlockSpec((1, d), lambda i: (0, 0)), pl.BlockSpec((tm, d), lambda i: (i, 0)),
                 pl.BlockSpec((1, 1, d), lambda i: (i // per, 0, 0))]
    return pl.pallas_call(
        functools.partial(_proj_residual_kernel, n_in=n_in),
        grid=(n // tm,),
        in_specs=in_specs,
        out_specs=pl.BlockSpec((tm, d), lambda i: (i, 0)),
        out_shape=jax.ShapeDtypeStruct((n, d), F32),
        compiler_params=_params("parallel"),
        name="proj_residual",
    )(*a_list, *w_list, bias, x, gate)


def _rope(xn, cos_t, sin_lo, sin_hi):
    return (xn * cos_t + pltpu.roll(xn, LANES - QK_ROPE_DIM // 2, axis=1) * sin_lo
            + pltpu.roll(xn, QK_ROPE_DIM // 2, axis=1) * sin_hi)


def _mla_prep_kernel(z_ref, halo_ref, pw_ref, ps_ref, cqg_ref, wuq_ref, ckvg_ref, wuk_ref, wuv_ref,
                     qg_ref, kg_ref, cos_ref, slo_ref, shi_ref,
                     yp_ref, q_ref, k_ref, v_ref, ext_ref, *, tm, per):
    si = pl.program_id(0) % per
    u = z_ref[:, 0:POOL_WIDTH]
    ext_ref[0:POOL_HALO, :] = jnp.where(si == 0, 0.0, halo_ref[...])
    ext_ref[POOL_HALO:, :] = u
    t = si * tm + lax.broadcasted_iota(jnp.int32, (tm, 1), 0)
    for g, w in enumerate(POOL_WINDOWS):
        cols = slice(g * POOL_GROUP_DIM, (g + 1) * POOL_GROUP_DIM)
        ug = u[:, cols]
        s = ug
        for j in range(1, w):
            s = s + ext_ref[POOL_HALO - j:POOL_HALO - j + tm, cols]
        cnt = jnp.minimum(t + 1, w).astype(F32)
        pooled = s / cnt - ug
        yp = jnp.dot(pooled.astype(BF16), pw_ref[g], preferred_element_type=F32) * ps_ref[:, cols]
        yp_ref[:, cols] = yp.astype(yp_ref.dtype)

    cos_t, sin_lo, sin_hi = cos_ref[...], slo_ref[...], shi_ref[...]
    inv_head = 1.0 / QK_HEAD_DIM

    cq = z_ref[:, Z_CQ:Z_CKV]
    cqn = cq * lax.rsqrt(jnp.mean(cq * cq, axis=-1, keepdims=True) + EPS) * cqg_ref[...]
    qf = jnp.dot(cqn.astype(BF16), wuq_ref[...], preferred_element_type=F32)
    q_scale = QK_HEAD_DIM ** -0.5
    for h in range(MLA_HEADS):
        qh = qf[:, h * LANES:(h + 1) * LANES]
        ss = jnp.sum(qh * qh, axis=-1, keepdims=True) * inv_head
        qn = qh * lax.rsqrt(ss + EPS) * qg_ref[...]
        q_ref[0, h] = (_rope(qn, cos_t, sin_lo, sin_hi) * q_scale).astype(q_ref.dtype)

    ckv = z_ref[:, Z_CKV:Z_ROPE]
    ckvn = (ckv * lax.rsqrt(jnp.mean(ckv * ckv, axis=-1, keepdims=True) + EPS) * ckvg_ref[...]).astype(BF16)
    kf = jnp.dot(ckvn, wuk_ref[...], preferred_element_type=F32)
    vf = jnp.dot(ckvn, wuv_ref[...], preferred_element_type=F32)
    k_rope = z_ref[:, Z_ROPE:Z_WIDTH]
    for h in range(MLA_HEADS):
        kh = kf[:, h * LANES:(h + 1) * LANES] + k_rope
        ss = jnp.sum(kh * kh, axis=-1, keepdims=True) * inv_head
        kn = kh * lax.rsqrt(ss + EPS) * kg_ref[...]
        k_ref[0, h] = _rope(kn, cos_t, sin_lo, sin_hi).astype(k_ref.dtype)
        v_ref[0, h] = vf[:, h * LANES:(h + 1) * LANES].astype(v_ref.dtype)


def mla_prep(z, pool_w, pool_scale, cq_g, wuq_pad, ckv_g, wuk_pad, wuv_pad, qg_pad, kg_pad,
             cos_t, sin_lo, sin_hi, batch, seq, tm=256):
    n = z.shape[0]
    per = seq // tm
    hb = tm // POOL_HALO
    full = lambda a: pl.BlockSpec(a.shape, lambda i: (0,) * a.ndim)
    tab = pl.BlockSpec((tm, LANES), lambda i: (i, 0))
    head_out = pl.BlockSpec((1, MLA_HEADS, tm, LANES), lambda i: (i // per, 0, i % per, 0))
    head_shape = jax.ShapeDtypeStruct((batch, MLA_HEADS, seq, LANES), BF16)
    return pl.pallas_call(
        functools.partial(_mla_prep_kernel, tm=tm, per=per),
        grid=(n // tm,),
        in_specs=[pl.BlockSpec((tm, Z_WIDTH), lambda i: (i, 0)),
                  pl.BlockSpec((POOL_HALO, POOL_WIDTH), lambda i: (jnp.maximum(i * hb - 1, 0), 0)),
                  full(pool_w), full(pool_scale), full(cq_g), full(wuq_pad), full(ckv_g), full(wuk_pad),
                  full(wuv_pad), full(qg_pad), full(kg_pad), tab, tab, tab],
        out_specs=[pl.BlockSpec((tm, POOL_WIDTH), lambda i: (i, 0)), head_out, head_out, head_out],
        out_shape=[jax.ShapeDtypeStruct((n, POOL_WIDTH), BF16), head_shape, head_shape, head_shape],
        scratch_shapes=[pltpu.VMEM((tm + POOL_HALO, POOL_WIDTH), F32)],
        compiler_params=_params("parallel"),
        name="mla_prep",
    )(z, z, pool_w, pool_scale, cq_g, wuq_pad, ckv_g, wuk_pad, wuv_pad, qg_pad, kg_pad, cos_t, sin_lo, sin_hi)


def _flash_kernel(q_ref, k_ref, v_ref, o_ref, m_ref, l_ref, acc_ref, *, tq, heads):
    qi = pl.program_id(2)
    rows = lax.broadcasted_iota(jnp.int32, (tq, tq), 0) // CHUNK
    cols = lax.broadcasted_iota(jnp.int32, (tq, tq), 1) // CHUNK
    diag_mask = cols <= rows
    outs = []
    for hh in range(heads):
        q = q_ref[0, hh]
        m_ref[...] = jnp.full(m_ref.shape, MASK_VALUE, F32)
        l_ref[...] = jnp.zeros(l_ref.shape, F32)
        acc_ref[...] = jnp.zeros(acc_ref.shape, F32)

        def step(j, masked, hh=hh, q=q):
            start = pl.multiple_of(j * tq, tq)
            k = k_ref[0, hh, pl.ds(start, tq), :]
            v = v_ref[0, hh, pl.ds(start, tq), :]
            s = lax.dot_general(q, k, (((1,), (1,)), ((), ())), preferred_element_type=F32)
            if masked:
                s = jnp.where(diag_mask, s, MASK_VALUE)
            m_prev = m_ref[...]
            m_new = jnp.maximum(m_prev, jnp.max(s, axis=-1, keepdims=True))
            alpha = jnp.exp(m_prev - m_new)
            p = jnp.exp(s - m_new)
            l_ref[...] = alpha * l_ref[...] + jnp.sum(p, axis=-1, keepdims=True)
            acc_ref[...] = alpha * acc_ref[...] + jnp.dot(p.astype(v.dtype), v, preferred_element_type=F32)
            m_ref[...] = m_new

        def full_step(j, carry):
            step(j, False)
            return carry

        lax.fori_loop(0, qi, full_step, 0)
        step(qi, True)
        outs.append((acc_ref[...] / l_ref[...])[:, :V_HEAD_DIM])
    o_ref[0] = jnp.concatenate(outs, axis=-1).astype(o_ref.dtype)


def flash_attention(q, k, v, tq=512):
    b, h, s, _ = q.shape
    heads = LANES // V_HEAD_DIM
    return pl.pallas_call(
        functools.partial(_flash_kernel, tq=tq, heads=heads),
        grid=(b, h // heads, s // tq),
        in_specs=[pl.BlockSpec((1, heads, tq, LANES), lambda bi, hi, qi: (bi, hi, qi, 0)),
                  pl.BlockSpec((1, heads, s, LANES), lambda bi, hi, qi: (bi, hi, 0, 0)),
                  pl.BlockSpec((1, heads, s, LANES), lambda bi, hi, qi: (bi, hi, 0, 0))],
        out_specs=pl.BlockSpec((1, tq, LANES), lambda bi, hi, qi: (bi, qi, hi)),
        out_shape=jax.ShapeDtypeStruct((b, s, h * V_HEAD_DIM), BF16),
        scratch_shapes=[pltpu.VMEM((tq, 1), F32), pltpu.VMEM((tq, 1), F32), pltpu.VMEM((tq, LANES), F32)],
        compiler_params=_params("parallel", "parallel", "parallel"),
        name="flash_attention",
    )(q, k, v)


def _dwconv_kernel(u_ref, halo_ref, w_ref, b_ref, g_ref, beta_ref, o_ref, ext_ref, *, tm, per):
    si = pl.program_id(0) % per
    ext_ref[0:CONV_HALO, :] = jnp.where(si == 0, 0.0, halo_ref[...])
    ext_ref[CONV_HALO:, :] = u_ref[...]
    first = CONV_HALO - (CONV_WIDTH - 1)
    acc = jnp.broadcast_to(b_ref[...], (tm, b_ref.shape[1]))
    for j in range(CONV_WIDTH):
        acc = acc + w_ref[j:j + 1, :] * ext_ref[first + j:first + j + tm, :]
    mu = jnp.mean(acc, axis=-1, keepdims=True)
    cen = acc - mu
    var = jnp.mean(cen * cen, axis=-1, keepdims=True)
    y = cen * lax.rsqrt(var + EPS) * g_ref[...] + beta_ref[...]
    o_ref[...] = (y * jax.nn.sigmoid(y)).astype(o_ref.dtype)


def dwconv_ln_silu(u, w_pad, b, g, beta, seq, tm=256):
    n, d = u.shape
    per = seq // tm
    hb = tm // CONV_HALO
    full = lambda a: pl.BlockSpec(a.shape, lambda i: (0, 0))
    return pl.pallas_call(
        functools.partial(_dwconv_kernel, tm=tm, per=per),
        grid=(n // tm,),
        in_specs=[pl.BlockSpec((tm, d), lambda i: (i, 0)),
                  pl.BlockSpec((CONV_HALO, d), lambda i: (jnp.maximum(i * hb - 1, 0), 0)),
                  full(w_pad), full(b), full(g), full(beta)],
        out_specs=pl.BlockSpec((tm, d), lambda i: (i, 0)),
        out_shape=jax.ShapeDtypeStruct((n, d), BF16),
        scratch_shapes=[pltpu.VMEM((tm + CONV_HALO, d), F32)],
        compiler_params=_params("parallel"),
        name="dwconv_ln_silu",
    )(u, u, w_pad, b, g, beta)


def _router_kernel(lg_ref, idx_ref, gate_ref, cnt_ref, carry_ref, *, tm):
    @pl.when(pl.program_id(0) == 0)
    def _():
        carry_ref[...] = jnp.zeros(carry_ref.shape, F32)

    lane = lax.broadcasted_iota(jnp.int32, (tm, LANES), 1)
    lane_f = lane.astype(F32)
    neg = -jnp.inf
    logits = jnp.where(lane < N_EXPERTS, lg_ref[...], neg)
    picks, vals, ids = [], [], []
    for _ in range(TOP_K):
        mx = jnp.max(logits, axis=-1, keepdims=True)
        idx = jnp.min(jnp.where(logits == mx, lane_f, float(LANES)), axis=-1, keepdims=True)
        pick = lane_f == idx
        picks.append(pick)
        vals.append(mx)
        ids.append(idx)
        logits = jnp.where(pick, neg, logits)
    exps = [jnp.exp(v - vals[0]) for v in vals]
    den = exps[0]
    for e in exps[1:]:
        den = den + e

    chosen = jnp.zeros((tm, LANES), F32)
    for pick in picks:
        chosen = chosen + pick.astype(F32)
    r_io = lax.broadcasted_iota(jnp.int32, (tm, tm), 0)
    c_io = lax.broadcasted_iota(jnp.int32, (tm, tm), 1)
    earlier = (c_io < r_io).astype(BF16)
    before = jnp.dot(earlier, chosen.astype(BF16), preferred_element_type=F32) + carry_ref[0:1, :]
    idx_out = jnp.zeros((tm, LANES), F32)
    gate_out = jnp.zeros((tm, LANES), F32)
    for k in range(TOP_K):
        rank = jnp.sum(jnp.where(picks[k], before, 0.0), axis=-1, keepdims=True)
        idx_out = jnp.where(lane == k, ids[k], idx_out)
        idx_out = jnp.where(lane == TOP_K + k, rank, idx_out)
        gate_out = jnp.where(lane == k, exps[k] / den, gate_out)
    idx_ref[...] = idx_out.astype(jnp.int32)
    gate_ref[...] = gate_out
    total = carry_ref[0:1, :] + jnp.sum(chosen, axis=0, keepdims=True)
    carry_ref[...] = jnp.broadcast_to(total, carry_ref.shape)
    cnt_ref[...] = jnp.broadcast_to(total, cnt_ref.shape).astype(jnp.int32)


def router(logits, tm=512):
    n = logits.shape[0]
    row = pl.BlockSpec((tm, LANES), lambda i: (i, 0))
    return pl.pallas_call(
        functools.partial(_router_kernel, tm=tm),
        grid=(n // tm,),
        in_specs=[row],
        out_specs=[row, row, pl.BlockSpec((8, LANES), lambda i: (0, 0))],
        out_shape=[jax.ShapeDtypeStruct((n, LANES), jnp.int32), jax.ShapeDtypeStruct((n, LANES), F32),
                   jax.ShapeDtypeStruct((8, LANES), jnp.int32)],
        scratch_shapes=[pltpu.VMEM((8, LANES), F32)],
        compiler_params=_params("arbitrary"),
        name="router",
    )(logits)


def _moe_kernel(be_ref, nv_ref, src_ref, h_hbm, w1_ref, b1g_ref, b1u_ref, w2_ref, b2_ref, y_hbm,
                xs_ref, yb_ref, wg_ref, wu_ref, w2s_ref, gsem, ssem, *, n_tok):
    i = pl.program_id(0)
    nvalid = nv_ref[i]
    d = xs_ref.shape[1]
    pair = 2 * LANES

    @pl.when(nvalid > 0)
    def _():
        e = be_ref[i]
        prev = be_ref[jnp.maximum(i - 1, 0)]

        @pl.when((i == 0) | (e != prev))
        def _():
            r_io = lax.broadcasted_iota(jnp.int32, (pair, pair), 0)
            c_io = lax.broadcasted_iota(jnp.int32, (pair, pair), 1)
            want = jnp.where(c_io < LANES, 2 * c_io, 2 * (c_io - LANES) + 1)
            sel = (r_io == want).astype(BF16)
            for c in range(wg_ref.shape[1] // LANES):
                slab = w1_ref[0, :, c * pair:(c + 1) * pair].astype(BF16)
                split = jnp.dot(slab, sel, preferred_element_type=F32)
                wg_ref[:, c * LANES:(c + 1) * LANES] = split[:, :LANES].astype(BF16)
                wu_ref[:, c * LANES:(c + 1) * LANES] = split[:, LANES:].astype(BF16)
            w2s_ref[...] = w2_ref[0].astype(BF16)

        def row_copy(r):
            tok = jnp.maximum(src_ref[0, 0, r], 0) // TOP_K
            return pltpu.make_async_copy(h_hbm.at[pl.ds(tok, 1), :], xs_ref.at[pl.ds(r, 1), :], gsem)

        def gather_start(r, c):
            row_copy(r).start()
            return c

        def gather_wait(r, c):
            row_copy(r).wait()
            return c

        lax.fori_loop(0, MOE_BLOCK, gather_start, 0)
        lax.fori_loop(0, MOE_BLOCK, gather_wait, 0)

        x = xs_ref[...].astype(BF16)
        hg = jnp.dot(x, wg_ref[...], preferred_element_type=F32) + b1g_ref[0]
        hu = jnp.dot(x, wu_ref[...], preferred_element_type=F32) + b1u_ref[0]
        gate = jnp.minimum(hg, SWIGLU_LIMIT)
        up = jnp.clip(hu, -SWIGLU_LIMIT, SWIGLU_LIMIT)
        act = gate * jax.nn.sigmoid(SWIGLU_ALPHA * gate) * (up + 1.0)
        yb_ref[...] = jnp.dot(act.astype(BF16), w2s_ref[...], preferred_element_type=F32) + b2_ref[0]

        def out_copy(r):
            m = src_ref[0, 0, r]
            dst = (m % TOP_K) * n_tok + m // TOP_K
            return pltpu.make_async_copy(yb_ref.at[pl.ds(r, 1), :], y_hbm.at[pl.ds(dst, 1), :], ssem)

        def scatter_start(r, c):
            out_copy(r).start()
            return c

        def scatter_wait(r, c):
            out_copy(r).wait()
            return c

        lax.fori_loop(0, nvalid, scatter_start, 0)
        lax.fori_loop(0, nvalid, scatter_wait, 0)


def moe_ffn(h, block_e, block_valid, row_src, w1, b1g, b1u, w2, b2):
    n_tok, d = h.shape
    n_blocks = block_e.shape[0]
    n_exp, _, f2 = w1.shape
    f = f2 // 2
    grid_spec = pltpu.PrefetchScalarGridSpec(
        num_scalar_prefetch=2,
        grid=(n_blocks,),
        in_specs=[
            pl.BlockSpec((1, 1, MOE_BLOCK), lambda i, be, nv: (i, 0, 0), memory_space=pltpu.SMEM),
            pl.BlockSpec(memory_space=pl.ANY),
            pl.BlockSpec((1, d, f2), lambda i, be, nv: (be[i], 0, 0)),
            pl.BlockSpec((1, 1, f), lambda i, be, nv: (be[i], 0, 0)),
            pl.BlockSpec((1, 1, f), lambda i, be, nv: (be[i], 0, 0)),
            pl.BlockSpec((1, f, d), lambda i, be, nv: (be[i], 0, 0)),
            pl.BlockSpec((1, 1, d), lambda i, be, nv: (be[i], 0, 0)),
        ],
        out_specs=pl.BlockSpec(memory_space=pl.ANY),
        scratch_shapes=[pltpu.VMEM((MOE_BLOCK, d), F32), pltpu.VMEM((MOE_BLOCK, d), F32),
                        pltpu.VMEM((d, f), BF16), pltpu.VMEM((d, f), BF16), pltpu.VMEM((f, d), BF16),
                        pltpu.SemaphoreType.DMA, pltpu.SemaphoreType.DMA],
    )
    return pl.pallas_call(
        functools.partial(_moe_kernel, n_tok=n_tok),
        grid_spec=grid_spec,
        out_shape=jax.ShapeDtypeStruct((TOP_K * n_tok, d), F32),
        compiler_params=_params("arbitrary"),
        name="moe_ffn",
    )(block_e, block_valid, row_src, h, w1, b1g.reshape(n_exp, 1, f), b1u.reshape(n_exp, 1, f),
      w2, b2.reshape(n_exp, 1, d))


def _combine_kernel(x_ref, y_ref, gates_ref, gate_ref, o_ref):
    g = gates_ref[...]
    acc = g[:, 0:1] * y_ref[0]
    for k in range(1, TOP_K):
        acc = acc + g[:, k:k + 1] * y_ref[k]
    o_ref[...] = x_ref[...] + gate_ref[0] * acc


def moe_combine(x, y, gates, gate_mod, seq, tm=256):
    n, d = x.shape
    per = seq // tm
    return pl.pallas_call(
        _combine_kernel,
        grid=(n // tm,),
        in_specs=[pl.BlockSpec((tm, d), lambda i: (i, 0)),
                  pl.BlockSpec((TOP_K, tm, d), lambda i: (0, i, 0)),
                  pl.BlockSpec((tm, LANES), lambda i: (i, 0)),
                  pl.BlockSpec((1, 1, d), lambda i: (i // per, 0, 0))],
        out_specs=pl.BlockSpec((tm, d), lambda i: (i, 0)),
        out_shape=jax.ShapeDtypeStruct((n, d), F32),
        compiler_params=_params("parallel"),
        name="moe_combine",
    )(x, y.reshape(TOP_K, n, d), gates, gate_mod)


def moe_layer(x, mods, norm_g, router_w, router_b, w1, b1, w2, b2, seq):
    n, d = x.shape
    shift, scale, gate = mods
    rw_pad = jnp.pad(router_w, ((0, 0), (0, LANES - N_EXPERTS)))
    rb_pad = jnp.pad(router_b, (0, LANES - N_EXPERTS)).reshape(1, LANES)
    h, logits = norm_mod_router(x, norm_g.reshape(1, d), scale, shift, rw_pad, rb_pad, seq)
    idx, gates, counts = router(logits)

    top_i = idx[:, :TOP_K]
    rank = idx[:, TOP_K:2 * TOP_K]
    counts = counts[0, :N_EXPERTS]
    padded = (counts + MOE_BLOCK - 1) // MOE_BLOCK * MOE_BLOCK
    pad_ends = jnp.cumsum(padded)
    pad_starts = pad_ends - padded
    dest = (pad_starts[top_i] + rank).reshape(-1)
    m_total = n * TOP_K
    n_blocks = -(-m_total // MOE_BLOCK) + N_EXPERTS
    row_src = jnp.full((n_blocks * MOE_BLOCK,), -1, jnp.int32).at[dest].set(
        jnp.arange(m_total, dtype=jnp.int32))
    block_start = jnp.arange(n_blocks, dtype=jnp.int32) * MOE_BLOCK
    block_e = jnp.minimum(jnp.searchsorted(pad_ends, block_start, side='right'), N_EXPERTS - 1).astype(jnp.int32)
    block_valid = jnp.clip(pad_starts[block_e] + counts[block_e] - block_start, 0, MOE_BLOCK).astype(jnp.int32)

    y = moe_ffn(h, block_e, block_valid, row_src.reshape(n_blocks, 1, MOE_BLOCK),
                w1, b1[:, 0::2], b1[:, 1::2], w2, b2)
    return moe_combine(x, y, gates, gate, seq)


def _pad_heads(w, width):
    k = w.shape[0]
    w = w.reshape(k, MLA_HEADS, width)
    return jnp.pad(w, ((0, 0), (0, 0), (0, LANES - width))).reshape(k, MLA_HEADS * LANES)


def _rope_tables(positions):
    half = QK_ROPE_DIM // 2
    inv = 1.0 / (ROPE_THETA ** (jnp.arange(0, QK_ROPE_DIM, 2, dtype=F32) / QK_ROPE_DIM))
    ang = positions.reshape(-1).astype(F32)[:, None] * inv
    cos, sin = jnp.cos(ang), jnp.sin(ang)
    n = ang.shape[0]
    ones = jnp.ones((n, QK_NOPE_DIM), F32)
    zeros = lambda w: jnp.zeros((n, w), F32)
    tail = LANES - QK_HEAD_DIM
    cos_t = jnp.concatenate([ones, cos, cos, zeros(tail)], axis=1)
    sin_lo = jnp.concatenate([zeros(QK_NOPE_DIM), -sin, zeros(half), zeros(tail)], axis=1)
    sin_hi = jnp.concatenate([zeros(QK_NOPE_DIM), zeros(half), sin, zeros(tail)], axis=1)
    return cos_t, sin_lo, sin_hi


def _split_mods(m, batch):
    d = m.shape[-1] // 3
    m = m[:batch]
    return tuple(m[:, None, j * d:(j + 1) * d] for j in range(3))


def pool_mla_layer(x, mods, norm_g, tables, w_in, pool_w, pool_scale, cq_norm_g, w_uq, ckv_norm_g, w_ukv,
                   q_norm_g, k_norm_g, w_out, batch, seq):
    n, d = x.shape
    shift, scale, gate = mods
    h = norm_mod(x, norm_g.reshape(1, d), scale, shift, seq)
    rope_cols = jnp.pad(w_in[:, Z_ROPE:], ((0, 0), (QK_NOPE_DIM, LANES - QK_HEAD_DIM)))
    w_in_pad = jnp.concatenate([w_in[:, :Z_ROPE], rope_cols], axis=1).astype(BF16)
    z = matmul(h, w_in_pad)
    w_ukv_h = w_ukv.reshape(KV_LORA_RANK, MLA_HEADS, QK_NOPE_DIM + V_HEAD_DIM)
    wuk_pad = _pad_heads(w_ukv_h[:, :, :QK_NOPE_DIM].reshape(KV_LORA_RANK, -1), QK_NOPE_DIM).astype(BF16)
    wuv_pad = _pad_heads(w_ukv_h[:, :, QK_NOPE_DIM:].reshape(KV_LORA_RANK, -1), V_HEAD_DIM).astype(BF16)
    wuq_pad = _pad_heads(w_uq, QK_HEAD_DIM).astype(BF16)
    pad_g = lambda g: jnp.pad(g, (0, LANES - QK_HEAD_DIM)).reshape(1, LANES)
    y_pool, q, k, v = mla_prep(z, pool_w.astype(BF16), pool_scale.reshape(1, -1), cq_norm_g.reshape(1, -1), wuq_pad,
                               ckv_norm_g.reshape(1, -1), wuk_pad, wuv_pad, pad_g(q_norm_g), pad_g(k_norm_g),
                               *tables, batch, seq)
    y_att = flash_attention(q, k, v).reshape(n, MLA_HEADS * V_HEAD_DIM)
    w_out_b = w_out.astype(BF16)
    return proj_residual([y_pool, y_att], [w_out_b[:POOL_WIDTH], w_out_b[POOL_WIDTH:]],
                         jnp.zeros((1, d), F32), x, gate, seq)


def conformer_layer(x, mods, norm_g, pw1_w, pw1_b, dw_w, dw_b, ln_g, ln_b, pw2_w, pw2_b, seq):
    n, d = x.shape
    shift, scale, gate = mods
    h = norm_mod(x, norm_g.reshape(1, d), scale, shift, seq)
    cd = pw1_w.shape[1] // 2
    pw1 = pw1_w.astype(BF16)
    u = glu_matmul(h, pw1[:, :cd], pw1[:, cd:], pw1_b[:cd].reshape(1, cd), pw1_b[cd:].reshape(1, cd))
    w_pad = jnp.pad(dw_w, ((0, CONV_HALO - CONV_WIDTH), (0, 0)))
    u = dwconv_ln_silu(u, w_pad, dw_b.reshape(1, cd), ln_g.reshape(1, cd), ln_b.reshape(1, cd), seq)
    return proj_residual([u], [pw2_w.astype(BF16)], pw2_b.reshape(1, d), x, gate, seq)


def kernel(x, c, positions, ada_mix_w, ada_mix_b, norm_mix_g, w_in, pool_w, pool_scale, cq_norm_g, w_uq,
           ckv_norm_g, w_ukv, q_norm_g, k_norm_g, w_out, conv_pw1_w, conv_pw1_b, conv_dw_w, conv_dw_b,
           conv_ln_g, conv_ln_b, conv_pw2_w, conv_pw2_b, ada_ffn_w, ada_ffn_b, norm_ffn_g, router_w,
           router_b, moe_w1, moe_b1, moe_w2, moe_b2):
    batch, seq, d = x.shape
    depth = ada_mix_w.shape[0]
    c_pad = jnp.pad(c, ((0, 8 - batch), (0, 0)))
    mix_mods = adaln(c_pad, ada_mix_w, ada_mix_b)
    ffn_mods = adaln(c_pad, ada_ffn_w, ada_ffn_b)
    tables = _rope_tables(positions)
    xf = x.reshape(batch * seq, d)
    for layer in range(depth):
        i = layer // 2
        mods = _split_mods(mix_mods[layer], batch)
        if layer % 2 == 0:
            xf = pool_mla_layer(xf, mods, norm_mix_g[layer], tables, w_in[i], pool_w[i], pool_scale[i],
                                cq_norm_g[i], w_uq[i], ckv_norm_g[i], w_ukv[i], q_norm_g[i], k_norm_g[i],
                                w_out[i], batch, seq)
        else:
            xf = conformer_layer(xf, mods, norm_mix_g[layer], conv_pw1_w[i], conv_pw1_b[i], conv_dw_w[i],
                                 conv_dw_b[i], conv_ln_g[i], conv_ln_b[i], conv_pw2_w[i], conv_pw2_b[i], seq)
        xf = moe_layer(xf, _split_mods(ffn_mods[layer], batch), norm_ffn_g[layer], router_w[layer],
                       router_b[layer], moe_w1[layer], moe_b1[layer], moe_w2[layer], moe_b2[layer], seq)
    return xf.reshape(batch, seq, d)
```

```python
import functools

import jax
import jax.numpy as jnp
from jax import lax
from jax.experimental import pallas as pl
from jax.experimental.pallas import tpu as pltpu

F32 = jnp.float32
BF16 = jnp.bfloat16
HIGHEST = lax.Precision.HIGHEST

EPS = 1e-6
POOL_WINDOWS = (2, 4, 8, 16)
POOL_GROUP_DIM = 128
POOL_WIDTH = POOL_GROUP_DIM * len(POOL_WINDOWS)
MLA_HEADS = 8
QK_NOPE_DIM = 64
QK_ROPE_DIM = 32
QK_HEAD_DIM = QK_NOPE_DIM + QK_ROPE_DIM
V_HEAD_DIM = 64
Q_LORA_RANK = 384
KV_LORA_RANK = 256
ROPE_THETA = 10000.0
CHUNK = 64
CONV_WIDTH = 31
N_EXPERTS = 32
TOP_K = 4
SWIGLU_ALPHA = 1.702
SWIGLU_LIMIT = 7.0
MOE_BLOCK = 256

LANES = 128
POOL_HALO = 16
CONV_HALO = 32
MASK_VALUE = -1e30
VMEM_LIMIT = 52 * 1024 * 1024

Z_CQ = POOL_WIDTH
Z_CKV = Z_CQ + Q_LORA_RANK
Z_ROPE = Z_CKV + KV_LORA_RANK
Z_WIDTH = Z_ROPE + LANES


def _params(*sem, vmem=None):
    return pltpu.CompilerParams(dimension_semantics=sem, vmem_limit_bytes=vmem or VMEM_LIMIT,
                                disable_bounds_checks=True)


def _adaln_kernel(c_ref, w_ref, b_ref, o_ref):
    c = c_ref[...]
    s = c * jax.nn.sigmoid(c)
    o_ref[0] = jnp.dot(s, w_ref[0], preferred_element_type=F32, precision=HIGHEST) + b_ref[0]


def adaln(c_pad, w, b):
    n_l, d, d3 = w.shape
    tn = 512
    return pl.pallas_call(
        _adaln_kernel,
        grid=(n_l, d3 // tn),
        in_specs=[pl.BlockSpec((8, d), lambda l, j: (0, 0)),
                  pl.BlockSpec((1, d, tn), lambda l, j: (l, 0, j)),
                  pl.BlockSpec((1, 1, tn), lambda l, j: (l, 0, j))],
        out_specs=pl.BlockSpec((1, 8, tn), lambda l, j: (l, 0, j)),
        out_shape=jax.ShapeDtypeStruct((n_l, 8, d3), F32),
        compiler_params=_params("parallel", "parallel"),
        name="adaln",
    )(c_pad, w, b.reshape(n_l, 1, d3))


def _modulated_norm(x, g, scale, shift):
    ms = jnp.mean(x * x, axis=-1, keepdims=True)
    return x * lax.rsqrt(ms + EPS) * g * (1.0 + scale) + shift


def _norm_mod_kernel(x_ref, g_ref, sc_ref, sh_ref, o_ref):
    o_ref[...] = _modulated_norm(x_ref[...], g_ref[...], sc_ref[0], sh_ref[0]).astype(o_ref.dtype)


def _norm_mod_router_kernel(x_ref, g_ref, sc_ref, sh_ref, rw_ref, rb_ref, o_ref, lg_ref):
    h = _modulated_norm(x_ref[...], g_ref[...], sc_ref[0], sh_ref[0])
    o_ref[...] = h
    lg_ref[...] = jnp.dot(h, rw_ref[...], preferred_element_type=F32, precision=HIGHEST) + rb_ref[...]


def norm_mod(x, g, scale, shift, seq, tm=512):
    n, d = x.shape
    per = seq // tm
    vec = pl.BlockSpec((1, 1, d), lambda i: (i // per, 0, 0))
    return pl.pallas_call(
        _norm_mod_kernel,
        grid=(n // tm,),
        in_specs=[pl.BlockSpec((tm, d), lambda i: (i, 0)), pl.BlockSpec((1, d), lambda i: (0, 0)), vec, vec],
        out_specs=pl.BlockSpec((tm, d), lambda i: (i, 0)),
        out_shape=jax.ShapeDtypeStruct((n, d), BF16),
        compiler_params=_params("parallel"),
        name="norm_mod",
    )(x, g, scale, shift)


def norm_mod_router(x, g, scale, shift, rw_pad, rb_pad, seq, tm=512):
    n, d = x.shape
    per = seq // tm
    vec = pl.BlockSpec((1, 1, d), lambda i: (i // per, 0, 0))
    return pl.pallas_call(
        _norm_mod_router_kernel,
        grid=(n // tm,),
        in_specs=[pl.BlockSpec((tm, d), lambda i: (i, 0)), pl.BlockSpec((1, d), lambda i: (0, 0)), vec, vec,
                  pl.BlockSpec((d, LANES), lambda i: (0, 0)), pl.BlockSpec((1, LANES), lambda i: (0, 0))],
        out_specs=[pl.BlockSpec((tm, d), lambda i: (i, 0)), pl.BlockSpec((tm, LANES), lambda i: (i, 0))],
        out_shape=[jax.ShapeDtypeStruct((n, d), F32), jax.ShapeDtypeStruct((n, LANES), F32)],
        compiler_params=_params("parallel"),
        name="norm_mod_router",
    )(x, g, scale, shift, rw_pad, rb_pad)


def _matmul_kernel(a_ref, w_ref, o_ref):
    o_ref[...] = jnp.dot(a_ref[...], w_ref[...], preferred_element_type=F32).astype(o_ref.dtype)


def matmul(a, w, out_dtype=F32, tm=512):
    n, k = a.shape
    m = w.shape[1]
    return pl.pallas_call(
        _matmul_kernel,
        grid=(n // tm,),
        in_specs=[pl.BlockSpec((tm, k), lambda i: (i, 0)), pl.BlockSpec((k, m), lambda i: (0, 0))],
        out_specs=pl.BlockSpec((tm, m), lambda i: (i, 0)),
        out_shape=jax.ShapeDtypeStruct((n, m), out_dtype),
        compiler_params=_params("parallel"),
        name="matmul",
    )(a, w)


def _glu_kernel(a_ref, wv_ref, wg_ref, bv_ref, bg_ref, o_ref):
    a = a_ref[...]
    val = jnp.dot(a, wv_ref[...], preferred_element_type=F32) + bv_ref[...]
    gt = jnp.dot(a, wg_ref[...], preferred_element_type=F32) + bg_ref[...]
    o_ref[...] = val * jax.nn.sigmoid(gt)


def glu_matmul(a, wv, wg, bv, bg, tm=512):
    n, k = a.shape
    m = wv.shape[1]
    full = lambda r, c: pl.BlockSpec((r, c), lambda i: (0, 0))
    return pl.pallas_call(
        _glu_kernel,
        grid=(n // tm,),
        in_specs=[pl.BlockSpec((tm, k), lambda i: (i, 0)), full(k, m), full(k, m), full(1, m), full(1, m)],
        out_specs=pl.BlockSpec((tm, m), lambda i: (i, 0)),
        out_shape=jax.ShapeDtypeStruct((n, m), F32),
        compiler_params=_params("parallel"),
        name="glu_matmul",
    )(a, wv, wg, bv, bg)


def _proj_residual_kernel(*refs, n_in):
    a_refs, w_refs = refs[:n_in], refs[n_in:2 * n_in]
    b_ref, x_ref, gate_ref, o_ref = refs[2 * n_in:]
    acc = b_ref[...]
    for a_ref, w_ref in zip(a_refs, w_refs):
        acc = acc + jnp.dot(a_ref[...], w_ref[...], preferred_element_type=F32)
    o_ref[...] = x_ref[...] + gate_ref[0] * acc


def proj_residual(a_list, w_list, bias, x, gate, seq, tm=512):
    n, d = x.shape
    per = seq // tm
    n_in = len(a_list)
    in_specs = [pl.BlockSpec((tm, a.shape[1]), lambda i: (i, 0)) for a in a_list]
    in_specs += [pl.BlockSpec(w.shape, lambda i: (0, 0)) for w in w_list]
    in_specs += [pl.BlockSpec((1, d), lambda i: (0, 0)), pl.BlockSpec((tm, d), lambda i: (i, 0)),
                 pl.BlockSpec((1, 1, d), lambda i: (i // per, 0, 0))]
    return pl.pallas_call(
        functools.partial(_proj_residual_kernel, n_in=n_in),
        grid=(n // tm,),
        in_specs=in_specs,
        out_specs=pl.BlockSpec((tm, d), lambda i: (i, 0)),
        out_shape=jax.ShapeDtypeStruct((n, d), F32),
        compiler_params=_params("parallel"),
        name="proj_residual",
    )(*a_list, *w_list, bias, x, gate)


def _rope(xn, cos_t, sin_lo, sin_hi):
    return (xn * cos_t + pltpu.roll(xn, LANES - QK_ROPE_DIM // 2, axis=1) * sin_lo
            + pltpu.roll(xn, QK_ROPE_DIM // 2, axis=1) * sin_hi)


def _mla_prep_kernel(z_ref, halo_ref, pw_ref, ps_ref, cqg_ref, wuq_ref, ckvg_ref, wuk_ref, wuv_ref,
                     qg_ref, kg_ref, cos_ref, slo_ref, shi_ref,
                     yp_ref, q_ref, k_ref, v_ref, ext_ref, *, tm, per):
    si = pl.program_id(0) % per
    u = z_ref[:, 0:POOL_WIDTH]
    ext_ref[0:POOL_HALO, :] = jnp.where(si == 0, 0.0, halo_ref[...])
    ext_ref[POOL_HALO:, :] = u
    t = si * tm + lax.broadcasted_iota(jnp.int32, (tm, 1), 0)
    for g, w in enumerate(POOL_WINDOWS):
        cols = slice(g * POOL_GROUP_DIM, (g + 1) * POOL_GROUP_DIM)
        ug = u[:, cols]
        s = ug
        for j in range(1, w):
            s = s + ext_ref[POOL_HALO - j:POOL_HALO - j + tm, cols]
        cnt = jnp.minimum(t + 1, w).astype(F32)
        pooled = s / cnt - ug
        yp = jnp.dot(pooled.astype(BF16), pw_ref[g], preferred_element_type=F32) * ps_ref[:, cols]
        yp_ref[:, cols] = yp.astype(yp_ref.dtype)

    cos_t, sin_lo, sin_hi = cos_ref[...], slo_ref[...], shi_ref[...]
    inv_head = 1.0 / QK_HEAD_DIM

    cq = z_ref[:, Z_CQ:Z_CKV]
    cqn = cq * lax.rsqrt(jnp.mean(cq * cq, axis=-1, keepdims=True) + EPS) * cqg_ref[...]
    qf = jnp.dot(cqn.astype(BF16), wuq_ref[...], preferred_element_type=F32)
    q_scale = QK_HEAD_DIM ** -0.5
    for h in range(MLA_HEADS):
        qh = qf[:, h * LANES:(h + 1) * LANES]
        ss = jnp.sum(qh * qh, axis=-1, keepdims=True) * inv_head
        qn = qh * lax.rsqrt(ss + EPS) * qg_ref[...]
        q_ref[0, h] = (_rope(qn, cos_t, sin_lo, sin_hi) * q_scale).astype(q_ref.dtype)

    ckv = z_ref[:, Z_CKV:Z_ROPE]
    ckvn = (ckv * lax.rsqrt(jnp.mean(ckv * ckv, axis=-1, keepdims=True) + EPS) * ckvg_ref[...]).astype(BF16)
    kf = jnp.dot(ckvn, wuk_ref[...], preferred_element_type=F32)
    vf = jnp.dot(ckvn, wuv_ref[...], preferred_element_type=F32)
    k_rope = z_ref[:, Z_ROPE:Z_WIDTH]
    for h in range(MLA_HEADS):
        kh = kf[:, h * LANES:(h + 1) * LANES] + k_rope
        ss = jnp.sum(kh * kh, axis=-1, keepdims=True) * inv_head
        kn = kh * lax.rsqrt(ss + EPS) * kg_ref[...]
        k_ref[0, h] = _rope(kn, cos_t, sin_lo, sin_hi).astype(k_ref.dtype)
        v_ref[0, h] = vf[:, h * LANES:(h + 1) * LANES].astype(v_ref.dtype)


def mla_prep(z, pool_w, pool_scale, cq_g, wuq_pad, ckv_g, wuk_pad, wuv_pad, qg_pad, kg_pad,
             cos_t, sin_lo, sin_hi, batch, seq, tm=256):
    n = z.shape[0]
    per = seq // tm
    hb = tm // POOL_HALO
    full = lambda a: pl.BlockSpec(a.shape, lambda i: (0,) * a.ndim)
    tab = pl.BlockSpec((tm, LANES), lambda i: (i, 0))
    head_out = pl.BlockSpec((1, MLA_HEADS, tm, LANES), lambda i: (i // per, 0, i % per, 0))
    head_shape = jax.ShapeDtypeStruct((batch, MLA_HEADS, seq, LANES), BF16)
    return pl.pallas_call(
        functools.partial(_mla_prep_kernel, tm=tm, per=per),
        grid=(n // tm,),
        in_specs=[pl.BlockSpec((tm, Z_WIDTH), lambda i: (i, 0)),
                  pl.BlockSpec((POOL_HALO, POOL_WIDTH), lambda i: (jnp.maximum(i * hb - 1, 0), 0)),
                  full(pool_w), full(pool_scale), full(cq_g), full(wuq_pad), full(ckv_g), full(wuk_pad),
                  full(wuv_pad), full(qg_pad), full(kg_pad), tab, tab, tab],
        out_specs=[pl.BlockSpec((tm, POOL_WIDTH), lambda i: (i, 0)), head_out, head_out, head_out],
        out_shape=[jax.ShapeDtypeStruct((n, POOL_WIDTH), BF16), head_shape, head_shape, head_shape],
        scratch_shapes=[pltpu.VMEM((tm + POOL_HALO, POOL_WIDTH), F32)],
        compiler_params=_params("parallel"),
        name="mla_prep",
    )(z, z, pool_w, pool_scale, cq_g, wuq_pad, ckv_g, wuk_pad, wuv_pad, qg_pad, kg_pad, cos_t, sin_lo, sin_hi)


def _flash_kernel(q_ref, k_ref, v_ref, o_ref, m_ref, l_ref, acc_ref, *, tq, heads):
    qi = pl.program_id(2)
    rows = lax.broadcasted_iota(jnp.int32, (tq, tq), 0) // CHUNK
    cols = lax.broadcasted_iota(jnp.int32, (tq, tq), 1) // CHUNK
    diag_mask = cols <= rows
    slabs = tq // LANES
    m_ref[...] = jnp.full(m_ref.shape, MASK_VALUE, F32)
    l_ref[...] = jnp.zeros(l_ref.shape, F32)
    acc_ref[...] = jnp.zeros(acc_ref.shape, F32)

    def step(j, masked):
        start = pl.multiple_of(j * tq, tq)
        for hh in range(heads):
            k = k_ref[0, hh, pl.ds(start, tq), :]
            v = v_ref[0, hh, pl.ds(start, tq), :]
            s = lax.dot_general(q_ref[0, hh], k, (((1,), (1,)), ((), ())), preferred_element_type=F32)
            if masked:
                s = jnp.where(diag_mask, s, MASK_VALUE)
            cols_s = [s[:, c * LANES:(c + 1) * LANES] for c in range(slabs)]
            s_max = cols_s[0]
            for sc in cols_s[1:]:
                s_max = jnp.maximum(s_max, sc)
            m_prev = m_ref[hh]
            m_new = jnp.maximum(m_prev, jnp.max(s_max, axis=-1, keepdims=True))
            alpha = jnp.exp(m_prev - m_new)
            ps = [jnp.exp(sc - m_new) for sc in cols_s]
            l_part = ps[0]
            for pc in ps[1:]:
                l_part = l_part + pc
            p = jnp.concatenate(ps, axis=1).astype(v.dtype)
            l_ref[hh] = alpha * l_ref[hh] + l_part
            acc_ref[hh] = alpha * acc_ref[hh] + jnp.dot(p, v, preferred_element_type=F32)
            m_ref[hh] = m_new

    def full_step(j, carry):
        step(j, False)
        return carry

    lax.fori_loop(0, qi, full_step, 0)
    step(qi, True)
    outs = []
    for hh in range(heads):
        l_row = jnp.sum(l_ref[hh], axis=-1, keepdims=True)
        outs.append((acc_ref[hh] / l_row)[:, :V_HEAD_DIM])
    o_ref[0] = jnp.concatenate(outs, axis=-1).astype(o_ref.dtype)


def flash_attention(q, k, v, tq=512):
    b, h, s, _ = q.shape
    heads = LANES // V_HEAD_DIM
    return pl.pallas_call(
        functools.partial(_flash_kernel, tq=tq, heads=heads),
        grid=(b, h // heads, s // tq),
        in_specs=[pl.BlockSpec((1, heads, tq, LANES), lambda bi, hi, qi: (bi, hi, qi, 0)),
                  pl.BlockSpec((1, heads, s, LANES), lambda bi, hi, qi: (bi, hi, 0, 0)),
                  pl.BlockSpec((1, heads, s, LANES), lambda bi, hi, qi: (bi, hi, 0, 0))],
        out_specs=pl.BlockSpec((1, tq, LANES), lambda bi, hi, qi: (bi, qi, hi)),
        out_shape=jax.ShapeDtypeStruct((b, s, h * V_HEAD_DIM), BF16),
        scratch_shapes=[pltpu.VMEM((heads, tq, LANES), F32) for _ in range(3)],
        compiler_params=_params("parallel", "parallel", "parallel"),
        name="flash_attention",
    )(q, k, v)


def _dwconv_kernel(u_ref, halo_ref, w_ref, b_ref, g_ref, beta_ref, o_ref, ext_ref, *, tm, per):
    si = pl.program_id(0) % per
    ext_ref[0:CONV_HALO, :] = jnp.where(si == 0, 0.0, halo_ref[...])
    ext_ref[CONV_HALO:, :] = u_ref[...]
    first = CONV_HALO - (CONV_WIDTH - 1)
    acc = jnp.broadcast_to(b_ref[...], (tm, b_ref.shape[1]))
    for j in range(CONV_WIDTH):
        acc = acc + w_ref[j:j + 1, :] * ext_ref[first + j:first + j + tm, :]
    mu = jnp.mean(acc, axis=-1, keepdims=True)
    cen = acc - mu
    var = jnp.mean(cen * cen, axis=-1, keepdims=True)
    y = cen * lax.rsqrt(var + EPS) * g_ref[...] + beta_ref[...]
    o_ref[...] = (y * jax.nn.sigmoid(y)).astype(o_ref.dtype)


def dwconv_ln_silu(u, w_pad, b, g, beta, seq, tm=256):
    n, d = u.shape
    per = seq // tm
    hb = tm // CONV_HALO
    full = lambda a: pl.BlockSpec(a.shape, lambda i: (0, 0))
    return pl.pallas_call(
        functools.partial(_dwconv_kernel, tm=tm, per=per),
        grid=(n // tm,),
        in_specs=[pl.BlockSpec((tm, d), lambda i: (i, 0)),
                  pl.BlockSpec((CONV_HALO, d), lambda i: (jnp.maximum(i * hb - 1, 0), 0)),
                  full(w_pad), full(b), full(g), full(beta)],
        out_specs=pl.BlockSpec((tm, d), lambda i: (i, 0)),
        out_shape=jax.ShapeDtypeStruct((n, d), BF16),
        scratch_shapes=[pltpu.VMEM((tm + CONV_HALO, d), F32)],
        compiler_params=_params("parallel"),
        name="dwconv_ln_silu",
    )(u, u, w_pad, b, g, beta)


def _router_kernel(lg_ref, idx_ref, gate_ref, cnt_ref, carry_ref, *, tm):
    @pl.when(pl.program_id(0) == 0)
    def _():
        carry_ref[...] = jnp.zeros(carry_ref.shape, F32)

    lane = lax.broadcasted_iota(jnp.int32, (tm, LANES), 1)
    lane_f = lane.astype(F32)
    neg = -jnp.inf
    logits = jnp.where(lane < N_EXPERTS, lg_ref[...], neg)
    picks, vals, ids = [], [], []
    for _ in range(TOP_K):
        mx = jnp.max(logits, axis=-1, keepdims=True)
        idx = jnp.min(jnp.where(logits == mx, lane_f, float(LANES)), axis=-1, keepdims=True)
        pick = lane_f == idx
        picks.append(pick)
        vals.append(mx)
        ids.append(idx)
        logits = jnp.where(pick, neg, logits)
    exps = [jnp.exp(v - vals[0]) for v in vals]
    den = exps[0]
    for e in exps[1:]:
        den = den + e

    chosen = jnp.zeros((tm, LANES), F32)
    for pick in picks:
        chosen = chosen + pick.astype(F32)
    r_io = lax.broadcasted_iota(jnp.int32, (tm, tm), 0)
    c_io = lax.broadcasted_iota(jnp.int32, (tm, tm), 1)
    earlier = (c_io < r_io).astype(BF16)
    before = jnp.dot(earlier, chosen.astype(BF16), preferred_element_type=F32) + carry_ref[0:1, :]
    idx_out = jnp.zeros((tm, LANES), F32)
    gate_out = jnp.zeros((tm, LANES), F32)
    for k in range(TOP_K):
        rank = jnp.sum(jnp.where(picks[k], before, 0.0), axis=-1, keepdims=True)
        idx_out = jnp.where(lane == k, ids[k], idx_out)
        idx_out = jnp.where(lane == TOP_K + k, rank, idx_out)
        gate_out = jnp.where(lane == k, exps[k] / den, gate_out)
    idx_ref[...] = idx_out.astype(jnp.int32)
    gate_ref[...] = gate_out
    total = carry_ref[0:1, :] + jnp.sum(chosen, axis=0, keepdims=True)
    carry_ref[...] = jnp.broadcast_to(total, carry_ref.shape)
    cnt_ref[...] = jnp.broadcast_to(total, cnt_ref.shape).astype(jnp.int32)


def router(logits, tm=512):
    n = logits.shape[0]
    row = pl.BlockSpec((tm, LANES), lambda i: (i, 0))
    return pl.pallas_call(
        functools.partial(_router_kernel, tm=tm),
        grid=(n // tm,),
        in_specs=[row],
        out_specs=[row, row, pl.BlockSpec((8, LANES), lambda i: (0, 0))],
        out_shape=[jax.ShapeDtypeStruct((n, LANES), jnp.int32), jax.ShapeDtypeStruct((n, LANES), F32),
                   jax.ShapeDtypeStruct((8, LANES), jnp.int32)],
        scratch_shapes=[pltpu.VMEM((8, LANES), F32)],
        compiler_params=_params("arbitrary"),
        name="router",
    )(logits)


def _dispatch_kernel(pad_at_ref, pad_n_ref, nu_ref, dest_ref, h_ref, xs_hbm, zero_ref, sem, zsem, *, tm):
    @pl.when(pl.program_id(0) == 0)
    def _():
        zero_ref[...] = jnp.zeros(zero_ref.shape, F32)

        def zero_block(b):
            at = pl.multiple_of(b * MOE_BLOCK, MOE_BLOCK)
            return pltpu.make_async_copy(zero_ref, xs_hbm.at[pl.ds(at, MOE_BLOCK), :], zsem)

        def tail_start(b, c):
            zero_block(b).start()
            return c

        def tail_wait(b, c):
            zero_block(b).wait()
            return c

        n_blocks = xs_hbm.shape[0] // MOE_BLOCK
        lax.fori_loop(nu_ref[0], n_blocks, tail_start, 0)
        lax.fori_loop(nu_ref[0], n_blocks, tail_wait, 0)

        def fill(e, c):
            n = pad_n_ref[e]
            at = pad_at_ref[e]

            def zero_row(j):
                return pltpu.make_async_copy(zero_ref.at[pl.ds(0, 1), :], xs_hbm.at[pl.ds(at + j, 1), :], zsem)

            def fill_start(j, cc):
                zero_row(j).start()
                return cc

            def fill_wait(j, cc):
                zero_row(0).wait()
                return cc

            lax.fori_loop(0, n, fill_start, 0)
            lax.fori_loop(0, n, fill_wait, 0)
            return c

        lax.fori_loop(0, N_EXPERTS, fill, 0)

    def start(r, c):
        for k in range(TOP_K):
            dst = dest_ref[0, 0, r * TOP_K + k]
            pltpu.make_async_copy(h_ref.at[pl.ds(r, 1), :], xs_hbm.at[pl.ds(dst, 1), :], sem).start()
        return c

    def wait(r, c):
        for k in range(TOP_K):
            pltpu.make_async_copy(h_ref.at[pl.ds(0, 1), :], xs_hbm.at[pl.ds(0, 1), :], sem).wait()
        return c

    lax.fori_loop(0, tm, start, 0)
    lax.fori_loop(0, tm, wait, 0)


def moe_dispatch(h, dest, pad_at, pad_n, n_used, n_rows, tm=256):
    n, d = h.shape
    grid_spec = pltpu.PrefetchScalarGridSpec(
        num_scalar_prefetch=3,
        grid=(n // tm,),
        in_specs=[pl.BlockSpec((1, 1, tm * TOP_K), lambda i, a, b, c: (i, 0, 0), memory_space=pltpu.SMEM),
                  pl.BlockSpec((tm, d), lambda i, a, b, c: (i, 0))],
        out_specs=pl.BlockSpec(memory_space=pl.ANY),
        scratch_shapes=[pltpu.VMEM((MOE_BLOCK, d), F32), pltpu.SemaphoreType.DMA, pltpu.SemaphoreType.DMA],
    )
    return pl.pallas_call(
        functools.partial(_dispatch_kernel, tm=tm),
        grid_spec=grid_spec,
        out_shape=jax.ShapeDtypeStruct((n_rows, d), F32),
        compiler_params=_params("arbitrary"),
        name="moe_dispatch",
    )(pad_at, pad_n, n_used, dest.reshape(n // tm, 1, tm * TOP_K), h)


def _ffn_kernel(be_ref, nu_ref, xs_ref, w1_ref, b1g_ref, b1u_ref, w2_ref, b2_ref, y_ref,
                wg_ref, wu_ref, w2s_ref):
    i = pl.program_id(0)
    pair = 2 * LANES

    @pl.when(i < nu_ref[0])
    def _():
        e = be_ref[i]
        prev = be_ref[jnp.maximum(i - 1, 0)]

        @pl.when((i == 0) | (e != prev))
        def _():
            r_io = lax.broadcasted_iota(jnp.int32, (pair, pair), 0)
            c_io = lax.broadcasted_iota(jnp.int32, (pair, pair), 1)
            want = jnp.where(c_io < LANES, 2 * c_io, 2 * (c_io - LANES) + 1)
            sel = (r_io == want).astype(BF16)
            for c in range(wg_ref.shape[1] // LANES):
                slab = w1_ref[0, 0, :, c * pair:(c + 1) * pair].astype(BF16)
                split = jnp.dot(slab, sel, preferred_element_type=F32)
                wg_ref[:, c * LANES:(c + 1) * LANES] = split[:, :LANES].astype(BF16)
                wu_ref[:, c * LANES:(c + 1) * LANES] = split[:, LANES:].astype(BF16)
            w2s_ref[...] = w2_ref[0, 0].astype(BF16)

        x = xs_ref[...].astype(BF16)
        hg = jnp.dot(x, wg_ref[...], preferred_element_type=F32) + b1g_ref[0]
        hu = jnp.dot(x, wu_ref[...], preferred_element_type=F32) + b1u_ref[0]
        gate = jnp.minimum(hg, SWIGLU_LIMIT)
        up = jnp.clip(hu, -SWIGLU_LIMIT, SWIGLU_LIMIT)
        act = gate * jax.nn.sigmoid(SWIGLU_ALPHA * gate) * (up + 1.0)
        y_ref[...] = jnp.dot(act.astype(BF16), w2s_ref[...], preferred_element_type=F32) + b2_ref[0]

    @pl.when(i >= nu_ref[0])
    def _():
        y_ref[...] = jnp.zeros(y_ref.shape, F32)


def moe_ffn(xs, block_e, n_used, layer, w1, b1g, b1u, w2, b2):
    n_rows, d = xs.shape
    n_blocks = n_rows // MOE_BLOCK
    n_exp, f2 = w1.shape[1], w1.shape[3]
    f = f2 // 2
    rows = lambda i, be, nu: (jnp.minimum(i, nu[0] - 1), 0)
    vec = lambda width: pl.BlockSpec((1, 1, width), lambda i, be, nu: (be[i], 0, 0))
    grid_spec = pltpu.PrefetchScalarGridSpec(
        num_scalar_prefetch=2,
        grid=(n_blocks,),
        in_specs=[
            pl.BlockSpec((MOE_BLOCK, d), rows),
            pl.BlockSpec((1, 1, d, f2), lambda i, be, nu: (layer, be[i], 0, 0)),
            vec(f), vec(f),
            pl.BlockSpec((1, 1, f, d), lambda i, be, nu: (layer, be[i], 0, 0)),
            vec(d),
        ],
        out_specs=pl.BlockSpec((MOE_BLOCK, d), lambda i, be, nu: (i, 0)),
        scratch_shapes=[pltpu.VMEM((d, f), BF16), pltpu.VMEM((d, f), BF16), pltpu.VMEM((f, d), BF16)],
    )
    return pl.pallas_call(
        _ffn_kernel,
        grid_spec=grid_spec,
        out_shape=jax.ShapeDtypeStruct((n_rows, d), F32),
        compiler_params=_params("arbitrary"),
        name="moe_ffn",
    )(block_e, n_used, xs, w1, b1g.reshape(n_exp, 1, f), b1u.reshape(n_exp, 1, f), w2, b2.reshape(n_exp, 1, d))


def _combine_kernel(dest_ref, x_ref, gates_ref, gate_ref, ys_hbm, o_ref, ybuf_ref, sem, *, tm):
    def start(r, c):
        for k in range(TOP_K):
            src = dest_ref[0, 0, r * TOP_K + k]
            pltpu.make_async_copy(ys_hbm.at[pl.ds(src, 1), :], ybuf_ref.at[k, pl.ds(r, 1), :], sem).start()
        return c

    def wait(r, c):
        for k in range(TOP_K):
            pltpu.make_async_copy(ys_hbm.at[pl.ds(0, 1), :], ybuf_ref.at[0, pl.ds(0, 1), :], sem).wait()
        return c

    lax.fori_loop(0, tm, start, 0)
    lax.fori_loop(0, tm, wait, 0)
    g = gates_ref[...]
    acc = g[:, 0:1] * ybuf_ref[0]
    for k in range(1, TOP_K):
        acc = acc + g[:, k:k + 1] * ybuf_ref[k]
    o_ref[...] = x_ref[...] + gate_ref[0] * acc


def moe_combine(x, ys, dest, gates, gate_mod, seq, tm=256):
    n, d = x.shape
    per = seq // tm
    return pl.pallas_call(
        functools.partial(_combine_kernel, tm=tm),
        grid=(n // tm,),
        in_specs=[pl.BlockSpec((1, 1, tm * TOP_K), lambda i: (i, 0, 0), memory_space=pltpu.SMEM),
                  pl.BlockSpec((tm, d), lambda i: (i, 0)),
                  pl.BlockSpec((tm, LANES), lambda i: (i, 0)),
                  pl.BlockSpec((1, 1, d), lambda i: (i // per, 0, 0)),
                  pl.BlockSpec(memory_space=pl.ANY)],
        out_specs=pl.BlockSpec((tm, d), lambda i: (i, 0)),
        out_shape=jax.ShapeDtypeStruct((n, d), F32),
        scratch_shapes=[pltpu.VMEM((TOP_K, tm, d), F32), pltpu.SemaphoreType.DMA],
        compiler_params=_params("arbitrary"),
        name="moe_combine",
    )(dest.reshape(n // tm, 1, tm * TOP_K), x, gates, gate_mod, ys)


def moe_layer(x, mods, norm_g, router_w, router_b, layer, w1, b1, w2, b2, seq):
    n, d = x.shape
    shift, scale, gate = mods
    rw_pad = jnp.pad(router_w, ((0, 0), (0, LANES - N_EXPERTS)))
    rb_pad = jnp.pad(router_b, (0, LANES - N_EXPERTS)).reshape(1, LANES)
    h, logits = norm_mod_router(x, norm_g.reshape(1, d), scale, shift, rw_pad, rb_pad, seq)
    idx, gates, counts = router(logits)

    top_i = idx[:, :TOP_K]
    rank = idx[:, TOP_K:2 * TOP_K]
    counts = counts[0, :N_EXPERTS]
    experts = jnp.arange(N_EXPERTS, dtype=jnp.int32)
    padded = (counts + MOE_BLOCK - 1) // MOE_BLOCK * MOE_BLOCK
    pad_ends = jnp.sum(jnp.where(experts[:, None] >= experts[None, :], padded[None, :], 0), axis=1)
    pad_starts = pad_ends - padded
    dest = jnp.sum(jnp.where(top_i[..., None] == experts, pad_starts, 0), axis=-1) + rank
    n_blocks = -(-n * TOP_K // MOE_BLOCK) + N_EXPERTS
    block_start = jnp.arange(n_blocks, dtype=jnp.int32) * MOE_BLOCK
    block_e = jnp.minimum(jnp.sum((pad_ends[None, :] <= block_start[:, None]).astype(jnp.int32), axis=1),
                          N_EXPERTS - 1)
    n_used = (pad_ends[N_EXPERTS - 1:] // MOE_BLOCK).astype(jnp.int32)

    xs = moe_dispatch(h, dest, pad_starts + counts, padded - counts, n_used, n_blocks * MOE_BLOCK)
    ys = moe_ffn(xs, block_e, n_used, layer, w1, b1[layer][:, 0::2], b1[layer][:, 1::2], w2, b2[layer])
    return moe_combine(x, ys, dest, gates, gate, seq)


def _pad_heads(w, width):
    k = w.shape[0]
    w = w.reshape(k, MLA_HEADS, width)
    return jnp.pad(w, ((0, 0), (0, 0), (0, LANES - width))).reshape(k, MLA_HEADS * LANES)


def _rope_tables(positions):
    half = QK_ROPE_DIM // 2
    inv = 1.0 / (ROPE_THETA ** (jnp.arange(0, QK_ROPE_DIM, 2, dtype=F32) / QK_ROPE_DIM))
    ang = positions.reshape(-1).astype(F32)[:, None] * inv
    cos, sin = jnp.cos(ang), jnp.sin(ang)
    n = ang.shape[0]
    ones = jnp.ones((n, QK_NOPE_DIM), F32)
    zeros = lambda w: jnp.zeros((n, w), F32)
    tail = LANES - QK_HEAD_DIM
    cos_t = jnp.concatenate([ones, cos, cos, zeros(tail)], axis=1)
    sin_lo = jnp.concatenate([zeros(QK_NOPE_DIM), -sin, zeros(half), zeros(tail)], axis=1)
    sin_hi = jnp.concatenate([zeros(QK_NOPE_DIM), zeros(half), sin, zeros(tail)], axis=1)
    return cos_t, sin_lo, sin_hi


def _split_mods(m, batch):
    d = m.shape[-1] // 3
    m = m[:batch]
    return tuple(m[:, None, j * d:(j + 1) * d] for j in range(3))


def pool_mla_layer(x, mods, norm_g, tables, w_in, pool_w, pool_scale, cq_norm_g, w_uq, ckv_norm_g, w_ukv,
                   q_norm_g, k_norm_g, w_out, batch, seq):
    n, d = x.shape
    shift, scale, gate = mods
    h = norm_mod(x, norm_g.reshape(1, d), scale, shift, seq)
    rope_cols = jnp.pad(w_in[:, Z_ROPE:], ((0, 0), (QK_NOPE_DIM, LANES - QK_HEAD_DIM)))
    w_in_pad = jnp.concatenate([w_in[:, :Z_ROPE], rope_cols], axis=1).astype(BF16)
    z = matmul(h, w_in_pad)
    w_ukv_h = w_ukv.reshape(KV_LORA_RANK, MLA_HEADS, QK_NOPE_DIM + V_HEAD_DIM)
    wuk_pad = _pad_heads(w_ukv_h[:, :, :QK_NOPE_DIM].reshape(KV_LORA_RANK, -1), QK_NOPE_DIM).astype(BF16)
    wuv_pad = _pad_heads(w_ukv_h[:, :, QK_NOPE_DIM:].reshape(KV_LORA_RANK, -1), V_HEAD_DIM).astype(BF16)
    wuq_pad = _pad_heads(w_uq, QK_HEAD_DIM).astype(BF16)
    pad_g = lambda g: jnp.pad(g, (0, LANES - QK_HEAD_DIM)).reshape(1, LANES)
    y_pool, q, k, v = mla_prep(z, pool_w.astype(BF16), pool_scale.reshape(1, -1), cq_norm_g.reshape(1, -1), wuq_pad,
                               ckv_norm_g.reshape(1, -1), wuk_pad, wuv_pad, pad_g(q_norm_g), pad_g(k_norm_g),
                               *tables, batch, seq)
    y_att = flash_attention(q, k, v).reshape(n, MLA_HEADS * V_HEAD_DIM)
    w_out_b = w_out.astype(BF16)
    return proj_residual([y_pool, y_att], [w_out_b[:POOL_WIDTH], w_out_b[POOL_WIDTH:]],
                         jnp.zeros((1, d), F32), x, gate, seq)


def conformer_layer(x, mods, norm_g, pw1_w, pw1_b, dw_w, dw_b, ln_g, ln_b, pw2_w, pw2_b, seq):
    n, d = x.shape
    shift, scale, gate = mods
    h = norm_mod(x, norm_g.reshape(1, d), scale, shift, seq)
    cd = pw1_w.shape[1] // 2
    pw1 = pw1_w.astype(BF16)
    u = glu_matmul(h, pw1[:, :cd], pw1[:, cd:], pw1_b[:cd].reshape(1, cd), pw1_b[cd:].reshape(1, cd))
    w_pad = jnp.pad(dw_w, ((0, CONV_HALO - CONV_WIDTH), (0, 0)))
    u = dwconv_ln_silu(u, w_pad, dw_b.reshape(1, cd), ln_g.reshape(1, cd), ln_b.reshape(1, cd), seq)
    return proj_residual([u], [pw2_w.astype(BF16)], pw2_b.reshape(1, d), x, gate, seq)


def kernel(x, c, positions, ada_mix_w, ada_mix_b, norm_mix_g, w_in, pool_w, pool_scale, cq_norm_g, w_uq,
           ckv_norm_g, w_ukv, q_norm_g, k_norm_g, w_out, conv_pw1_w, conv_pw1_b, conv_dw_w, conv_dw_b,
           conv_ln_g, conv_ln_b, conv_pw2_w, conv_pw2_b, ada_ffn_w, ada_ffn_b, norm_ffn_g, router_w,
           router_b, moe_w1, moe_b1, moe_w2, moe_b2):
    batch, seq, d = x.shape
    depth = ada_mix_w.shape[0]
    c_pad = jnp.pad(c, ((0, 8 - batch), (0, 0)))
    mix_mods = adaln(c_pad, ada_mix_w, ada_mix_b)
    ffn_mods = adaln(c_pad, ada_ffn_w, ada_ffn_b)
    tables = _rope_tables(positions)
    xf = x.reshape(batch * seq, d)
    for layer in range(depth):
        i = layer // 2
        mods = _split_mods(mix_mods[layer], batch)
        if layer % 2 == 0:
            xf = pool_mla_layer(xf, mods, norm_mix_g[layer], tables, w_in[i], pool_w[i], pool_scale[i],
                                cq_norm_g[i], w_uq[i], ckv_norm_g[i], w_ukv[i], q_norm_g[i], k_norm_g[i],
                                w_out[i], batch, seq)
        else:
            xf = conformer_layer(xf, mods, norm_mix_g[layer], conv_pw1_w[i], conv_pw1_b[i], conv_dw_w[i],
                                 conv_dw_b[i], conv_ln_g[i], conv_ln_b[i], conv_pw2_w[i], conv_pw2_b[i], seq)
        xf = moe_layer(xf, _split_mods(ffn_mods[layer], batch), norm_ffn_g[layer], router_w[layer],
                       router_b[layer], layer, moe_w1, moe_b1, moe_w2, moe_b2, seq)
    return xf.reshape(batch, seq, d)
```

```python
import functools

import jax
import jax.numpy as jnp
from jax import lax
from jax.experimental import pallas as pl
from jax.experimental.pallas import tpu as pltpu
from jax.experimental.pallas import tpu_sc as plsc

F32 = jnp.float32
BF16 = jnp.bfloat16
HIGHEST = lax.Precision.HIGHEST

EPS = 1e-6
POOL_WINDOWS = (2, 4, 8, 16)
POOL_GROUP_DIM = 128
POOL_WIDTH = POOL_GROUP_DIM * len(POOL_WINDOWS)
MLA_HEADS = 8
QK_NOPE_DIM = 64
QK_ROPE_DIM = 32
QK_HEAD_DIM = QK_NOPE_DIM + QK_ROPE_DIM
V_HEAD_DIM = 64
Q_LORA_RANK = 384
KV_LORA_RANK = 256
ROPE_THETA = 10000.0
CHUNK = 64
CONV_WIDTH = 31
N_EXPERTS = 32
TOP_K = 4
SWIGLU_ALPHA = 1.702
SWIGLU_LIMIT = 7.0
MOE_BLOCK = 256

LANES = 128
POOL_HALO = 16
CONV_HALO = 32
MASK_VALUE = -1e30
VMEM_LIMIT = 52 * 1024 * 1024
SC_CORES = 2
SC_SUBCORES = 16
SC_WORKERS = SC_CORES * SC_SUBCORES
SC_CHUNK = 64

Z_CQ = POOL_WIDTH
Z_CKV = Z_CQ + Q_LORA_RANK
Z_ROPE = Z_CKV + KV_LORA_RANK
Z_WIDTH = Z_ROPE + LANES


def _params(*sem, vmem=None):
    return pltpu.CompilerParams(dimension_semantics=sem, vmem_limit_bytes=vmem or VMEM_LIMIT,
                                disable_bounds_checks=True)


def _adaln_kernel(c_ref, w_ref, b_ref, o_ref):
    c = c_ref[...]
    s = c * jax.nn.sigmoid(c)
    o_ref[0] = jnp.dot(s, w_ref[0], preferred_element_type=F32, precision=HIGHEST) + b_ref[0]


def adaln(c_pad, w, b):
    n_l, d, d3 = w.shape
    tn = 512
    return pl.pallas_call(
        _adaln_kernel,
        grid=(n_l, d3 // tn),
        in_specs=[pl.BlockSpec((8, d), lambda l, j: (0, 0)),
                  pl.BlockSpec((1, d, tn), lambda l, j: (l, 0, j)),
                  pl.BlockSpec((1, 1, tn), lambda l, j: (l, 0, j))],
        out_specs=pl.BlockSpec((1, 8, tn), lambda l, j: (l, 0, j)),
        out_shape=jax.ShapeDtypeStruct((n_l, 8, d3), F32),
        compiler_params=_params("parallel", "parallel"),
        name="adaln",
    )(c_pad, w, b.reshape(n_l, 1, d3))


def _modulated_norm(x, g, scale, shift):
    ms = jnp.mean(x * x, axis=-1, keepdims=True)
    return x * lax.rsqrt(ms + EPS) * g * (1.0 + scale) + shift


def _norm_mod_kernel(x_ref, g_ref, sc_ref, sh_ref, o_ref):
    o_ref[...] = _modulated_norm(x_ref[...], g_ref[...], sc_ref[0], sh_ref[0]).astype(o_ref.dtype)


def _pack_halves(x):
    w = x.shape[1] // 2
    lo = lax.bitcast_convert_type(x[:, :w].astype(BF16).astype(F32), jnp.uint32)
    hi = lax.bitcast_convert_type(x[:, w:].astype(BF16).astype(F32), jnp.uint32)
    return lax.bitcast_convert_type((lo >> 16) | (hi & jnp.uint32(0xFFFF0000)), jnp.int32)


def _unpack_halves(p):
    u = lax.bitcast_convert_type(p, jnp.uint32)
    lo = lax.bitcast_convert_type(u << 16, F32)
    hi = lax.bitcast_convert_type(u & jnp.uint32(0xFFFF0000), F32)
    return lo, hi


def _norm_mod_router_kernel(x_ref, g_ref, sc_ref, sh_ref, rw_ref, rb_ref, o_ref, lg_ref):
    h = _modulated_norm(x_ref[...], g_ref[...], sc_ref[0], sh_ref[0])
    o_ref[...] = _pack_halves(h)
    lg_ref[...] = jnp.dot(h, rw_ref[...], preferred_element_type=F32, precision=HIGHEST) + rb_ref[...]


def norm_mod(x, g, scale, shift, seq, tm=512):
    n, d = x.shape
    per = seq // tm
    vec = pl.BlockSpec((1, 1, d), lambda i: (i // per, 0, 0))
    return pl.pallas_call(
        _norm_mod_kernel,
        grid=(n // tm,),
        in_specs=[pl.BlockSpec((tm, d), lambda i: (i, 0)), pl.BlockSpec((1, d), lambda i: (0, 0)), vec, vec],
        out_specs=pl.BlockSpec((tm, d), lambda i: (i, 0)),
        out_shape=jax.ShapeDtypeStruct((n, d), BF16),
        compiler_params=_params("parallel"),
        name="norm_mod",
    )(x, g, scale, shift)


def norm_mod_router(x, g, scale, shift, rw_pad, rb_pad, seq, tm=512):
    n, d = x.shape
    per = seq // tm
    vec = pl.BlockSpec((1, 1, d), lambda i: (i // per, 0, 0))
    return pl.pallas_call(
        _norm_mod_router_kernel,
        grid=(n // tm,),
        in_specs=[pl.BlockSpec((tm, d), lambda i: (i, 0)), pl.BlockSpec((1, d), lambda i: (0, 0)), vec, vec,
                  pl.BlockSpec((d, LANES), lambda i: (0, 0)), pl.BlockSpec((1, LANES), lambda i: (0, 0))],
        out_specs=[pl.BlockSpec((tm, d // 2), lambda i: (i, 0)), pl.BlockSpec((tm, LANES), lambda i: (i, 0))],
        out_shape=[jax.ShapeDtypeStruct((n, d // 2), jnp.int32), jax.ShapeDtypeStruct((n, LANES), F32)],
        compiler_params=_params("parallel"),
        name="norm_mod_router",
    )(x, g, scale, shift, rw_pad, rb_pad)


def _matmul_kernel(a_ref, w_ref, o_ref):
    o_ref[...] = jnp.dot(a_ref[...], w_ref[...], preferred_element_type=F32).astype(o_ref.dtype)


def matmul(a, w, out_dtype=F32, tm=512):
    n, k = a.shape
    m = w.shape[1]
    return pl.pallas_call(
        _matmul_kernel,
        grid=(n // tm,),
        in_specs=[pl.BlockSpec((tm, k), lambda i: (i, 0)), pl.BlockSpec((k, m), lambda i: (0, 0))],
        out_specs=pl.BlockSpec((tm, m), lambda i: (i, 0)),
        out_shape=jax.ShapeDtypeStruct((n, m), out_dtype),
        compiler_params=_params("parallel"),
        name="matmul",
    )(a, w)


def _glu_kernel(a_ref, wv_ref, wg_ref, bv_ref, bg_ref, o_ref):
    a = a_ref[...]
    val = jnp.dot(a, wv_ref[...], preferred_element_type=F32) + bv_ref[...]
    gt = jnp.dot(a, wg_ref[...], preferred_element_type=F32) + bg_ref[...]
    o_ref[...] = val * jax.nn.sigmoid(gt)


def glu_matmul(a, wv, wg, bv, bg, tm=512):
    n, k = a.shape
    m = wv.shape[1]
    full = lambda r, c: pl.BlockSpec((r, c), lambda i: (0, 0))
    return pl.pallas_call(
        _glu_kernel,
        grid=(n // tm,),
        in_specs=[pl.BlockSpec((tm, k), lambda i: (i, 0)), full(k, m), full(k, m), full(1, m), full(1, m)],
        out_specs=pl.BlockSpec((tm, m), lambda i: (i, 0)),
        out_shape=jax.ShapeDtypeStruct((n, m), F32),
        compiler_params=_params("parallel"),
        name="glu_matmul",
    )(a, wv, wg, bv, bg)


def _proj_residual_kernel(*refs, n_in):
    a_refs, w_refs = refs[:n_in], refs[n_in:2 * n_in]
    b_ref, x_ref, gate_ref, o_ref = refs[2 * n_in:]
    acc = b_ref[...]
    for a_ref, w_ref in zip(a_refs, w_refs):
        acc = acc + jnp.dot(a_ref[...], w_ref[...], preferred_element_type=F32)
    o_ref[...] = x_ref[...] + gate_ref[0] * acc


def proj_residual(a_list, w_list, bias, x, gate, seq, tm=512):
    n, d = x.shape
    per = seq // tm
    n_in = len(a_list)
    in_specs = [pl.BlockSpec((tm, a.shape[1]), lambda i: (i, 0)) for a in a_list]
    in_specs += [pl.BlockSpec(w.shape, lambda i: (0, 0)) for w in w_list]
    in_specs += [pl.BlockSpec((1, d), lambda i: (0, 0)), pl.BlockSpec((tm, d), lambda i: (i, 0)),
                 pl.BlockSpec((1, 1, d), lambda i: (i // per, 0, 0))]
    return pl.pallas_call(
        functools.partial(_proj_residual_kernel, n_in=n_in),
        grid=(n // tm,),
        in_specs=in_specs,
        out_specs=pl.BlockSpec((tm, d), lambda i: (i, 0)),
        out_shape=jax.ShapeDtypeStruct((n, d), F32),
        compiler_params=_params("parallel"),
        name="proj_residual",
    )(*a_list, *w_list, bias, x, gate)


def _rope(xn, cos_t, sin_lo, sin_hi):
    return (xn * cos_t + pltpu.roll(xn, LANES - QK_ROPE_DIM // 2, axis=1) * sin_lo
            + pltpu.roll(xn, QK_ROPE_DIM // 2, axis=1) * sin_hi)


def _mla_prep_kernel(z_ref, halo_ref, pw_ref, ps_ref, cqg_ref, wuq_ref, ckvg_ref, wuk_ref, wuv_ref,
                     qg_ref, kg_ref, cos_ref, slo_ref, shi_ref,
                     yp_ref, q_ref, k_ref, v_ref, ext_ref, *, tm, per):
    si = pl.program_id(0) % per
    u = z_ref[:, 0:POOL_WIDTH]
    ext_ref[0:POOL_HALO, :] = jnp.where(si == 0, 0.0, halo_ref[...])
    ext_ref[POOL_HALO:, :] = u
    t = si * tm + lax.broadcasted_iota(jnp.int32, (tm, 1), 0)
    for g, w in enumerate(POOL_WINDOWS):
        cols = slice(g * POOL_GROUP_DIM, (g + 1) * POOL_GROUP_DIM)
        ug = u[:, cols]
        s = ug
        for j in range(1, w):
            s = s + ext_ref[POOL_HALO - j:POOL_HALO - j + tm, cols]
        cnt = jnp.minimum(t + 1, w).astype(F32)
        pooled = s / cnt - ug
        yp = jnp.dot(pooled.astype(BF16), pw_ref[g], preferred_element_type=F32) * ps_ref[:, cols]
        yp_ref[:, cols] = yp.astype(yp_ref.dtype)

    cos_t, sin_lo, sin_hi = cos_ref[...], slo_ref[...], shi_ref[...]
    inv_head = 1.0 / QK_HEAD_DIM

    cq = z_ref[:, Z_CQ:Z_CKV]
    cqn = cq * lax.rsqrt(jnp.mean(cq * cq, axis=-1, keepdims=True) + EPS) * cqg_ref[...]
    qf = jnp.dot(cqn.astype(BF16), wuq_ref[...], preferred_element_type=F32)
    q_scale = QK_HEAD_DIM ** -0.5
    for h in range(MLA_HEADS):
        qh = qf[:, h * LANES:(h + 1) * LANES]
        ss = jnp.sum(qh * qh, axis=-1, keepdims=True) * inv_head
        qn = qh * lax.rsqrt(ss + EPS) * qg_ref[...]
        q_ref[0, h] = (_rope(qn, cos_t, sin_lo, sin_hi) * q_scale).astype(q_ref.dtype)

    ckv = z_ref[:, Z_CKV:Z_ROPE]
    ckvn = (ckv * lax.rsqrt(jnp.mean(ckv * ckv, axis=-1, keepdims=True) + EPS) * ckvg_ref[...]).astype(BF16)
    kf = jnp.dot(ckvn, wuk_ref[...], preferred_element_type=F32)
    vf = jnp.dot(ckvn, wuv_ref[...], preferred_element_type=F32)
    k_rope = z_ref[:, Z_ROPE:Z_WIDTH]
    for h in range(MLA_HEADS):
        kh = kf[:, h * LANES:(h + 1) * LANES] + k_rope
        ss = jnp.sum(kh * kh, axis=-1, keepdims=True) * inv_head
        kn = kh * lax.rsqrt(ss + EPS) * kg_ref[...]
        k_ref[0, h] = _rope(kn, cos_t, sin_lo, sin_hi).astype(k_ref.dtype)
        v_ref[0, h] = vf[:, h * LANES:(h + 1) * LANES].astype(v_ref.dtype)


def mla_prep(z, pool_w, pool_scale, cq_g, wuq_pad, ckv_g, wuk_pad, wuv_pad, qg_pad, kg_pad,
             cos_t, sin_lo, sin_hi, batch, seq, tm=256):
    n = z.shape[0]
    per = seq // tm
    hb = tm // POOL_HALO
    full = lambda a: pl.BlockSpec(a.shape, lambda i: (0,) * a.ndim)
    tab = pl.BlockSpec((tm, LANES), lambda i: (i, 0))
    head_out = pl.BlockSpec((1, MLA_HEADS, tm, LANES), lambda i: (i // per, 0, i % per, 0))
    head_shape = jax.ShapeDtypeStruct((batch, MLA_HEADS, seq, LANES), BF16)
    return pl.pallas_call(
        functools.partial(_mla_prep_kernel, tm=tm, per=per),
        grid=(n // tm,),
        in_specs=[pl.BlockSpec((tm, Z_WIDTH), lambda i: (i, 0)),
                  pl.BlockSpec((POOL_HALO, POOL_WIDTH), lambda i: (jnp.maximum(i * hb - 1, 0), 0)),
                  full(pool_w), full(pool_scale), full(cq_g), full(wuq_pad), full(ckv_g), full(wuk_pad),
                  full(wuv_pad), full(qg_pad), full(kg_pad), tab, tab, tab],
        out_specs=[pl.BlockSpec((tm, POOL_WIDTH), lambda i: (i, 0)), head_out, head_out, head_out],
        out_shape=[jax.ShapeDtypeStruct((n, POOL_WIDTH), BF16), head_shape, head_shape, head_shape],
        scratch_shapes=[pltpu.VMEM((tm + POOL_HALO, POOL_WIDTH), F32)],
        compiler_params=_params("parallel"),
        name="mla_prep",
    )(z, z, pool_w, pool_scale, cq_g, wuq_pad, ckv_g, wuk_pad, wuv_pad, qg_pad, kg_pad, cos_t, sin_lo, sin_hi)


def _flash_kernel(q_ref, k_ref, v_ref, o_ref, m_ref, l_ref, acc_ref, *, tq, heads):
    qi = pl.program_id(2)
    rows = lax.broadcasted_iota(jnp.int32, (tq, tq), 0) // CHUNK
    cols = lax.broadcasted_iota(jnp.int32, (tq, tq), 1) // CHUNK
    diag_mask = cols <= rows
    slabs = tq // LANES
    m_ref[...] = jnp.full(m_ref.shape, MASK_VALUE, F32)
    l_ref[...] = jnp.zeros(l_ref.shape, F32)
    acc_ref[...] = jnp.zeros(acc_ref.shape, F32)

    def step(j, masked):
        start = pl.multiple_of(j * tq, tq)
        for hh in range(heads):
            k = k_ref[0, hh, pl.ds(start, tq), :]
            v = v_ref[0, hh, pl.ds(start, tq), :]
            s = lax.dot_general(q_ref[0, hh], k, (((1,), (1,)), ((), ())), preferred_element_type=F32)
            if masked:
                s = jnp.where(diag_mask, s, MASK_VALUE)
            cols_s = [s[:, c * LANES:(c + 1) * LANES] for c in range(slabs)]
            s_max = cols_s[0]
            for sc in cols_s[1:]:
                s_max = jnp.maximum(s_max, sc)
            m_prev = m_ref[hh]
            m_new = jnp.maximum(m_prev, jnp.max(s_max, axis=-1, keepdims=True))
            alpha = jnp.exp(m_prev - m_new)
            ps = [jnp.exp(sc - m_new) for sc in cols_s]
            l_part = ps[0]
            for pc in ps[1:]:
                l_part = l_part + pc
            p = jnp.concatenate(ps, axis=1).astype(v.dtype)
            l_ref[hh] = alpha * l_ref[hh] + l_part
            acc_ref[hh] = alpha * acc_ref[hh] + jnp.dot(p, v, preferred_element_type=F32)
            m_ref[hh] = m_new

    def full_step(j, carry):
        step(j, False)
        return carry

    lax.fori_loop(0, qi, full_step, 0)
    step(qi, True)
    outs = []
    for hh in range(heads):
        l_row = jnp.sum(l_ref[hh], axis=-1, keepdims=True)
        outs.append((acc_ref[hh] / l_row)[:, :V_HEAD_DIM])
    o_ref[0] = jnp.concatenate(outs, axis=-1).astype(o_ref.dtype)


def flash_attention(q, k, v, tq=512):
    b, h, s, _ = q.shape
    heads = LANES // V_HEAD_DIM
    return pl.pallas_call(
        functools.partial(_flash_kernel, tq=tq, heads=heads),
        grid=(b, h // heads, s // tq),
        in_specs=[pl.BlockSpec((1, heads, tq, LANES), lambda bi, hi, qi: (bi, hi, qi, 0)),
                  pl.BlockSpec((1, heads, s, LANES), lambda bi, hi, qi: (bi, hi, 0, 0)),
                  pl.BlockSpec((1, heads, s, LANES), lambda bi, hi, qi: (bi, hi, 0, 0))],
        out_specs=pl.BlockSpec((1, tq, LANES), lambda bi, hi, qi: (bi, qi, hi)),
        out_shape=jax.ShapeDtypeStruct((b, s, h * V_HEAD_DIM), BF16),
        scratch_shapes=[pltpu.VMEM((heads, tq, LANES), F32) for _ in range(3)],
        compiler_params=_params("parallel", "parallel", "parallel"),
        name="flash_attention",
    )(q, k, v)


def _dwconv_kernel(u_ref, halo_ref, w_ref, b_ref, g_ref, beta_ref, o_ref, ext_ref, *, tm, per):
    si = pl.program_id(0) % per
    ext_ref[0:CONV_HALO, :] = jnp.where(si == 0, 0.0, halo_ref[...])
    ext_ref[CONV_HALO:, :] = u_ref[...]
    first = CONV_HALO - (CONV_WIDTH - 1)
    acc = jnp.broadcast_to(b_ref[...], (tm, b_ref.shape[1]))
    for j in range(CONV_WIDTH):
        acc = acc + w_ref[j:j + 1, :] * ext_ref[first + j:first + j + tm, :]
    mu = jnp.mean(acc, axis=-1, keepdims=True)
    cen = acc - mu
    var = jnp.mean(cen * cen, axis=-1, keepdims=True)
    y = cen * lax.rsqrt(var + EPS) * g_ref[...] + beta_ref[...]
    o_ref[...] = (y * jax.nn.sigmoid(y)).astype(o_ref.dtype)


def dwconv_ln_silu(u, w_pad, b, g, beta, seq, tm=256):
    n, d = u.shape
    per = seq // tm
    hb = tm // CONV_HALO
    full = lambda a: pl.BlockSpec(a.shape, lambda i: (0, 0))
    return pl.pallas_call(
        functools.partial(_dwconv_kernel, tm=tm, per=per),
        grid=(n // tm,),
        in_specs=[pl.BlockSpec((tm, d), lambda i: (i, 0)),
                  pl.BlockSpec((CONV_HALO, d), lambda i: (jnp.maximum(i * hb - 1, 0), 0)),
                  full(w_pad), full(b), full(g), full(beta)],
        out_specs=pl.BlockSpec((tm, d), lambda i: (i, 0)),
        out_shape=jax.ShapeDtypeStruct((n, d), BF16),
        scratch_shapes=[pltpu.VMEM((tm + CONV_HALO, d), F32)],
        compiler_params=_params("parallel"),
        name="dwconv_ln_silu",
    )(u, u, w_pad, b, g, beta)


def _router_kernel(lg_ref, idx_ref, gate_ref, cnt_ref, carry_ref, *, tm):
    @pl.when(pl.program_id(0) == 0)
    def _():
        carry_ref[...] = jnp.zeros(carry_ref.shape, F32)

    lane = lax.broadcasted_iota(jnp.int32, (tm, LANES), 1)
    lane_f = lane.astype(F32)
    neg = -jnp.inf
    logits = jnp.where(lane < N_EXPERTS, lg_ref[...], neg)
    picks, vals, ids = [], [], []
    for _ in range(TOP_K):
        mx = jnp.max(logits, axis=-1, keepdims=True)
        idx = jnp.min(jnp.where(logits == mx, lane_f, float(LANES)), axis=-1, keepdims=True)
        pick = lane_f == idx
        picks.append(pick)
        vals.append(mx)
        ids.append(idx)
        logits = jnp.where(pick, neg, logits)
    exps = [jnp.exp(v - vals[0]) for v in vals]
    den = exps[0]
    for e in exps[1:]:
        den = den + e

    chosen = jnp.zeros((tm, LANES), F32)
    for pick in picks:
        chosen = chosen + pick.astype(F32)
    r_io = lax.broadcasted_iota(jnp.int32, (tm, tm), 0)
    c_io = lax.broadcasted_iota(jnp.int32, (tm, tm), 1)
    earlier = (c_io < r_io).astype(BF16)
    before = jnp.dot(earlier, chosen.astype(BF16), preferred_element_type=F32) + carry_ref[0:1, :]
    idx_out = jnp.zeros((tm, LANES), F32)
    gate_out = jnp.zeros((tm, LANES), F32)
    for k in range(TOP_K):
        rank = jnp.sum(jnp.where(picks[k], before, 0.0), axis=-1, keepdims=True)
        idx_out = jnp.where(lane == k, ids[k], idx_out)
        idx_out = jnp.where(lane == TOP_K + k, rank, idx_out)
        gate_out = jnp.where(lane == k, exps[k] / den, gate_out)
    idx_ref[...] = idx_out.astype(jnp.int32)
    gate_ref[...] = gate_out
    total = carry_ref[0:1, :] + jnp.sum(chosen, axis=0, keepdims=True)
    carry_ref[...] = jnp.broadcast_to(total, carry_ref.shape)
    cnt_ref[...] = jnp.broadcast_to(total, cnt_ref.shape).astype(jnp.int32)


def router(logits, tm=512):
    n = logits.shape[0]
    row = pl.BlockSpec((tm, LANES), lambda i: (i, 0))
    return pl.pallas_call(
        functools.partial(_router_kernel, tm=tm),
        grid=(n // tm,),
        in_specs=[row],
        out_specs=[row, row, pl.BlockSpec((8, LANES), lambda i: (0, 0))],
        out_shape=[jax.ShapeDtypeStruct((n, LANES), jnp.int32), jax.ShapeDtypeStruct((n, LANES), F32),
                   jax.ShapeDtypeStruct((8, LANES), jnp.int32)],
        scratch_shapes=[pltpu.VMEM((8, LANES), F32)],
        compiler_params=_params("arbitrary"),
        name="router",
    )(logits)


def _sc_worker_base(per_worker):
    return (lax.axis_index("s") * SC_CORES + lax.axis_index("c")) * per_worker


def sc_scatter_rows(src, dest_flat, n_rows):
    n, d = src.shape
    per_w = n // SC_WORKERS
    n_chunks = per_w // SC_CHUNK
    mesh = plsc.VectorSubcoreMesh(core_axis_name="c", subcore_axis_name="s")

    @functools.partial(
        pl.kernel, out_type=jax.ShapeDtypeStruct((n_rows, d), src.dtype), mesh=mesh,
        scratch_types=[pltpu.VMEM((per_w,), jnp.int32) for _ in range(TOP_K)]
        + [pltpu.VMEM((SC_CHUNK, d), src.dtype), pltpu.SemaphoreType.DMA],
        name="sc_scatter_rows")
    def scatter(src_hbm, dest_hbm, out_hbm, *scratch):
        idx_refs, rows_ref, sem = scratch[:TOP_K], scratch[TOP_K], scratch[TOP_K + 1]
        base = _sc_worker_base(per_w)
        for k, idx_ref in enumerate(idx_refs):
            pltpu.sync_copy(dest_hbm.at[pl.ds(k * n + base, per_w)], idx_ref)

        @pl.loop(0, n_chunks)
        def _(j):
            off = j * SC_CHUNK
            pltpu.sync_copy(src_hbm.at[pl.ds(base + off, SC_CHUNK)], rows_ref)
            for idx_ref in idx_refs:
                pltpu.async_copy(rows_ref, out_hbm.at[idx_ref.at[pl.ds(off, SC_CHUNK)]], sem).wait()

    return scatter(src, dest_flat)


def sc_gather_rows(table, idx):
    b = idx.shape[0]
    d = table.shape[1]
    per_w = b // SC_WORKERS
    n_chunks = per_w // SC_CHUNK
    mesh = plsc.VectorSubcoreMesh(core_axis_name="c", subcore_axis_name="s")

    @functools.partial(
        pl.kernel, out_type=jax.ShapeDtypeStruct((b, d), table.dtype), mesh=mesh,
        scratch_types=[pltpu.VMEM((per_w,), jnp.int32), pltpu.VMEM((SC_CHUNK, d), table.dtype),
                       pltpu.SemaphoreType.DMA],
        name="sc_gather_rows")
    def gather(table_hbm, idx_hbm, out_hbm, idx_ref, rows_ref, sem):
        base = _sc_worker_base(per_w)
        pltpu.sync_copy(idx_hbm.at[pl.ds(base, per_w)], idx_ref)

        @pl.loop(0, n_chunks)
        def _(j):
            off = j * SC_CHUNK
            pltpu.async_copy(table_hbm.at[idx_ref.at[pl.ds(off, SC_CHUNK)]], rows_ref, sem).wait()
            pltpu.sync_copy(rows_ref, out_hbm.at[pl.ds(base + off, SC_CHUNK)])

    return gather(table, idx)


def _ffn_kernel(be_ref, nv_ref, nu_ref, xs_ref, w1_ref, b1g_ref, b1u_ref, w2_ref, b2_ref, y_ref,
                wg_ref, wu_ref, w2s_ref):
    i = pl.program_id(0)
    pair = 2 * LANES

    @pl.when(i < nu_ref[0])
    def _():
        e = be_ref[i]
        prev = be_ref[jnp.maximum(i - 1, 0)]

        @pl.when((i == 0) | (e != prev))
        def _():
            r_io = lax.broadcasted_iota(jnp.int32, (pair, pair), 0)
            c_io = lax.broadcasted_iota(jnp.int32, (pair, pair), 1)
            want = jnp.where(c_io < LANES, 2 * c_io, 2 * (c_io - LANES) + 1)
            sel = (r_io == want).astype(BF16)
            for c in range(wg_ref.shape[1] // LANES):
                slab = w1_ref[0, 0, :, c * pair:(c + 1) * pair].astype(BF16)
                split = jnp.dot(slab, sel, preferred_element_type=F32)
                wg_ref[:, c * LANES:(c + 1) * LANES] = split[:, :LANES].astype(BF16)
                wu_ref[:, c * LANES:(c + 1) * LANES] = split[:, LANES:].astype(BF16)
            w2s_ref[...] = w2_ref[0, 0].astype(BF16)

        row = lax.broadcasted_iota(jnp.int32, xs_ref.shape, 0)
        lo, hi = _unpack_halves(jnp.where(row < nv_ref[i], xs_ref[...], 0))
        x = jnp.concatenate([lo, hi], axis=1).astype(BF16)
        hg = jnp.dot(x, wg_ref[...], preferred_element_type=F32) + b1g_ref[0]
        hu = jnp.dot(x, wu_ref[...], preferred_element_type=F32) + b1u_ref[0]
        gate = jnp.minimum(hg, SWIGLU_LIMIT)
        up = jnp.clip(hu, -SWIGLU_LIMIT, SWIGLU_LIMIT)
        act = gate * jax.nn.sigmoid(SWIGLU_ALPHA * gate) * (up + 1.0)
        y = jnp.dot(act.astype(BF16), w2s_ref[...], preferred_element_type=F32) + b2_ref[0]
        y_ref[...] = _pack_halves(y)

    @pl.when(i >= nu_ref[0])
    def _():
        y_ref[...] = jnp.zeros(y_ref.shape, y_ref.dtype)


def moe_ffn(xs, block_e, block_valid, n_used, layer, w1, b1g, b1u, w2, b2):
    n_rows, half = xs.shape
    d = 2 * half
    n_blocks = n_rows // MOE_BLOCK
    n_exp, f2 = w1.shape[1], w1.shape[3]
    f = f2 // 2
    rows = lambda i, be, nv, nu: (jnp.minimum(i, nu[0] - 1), 0)
    vec = lambda width: pl.BlockSpec((1, 1, width), lambda i, be, nv, nu: (be[i], 0, 0))
    grid_spec = pltpu.PrefetchScalarGridSpec(
        num_scalar_prefetch=3,
        grid=(n_blocks,),
        in_specs=[
            pl.BlockSpec((MOE_BLOCK, half), rows),
            pl.BlockSpec((1, 1, d, f2), lambda i, be, nv, nu: (layer, be[i], 0, 0)),
            vec(f), vec(f),
            pl.BlockSpec((1, 1, f, d), lambda i, be, nv, nu: (layer, be[i], 0, 0)),
            vec(d),
        ],
        out_specs=pl.BlockSpec((MOE_BLOCK, half), lambda i, be, nv, nu: (i, 0)),
        scratch_shapes=[pltpu.VMEM((d, f), BF16), pltpu.VMEM((d, f), BF16), pltpu.VMEM((f, d), BF16)],
    )
    return pl.pallas_call(
        _ffn_kernel,
        grid_spec=grid_spec,
        out_shape=jax.ShapeDtypeStruct((n_rows, half), jnp.int32),
        compiler_params=_params("arbitrary"),
        name="moe_ffn",
    )(block_e, block_valid, n_used, xs, w1, b1g.reshape(n_exp, 1, f), b1u.reshape(n_exp, 1, f), w2,
      b2.reshape(n_exp, 1, d))


def _combine_kernel(x_ref, y_ref, gates_ref, gate_ref, o_ref):
    g = gates_ref[...]
    half = y_ref.shape[2]
    acc_lo = acc_hi = None
    for k in range(TOP_K):
        lo, hi = _unpack_halves(y_ref[k])
        gk = g[:, k:k + 1]
        acc_lo = gk * lo if acc_lo is None else acc_lo + gk * lo
        acc_hi = gk * hi if acc_hi is None else acc_hi + gk * hi
    o_ref[:, :half] = x_ref[:, :half] + gate_ref[0, :, :half] * acc_lo
    o_ref[:, half:] = x_ref[:, half:] + gate_ref[0, :, half:] * acc_hi


def moe_combine(x, y, gates, gate_mod, seq, tm=512):
    n, d = x.shape
    per = seq // tm
    return pl.pallas_call(
        _combine_kernel,
        grid=(n // tm,),
        in_specs=[pl.BlockSpec((tm, d), lambda i: (i, 0)),
                  pl.BlockSpec((TOP_K, tm, d // 2), lambda i: (0, i, 0)),
                  pl.BlockSpec((tm, LANES), lambda i: (i, 0)),
                  pl.BlockSpec((1, 1, d), lambda i: (i // per, 0, 0))],
        out_specs=pl.BlockSpec((tm, d), lambda i: (i, 0)),
        out_shape=jax.ShapeDtypeStruct((n, d), F32),
        compiler_params=_params("parallel"),
        name="moe_combine",
    )(x, y, gates, gate_mod)


def moe_layer(x, mods, norm_g, router_w, router_b, layer, w1, b1, w2, b2, seq):
    n, d = x.shape
    shift, scale, gate = mods
    rw_pad = jnp.pad(router_w, ((0, 0), (0, LANES - N_EXPERTS)))
    rb_pad = jnp.pad(router_b, (0, LANES - N_EXPERTS)).reshape(1, LANES)
    h, logits = norm_mod_router(x, norm_g.reshape(1, d), scale, shift, rw_pad, rb_pad, seq)
    idx, gates, counts = router(logits)

    top_i = idx[:, :TOP_K]
    rank = idx[:, TOP_K:2 * TOP_K]
    counts = counts[0, :N_EXPERTS]
    experts = jnp.arange(N_EXPERTS, dtype=jnp.int32)
    padded = (counts + MOE_BLOCK - 1) // MOE_BLOCK * MOE_BLOCK
    pad_ends = jnp.sum(jnp.where(experts[:, None] >= experts[None, :], padded[None, :], 0), axis=1)
    pad_starts = pad_ends - padded
    dest = jnp.sum(jnp.where(top_i[..., None] == experts, pad_starts, 0), axis=-1) + rank
    n_blocks = -(-n * TOP_K // MOE_BLOCK) + N_EXPERTS
    block_start = jnp.arange(n_blocks, dtype=jnp.int32) * MOE_BLOCK
    block_e = jnp.minimum(jnp.sum((pad_ends[None, :] <= block_start[:, None]).astype(jnp.int32), axis=1),
                          N_EXPERTS - 1)
    n_used = (pad_ends[N_EXPERTS - 1:] // MOE_BLOCK).astype(jnp.int32)
    seg_end = jnp.sum(jnp.where(block_e[:, None] == experts, pad_starts + counts, 0), axis=1)
    block_valid = jnp.clip(seg_end - block_start, 0, MOE_BLOCK).astype(jnp.int32)
    dest_flat = dest.T.reshape(-1)

    xs = sc_scatter_rows(h, dest_flat, n_blocks * MOE_BLOCK)
    ys = moe_ffn(xs, block_e, block_valid, n_used, layer, w1, b1[layer][:, 0::2], b1[layer][:, 1::2], w2, b2[layer])
    y = sc_gather_rows(ys, dest_flat).reshape(TOP_K, n, d // 2)
    return moe_combine(x, y, gates, gate, seq)


def _pad_heads(w, width):
    k = w.shape[0]
    w = w.reshape(k, MLA_HEADS, width)
    return jnp.pad(w, ((0, 0), (0, 0), (0, LANES - width))).reshape(k, MLA_HEADS * LANES)


def _rope_tables(positions):
    half = QK_ROPE_DIM // 2
    inv = 1.0 / (ROPE_THETA ** (jnp.arange(0, QK_ROPE_DIM, 2, dtype=F32) / QK_ROPE_DIM))
    ang = positions.reshape(-1).astype(F32)[:, None] * inv
    cos, sin = jnp.cos(ang), jnp.sin(ang)
    n = ang.shape[0]
    ones = jnp.ones((n, QK_NOPE_DIM), F32)
    zeros = lambda w: jnp.zeros((n, w), F32)
    tail = LANES - QK_HEAD_DIM
    cos_t = jnp.concatenate([ones, cos, cos, zeros(tail)], axis=1)
    sin_lo = jnp.concatenate([zeros(QK_NOPE_DIM), -sin, zeros(half), zeros(tail)], axis=1)
    sin_hi = jnp.concatenate([zeros(QK_NOPE_DIM), zeros(half), sin, zeros(tail)], axis=1)
    return cos_t, sin_lo, sin_hi


def _split_mods(m, batch):
    d = m.shape[-1] // 3
    m = m[:batch]
    return tuple(m[:, None, j * d:(j + 1) * d] for j in range(3))


def pool_mla_layer(x, mods, norm_g, tables, w_in, pool_w, pool_scale, cq_norm_g, w_uq, ckv_norm_g, w_ukv,
                   q_norm_g, k_norm_g, w_out, batch, seq):
    n, d = x.shape
    shift, scale, gate = mods
    h = norm_mod(x, norm_g.reshape(1, d), scale, shift, seq)
    rope_cols = jnp.pad(w_in[:, Z_ROPE:], ((0, 0), (QK_NOPE_DIM, LANES - QK_HEAD_DIM)))
    w_in_pad = jnp.concatenate([w_in[:, :Z_ROPE], rope_cols], axis=1).astype(BF16)
    z = matmul(h, w_in_pad)
    w_ukv_h = w_ukv.reshape(KV_LORA_RANK, MLA_HEADS, QK_NOPE_DIM + V_HEAD_DIM)
    wuk_pad = _pad_heads(w_ukv_h[:, :, :QK_NOPE_DIM].reshape(KV_LORA_RANK, -1), QK_NOPE_DIM).astype(BF16)
    wuv_pad = _pad_heads(w_ukv_h[:, :, QK_NOPE_DIM:].reshape(KV_LORA_RANK, -1), V_HEAD_DIM).astype(BF16)
    wuq_pad = _pad_heads(w_uq, QK_HEAD_DIM).astype(BF16)
    pad_g = lambda g: jnp.pad(g, (0, LANES - QK_HEAD_DIM)).reshape(1, LANES)
    y_pool, q, k, v = mla_prep(z, pool_w.astype(BF16), pool_scale.reshape(1, -1), cq_norm_g.reshape(1, -1), wuq_pad,
                               ckv_norm_g.reshape(1, -1), wuk_pad, wuv_pad, pad_g(q_norm_g), pad_g(k_norm_g),
                               *tables, batch, seq)
    y_att = flash_attention(q, k, v).reshape(n, MLA_HEADS * V_HEAD_DIM)
    w_out_b = w_out.astype(BF16)
    return proj_residual([y_pool, y_att], [w_out_b[:POOL_WIDTH], w_out_b[POOL_WIDTH:]],
                         jnp.zeros((1, d), F32), x, gate, seq)


def conformer_layer(x, mods, norm_g, pw1_w, pw1_b, dw_w, dw_b, ln_g, ln_b, pw2_w, pw2_b, seq):
    n, d = x.shape
    shift, scale, gate = mods
    h = norm_mod(x, norm_g.reshape(1, d), scale, shift, seq)
    cd = pw1_w.shape[1] // 2
    pw1 = pw1_w.astype(BF16)
    u = glu_matmul(h, pw1[:, :cd], pw1[:, cd:], pw1_b[:cd].reshape(1, cd), pw1_b[cd:].reshape(1, cd))
    w_pad = jnp.pad(dw_w, ((0, CONV_HALO - CONV_WIDTH), (0, 0)))
    u = dwconv_ln_silu(u, w_pad, dw_b.reshape(1, cd), ln_g.reshape(1, cd), ln_b.reshape(1, cd), seq)
    return proj_residual([u], [pw2_w.astype(BF16)], pw2_b.reshape(1, d), x, gate, seq)


def kernel(x, c, positions, ada_mix_w, ada_mix_b, norm_mix_g, w_in, pool_w, pool_scale, cq_norm_g, w_uq,
           ckv_norm_g, w_ukv, q_norm_g, k_norm_g, w_out, conv_pw1_w, conv_pw1_b, conv_dw_w, conv_dw_b,
           conv_ln_g, conv_ln_b, conv_pw2_w, conv_pw2_b, ada_ffn_w, ada_ffn_b, norm_ffn_g, router_w,
           router_b, moe_w1, moe_b1, moe_w2, moe_b2):
    batch, seq, d = x.shape
    depth = ada_mix_w.shape[0]
    c_pad = jnp.pad(c, ((0, 8 - batch), (0, 0)))
    mix_mods = adaln(c_pad, ada_mix_w, ada_mix_b)
    ffn_mods = adaln(c_pad, ada_ffn_w, ada_ffn_b)
    tables = _rope_tables(positions)
    xf = x.reshape(batch * seq, d)
    for layer in range(depth):
        i = layer // 2
        mods = _split_mods(mix_mods[layer], batch)
        if layer % 2 == 0:
            xf = pool_mla_layer(xf, mods, norm_mix_g[layer], tables, w_in[i], pool_w[i], pool_scale[i],
                                cq_norm_g[i], w_uq[i], ckv_norm_g[i], w_ukv[i], q_norm_g[i], k_norm_g[i],
                                w_out[i], batch, seq)
        else:
            xf = conformer_layer(xf, mods, norm_mix_g[layer], conv_pw1_w[i], conv_pw1_b[i], conv_dw_w[i],
                                 conv_dw_b[i], conv_ln_g[i], conv_ln_b[i], conv_pw2_w[i], conv_pw2_b[i], seq)
        xf = moe_layer(xf, _split_mods(ffn_mods[layer], batch), norm_ffn_g[layer], router_w[layer],
                       router_b[layer], layer, moe_w1, moe_b1, moe_w2, moe_b2, seq)
    return xf.reshape(batch, seq, d)
```

```python
import functools

import jax
import jax.numpy as jnp
from jax import lax
from jax.experimental import pallas as pl
from jax.experimental.pallas import tpu as pltpu
from jax.experimental.pallas import tpu_sc as plsc

F32 = jnp.float32
BF16 = jnp.bfloat16
HIGHEST = lax.Precision.HIGHEST

EPS = 1e-6
POOL_WINDOWS = (2, 4, 8, 16)
POOL_GROUP_DIM = 128
POOL_WIDTH = POOL_GROUP_DIM * len(POOL_WINDOWS)
MLA_HEADS = 8
QK_NOPE_DIM = 64
QK_ROPE_DIM = 32
QK_HEAD_DIM = QK_NOPE_DIM + QK_ROPE_DIM
V_HEAD_DIM = 64
Q_LORA_RANK = 384
KV_LORA_RANK = 256
ROPE_THETA = 10000.0
CHUNK = 64
CONV_WIDTH = 31
N_EXPERTS = 32
TOP_K = 4
SWIGLU_ALPHA = 1.702
SWIGLU_LIMIT = 7.0
MOE_BLOCK = 256
MOE_STEP = 512

LANES = 128
SUBLANES = 8
CONV_PATCH_ROWS = 64
CONV_PATCH_COLS = 256
POOL_HALO = 16
CONV_HALO = 32
MASK_VALUE = -1e30
LOG2_E = 1.4426950408889634
VMEM_LIMIT = 52 * 1024 * 1024
SC_CORES = 2
SC_SUBCORES = 16
SC_WORKERS = SC_CORES * SC_SUBCORES
SC_CHUNK = 64

Z_CQ = POOL_WIDTH
Z_CKV = Z_CQ + Q_LORA_RANK
Z_ROPE = Z_CKV + KV_LORA_RANK
Z_WIDTH = Z_ROPE + LANES


def _params(*sem, vmem=None):
    return pltpu.CompilerParams(dimension_semantics=sem, vmem_limit_bytes=vmem or VMEM_LIMIT,
                                disable_bounds_checks=True)


def _adaln_kernel(c_ref, w_ref, b_ref, o_ref):
    c = c_ref[...]
    s = c * jax.nn.sigmoid(c)
    o_ref[0] = jnp.dot(s, w_ref[0], preferred_element_type=F32, precision=HIGHEST) + b_ref[0]


def adaln(c_pad, w, b):
    n_l, d, d3 = w.shape
    tn = 512
    return pl.pallas_call(
        _adaln_kernel,
        grid=(n_l, d3 // tn),
        in_specs=[pl.BlockSpec((8, d), lambda l, j: (0, 0)),
                  pl.BlockSpec((1, d, tn), lambda l, j: (l, 0, j)),
                  pl.BlockSpec((1, 1, tn), lambda l, j: (l, 0, j))],
        out_specs=pl.BlockSpec((1, 8, tn), lambda l, j: (l, 0, j)),
        out_shape=jax.ShapeDtypeStruct((n_l, 8, d3), F32),
        compiler_params=_params("parallel", "parallel"),
        name="adaln",
    )(c_pad, w, b.reshape(n_l, 1, d3))


def _modulated_norm(x, g, scale, shift):
    ms = jnp.mean(x * x, axis=-1, keepdims=True)
    return x * lax.rsqrt(ms + EPS) * g * (1.0 + scale) + shift


def _norm_mod_kernel(x_ref, g_ref, sc_ref, sh_ref, o_ref):
    o_ref[...] = _modulated_norm(x_ref[...], g_ref[...], sc_ref[0], sh_ref[0]).astype(o_ref.dtype)


def _pack_halves(x):
    w = x.shape[1] // 2
    lo = lax.bitcast_convert_type(x[:, :w].astype(BF16).astype(F32), jnp.uint32)
    hi = lax.bitcast_convert_type(x[:, w:].astype(BF16).astype(F32), jnp.uint32)
    return lax.bitcast_convert_type((lo >> 16) | (hi & jnp.uint32(0xFFFF0000)), jnp.int32)


def _unpack_halves(p):
    u = lax.bitcast_convert_type(p, jnp.uint32)
    lo = lax.bitcast_convert_type(u << 16, F32)
    hi = lax.bitcast_convert_type(u & jnp.uint32(0xFFFF0000), F32)
    return lo, hi


def _norm_mod_router_kernel(x_ref, g_ref, sc_ref, sh_ref, rw_ref, rb_ref, o_ref, lg_ref):
    h = _modulated_norm(x_ref[...], g_ref[...], sc_ref[0], sh_ref[0])
    o_ref[...] = _pack_halves(h)
    lg_ref[...] = jnp.dot(h, rw_ref[...], preferred_element_type=F32, precision=HIGHEST) + rb_ref[...]


def norm_mod(x, g, scale, shift, seq, tm=512):
    n, d = x.shape
    per = seq // tm
    vec = pl.BlockSpec((1, 1, d), lambda i: (i // per, 0, 0))
    return pl.pallas_call(
        _norm_mod_kernel,
        grid=(n // tm,),
        in_specs=[pl.BlockSpec((tm, d), lambda i: (i, 0)), pl.BlockSpec((1, d), lambda i: (0, 0)), vec, vec],
        out_specs=pl.BlockSpec((tm, d), lambda i: (i, 0)),
        out_shape=jax.ShapeDtypeStruct((n, d), BF16),
        compiler_params=_params("parallel"),
        name="norm_mod",
    )(x, g, scale, shift)


def norm_mod_router(x, g, scale, shift, rw_pad, rb_pad, seq, tm=512):
    n, d = x.shape
    per = seq // tm
    vec = pl.BlockSpec((1, 1, d), lambda i: (i // per, 0, 0))
    return pl.pallas_call(
        _norm_mod_router_kernel,
        grid=(n // tm,),
        in_specs=[pl.BlockSpec((tm, d), lambda i: (i, 0)), pl.BlockSpec((1, d), lambda i: (0, 0)), vec, vec,
                  pl.BlockSpec((d, LANES), lambda i: (0, 0)), pl.BlockSpec((1, LANES), lambda i: (0, 0))],
        out_specs=[pl.BlockSpec((tm, d // 2), lambda i: (i, 0)), pl.BlockSpec((tm, LANES), lambda i: (i, 0))],
        out_shape=[jax.ShapeDtypeStruct((n, d // 2), jnp.int32), jax.ShapeDtypeStruct((n, LANES), F32)],
        compiler_params=_params("parallel"),
        name="norm_mod_router",
    )(x, g, scale, shift, rw_pad, rb_pad)


def _matmul_kernel(a_ref, w_ref, o_ref):
    o_ref[...] = jnp.dot(a_ref[...], w_ref[...], preferred_element_type=F32).astype(o_ref.dtype)


def matmul(a, w, out_dtype=F32, tm=512):
    n, k = a.shape
    m = w.shape[1]
    return pl.pallas_call(
        _matmul_kernel,
        grid=(n // tm,),
        in_specs=[pl.BlockSpec((tm, k), lambda i: (i, 0)), pl.BlockSpec((k, m), lambda i: (0, 0))],
        out_specs=pl.BlockSpec((tm, m), lambda i: (i, 0)),
        out_shape=jax.ShapeDtypeStruct((n, m), out_dtype),
        compiler_params=_params("parallel"),
        name="matmul",
    )(a, w)


def _glu_kernel(a_ref, wv_ref, wg_ref, bv_ref, bg_ref, o_ref):
    a = a_ref[...]
    val = jnp.dot(a, wv_ref[...], preferred_element_type=F32) + bv_ref[...]
    gt = jnp.dot(a, wg_ref[...], preferred_element_type=F32) + bg_ref[...]
    o_ref[...] = val * jax.nn.sigmoid(gt)


def glu_matmul(a, wv, wg, bv, bg, tm=512):
    n, k = a.shape
    m = wv.shape[1]
    full = lambda r, c: pl.BlockSpec((r, c), lambda i: (0, 0))
    return pl.pallas_call(
        _glu_kernel,
        grid=(n // tm,),
        in_specs=[pl.BlockSpec((tm, k), lambda i: (i, 0)), full(k, m), full(k, m), full(1, m), full(1, m)],
        out_specs=pl.BlockSpec((tm, m), lambda i: (i, 0)),
        out_shape=jax.ShapeDtypeStruct((n, m), F32),
        compiler_params=_params("parallel"),
        name="glu_matmul",
    )(a, wv, wg, bv, bg)


def _proj_residual_kernel(*refs, n_in):
    a_refs, w_refs = refs[:n_in], refs[n_in:2 * n_in]
    b_ref, x_ref, gate_ref, o_ref = refs[2 * n_in:]
    acc = b_ref[...]
    for a_ref, w_ref in zip(a_refs, w_refs):
        acc = acc + jnp.dot(a_ref[...], w_ref[...], preferred_element_type=F32)
    o_ref[...] = x_ref[...] + gate_ref[0] * acc


def proj_residual(a_list, w_list, bias, x, gate, seq, tm=512):
    n, d = x.shape
    per = seq // tm
    n_in = len(a_list)
    in_specs = [pl.BlockSpec((tm, a.shape[1]), lambda i: (i, 0)) for a in a_list]
    in_specs += [pl.BlockSpec(w.shape, lambda i: (0, 0)) for w in w_list]
    in_specs += [pl.BlockSpec((1, d), lambda i: (0, 0)), pl.BlockSpec((tm, d), lambda i: (i, 0)),
                 pl.BlockSpec((1, 1, d), lambda i: (i // per, 0, 0))]
    return pl.pallas_call(
        functools.partial(_proj_residual_kernel, n_in=n_in),
        grid=(n // tm,),
        in_specs=in_specs,
        out_specs=pl.BlockSpec((tm, d), lambda i: (i, 0)),
        out_shape=jax.ShapeDtypeStruct((n, d), F32),
        compiler_params=_params("parallel"),
        name="proj_residual",
    )(*a_list, *w_list, bias, x, gate)


def _rope(xn, cos_t, sin_lo, sin_hi):
    return (xn * cos_t + pltpu.roll(xn, LANES - QK_ROPE_DIM // 2, axis=1) * sin_lo
            + pltpu.roll(xn, QK_ROPE_DIM // 2, axis=1) * sin_hi)


def _mla_prep_kernel(z_ref, halo_ref, pw_ref, ps_ref, cqg_ref, wuq_ref, ckvg_ref, wuk_ref, wuv_ref,
                     qg_ref, kg_ref, cos_ref, slo_ref, shi_ref,
                     yp_ref, q_ref, k_ref, v_ref, ext_ref, *, tm, per):
    si = pl.program_id(0) % per
    u = z_ref[:, 0:POOL_WIDTH]
    ext_ref[0:POOL_HALO, :] = jnp.where(si == 0, 0.0, halo_ref[...])
    ext_ref[POOL_HALO:, :] = u
    t = si * tm + lax.broadcasted_iota(jnp.int32, (tm, 1), 0)
    for g, w in enumerate(POOL_WINDOWS):
        cols = slice(g * POOL_GROUP_DIM, (g + 1) * POOL_GROUP_DIM)
        ug = u[:, cols]
        s = ug
        for j in range(1, w):
            s = s + ext_ref[POOL_HALO - j:POOL_HALO - j + tm, cols]
        cnt = jnp.minimum(t + 1, w).astype(F32)
        pooled = s / cnt - ug
        yp = jnp.dot(pooled.astype(BF16), pw_ref[g], preferred_element_type=F32) * ps_ref[:, cols]
        yp_ref[:, cols] = yp.astype(yp_ref.dtype)

    cos_t, sin_lo, sin_hi = cos_ref[...], slo_ref[...], shi_ref[...]
    inv_head = 1.0 / QK_HEAD_DIM

    cq = z_ref[:, Z_CQ:Z_CKV]
    cqn = cq * lax.rsqrt(jnp.mean(cq * cq, axis=-1, keepdims=True) + EPS) * cqg_ref[...]
    qf = jnp.dot(cqn.astype(BF16), wuq_ref[...], preferred_element_type=F32)
    q_scale = QK_HEAD_DIM ** -0.5 * LOG2_E
    for h in range(MLA_HEADS):
        qh = qf[:, h * LANES:(h + 1) * LANES]
        ss = jnp.sum(qh * qh, axis=-1, keepdims=True) * inv_head
        qn = qh * lax.rsqrt(ss + EPS) * qg_ref[...]
        q_ref[0, h] = (_rope(qn, cos_t, sin_lo, sin_hi) * q_scale).astype(q_ref.dtype)

    ckv = z_ref[:, Z_CKV:Z_ROPE]
    ckvn = (ckv * lax.rsqrt(jnp.mean(ckv * ckv, axis=-1, keepdims=True) + EPS) * ckvg_ref[...]).astype(BF16)
    kf = jnp.dot(ckvn, wuk_ref[...], preferred_element_type=F32)
    vf = jnp.dot(ckvn, wuv_ref[...], preferred_element_type=F32)
    k_rope = z_ref[:, Z_ROPE:Z_WIDTH]
    ones_lane = lax.broadcasted_iota(jnp.int32, (tm, LANES), 1) == V_HEAD_DIM
    for h in range(MLA_HEADS):
        kh = kf[:, h * LANES:(h + 1) * LANES] + k_rope
        ss = jnp.sum(kh * kh, axis=-1, keepdims=True) * inv_head
        kn = kh * lax.rsqrt(ss + EPS) * kg_ref[...]
        k_ref[0, h] = _rope(kn, cos_t, sin_lo, sin_hi).astype(k_ref.dtype)
        v_ref[0, h] = jnp.where(ones_lane, 1.0, vf[:, h * LANES:(h + 1) * LANES]).astype(v_ref.dtype)


def mla_prep(z, pool_w, pool_scale, cq_g, wuq_pad, ckv_g, wuk_pad, wuv_pad, qg_pad, kg_pad,
             cos_t, sin_lo, sin_hi, batch, seq, tm=256):
    n = z.shape[0]
    per = seq // tm
    hb = tm // POOL_HALO
    full = lambda a: pl.BlockSpec(a.shape, lambda i: (0,) * a.ndim)
    tab = pl.BlockSpec((tm, LANES), lambda i: (i, 0))
    head_out = pl.BlockSpec((1, MLA_HEADS, tm, LANES), lambda i: (i // per, 0, i % per, 0))
    head_shape = jax.ShapeDtypeStruct((batch, MLA_HEADS, seq, LANES), BF16)
    return pl.pallas_call(
        functools.partial(_mla_prep_kernel, tm=tm, per=per),
        grid=(n // tm,),
        in_specs=[pl.BlockSpec((tm, Z_WIDTH), lambda i: (i, 0)),
                  pl.BlockSpec((POOL_HALO, POOL_WIDTH), lambda i: (jnp.maximum(i * hb - 1, 0), 0)),
                  full(pool_w), full(pool_scale), full(cq_g), full(wuq_pad), full(ckv_g), full(wuk_pad),
                  full(wuv_pad), full(qg_pad), full(kg_pad), tab, tab, tab],
        out_specs=[pl.BlockSpec((tm, POOL_WIDTH), lambda i: (i, 0)), head_out, head_out, head_out],
        out_shape=[jax.ShapeDtypeStruct((n, POOL_WIDTH), BF16), head_shape, head_shape, head_shape],
        scratch_shapes=[pltpu.VMEM((tm + POOL_HALO, POOL_WIDTH), F32)],
        compiler_params=_params("parallel"),
        name="mla_prep",
    )(z, z, pool_w, pool_scale, cq_g, wuq_pad, ckv_g, wuk_pad, wuv_pad, qg_pad, kg_pad, cos_t, sin_lo, sin_hi)


def _flash_kernel(q_ref, k_ref, v_ref, o_ref, m_ref, acc_ref, *, tq, tk, heads):
    qi = pl.program_id(2)
    sub = tq // tk
    slabs = tk // LANES
    q_chunk = lax.broadcasted_iota(jnp.int32, (tq, tk), 0) // CHUNK
    k_chunk = lax.broadcasted_iota(jnp.int32, (tq, tk), 1) // CHUNK
    m_ref[...] = jnp.full(m_ref.shape, MASK_VALUE, F32)
    acc_ref[...] = jnp.zeros(acc_ref.shape, F32)

    def step(j, mask):
        start = pl.multiple_of(j * tk, tk)
        for hh in range(heads):
            k = k_ref[0, hh, pl.ds(start, tk), :]
            v = v_ref[0, hh, pl.ds(start, tk), :]
            s = lax.dot_general(q_ref[0, hh], k, (((1,), (1,)), ((), ())), preferred_element_type=F32)
            if mask is not None:
                s = jnp.where(mask, s, MASK_VALUE)
            cols_s = [s[:, c * LANES:(c + 1) * LANES] for c in range(slabs)]
            s_max = cols_s[0]
            for sc in cols_s[1:]:
                s_max = jnp.maximum(s_max, sc)
            m_prev = m_ref[hh]
            m_new = jnp.maximum(m_prev, jnp.max(s_max, axis=-1, keepdims=True))
            alpha = jnp.exp2(m_prev - m_new)
            p = jnp.concatenate([jnp.exp2(sc - m_new) for sc in cols_s], axis=1).astype(v.dtype)
            acc_ref[hh] = alpha * acc_ref[hh] + jnp.dot(p, v, preferred_element_type=F32)
            m_ref[hh] = m_new

    def full_step(j, carry):
        step(j, None)
        return carry

    lax.fori_loop(0, sub * qi, full_step, 0)
    for dd in range(sub):
        step(sub * qi + dd, k_chunk + dd * (tk // CHUNK) <= q_chunk)
    outs = []
    for hh in range(heads):
        acc = acc_ref[hh]
        outs.append((acc / acc[:, V_HEAD_DIM:V_HEAD_DIM + 1])[:, :V_HEAD_DIM])
    o_ref[0] = jnp.concatenate(outs, axis=-1).astype(o_ref.dtype)


def flash_attention(q, k, v, tq=1024, tk=512):
    b, h, s, _ = q.shape
    heads = LANES // V_HEAD_DIM
    return pl.pallas_call(
        functools.partial(_flash_kernel, tq=tq, tk=tk, heads=heads),
        grid=(b, h // heads, s // tq),
        in_specs=[pl.BlockSpec((1, heads, tq, LANES), lambda bi, hi, qi: (bi, hi, qi, 0)),
                  pl.BlockSpec((1, heads, s, LANES), lambda bi, hi, qi: (bi, hi, 0, 0)),
                  pl.BlockSpec((1, heads, s, LANES), lambda bi, hi, qi: (bi, hi, 0, 0))],
        out_specs=pl.BlockSpec((1, tq, LANES), lambda bi, hi, qi: (bi, qi, hi)),
        out_shape=jax.ShapeDtypeStruct((b, s, h * V_HEAD_DIM), BF16),
        scratch_shapes=[pltpu.VMEM((heads, tq, LANES), F32) for _ in range(2)],
        compiler_params=_params("parallel", "parallel", "parallel"),
        name="flash_attention",
    )(q, k, v)


def _dwconv_kernel(u_ref, halo_ref, w_ref, b_ref, g_ref, beta_ref, o_ref, ext_ref, sh_ref, conv_ref, *, tm, per):
    si = pl.program_id(0) % per
    d = u_ref.shape[1]
    ext_ref[0:CONV_HALO, :] = jnp.where(si == 0, 0.0, halo_ref[...])
    ext_ref[CONV_HALO:, :] = u_ref[...]
    span = sh_ref.shape[1]
    for b in range(1, SUBLANES):
        sh_ref[b - 1] = ext_ref[b:b + span, :]
    first = CONV_HALO - (CONV_WIDTH - 1)
    for r0 in range(0, tm, CONV_PATCH_ROWS):
        for c0 in range(0, d, CONV_PATCH_COLS):
            cols = slice(c0, c0 + CONV_PATCH_COLS)
            view = (CONV_PATCH_ROWS // SUBLANES, SUBLANES, CONV_PATCH_COLS)
            patch = jnp.broadcast_to(b_ref[:, cols], view)
            for j in range(CONV_WIDTH):
                a, b = divmod(first + j, SUBLANES)
                rows = slice(SUBLANES * a + r0, SUBLANES * a + r0 + CONV_PATCH_ROWS)
                win = ext_ref[rows, cols] if b == 0 else sh_ref[b - 1, rows, cols]
                tap = jnp.broadcast_to(w_ref[j:j + 1, cols], view[1:])
                patch = patch + tap[None] * win.reshape(view)
            conv_ref[r0:r0 + CONV_PATCH_ROWS, cols] = patch.reshape(CONV_PATCH_ROWS, CONV_PATCH_COLS)
    acc = conv_ref[...]
    mu = jnp.mean(acc, axis=-1, keepdims=True)
    cen = acc - mu
    var = jnp.mean(cen * cen, axis=-1, keepdims=True)
    y = cen * lax.rsqrt(var + EPS) * g_ref[...] + beta_ref[...]
    o_ref[...] = (y * jax.nn.sigmoid(y)).astype(o_ref.dtype)


def dwconv_ln_silu(u, w_pad, b, g, beta, seq, tm=256):
    n, d = u.shape
    per = seq // tm
    hb = tm // CONV_HALO
    full = lambda a: pl.BlockSpec(a.shape, lambda i: (0, 0))
    return pl.pallas_call(
        functools.partial(_dwconv_kernel, tm=tm, per=per),
        grid=(n // tm,),
        in_specs=[pl.BlockSpec((tm, d), lambda i: (i, 0)),
                  pl.BlockSpec((CONV_HALO, d), lambda i: (jnp.maximum(i * hb - 1, 0), 0)),
                  full(w_pad), full(b), full(g), full(beta)],
        out_specs=pl.BlockSpec((tm, d), lambda i: (i, 0)),
        out_shape=jax.ShapeDtypeStruct((n, d), BF16),
        scratch_shapes=[pltpu.VMEM((tm + CONV_HALO, d), F32),
                        pltpu.VMEM((SUBLANES - 1, tm + CONV_HALO - SUBLANES, d), F32),
                        pltpu.VMEM((tm, d), F32)],
        compiler_params=_params("parallel"),
        name="dwconv_ln_silu",
    )(u, u, w_pad, b, g, beta)


def _router_kernel(lg_ref, idx_ref, gate_ref, cnt_ref, carry_ref, *, tm):
    @pl.when(pl.program_id(0) == 0)
    def _():
        carry_ref[...] = jnp.zeros(carry_ref.shape, F32)

    lane = lax.broadcasted_iota(jnp.int32, (tm, LANES), 1)
    lane_f = lane.astype(F32)
    neg = -jnp.inf
    logits = jnp.where(lane < N_EXPERTS, lg_ref[...], neg)
    picks, vals, ids = [], [], []
    for _ in range(TOP_K):
        mx = jnp.max(logits, axis=-1, keepdims=True)
        idx = jnp.min(jnp.where(logits == mx, lane_f, float(LANES)), axis=-1, keepdims=True)
        pick = lane_f == idx
        picks.append(pick)
        vals.append(mx)
        ids.append(idx)
        logits = jnp.where(pick, neg, logits)
    exps = [jnp.exp(v - vals[0]) for v in vals]
    den = exps[0]
    for e in exps[1:]:
        den = den + e

    chosen = jnp.zeros((tm, LANES), F32)
    for pick in picks:
        chosen = chosen + pick.astype(F32)
    r_io = lax.broadcasted_iota(jnp.int32, (tm, tm), 0)
    c_io = lax.broadcasted_iota(jnp.int32, (tm, tm), 1)
    earlier = (c_io < r_io).astype(BF16)
    before = jnp.dot(earlier, chosen.astype(BF16), preferred_element_type=F32) + carry_ref[0:1, :]
    idx_out = jnp.zeros((tm, LANES), F32)
    gate_out = jnp.zeros((tm, LANES), F32)
    for k in range(TOP_K):
        rank = jnp.sum(jnp.where(picks[k], before, 0.0), axis=-1, keepdims=True)
        idx_out = jnp.where(lane == k, ids[k], idx_out)
        idx_out = jnp.where(lane == TOP_K + k, rank, idx_out)
        gate_out = jnp.where(lane == k, exps[k] / den, gate_out)
    idx_ref[...] = idx_out.astype(jnp.int32)
    gate_ref[...] = gate_out
    total = carry_ref[0:1, :] + jnp.sum(chosen, axis=0, keepdims=True)
    carry_ref[...] = jnp.broadcast_to(total, carry_ref.shape)
    cnt_ref[...] = jnp.broadcast_to(total, cnt_ref.shape).astype(jnp.int32)


def router(logits, tm=512):
    n = logits.shape[0]
    row = pl.BlockSpec((tm, LANES), lambda i: (i, 0))
    return pl.pallas_call(
        functools.partial(_router_kernel, tm=tm),
        grid=(n // tm,),
        in_specs=[row],
        out_specs=[row, row, pl.BlockSpec((8, LANES), lambda i: (0, 0))],
        out_shape=[jax.ShapeDtypeStruct((n, LANES), jnp.int32), jax.ShapeDtypeStruct((n, LANES), F32),
                   jax.ShapeDtypeStruct((8, LANES), jnp.int32)],
        scratch_shapes=[pltpu.VMEM((8, LANES), F32)],
        compiler_params=_params("arbitrary"),
        name="router",
    )(logits)


def _sc_worker_base(per_worker):
    return (lax.axis_index("s") * SC_CORES + lax.axis_index("c")) * per_worker


def sc_scatter_rows(src, dest_flat, n_rows):
    n, d = src.shape
    per_w = n // SC_WORKERS
    n_chunks = per_w // SC_CHUNK
    mesh = plsc.VectorSubcoreMesh(core_axis_name="c", subcore_axis_name="s")

    @functools.partial(
        pl.kernel, out_type=jax.ShapeDtypeStruct((n_rows, d), src.dtype), mesh=mesh,
        scratch_types=[pltpu.VMEM((per_w,), jnp.int32) for _ in range(TOP_K)]
        + [pltpu.VMEM((SC_CHUNK, d), src.dtype), pltpu.SemaphoreType.DMA],
        name="sc_scatter_rows")
    def scatter(src_hbm, dest_hbm, out_hbm, *scratch):
        idx_refs, rows_ref, sem = scratch[:TOP_K], scratch[TOP_K], scratch[TOP_K + 1]
        base = _sc_worker_base(per_w)
        for k, idx_ref in enumerate(idx_refs):
            pltpu.sync_copy(dest_hbm.at[pl.ds(k * n + base, per_w)], idx_ref)

        @pl.loop(0, n_chunks)
        def _(j):
            off = j * SC_CHUNK
            pltpu.sync_copy(src_hbm.at[pl.ds(base + off, SC_CHUNK)], rows_ref)
            for idx_ref in idx_refs:
                pltpu.async_copy(rows_ref, out_hbm.at[idx_ref.at[pl.ds(off, SC_CHUNK)]], sem).wait()

    return scatter(src, dest_flat)


def sc_gather_rows(table, idx):
    b = idx.shape[0]
    d = table.shape[1]
    per_w = b // SC_WORKERS
    n_chunks = per_w // SC_CHUNK
    mesh = plsc.VectorSubcoreMesh(core_axis_name="c", subcore_axis_name="s")

    @functools.partial(
        pl.kernel, out_type=jax.ShapeDtypeStruct((b, d), table.dtype), mesh=mesh,
        scratch_types=[pltpu.VMEM((per_w,), jnp.int32), pltpu.VMEM((SC_CHUNK, d), table.dtype),
                       pltpu.SemaphoreType.DMA],
        name="sc_gather_rows")
    def gather(table_hbm, idx_hbm, out_hbm, idx_ref, rows_ref, sem):
        base = _sc_worker_base(per_w)
        pltpu.sync_copy(idx_hbm.at[pl.ds(base, per_w)], idx_ref)

        @pl.loop(0, n_chunks)
        def _(j):
            off = j * SC_CHUNK
            pltpu.async_copy(table_hbm.at[idx_ref.at[pl.ds(off, SC_CHUNK)]], rows_ref, sem).wait()
            pltpu.sync_copy(rows_ref, out_hbm.at[pl.ds(base + off, SC_CHUNK)])

    return gather(table, idx)


def _ffn_kernel(be_ref, nv_ref, nu_ref, xs_ref, w1_ref, b1g_ref, b1u_ref, w2_ref, b2_ref, y_ref,
                wg_ref, wu_ref, w2s_ref):
    i = pl.program_id(0)
    pair = 2 * LANES

    @pl.when(i < nu_ref[0])
    def _():
        e = be_ref[i]
        prev = be_ref[jnp.maximum(i - 1, 0)]

        @pl.when((i == 0) | (e != prev))
        def _():
            r_io = lax.broadcasted_iota(jnp.int32, (pair, pair), 0)
            c_io = lax.broadcasted_iota(jnp.int32, (pair, pair), 1)
            want = jnp.where(c_io < LANES, 2 * c_io, 2 * (c_io - LANES) + 1)
            sel = (r_io == want).astype(BF16)
            for c in range(wg_ref.shape[1] // LANES):
                slab = w1_ref[0, 0, :, c * pair:(c + 1) * pair].astype(BF16)
                split = jnp.dot(slab, sel, preferred_element_type=F32)
                wg_ref[:, c * LANES:(c + 1) * LANES] = split[:, :LANES].astype(BF16)
                wu_ref[:, c * LANES:(c + 1) * LANES] = split[:, LANES:].astype(BF16)
            w2s_ref[...] = w2_ref[0, 0].astype(BF16)

        nvalid = nv_ref[i]

        def ffn_rows(rows):
            row = lax.broadcasted_iota(jnp.int32, (rows, xs_ref.shape[1]), 0)
            lo, hi = _unpack_halves(jnp.where(row < nvalid, xs_ref[0:rows, :], 0))
            x = jnp.concatenate([lo, hi], axis=1).astype(BF16)
            hg = jnp.dot(x, wg_ref[...], preferred_element_type=F32) + b1g_ref[0]
            hu = jnp.dot(x, wu_ref[...], preferred_element_type=F32) + b1u_ref[0]
            gate = jnp.minimum(hg, SWIGLU_LIMIT)
            up = jnp.clip(hu, -SWIGLU_LIMIT, SWIGLU_LIMIT)
            act = gate * jax.nn.sigmoid(SWIGLU_ALPHA * gate) * (up + 1.0)
            y = jnp.dot(act.astype(BF16), w2s_ref[...], preferred_element_type=F32) + b2_ref[0]
            y_ref[0:rows, :] = _pack_halves(y)
            if rows < y_ref.shape[0]:
                y_ref[rows:, :] = jnp.zeros((y_ref.shape[0] - rows, y_ref.shape[1]), y_ref.dtype)

        n_sub = y_ref.shape[0] // MOE_BLOCK
        for sub in range(1, n_sub + 1):
            lo_rows, hi_rows = (sub - 1) * MOE_BLOCK, sub * MOE_BLOCK
            pl.when((nvalid > lo_rows) & (nvalid <= hi_rows))(functools.partial(ffn_rows, hi_rows))

    @pl.when(i >= nu_ref[0])
    def _():
        y_ref[...] = jnp.zeros(y_ref.shape, y_ref.dtype)


def moe_ffn(xs, block_e, block_valid, n_used, layer, w1, b1g, b1u, w2, b2):
    n_rows, half = xs.shape
    d = 2 * half
    n_blocks = n_rows // MOE_STEP
    n_exp, f2 = w1.shape[1], w1.shape[3]
    f = f2 // 2
    rows = lambda i, be, nv, nu: (jnp.minimum(i, nu[0] - 1), 0)
    vec = lambda width: pl.BlockSpec((1, 1, width), lambda i, be, nv, nu: (be[i], 0, 0))
    grid_spec = pltpu.PrefetchScalarGridSpec(
        num_scalar_prefetch=3,
        grid=(n_blocks,),
        in_specs=[
            pl.BlockSpec((MOE_STEP, half), rows),
            pl.BlockSpec((1, 1, d, f2), lambda i, be, nv, nu: (layer, be[i], 0, 0)),
            vec(f), vec(f),
            pl.BlockSpec((1, 1, f, d), lambda i, be, nv, nu: (layer, be[i], 0, 0)),
            vec(d),
        ],
        out_specs=pl.BlockSpec((MOE_STEP, half), lambda i, be, nv, nu: (i, 0)),
        scratch_shapes=[pltpu.VMEM((d, f), BF16), pltpu.VMEM((d, f), BF16), pltpu.VMEM((f, d), BF16)],
    )
    return pl.pallas_call(
        _ffn_kernel,
        grid_spec=grid_spec,
        out_shape=jax.ShapeDtypeStruct((n_rows, half), jnp.int32),
        compiler_params=_params("arbitrary"),
        name="moe_ffn",
    )(block_e, block_valid, n_used, xs, w1, b1g.reshape(n_exp, 1, f), b1u.reshape(n_exp, 1, f), w2,
      b2.reshape(n_exp, 1, d))


def _combine_kernel(x_ref, y_ref, gates_ref, gate_ref, o_ref):
    g = gates_ref[...]
    half = y_ref.shape[2]
    acc_lo = acc_hi = None
    for k in range(TOP_K):
        lo, hi = _unpack_halves(y_ref[k])
        gk = g[:, k:k + 1]
        acc_lo = gk * lo if acc_lo is None else acc_lo + gk * lo
        acc_hi = gk * hi if acc_hi is None else acc_hi + gk * hi
    o_ref[:, :half] = x_ref[:, :half] + gate_ref[0, :, :half] * acc_lo
    o_ref[:, half:] = x_ref[:, half:] + gate_ref[0, :, half:] * acc_hi


def moe_combine(x, y, gates, gate_mod, seq, tm=512):
    n, d = x.shape
    per = seq // tm
    return pl.pallas_call(
        _combine_kernel,
        grid=(n // tm,),
        in_specs=[pl.BlockSpec((tm, d), lambda i: (i, 0)),
                  pl.BlockSpec((TOP_K, tm, d // 2), lambda i: (0, i, 0)),
                  pl.BlockSpec((tm, LANES), lambda i: (i, 0)),
                  pl.BlockSpec((1, 1, d), lambda i: (i // per, 0, 0))],
        out_specs=pl.BlockSpec((tm, d), lambda i: (i, 0)),
        out_shape=jax.ShapeDtypeStruct((n, d), F32),
        compiler_params=_params("parallel"),
        name="moe_combine",
    )(x, y, gates, gate_mod)


def moe_layer(x, mods, norm_g, router_w, router_b, layer, w1, b1, w2, b2, seq):
    n, d = x.shape
    shift, scale, gate = mods
    rw_pad = jnp.pad(router_w, ((0, 0), (0, LANES - N_EXPERTS)))
    rb_pad = jnp.pad(router_b, (0, LANES - N_EXPERTS)).reshape(1, LANES)
    h, logits = norm_mod_router(x, norm_g.reshape(1, d), scale, shift, rw_pad, rb_pad, seq)
    idx, gates, counts = router(logits)

    top_i = idx[:, :TOP_K]
    rank = idx[:, TOP_K:2 * TOP_K]
    counts = counts[0, :N_EXPERTS]
    experts = jnp.arange(N_EXPERTS, dtype=jnp.int32)
    padded = (counts + MOE_STEP - 1) // MOE_STEP * MOE_STEP
    pad_ends = jnp.sum(jnp.where(experts[:, None] >= experts[None, :], padded[None, :], 0), axis=1)
    pad_starts = pad_ends - padded
    dest = jnp.sum(jnp.where(top_i[..., None] == experts, pad_starts, 0), axis=-1) + rank
    n_blocks = -(-n * TOP_K // MOE_STEP) + N_EXPERTS
    block_start = jnp.arange(n_blocks, dtype=jnp.int32) * MOE_STEP
    block_e = jnp.minimum(jnp.sum((pad_ends[None, :] <= block_start[:, None]).astype(jnp.int32), axis=1),
                          N_EXPERTS - 1)
    n_used = (pad_ends[N_EXPERTS - 1:] // MOE_STEP).astype(jnp.int32)
    seg_end = jnp.sum(jnp.where(block_e[:, None] == experts, pad_starts + counts, 0), axis=1)
    block_valid = jnp.clip(seg_end - block_start, 0, MOE_STEP).astype(jnp.int32)
    dest_flat = dest.T.reshape(-1)

    xs = sc_scatter_rows(h, dest_flat, n_blocks * MOE_STEP)
    ys = moe_ffn(xs, block_e, block_valid, n_used, layer, w1, b1[layer][:, 0::2], b1[layer][:, 1::2], w2, b2[layer])
    y = sc_gather_rows(ys, dest_flat).reshape(TOP_K, n, d // 2)
    return moe_combine(x, y, gates, gate, seq)


def _pad_heads(w, width):
    k = w.shape[0]
    w = w.reshape(k, MLA_HEADS, width)
    return jnp.pad(w, ((0, 0), (0, 0), (0, LANES - width))).reshape(k, MLA_HEADS * LANES)


def _rope_tables(positions):
    half = QK_ROPE_DIM // 2
    inv = 1.0 / (ROPE_THETA ** (jnp.arange(0, QK_ROPE_DIM, 2, dtype=F32) / QK_ROPE_DIM))
    ang = positions.reshape(-1).astype(F32)[:, None] * inv
    cos, sin = jnp.cos(ang), jnp.sin(ang)
    n = ang.shape[0]
    ones = jnp.ones((n, QK_NOPE_DIM), F32)
    zeros = lambda w: jnp.zeros((n, w), F32)
    tail = LANES - QK_HEAD_DIM
    cos_t = jnp.concatenate([ones, cos, cos, zeros(tail)], axis=1)
    sin_lo = jnp.concatenate([zeros(QK_NOPE_DIM), -sin, zeros(half), zeros(tail)], axis=1)
    sin_hi = jnp.concatenate([zeros(QK_NOPE_DIM), zeros(half), sin, zeros(tail)], axis=1)
    return cos_t, sin_lo, sin_hi


def _split_mods(m, batch):
    d = m.shape[-1] // 3
    m = m[:batch]
    return tuple(m[:, None, j * d:(j + 1) * d] for j in range(3))


def pool_mla_layer(x, mods, norm_g, tables, w_in, pool_w, pool_scale, cq_norm_g, w_uq, ckv_norm_g, w_ukv,
                   q_norm_g, k_norm_g, w_out, batch, seq):
    n, d = x.shape
    shift, scale, gate = mods
    h = norm_mod(x, norm_g.reshape(1, d), scale, shift, seq)
    rope_cols = jnp.pad(w_in[:, Z_ROPE:], ((0, 0), (QK_NOPE_DIM, LANES - QK_HEAD_DIM)))
    w_in_pad = jnp.concatenate([w_in[:, :Z_ROPE], rope_cols], axis=1).astype(BF16)
    z = matmul(h, w_in_pad)
    w_ukv_h = w_ukv.reshape(KV_LORA_RANK, MLA_HEADS, QK_NOPE_DIM + V_HEAD_DIM)
    wuk_pad = _pad_heads(w_ukv_h[:, :, :QK_NOPE_DIM].reshape(KV_LORA_RANK, -1), QK_NOPE_DIM).astype(BF16)
    wuv_pad = _pad_heads(w_ukv_h[:, :, QK_NOPE_DIM:].reshape(KV_LORA_RANK, -1), V_HEAD_DIM).astype(BF16)
    wuq_pad = _pad_heads(w_uq, QK_HEAD_DIM).astype(BF16)
    pad_g = lambda g: jnp.pad(g, (0, LANES - QK_HEAD_DIM)).reshape(1, LANES)
    y_pool, q, k, v = mla_prep(z, pool_w.astype(BF16), pool_scale.reshape(1, -1), cq_norm_g.reshape(1, -1), wuq_pad,
                               ckv_norm_g.reshape(1, -1), wuk_pad, wuv_pad, pad_g(q_norm_g), pad_g(k_norm_g),
                               *tables, batch, seq)
    y_att = flash_attention(q, k, v).reshape(n, MLA_HEADS * V_HEAD_DIM)
    w_out_b = w_out.astype(BF16)
    return proj_residual([y_pool, y_att], [w_out_b[:POOL_WIDTH], w_out_b[POOL_WIDTH:]],
                         jnp.zeros((1, d), F32), x, gate, seq)


def conformer_layer(x, mods, norm_g, pw1_w, pw1_b, dw_w, dw_b, ln_g, ln_b, pw2_w, pw2_b, seq):
    n, d = x.shape
    shift, scale, gate = mods
    h = norm_mod(x, norm_g.reshape(1, d), scale, shift, seq)
    cd = pw1_w.shape[1] // 2
    pw1 = pw1_w.astype(BF16)
    u = glu_matmul(h, pw1[:, :cd], pw1[:, cd:], pw1_b[:cd].reshape(1, cd), pw1_b[cd:].reshape(1, cd))
    w_pad = jnp.pad(dw_w, ((0, CONV_HALO - CONV_WIDTH), (0, 0)))
    u = dwconv_ln_silu(u, w_pad, dw_b.reshape(1, cd), ln_g.reshape(1, cd), ln_b.reshape(1, cd), seq)
    return proj_residual([u], [pw2_w.astype(BF16)], pw2_b.reshape(1, d), x, gate, seq)


def kernel(x, c, positions, ada_mix_w, ada_mix_b, norm_mix_g, w_in, pool_w, pool_scale, cq_norm_g, w_uq,
           ckv_norm_g, w_ukv, q_norm_g, k_norm_g, w_out, conv_pw1_w, conv_pw1_b, conv_dw_w, conv_dw_b,
           conv_ln_g, conv_ln_b, conv_pw2_w, conv_pw2_b, ada_ffn_w, ada_ffn_b, norm_ffn_g, router_w,
           router_b, moe_w1, moe_b1, moe_w2, moe_b2):
    batch, seq, d = x.shape
    depth = ada_mix_w.shape[0]
    c_pad = jnp.pad(c, ((0, 8 - batch), (0, 0)))
    mix_mods = adaln(c_pad, ada_mix_w, ada_mix_b)
    ffn_mods = adaln(c_pad, ada_ffn_w, ada_ffn_b)
    tables = _rope_tables(positions)
    xf = x.reshape(batch * seq, d)
    for layer in range(depth):
        i = layer // 2
        mods = _split_mods(mix_mods[layer], batch)
        if layer % 2 == 0:
            xf = pool_mla_layer(xf, mods, norm_mix_g[layer], tables, w_in[i], pool_w[i], pool_scale[i],
                                cq_norm_g[i], w_uq[i], ckv_norm_g[i], w_ukv[i], q_norm_g[i], k_norm_g[i],
                                w_out[i], batch, seq)
        else:
            xf = conformer_layer(xf, mods, norm_mix_g[layer], conv_pw1_w[i], conv_pw1_b[i], conv_dw_w[i],
                                 conv_dw_b[i], conv_ln_g[i], conv_ln_b[i], conv_pw2_w[i], conv_pw2_b[i], seq)
        xf = moe_layer(xf, _split_mods(ffn_mods[layer], batch), norm_ffn_g[layer], router_w[layer],
                       router_b[layer], layer, moe_w1, moe_b1, moe_w2, moe_b2, seq)
    return xf.reshape(batch, seq, d)
```

```python
import functools

import jax
import jax.numpy as jnp
from jax import lax
from jax.experimental import pallas as pl
from jax.experimental.pallas import tpu as pltpu
from jax.experimental.pallas import tpu_sc as plsc

F32 = jnp.float32
BF16 = jnp.bfloat16
HIGHEST = lax.Precision.HIGHEST

EPS = 1e-6
POOL_WINDOWS = (2, 4, 8, 16)
POOL_GROUP_DIM = 128
POOL_WIDTH = POOL_GROUP_DIM * len(POOL_WINDOWS)
MLA_HEADS = 8
QK_NOPE_DIM = 64
QK_ROPE_DIM = 32
QK_HEAD_DIM = QK_NOPE_DIM + QK_ROPE_DIM
V_HEAD_DIM = 64
Q_LORA_RANK = 384
KV_LORA_RANK = 256
ROPE_THETA = 10000.0
CHUNK = 64
CONV_WIDTH = 31
N_EXPERTS = 32
TOP_K = 4
SWIGLU_ALPHA = 1.702
SWIGLU_LIMIT = 7.0
MOE_BLOCK = 256
MOE_STEP = 512

LANES = 128
SUBLANES = 8
CONV_PATCH_ROWS = 64
CONV_PATCH_COLS = 256
POOL_HALO = 16
CONV_HALO = 32
MASK_VALUE = -1e30
LOG2_E = 1.4426950408889634
VMEM_LIMIT = 52 * 1024 * 1024
SC_CORES = 2
SC_SUBCORES = 16
SC_WORKERS = SC_CORES * SC_SUBCORES
SC_CHUNK = 64

Z_CQ = POOL_WIDTH
Z_CKV = Z_CQ + Q_LORA_RANK
Z_ROPE = Z_CKV + KV_LORA_RANK
Z_WIDTH = Z_ROPE + LANES


def _params(*sem, vmem=None):
    return pltpu.CompilerParams(dimension_semantics=sem, vmem_limit_bytes=vmem or VMEM_LIMIT,
                                disable_bounds_checks=True)


def _adaln_kernel(c_ref, w_ref, b_ref, o_ref):
    c = c_ref[...]
    s = c * jax.nn.sigmoid(c)
    o_ref[0] = jnp.dot(s, w_ref[0], preferred_element_type=F32, precision=HIGHEST) + b_ref[0]


def adaln(c_pad, w, b):
    n_l, d, d3 = w.shape
    tn = 512
    return pl.pallas_call(
        _adaln_kernel,
        grid=(n_l, d3 // tn),
        in_specs=[pl.BlockSpec((8, d), lambda l, j: (0, 0)),
                  pl.BlockSpec((1, d, tn), lambda l, j: (l, 0, j)),
                  pl.BlockSpec((1, 1, tn), lambda l, j: (l, 0, j))],
        out_specs=pl.BlockSpec((1, 8, tn), lambda l, j: (l, 0, j)),
        out_shape=jax.ShapeDtypeStruct((n_l, 8, d3), F32),
        compiler_params=_params("parallel", "parallel"),
        name="adaln",
    )(c_pad, w, b.reshape(n_l, 1, d3))


def _modulated_norm(x, g, scale, shift):
    ms = jnp.mean(x * x, axis=-1, keepdims=True)
    return x * lax.rsqrt(ms + EPS) * g * (1.0 + scale) + shift


def _pack_halves(x):
    w = x.shape[1] // 2
    lo = lax.bitcast_convert_type(x[:, :w].astype(BF16).astype(F32), jnp.uint32)
    hi = lax.bitcast_convert_type(x[:, w:].astype(BF16).astype(F32), jnp.uint32)
    return lax.bitcast_convert_type((lo >> 16) | (hi & jnp.uint32(0xFFFF0000)), jnp.int32)


def _unpack_halves(p):
    u = lax.bitcast_convert_type(p, jnp.uint32)
    lo = lax.bitcast_convert_type(u << 16, F32)
    hi = lax.bitcast_convert_type(u & jnp.uint32(0xFFFF0000), F32)
    return lo, hi


def _norm_matmul_kernel(x_ref, g_ref, sc_ref, sh_ref, w_ref, o_ref):
    h = _modulated_norm(x_ref[...], g_ref[...], sc_ref[0], sh_ref[0])
    o_ref[...] = jnp.dot(h.astype(BF16), w_ref[...], preferred_element_type=F32)


def norm_matmul(x, g, scale, shift, w, seq, tm=512):
    n, d = x.shape
    m = w.shape[1]
    per = seq // tm
    vec = pl.BlockSpec((1, 1, d), lambda i: (i // per, 0, 0))
    return pl.pallas_call(
        _norm_matmul_kernel,
        grid=(n // tm,),
        in_specs=[pl.BlockSpec((tm, d), lambda i: (i, 0)), pl.BlockSpec((1, d), lambda i: (0, 0)), vec, vec,
                  pl.BlockSpec((d, m), lambda i: (0, 0))],
        out_specs=pl.BlockSpec((tm, m), lambda i: (i, 0)),
        out_shape=jax.ShapeDtypeStruct((n, m), F32),
        compiler_params=_params("parallel"),
        name="norm_matmul",
    )(x, g, scale, shift, w)


def _matmul_kernel(a_ref, w_ref, o_ref):
    o_ref[...] = jnp.dot(a_ref[...], w_ref[...], preferred_element_type=F32).astype(o_ref.dtype)


def matmul(a, w, out_dtype=F32, tm=512):
    n, k = a.shape
    m = w.shape[1]
    return pl.pallas_call(
        _matmul_kernel,
        grid=(n // tm,),
        in_specs=[pl.BlockSpec((tm, k), lambda i: (i, 0)), pl.BlockSpec((k, m), lambda i: (0, 0))],
        out_specs=pl.BlockSpec((tm, m), lambda i: (i, 0)),
        out_shape=jax.ShapeDtypeStruct((n, m), out_dtype),
        compiler_params=_params("parallel"),
        name="matmul",
    )(a, w)


def _glu_kernel(a_ref, wv_ref, wg_ref, bv_ref, bg_ref, o_ref):
    a = a_ref[...]
    val = jnp.dot(a, wv_ref[...], preferred_element_type=F32) + bv_ref[...]
    gt = jnp.dot(a, wg_ref[...], preferred_element_type=F32) + bg_ref[...]
    o_ref[...] = val * jax.nn.sigmoid(gt)


def glu_matmul(a, wv, wg, bv, bg, tm=512):
    n, k = a.shape
    m = wv.shape[1]
    full = lambda r, c: pl.BlockSpec((r, c), lambda i: (0, 0))
    return pl.pallas_call(
        _glu_kernel,
        grid=(n // tm,),
        in_specs=[pl.BlockSpec((tm, k), lambda i: (i, 0)), full(k, m), full(k, m), full(1, m), full(1, m)],
        out_specs=pl.BlockSpec((tm, m), lambda i: (i, 0)),
        out_shape=jax.ShapeDtypeStruct((n, m), F32),
        compiler_params=_params("parallel"),
        name="glu_matmul",
    )(a, wv, wg, bv, bg)


def _proj_residual_router_kernel(*refs, n_in):
    a_refs, w_refs = refs[:n_in], refs[n_in:2 * n_in]
    b_ref, x_ref, gate_ref, g_ref, sc_ref, sh_ref, rw_ref, rb_ref, o_ref, h_ref, lg_ref = refs[2 * n_in:]
    acc = b_ref[...]
    for a_ref, w_ref in zip(a_refs, w_refs):
        acc = acc + jnp.dot(a_ref[...], w_ref[...], preferred_element_type=F32)
    x_new = x_ref[...] + gate_ref[0] * acc
    o_ref[...] = x_new
    h = _modulated_norm(x_new, g_ref[...], sc_ref[0], sh_ref[0])
    h_ref[...] = _pack_halves(h)
    lg_ref[...] = jnp.dot(h, rw_ref[...], preferred_element_type=F32, precision=HIGHEST) + rb_ref[...]


def proj_residual_router(a_list, w_list, bias, x, gate, g, scale, shift, rw_pad, rb_pad, seq, tm=512):
    n, d = x.shape
    per = seq // tm
    n_in = len(a_list)
    rows = lambda width: pl.BlockSpec((tm, width), lambda i: (i, 0))
    const = lambda a: pl.BlockSpec(a.shape, lambda i: (0, 0))
    vec = pl.BlockSpec((1, 1, d), lambda i: (i // per, 0, 0))
    in_specs = [rows(a.shape[1]) for a in a_list] + [const(w) for w in w_list]
    in_specs += [const(bias), rows(d), vec, const(g), vec, vec, const(rw_pad), const(rb_pad)]
    return pl.pallas_call(
        functools.partial(_proj_residual_router_kernel, n_in=n_in),
        grid=(n // tm,),
        in_specs=in_specs,
        out_specs=[rows(d), rows(d // 2), rows(LANES)],
        out_shape=[jax.ShapeDtypeStruct((n, d), F32), jax.ShapeDtypeStruct((n, d // 2), jnp.int32),
                   jax.ShapeDtypeStruct((n, LANES), F32)],
        compiler_params=_params("parallel"),
        name="proj_residual_router",
    )(*a_list, *w_list, bias, x, gate, g, scale, shift, rw_pad, rb_pad)


def _rope(xn, cos_t, sin_t):
    return xn * cos_t + pltpu.roll(xn, LANES // 2, axis=1) * sin_t


def _mla_prep_kernel(z_ref, halo_ref, pw_ref, ps_ref, cqg_ref, wuq_ref, ckvg_ref, wuk_ref, wuv_ref,
                     qg_ref, kg_ref, cos_ref, sin_ref,
                     yp_ref, q_ref, k_ref, v_ref, ext_ref, *, tm, per):
    si = pl.program_id(0) % per
    u = z_ref[:, 0:POOL_WIDTH]
    ext_ref[0:POOL_HALO, :] = jnp.where(si == 0, 0.0, halo_ref[...])
    ext_ref[POOL_HALO:, :] = u
    t = si * tm + lax.broadcasted_iota(jnp.int32, (tm, 1), 0)
    for g, w in enumerate(POOL_WINDOWS):
        cols = slice(g * POOL_GROUP_DIM, (g + 1) * POOL_GROUP_DIM)
        ug = u[:, cols]
        s = ug
        for j in range(1, w):
            s = s + ext_ref[POOL_HALO - j:POOL_HALO - j + tm, cols]
        cnt = jnp.minimum(t + 1, w).astype(F32)
        pooled = s / cnt - ug
        yp = jnp.dot(pooled.astype(BF16), pw_ref[g], preferred_element_type=F32) * ps_ref[:, cols]
        yp_ref[:, cols] = yp.astype(yp_ref.dtype)

    cos_t, sin_t = cos_ref[...], sin_ref[...]
    inv_head = 1.0 / QK_HEAD_DIM

    cq = z_ref[:, Z_CQ:Z_CKV]
    cqn = cq * lax.rsqrt(jnp.mean(cq * cq, axis=-1, keepdims=True) + EPS) * cqg_ref[...]
    qf = jnp.dot(cqn.astype(BF16), wuq_ref[...], preferred_element_type=F32)
    q_scale = QK_HEAD_DIM ** -0.5 * LOG2_E
    for h in range(MLA_HEADS):
        qh = qf[:, h * LANES:(h + 1) * LANES]
        ss = jnp.sum(qh * qh, axis=-1, keepdims=True) * inv_head
        qn = qh * lax.rsqrt(ss + EPS) * qg_ref[...]
        q_ref[0, h] = (_rope(qn, cos_t, sin_t) * q_scale).astype(q_ref.dtype)

    ckv = z_ref[:, Z_CKV:Z_ROPE]
    ckvn = (ckv * lax.rsqrt(jnp.mean(ckv * ckv, axis=-1, keepdims=True) + EPS) * ckvg_ref[...]).astype(BF16)
    kf = jnp.dot(ckvn, wuk_ref[...], preferred_element_type=F32)
    vf = jnp.dot(ckvn, wuv_ref[...], preferred_element_type=F32)
    k_rope = z_ref[:, Z_ROPE:Z_WIDTH]
    ones_lane = lax.broadcasted_iota(jnp.int32, (tm, LANES), 1) == V_HEAD_DIM
    for h in range(MLA_HEADS):
        kh = kf[:, h * LANES:(h + 1) * LANES] + k_rope
        ss = jnp.sum(kh * kh, axis=-1, keepdims=True) * inv_head
        kn = kh * lax.rsqrt(ss + EPS) * kg_ref[...]
        k_ref[0, h] = _rope(kn, cos_t, sin_t).astype(k_ref.dtype)
        v_ref[0, h] = jnp.where(ones_lane, 1.0, vf[:, h * LANES:(h + 1) * LANES]).astype(v_ref.dtype)


def mla_prep(z, pool_w, pool_scale, cq_g, wuq_pad, ckv_g, wuk_pad, wuv_pad, qg_pad, kg_pad,
             cos_t, sin_t, batch, seq, tm=512):
    n = z.shape[0]
    per = seq // tm
    hb = tm // POOL_HALO
    full = lambda a: pl.BlockSpec(a.shape, lambda i: (0,) * a.ndim)
    tab = pl.BlockSpec((tm, LANES), lambda i: (i, 0))
    head_out = pl.BlockSpec((1, MLA_HEADS, tm, LANES), lambda i: (i // per, 0, i % per, 0))
    head_shape = jax.ShapeDtypeStruct((batch, MLA_HEADS, seq, LANES), BF16)
    return pl.pallas_call(
        functools.partial(_mla_prep_kernel, tm=tm, per=per),
        grid=(n // tm,),
        in_specs=[pl.BlockSpec((tm, Z_WIDTH), lambda i: (i, 0)),
                  pl.BlockSpec((POOL_HALO, POOL_WIDTH), lambda i: (jnp.maximum(i * hb - 1, 0), 0)),
                  full(pool_w), full(pool_scale), full(cq_g), full(wuq_pad), full(ckv_g), full(wuk_pad),
                  full(wuv_pad), full(qg_pad), full(kg_pad), tab, tab],
        out_specs=[pl.BlockSpec((tm, POOL_WIDTH), lambda i: (i, 0)), head_out, head_out, head_out],
        out_shape=[jax.ShapeDtypeStruct((n, POOL_WIDTH), BF16), head_shape, head_shape, head_shape],
        scratch_shapes=[pltpu.VMEM((tm + POOL_HALO, POOL_WIDTH), F32)],
        compiler_params=_params("parallel"),
        name="mla_prep",
    )(z, z, pool_w, pool_scale, cq_g, wuq_pad, ckv_g, wuk_pad, wuv_pad, qg_pad, kg_pad, cos_t, sin_t)


def _flash_kernel(q_ref, k_ref, v_ref, o_ref, m_ref, acc_ref, *, tq, tk, heads):
    qi = pl.program_id(2)
    sub = tq // tk
    slabs = tk // LANES
    q_chunk = lax.broadcasted_iota(jnp.int32, (tq, tk), 0) // CHUNK
    k_chunk = lax.broadcasted_iota(jnp.int32, (tq, tk), 1) // CHUNK
    m_ref[...] = jnp.full(m_ref.shape, MASK_VALUE, F32)
    acc_ref[...] = jnp.zeros(acc_ref.shape, F32)

    def step(j, mask):
        start = pl.multiple_of(j * tk, tk)
        for hh in range(heads):
            k = k_ref[0, hh, pl.ds(start, tk), :]
            v = v_ref[0, hh, pl.ds(start, tk), :]
            s = lax.dot_general(q_ref[0, hh], k, (((1,), (1,)), ((), ())), preferred_element_type=F32)
            if mask is not None:
                s = jnp.where(mask, s, MASK_VALUE)
            cols_s = [s[:, c * LANES:(c + 1) * LANES] for c in range(slabs)]
            s_max = cols_s[0]
            for sc in cols_s[1:]:
                s_max = jnp.maximum(s_max, sc)
            m_prev = m_ref[hh]
            m_new = jnp.maximum(m_prev, jnp.max(s_max, axis=-1, keepdims=True))
            alpha = jnp.exp2(m_prev - m_new)
            p = jnp.concatenate([jnp.exp2(sc - m_new) for sc in cols_s], axis=1).astype(v.dtype)
            acc_ref[hh] = alpha * acc_ref[hh] + jnp.dot(p, v, preferred_element_type=F32)
            m_ref[hh] = m_new

    def full_step(j, carry):
        step(j, None)
        return carry

    lax.fori_loop(0, sub * qi, full_step, 0)
    for dd in range(sub):
        step(sub * qi + dd, k_chunk + dd * (tk // CHUNK) <= q_chunk)
    outs = []
    for hh in range(heads):
        acc = acc_ref[hh]
        outs.append((acc / acc[:, V_HEAD_DIM:V_HEAD_DIM + 1])[:, :V_HEAD_DIM])
    o_ref[0] = jnp.concatenate(outs, axis=-1).astype(o_ref.dtype)


def flash_attention(q, k, v, tq=1024, tk=512):
    b, h, s, _ = q.shape
    heads = LANES // V_HEAD_DIM
    return pl.pallas_call(
        functools.partial(_flash_kernel, tq=tq, tk=tk, heads=heads),
        grid=(b, h // heads, s // tq),
        in_specs=[pl.BlockSpec((1, heads, tq, LANES), lambda bi, hi, qi: (bi, hi, qi, 0)),
                  pl.BlockSpec((1, heads, s, LANES), lambda bi, hi, qi: (bi, hi, 0, 0)),
                  pl.BlockSpec((1, heads, s, LANES), lambda bi, hi, qi: (bi, hi, 0, 0))],
        out_specs=pl.BlockSpec((1, tq, LANES), lambda bi, hi, qi: (bi, qi, hi)),
        out_shape=jax.ShapeDtypeStruct((b, s, h * V_HEAD_DIM), BF16),
        scratch_shapes=[pltpu.VMEM((heads, tq, LANES), F32) for _ in range(2)],
        compiler_params=_params("parallel", "parallel", "parallel"),
        name="flash_attention",
    )(q, k, v)


def _dwconv_kernel(u_ref, halo_ref, w_ref, b_ref, g_ref, beta_ref, o_ref, ext_ref, sh_ref, conv_ref, *, tm, per):
    si = pl.program_id(0) % per
    d = u_ref.shape[1]
    ext_ref[0:CONV_HALO, :] = jnp.where(si == 0, 0.0, halo_ref[...])
    ext_ref[CONV_HALO:, :] = u_ref[...]
    span = sh_ref.shape[1]
    for b in range(1, SUBLANES):
        sh_ref[b - 1] = ext_ref[b:b + span, :]
    first = CONV_HALO - (CONV_WIDTH - 1)
    for r0 in range(0, tm, CONV_PATCH_ROWS):
        for c0 in range(0, d, CONV_PATCH_COLS):
            cols = slice(c0, c0 + CONV_PATCH_COLS)
            view = (CONV_PATCH_ROWS // SUBLANES, SUBLANES, CONV_PATCH_COLS)
            patch = jnp.broadcast_to(b_ref[:, cols], view)
            for j in range(CONV_WIDTH):
                a, b = divmod(first + j, SUBLANES)
                rows = slice(SUBLANES * a + r0, SUBLANES * a + r0 + CONV_PATCH_ROWS)
                win = ext_ref[rows, cols] if b == 0 else sh_ref[b - 1, rows, cols]
                tap = jnp.broadcast_to(w_ref[j:j + 1, cols], view[1:])
                patch = patch + tap[None] * win.reshape(view)
            conv_ref[r0:r0 + CONV_PATCH_ROWS, cols] = patch.reshape(CONV_PATCH_ROWS, CONV_PATCH_COLS)
    acc = conv_ref[...]
    mu = jnp.mean(acc, axis=-1, keepdims=True)
    cen = acc - mu
    var = jnp.mean(cen * cen, axis=-1, keepdims=True)
    y = cen * lax.rsqrt(var + EPS) * g_ref[...] + beta_ref[...]
    o_ref[...] = (y * jax.nn.sigmoid(y)).astype(o_ref.dtype)


def dwconv_ln_silu(u, w_pad, b, g, beta, seq, tm=256):
    n, d = u.shape
    per = seq // tm
    hb = tm // CONV_HALO
    full = lambda a: pl.BlockSpec(a.shape, lambda i: (0, 0))
    return pl.pallas_call(
        functools.partial(_dwconv_kernel, tm=tm, per=per),
        grid=(n // tm,),
        in_specs=[pl.BlockSpec((tm, d), lambda i: (i, 0)),
                  pl.BlockSpec((CONV_HALO, d), lambda i: (jnp.maximum(i * hb - 1, 0), 0)),
                  full(w_pad), full(b), full(g), full(beta)],
        out_specs=pl.BlockSpec((tm, d), lambda i: (i, 0)),
        out_shape=jax.ShapeDtypeStruct((n, d), BF16),
        scratch_shapes=[pltpu.VMEM((tm + CONV_HALO, d), F32),
                        pltpu.VMEM((SUBLANES - 1, tm + CONV_HALO - SUBLANES, d), F32),
                        pltpu.VMEM((tm, d), F32)],
        compiler_params=_params("parallel"),
        name="dwconv_ln_silu",
    )(u, u, w_pad, b, g, beta)


def _router_kernel(lg_ref, idx_ref, gate_ref, cnt_ref, carry_ref, *, tm):
    @pl.when(pl.program_id(0) == 0)
    def _():
        carry_ref[...] = jnp.zeros(carry_ref.shape, F32)

    lane = lax.broadcasted_iota(jnp.int32, (tm, LANES), 1)
    lane_f = lane.astype(F32)
    neg = -jnp.inf
    logits = jnp.where(lane < N_EXPERTS, lg_ref[...], neg)
    picks, vals, ids = [], [], []
    for _ in range(TOP_K):
        mx = jnp.max(logits, axis=-1, keepdims=True)
        idx = jnp.min(jnp.where(logits == mx, lane_f, float(LANES)), axis=-1, keepdims=True)
        pick = lane_f == idx
        picks.append(pick)
        vals.append(mx)
        ids.append(idx)
        logits = jnp.where(pick, neg, logits)
    exps = [jnp.exp(v - vals[0]) for v in vals]
    den = exps[0]
    for e in exps[1:]:
        den = den + e

    chosen = jnp.zeros((tm, LANES), F32)
    for pick in picks:
        chosen = chosen + pick.astype(F32)
    r_io = lax.broadcasted_iota(jnp.int32, (tm, tm), 0)
    c_io = lax.broadcasted_iota(jnp.int32, (tm, tm), 1)
    earlier = (c_io < r_io).astype(BF16)
    before = jnp.dot(earlier, chosen.astype(BF16), preferred_element_type=F32) + carry_ref[0:1, :]
    idx_out = jnp.zeros((tm, LANES), F32)
    gate_out = jnp.zeros((tm, LANES), F32)
    for k in range(TOP_K):
        rank = jnp.sum(jnp.where(picks[k], before, 0.0), axis=-1, keepdims=True)
        idx_out = jnp.where(lane == k, ids[k], idx_out)
        idx_out = jnp.where(lane == TOP_K + k, rank, idx_out)
        gate_out = jnp.where(lane == k, exps[k] / den, gate_out)
    idx_ref[...] = idx_out.astype(jnp.int32)
    gate_ref[...] = gate_out
    total = carry_ref[0:1, :] + jnp.sum(chosen, axis=0, keepdims=True)
    carry_ref[...] = jnp.broadcast_to(total, carry_ref.shape)
    cnt_ref[...] = jnp.broadcast_to(total, cnt_ref.shape).astype(jnp.int32)


def router(logits, tm=512):
    n = logits.shape[0]
    row = pl.BlockSpec((tm, LANES), lambda i: (i, 0))
    return pl.pallas_call(
        functools.partial(_router_kernel, tm=tm),
        grid=(n // tm,),
        in_specs=[row],
        out_specs=[row, row, pl.BlockSpec((8, LANES), lambda i: (0, 0))],
        out_shape=[jax.ShapeDtypeStruct((n, LANES), jnp.int32), jax.ShapeDtypeStruct((n, LANES), F32),
                   jax.ShapeDtypeStruct((8, LANES), jnp.int32)],
        scratch_shapes=[pltpu.VMEM((8, LANES), F32)],
        compiler_params=_params("arbitrary"),
        name="router",
    )(logits)


def _sc_worker_base(per_worker):
    return (lax.axis_index("s") * SC_CORES + lax.axis_index("c")) * per_worker


def sc_scatter_rows(src, dest_flat, n_rows):
    n, d = src.shape
    per_w = n // SC_WORKERS
    n_chunks = per_w // SC_CHUNK
    mesh = plsc.VectorSubcoreMesh(core_axis_name="c", subcore_axis_name="s")

    @functools.partial(
        pl.kernel, out_type=jax.ShapeDtypeStruct((n_rows, d), src.dtype), mesh=mesh,
        scratch_types=[pltpu.VMEM((per_w,), jnp.int32) for _ in range(TOP_K)]
        + [pltpu.VMEM((SC_CHUNK, d), src.dtype) for _ in range(2)] + [pltpu.SemaphoreType.DMA] * 4,
        name="sc_scatter_rows")
    def scatter(src_hbm, dest_hbm, out_hbm, *scratch):
        idx_refs = scratch[:TOP_K]
        bufs = scratch[TOP_K:TOP_K + 2]
        in_sems, out_sems = scratch[TOP_K + 2:TOP_K + 4], scratch[TOP_K + 4:TOP_K + 6]
        base = _sc_worker_base(per_w)
        for k, idx_ref in enumerate(idx_refs):
            pltpu.sync_copy(dest_hbm.at[pl.ds(k * n + base, per_w)], idx_ref)

        def load(j, slot):
            return pltpu.make_async_copy(src_hbm.at[pl.ds(base + j * SC_CHUNK, SC_CHUNK)], bufs[slot], in_sems[slot])

        def store_all(j, slot):
            copies = [pltpu.make_async_copy(bufs[slot], out_hbm.at[idx_ref.at[pl.ds(j * SC_CHUNK, SC_CHUNK)]],
                                            out_sems[slot]) for idx_ref in idx_refs]
            for cp in copies:
                cp.start()
            for cp in copies:
                cp.wait()

        load(0, 0).start()

        @pl.loop(0, n_chunks // 2)
        def _(p):
            j = 2 * p
            load(j + 1, 1).start()
            load(j, 0).wait()
            store_all(j, 0)

            @pl.when(j + 2 < n_chunks)
            def _():
                load(j + 2, 0).start()

            load(j + 1, 1).wait()
            store_all(j + 1, 1)

    return scatter(src, dest_flat)


def sc_gather_rows(table, idx):
    b = idx.shape[0]
    d = table.shape[1]
    per_w = b // SC_WORKERS
    n_chunks = per_w // SC_CHUNK
    mesh = plsc.VectorSubcoreMesh(core_axis_name="c", subcore_axis_name="s")

    @functools.partial(
        pl.kernel, out_type=jax.ShapeDtypeStruct((b, d), table.dtype), mesh=mesh,
        scratch_types=[pltpu.VMEM((per_w,), jnp.int32)] + [pltpu.VMEM((SC_CHUNK, d), table.dtype) for _ in range(2)]
        + [pltpu.SemaphoreType.DMA] * 4,
        name="sc_gather_rows")
    def gather(table_hbm, idx_hbm, out_hbm, idx_ref, buf0, buf1, gsem0, gsem1, osem0, osem1):
        bufs, in_sems, out_sems = (buf0, buf1), (gsem0, gsem1), (osem0, osem1)
        base = _sc_worker_base(per_w)
        pltpu.sync_copy(idx_hbm.at[pl.ds(base, per_w)], idx_ref)

        def fetch(j, slot):
            return pltpu.make_async_copy(table_hbm.at[idx_ref.at[pl.ds(j * SC_CHUNK, SC_CHUNK)]], bufs[slot],
                                         in_sems[slot])

        def store(j, slot):
            return pltpu.make_async_copy(bufs[slot], out_hbm.at[pl.ds(base + j * SC_CHUNK, SC_CHUNK)],
                                         out_sems[slot])

        fetch(0, 0).start()

        @pl.loop(0, n_chunks // 2)
        def _(p):
            j = 2 * p

            @pl.when(p > 0)
            def _():
                store(j - 1, 1).wait()

            fetch(j + 1, 1).start()
            fetch(j, 0).wait()
            store(j, 0).start()
            fetch(j + 1, 1).wait()
            store(j, 0).wait()

            @pl.when(j + 2 < n_chunks)
            def _():
                fetch(j + 2, 0).start()

            store(j + 1, 1).start()

        store(n_chunks - 1, 1).wait()

    return gather(table, idx)


def _ffn_kernel(be_ref, nv_ref, nu_ref, xs_ref, w1_ref, b1g_ref, b1u_ref, w2_ref, b2_ref, y_ref,
                wg_ref, wu_ref, w2s_ref):
    i = pl.program_id(0)
    pair = 2 * LANES

    @pl.when(i < nu_ref[0])
    def _():
        e = be_ref[i]
        prev = be_ref[jnp.maximum(i - 1, 0)]

        @pl.when((i == 0) | (e != prev))
        def _():
            r_io = lax.broadcasted_iota(jnp.int32, (pair, pair), 0)
            c_io = lax.broadcasted_iota(jnp.int32, (pair, pair), 1)
            want = jnp.where(c_io < LANES, 2 * c_io, 2 * (c_io - LANES) + 1)
            sel = (r_io == want).astype(BF16)
            for c in range(wg_ref.shape[1] // LANES):
                slab = w1_ref[0, 0, :, c * pair:(c + 1) * pair].astype(BF16)
                split = jnp.dot(slab, sel, preferred_element_type=F32)
                wg_ref[:, c * LANES:(c + 1) * LANES] = split[:, :LANES].astype(BF16)
                wu_ref[:, c * LANES:(c + 1) * LANES] = split[:, LANES:].astype(BF16)
            w2s_ref[...] = w2_ref[0, 0].astype(BF16)

        nvalid = nv_ref[i]

        def ffn_rows(rows):
            row = lax.broadcasted_iota(jnp.int32, (rows, xs_ref.shape[1]), 0)
            lo, hi = _unpack_halves(jnp.where(row < nvalid, xs_ref[0:rows, :], 0))
            x = jnp.concatenate([lo, hi], axis=1).astype(BF16)
            hg = jnp.dot(x, wg_ref[...], preferred_element_type=F32) + b1g_ref[0]
            hu = jnp.dot(x, wu_ref[...], preferred_element_type=F32) + b1u_ref[0]
            gate = jnp.minimum(hg, SWIGLU_LIMIT)
            up = jnp.clip(hu, -SWIGLU_LIMIT, SWIGLU_LIMIT)
            act = gate * jax.nn.sigmoid(SWIGLU_ALPHA * gate) * (up + 1.0)
            y = jnp.dot(act.astype(BF16), w2s_ref[...], preferred_element_type=F32) + b2_ref[0]
            y_ref[0:rows, :] = _pack_halves(y)
            if rows < y_ref.shape[0]:
                y_ref[rows:, :] = jnp.zeros((y_ref.shape[0] - rows, y_ref.shape[1]), y_ref.dtype)

        n_sub = y_ref.shape[0] // MOE_BLOCK
        for sub in range(1, n_sub + 1):
            lo_rows, hi_rows = (sub - 1) * MOE_BLOCK, sub * MOE_BLOCK
            pl.when((nvalid > lo_rows) & (nvalid <= hi_rows))(functools.partial(ffn_rows, hi_rows))

    @pl.when(i >= nu_ref[0])
    def _():
        y_ref[...] = jnp.zeros(y_ref.shape, y_ref.dtype)


def moe_ffn(xs, block_e, block_valid, n_used, layer, w1, b1g, b1u, w2, b2):
    n_rows, half = xs.shape
    d = 2 * half
    n_blocks = n_rows // MOE_STEP
    n_exp, f2 = w1.shape[1], w1.shape[3]
    f = f2 // 2
    rows = lambda i, be, nv, nu: (jnp.minimum(i, nu[0] - 1), 0)
    vec = lambda width: pl.BlockSpec((1, 1, width), lambda i, be, nv, nu: (be[i], 0, 0))
    grid_spec = pltpu.PrefetchScalarGridSpec(
        num_scalar_prefetch=3,
        grid=(n_blocks,),
        in_specs=[
            pl.BlockSpec((MOE_STEP, half), rows),
            pl.BlockSpec((1, 1, d, f2), lambda i, be, nv, nu: (layer, be[i], 0, 0)),
            vec(f), vec(f),
            pl.BlockSpec((1, 1, f, d), lambda i, be, nv, nu: (layer, be[i], 0, 0)),
            vec(d),
        ],
        out_specs=pl.BlockSpec((MOE_STEP, half), lambda i, be, nv, nu: (i, 0)),
        scratch_shapes=[pltpu.VMEM((d, f), BF16), pltpu.VMEM((d, f), BF16), pltpu.VMEM((f, d), BF16)],
    )
    return pl.pallas_call(
        _ffn_kernel,
        grid_spec=grid_spec,
        out_shape=jax.ShapeDtypeStruct((n_rows, half), jnp.int32),
        compiler_params=_params("arbitrary"),
        name="moe_ffn",
    )(block_e, block_valid, n_used, xs, w1, b1g.reshape(n_exp, 1, f), b1u.reshape(n_exp, 1, f), w2,
      b2.reshape(n_exp, 1, d))


def _combine_kernel(x_ref, y_ref, gates_ref, gate_ref, *rest, with_norm):
    g = gates_ref[...]
    half = y_ref.shape[2]
    acc_lo = acc_hi = None
    for k in range(TOP_K):
        lo, hi = _unpack_halves(y_ref[k])
        gk = g[:, k:k + 1]
        acc_lo = gk * lo if acc_lo is None else acc_lo + gk * lo
        acc_hi = gk * hi if acc_hi is None else acc_hi + gk * hi
    x_lo = x_ref[:, :half] + gate_ref[0, :, :half] * acc_lo
    x_hi = x_ref[:, half:] + gate_ref[0, :, half:] * acc_hi
    if with_norm:
        ng_ref, sc_ref, sh_ref, o_ref, h_ref = rest
        x_new = jnp.concatenate([x_lo, x_hi], axis=1)
        o_ref[...] = x_new
        h_ref[...] = _modulated_norm(x_new, ng_ref[...], sc_ref[0], sh_ref[0]).astype(h_ref.dtype)
    else:
        (o_ref,) = rest
        o_ref[:, :half] = x_lo
        o_ref[:, half:] = x_hi


def moe_combine(x, y, gates, gate_mod, seq, next_norm=None, tm=512):
    n, d = x.shape
    per = seq // tm
    rows = lambda width: pl.BlockSpec((tm, width), lambda i: (i, 0))
    vec = pl.BlockSpec((1, 1, d), lambda i: (i // per, 0, 0))
    in_specs = [rows(d), pl.BlockSpec((TOP_K, tm, d // 2), lambda i: (0, i, 0)), rows(LANES), vec]
    args = [x, y, gates, gate_mod]
    out_specs, out_shape = rows(d), jax.ShapeDtypeStruct((n, d), F32)
    if next_norm is not None:
        in_specs += [pl.BlockSpec((1, d), lambda i: (0, 0)), vec, vec]
        args += list(next_norm)
        out_specs, out_shape = [out_specs, rows(d)], [out_shape, jax.ShapeDtypeStruct((n, d), BF16)]
    return pl.pallas_call(
        functools.partial(_combine_kernel, with_norm=next_norm is not None),
        grid=(n // tm,),
        in_specs=in_specs,
        out_specs=out_specs,
        out_shape=out_shape,
        compiler_params=_params("parallel"),
        name="moe_combine",
    )(*args)


def moe_layer(x, mixer, mix_gate, mods, norm_g, router_w, router_b, layer, w1, b1, w2, b2, seq, next_norm):
    n, d = x.shape
    shift, scale, gate = mods
    rw_pad = jnp.pad(router_w, ((0, 0), (0, LANES - N_EXPERTS)))
    rb_pad = jnp.pad(router_b, (0, LANES - N_EXPERTS)).reshape(1, LANES)
    x, h, logits = proj_residual_router(*mixer, x, mix_gate, norm_g.reshape(1, d), scale, shift, rw_pad, rb_pad, seq)
    idx, gates, counts = router(logits)

    top_i = idx[:, :TOP_K]
    rank = idx[:, TOP_K:2 * TOP_K]
    counts = counts[0, :N_EXPERTS]
    experts = jnp.arange(N_EXPERTS, dtype=jnp.int32)
    padded = (counts + MOE_STEP - 1) // MOE_STEP * MOE_STEP
    pad_ends = jnp.sum(jnp.where(experts[:, None] >= experts[None, :], padded[None, :], 0), axis=1)
    pad_starts = pad_ends - padded
    dest = jnp.sum(jnp.where(top_i[..., None] == experts, pad_starts, 0), axis=-1) + rank
    n_blocks = -(-n * TOP_K // MOE_STEP) + N_EXPERTS
    block_start = jnp.arange(n_blocks, dtype=jnp.int32) * MOE_STEP
    block_e = jnp.minimum(jnp.sum((pad_ends[None, :] <= block_start[:, None]).astype(jnp.int32), axis=1),
                          N_EXPERTS - 1)
    n_used = (pad_ends[N_EXPERTS - 1:] // MOE_STEP).astype(jnp.int32)
    seg_end = jnp.sum(jnp.where(block_e[:, None] == experts, pad_starts + counts, 0), axis=1)
    block_valid = jnp.clip(seg_end - block_start, 0, MOE_STEP).astype(jnp.int32)
    dest_flat = dest.T.reshape(-1)

    xs = sc_scatter_rows(h, dest_flat, n_blocks * MOE_STEP)
    ys = moe_ffn(xs, block_e, block_valid, n_used, layer, w1, b1[layer][:, 0::2], b1[layer][:, 1::2], w2, b2[layer])
    y = sc_gather_rows(ys, dest_flat).reshape(TOP_K, n, d // 2)
    out = moe_combine(x, y, gates, gate, seq, next_norm)
    return out if next_norm is not None else (out, None)


def _pad_heads(w, width):
    k = w.shape[0]
    w = w.reshape(k, MLA_HEADS, width)
    return jnp.pad(w, ((0, 0), (0, 0), (0, LANES - width))).reshape(k, MLA_HEADS * LANES)


def _head_lane_source():
    half = QK_ROPE_DIM // 2
    first_nope = LANES // 2 - half
    lanes = (list(range(QK_NOPE_DIM, QK_NOPE_DIM + half)) + list(range(first_nope))
             + list(range(QK_NOPE_DIM + half, QK_HEAD_DIM)) + list(range(first_nope, QK_NOPE_DIM)))
    return jnp.array(lanes + [QK_HEAD_DIM] * (LANES - QK_HEAD_DIM), jnp.int32)


def _to_head_lanes(w):
    w = jnp.concatenate([w, jnp.zeros(w.shape[:-1] + (1,), w.dtype)], axis=-1)
    return jnp.take(w, _head_lane_source(), axis=-1)


def _rope_tables(positions):
    inv = 1.0 / (ROPE_THETA ** (jnp.arange(0, QK_ROPE_DIM, 2, dtype=F32) / QK_ROPE_DIM))
    ang = positions.reshape(-1).astype(F32)[:, None] * inv
    cos, sin = jnp.cos(ang), jnp.sin(ang)
    n = ang.shape[0]
    cos_t = _to_head_lanes(jnp.concatenate([jnp.ones((n, QK_NOPE_DIM), F32), cos, cos], axis=1))
    sin_t = _to_head_lanes(jnp.concatenate([jnp.zeros((n, QK_NOPE_DIM), F32), -sin, sin], axis=1))
    return cos_t, sin_t


def _split_mods(m, batch):
    d = m.shape[-1] // 3
    m = m[:batch]
    return tuple(m[:, None, j * d:(j + 1) * d] for j in range(3))


def pool_mla_mixer(x, h, mods, norm_g, tables, w_in, pool_w, pool_scale, cq_norm_g, w_uq, ckv_norm_g, w_ukv,
                   q_norm_g, k_norm_g, w_out, batch, seq):
    n, d = x.shape
    shift, scale, _ = mods
    rope_cols = _to_head_lanes(jnp.pad(w_in[:, Z_ROPE:], ((0, 0), (QK_NOPE_DIM, 0))))
    w_in_pad = jnp.concatenate([w_in[:, :Z_ROPE], rope_cols], axis=1).astype(BF16)
    if h is None:
        z = norm_matmul(x, norm_g.reshape(1, d), scale, shift, w_in_pad, seq)
    else:
        z = matmul(h, w_in_pad)
    w_ukv_h = w_ukv.reshape(KV_LORA_RANK, MLA_HEADS, QK_NOPE_DIM + V_HEAD_DIM)
    k_nope = jnp.pad(w_ukv_h[:, :, :QK_NOPE_DIM], ((0, 0), (0, 0), (0, QK_ROPE_DIM)))
    wuk_pad = _to_head_lanes(k_nope).reshape(KV_LORA_RANK, MLA_HEADS * LANES).astype(BF16)
    wuv_pad = _pad_heads(w_ukv_h[:, :, QK_NOPE_DIM:].reshape(KV_LORA_RANK, -1), V_HEAD_DIM).astype(BF16)
    wuq_pad = _to_head_lanes(w_uq.reshape(Q_LORA_RANK, MLA_HEADS, QK_HEAD_DIM))
    wuq_pad = wuq_pad.reshape(Q_LORA_RANK, MLA_HEADS * LANES).astype(BF16)
    pad_g = lambda g: _to_head_lanes(g).reshape(1, LANES)
    y_pool, q, k, v = mla_prep(z, pool_w.astype(BF16), pool_scale.reshape(1, -1), cq_norm_g.reshape(1, -1), wuq_pad,
                               ckv_norm_g.reshape(1, -1), wuk_pad, wuv_pad, pad_g(q_norm_g), pad_g(k_norm_g),
                               *tables, batch, seq)
    y_att = flash_attention(q, k, v).reshape(n, MLA_HEADS * V_HEAD_DIM)
    w_out_b = w_out.astype(BF16)
    return [y_pool, y_att], [w_out_b[:POOL_WIDTH], w_out_b[POOL_WIDTH:]], jnp.zeros((1, d), F32)


def conformer_mixer(h, pw1_w, pw1_b, dw_w, dw_b, ln_g, ln_b, pw2_w, pw2_b, seq):
    d = pw2_w.shape[1]
    cd = pw1_w.shape[1] // 2
    pw1 = pw1_w.astype(BF16)
    u = glu_matmul(h, pw1[:, :cd], pw1[:, cd:], pw1_b[:cd].reshape(1, cd), pw1_b[cd:].reshape(1, cd))
    w_pad = jnp.pad(dw_w, ((0, CONV_HALO - CONV_WIDTH), (0, 0)))
    u = dwconv_ln_silu(u, w_pad, dw_b.reshape(1, cd), ln_g.reshape(1, cd), ln_b.reshape(1, cd), seq)
    return [u], [pw2_w.astype(BF16)], pw2_b.reshape(1, d)


def kernel(x, c, positions, ada_mix_w, ada_mix_b, norm_mix_g, w_in, pool_w, pool_scale, cq_norm_g, w_uq,
           ckv_norm_g, w_ukv, q_norm_g, k_norm_g, w_out, conv_pw1_w, conv_pw1_b, conv_dw_w, conv_dw_b,
           conv_ln_g, conv_ln_b, conv_pw2_w, conv_pw2_b, ada_ffn_w, ada_ffn_b, norm_ffn_g, router_w,
           router_b, moe_w1, moe_b1, moe_w2, moe_b2):
    batch, seq, d = x.shape
    depth = ada_mix_w.shape[0]
    c_pad = jnp.pad(c, ((0, 8 - batch), (0, 0)))
    mix_mods = adaln(c_pad, ada_mix_w, ada_mix_b)
    ffn_mods = adaln(c_pad, ada_ffn_w, ada_ffn_b)
    tables = _rope_tables(positions)
    xf = x.reshape(batch * seq, d)
    h = None
    for layer in range(depth):
        i = layer // 2
        mods = _split_mods(mix_mods[layer], batch)
        if layer % 2 == 0:
            mixer = pool_mla_mixer(xf, h, mods, norm_mix_g[layer], tables, w_in[i], pool_w[i], pool_scale[i],
                                   cq_norm_g[i], w_uq[i], ckv_norm_g[i], w_ukv[i], q_norm_g[i], k_norm_g[i],
                                   w_out[i], batch, seq)
        else:
            mixer = conformer_mixer(h, conv_pw1_w[i], conv_pw1_b[i], conv_dw_w[i], conv_dw_b[i],
                                    conv_ln_g[i], conv_ln_b[i], conv_pw2_w[i], conv_pw2_b[i], seq)
        next_norm = None
        if layer + 1 < depth:
            n_shift, n_scale, _ = _split_mods(mix_mods[layer + 1], batch)
            next_norm = (norm_mix_g[layer + 1].reshape(1, d), n_scale, n_shift)
        xf, h = moe_layer(xf, mixer, mods[2], _split_mods(ffn_mods[layer], batch), norm_ffn_g[layer],
                          router_w[layer], router_b[layer], layer, moe_w1, moe_b1, moe_w2, moe_b2, seq, next_norm)
    return xf.reshape(batch, seq, d)
```

```python
import functools

import jax
import jax.numpy as jnp
from jax import lax
from jax.experimental import pallas as pl
from jax.experimental.pallas import tpu as pltpu
from jax.experimental.pallas import tpu_sc as plsc

F32 = jnp.float32
BF16 = jnp.bfloat16
HIGHEST = lax.Precision.HIGHEST

EPS = 1e-6
POOL_WINDOWS = (2, 4, 8, 16)
POOL_GROUP_DIM = 128
POOL_WIDTH = POOL_GROUP_DIM * len(POOL_WINDOWS)
MLA_HEADS = 8
QK_NOPE_DIM = 64
QK_ROPE_DIM = 32
QK_HEAD_DIM = QK_NOPE_DIM + QK_ROPE_DIM
V_HEAD_DIM = 64
Q_LORA_RANK = 384
KV_LORA_RANK = 256
ROPE_THETA = 10000.0
CHUNK = 64
CONV_WIDTH = 31
N_EXPERTS = 32
TOP_K = 4
SWIGLU_ALPHA = 1.702
SWIGLU_LIMIT = 7.0
MOE_BLOCK = 256
MOE_STEP = 512

LANES = 128
SUBLANES = 8
CONV_PATCH_ROWS = 64
CONV_PATCH_COLS = 256
POOL_HALO = 16
CONV_HALO = 32
MASK_VALUE = -1e30
LOG2_E = 1.4426950408889634
VMEM_LIMIT = 52 * 1024 * 1024
SC_CORES = 2
SC_SUBCORES = 16
SC_WORKERS = SC_CORES * SC_SUBCORES
SC_CHUNK = 64

Z_CQ = POOL_WIDTH
Z_CKV = Z_CQ + Q_LORA_RANK
Z_ROPE = Z_CKV + KV_LORA_RANK
Z_WIDTH = Z_ROPE + LANES


def _params(*sem, vmem=None):
    return pltpu.CompilerParams(dimension_semantics=sem, vmem_limit_bytes=vmem or VMEM_LIMIT,
                                disable_bounds_checks=True)


def _adaln_kernel(c_ref, w_ref, b_ref, o_ref):
    c = c_ref[...]
    s = c * jax.nn.sigmoid(c)
    o_ref[0] = jnp.dot(s, w_ref[0], preferred_element_type=F32, precision=HIGHEST) + b_ref[0]


def adaln(c_pad, w, b):
    n_l, d, d3 = w.shape
    tn = 512
    return pl.pallas_call(
        _adaln_kernel,
        grid=(n_l, d3 // tn),
        in_specs=[pl.BlockSpec((8, d), lambda l, j: (0, 0)),
                  pl.BlockSpec((1, d, tn), lambda l, j: (l, 0, j)),
                  pl.BlockSpec((1, 1, tn), lambda l, j: (l, 0, j))],
        out_specs=pl.BlockSpec((1, 8, tn), lambda l, j: (l, 0, j)),
        out_shape=jax.ShapeDtypeStruct((n_l, 8, d3), F32),
        compiler_params=_params("parallel", "parallel"),
        name="adaln",
    )(c_pad, w, b.reshape(n_l, 1, d3))


def _modulated_norm(x, g, scale, shift):
    ms = jnp.mean(x * x, axis=-1, keepdims=True)
    return x * lax.rsqrt(ms + EPS) * g * (1.0 + scale) + shift


def _pack_halves(x):
    w = x.shape[1] // 2
    lo = lax.bitcast_convert_type(x[:, :w].astype(BF16).astype(F32), jnp.uint32)
    hi = lax.bitcast_convert_type(x[:, w:].astype(BF16).astype(F32), jnp.uint32)
    return lax.bitcast_convert_type((lo >> 16) | (hi & jnp.uint32(0xFFFF0000)), jnp.int32)


def _unpack_halves(p):
    u = lax.bitcast_convert_type(p, jnp.uint32)
    lo = lax.bitcast_convert_type(u << 16, F32)
    hi = lax.bitcast_convert_type(u & jnp.uint32(0xFFFF0000), F32)
    return lo, hi


def _norm_matmul_kernel(x_ref, g_ref, sc_ref, sh_ref, w_ref, o_ref):
    h = _modulated_norm(x_ref[...], g_ref[...], sc_ref[0], sh_ref[0])
    o_ref[...] = jnp.dot(h.astype(BF16), w_ref[...], preferred_element_type=F32)


def norm_matmul(x, g, scale, shift, w, seq, tm=512):
    n, d = x.shape
    m = w.shape[1]
    per = seq // tm
    vec = pl.BlockSpec((1, 1, d), lambda i: (i // per, 0, 0))
    return pl.pallas_call(
        _norm_matmul_kernel,
        grid=(n // tm,),
        in_specs=[pl.BlockSpec((tm, d), lambda i: (i, 0)), pl.BlockSpec((1, d), lambda i: (0, 0)), vec, vec,
                  pl.BlockSpec((d, m), lambda i: (0, 0))],
        out_specs=pl.BlockSpec((tm, m), lambda i: (i, 0)),
        out_shape=jax.ShapeDtypeStruct((n, m), F32),
        compiler_params=_params("parallel"),
        name="norm_matmul",
    )(x, g, scale, shift, w)


def _matmul_kernel(a_ref, w_ref, o_ref):
    o_ref[...] = jnp.dot(a_ref[...], w_ref[...], preferred_element_type=F32).astype(o_ref.dtype)


def matmul(a, w, out_dtype=F32, tm=512):
    n, k = a.shape
    m = w.shape[1]
    return pl.pallas_call(
        _matmul_kernel,
        grid=(n // tm,),
        in_specs=[pl.BlockSpec((tm, k), lambda i: (i, 0)), pl.BlockSpec((k, m), lambda i: (0, 0))],
        out_specs=pl.BlockSpec((tm, m), lambda i: (i, 0)),
        out_shape=jax.ShapeDtypeStruct((n, m), out_dtype),
        compiler_params=_params("parallel"),
        name="matmul",
    )(a, w)


def _glu_kernel(a_ref, wv_ref, wg_ref, bv_ref, bg_ref, o_ref):
    a = a_ref[...]
    val = jnp.dot(a, wv_ref[...], preferred_element_type=F32) + bv_ref[...]
    gt = jnp.dot(a, wg_ref[...], preferred_element_type=F32) + bg_ref[...]
    o_ref[...] = val * jax.nn.sigmoid(gt)


def glu_matmul(a, wv, wg, bv, bg, tm=512):
    n, k = a.shape
    m = wv.shape[1]
    full = lambda r, c: pl.BlockSpec((r, c), lambda i: (0, 0))
    return pl.pallas_call(
        _glu_kernel,
        grid=(n // tm,),
        in_specs=[pl.BlockSpec((tm, k), lambda i: (i, 0)), full(k, m), full(k, m), full(1, m), full(1, m)],
        out_specs=pl.BlockSpec((tm, m), lambda i: (i, 0)),
        out_shape=jax.ShapeDtypeStruct((n, m), F32),
        compiler_params=_params("parallel"),
        name="glu_matmul",
    )(a, wv, wg, bv, bg)


def _proj_residual_router_kernel(*refs, n_in):
    a_refs, w_refs = refs[:n_in], refs[n_in:2 * n_in]
    b_ref, x_ref, gate_ref, g_ref, sc_ref, sh_ref, rw2_ref, rwh_ref, rb_ref, o_ref, h_ref, lg_ref = refs[2 * n_in:]
    acc = b_ref[...]
    for a_ref, w_ref in zip(a_refs, w_refs):
        acc = acc + jnp.dot(a_ref[...], w_ref[...], preferred_element_type=F32)
    x_new = x_ref[...] + gate_ref[0] * acc
    o_ref[...] = x_new
    h = _modulated_norm(x_new, g_ref[...], sc_ref[0], sh_ref[0])
    h_ref[...] = _pack_halves(h)
    h_hi = h.astype(BF16)
    h_lo = (h - h_hi.astype(F32)).astype(BF16)
    both = jnp.dot(h_hi, rw2_ref[...], preferred_element_type=F32)
    low = jnp.dot(h_lo, rwh_ref[...], preferred_element_type=F32)
    lg_ref[...] = both[:, :LANES] + both[:, LANES:] + low + rb_ref[...]


def proj_residual_router(a_list, w_list, bias, x, gate, g, scale, shift, rw_both, rw_hi, rb_pad, seq, tm=512):
    n, d = x.shape
    per = seq // tm
    n_in = len(a_list)
    rows = lambda width: pl.BlockSpec((tm, width), lambda i: (i, 0))
    const = lambda a: pl.BlockSpec(a.shape, lambda i: (0, 0))
    vec = pl.BlockSpec((1, 1, d), lambda i: (i // per, 0, 0))
    in_specs = [rows(a.shape[1]) for a in a_list] + [const(w) for w in w_list]
    in_specs += [const(bias), rows(d), vec, const(g), vec, vec, const(rw_both), const(rw_hi), const(rb_pad)]
    return pl.pallas_call(
        functools.partial(_proj_residual_router_kernel, n_in=n_in),
        grid=(n // tm,),
        in_specs=in_specs,
        out_specs=[rows(d), rows(d // 2), rows(LANES)],
        out_shape=[jax.ShapeDtypeStruct((n, d), F32), jax.ShapeDtypeStruct((n, d // 2), jnp.int32),
                   jax.ShapeDtypeStruct((n, LANES), F32)],
        compiler_params=_params("parallel"),
        name="proj_residual_router",
    )(*a_list, *w_list, bias, x, gate, g, scale, shift, rw_both, rw_hi, rb_pad)


def _rope(xn, cos_t, sin_t):
    return xn * cos_t + pltpu.roll(xn, LANES // 2, axis=1) * sin_t


def _mla_prep_kernel(z_ref, halo_ref, pw_ref, ps_ref, cqg_ref, wuq_ref, ckvg_ref, wuk_ref, wuv_ref,
                     qg_ref, kg_ref, cos_ref, sin_ref,
                     yp_ref, q_ref, k_ref, v_ref, ext_ref, *, tm, per):
    si = pl.program_id(0) % per
    u = z_ref[:, 0:POOL_WIDTH]
    ext_ref[0:POOL_HALO, :] = jnp.where(si == 0, 0.0, halo_ref[...])
    ext_ref[POOL_HALO:, :] = u
    t = si * tm + lax.broadcasted_iota(jnp.int32, (tm, 1), 0)
    for g, w in enumerate(POOL_WINDOWS):
        cols = slice(g * POOL_GROUP_DIM, (g + 1) * POOL_GROUP_DIM)
        ug = u[:, cols]
        s = ug
        for j in range(1, w):
            s = s + ext_ref[POOL_HALO - j:POOL_HALO - j + tm, cols]
        cnt = jnp.minimum(t + 1, w).astype(F32)
        pooled = s / cnt - ug
        yp = jnp.dot(pooled.astype(BF16), pw_ref[g], preferred_element_type=F32) * ps_ref[:, cols]
        yp_ref[:, cols] = yp.astype(yp_ref.dtype)

    cos_t, sin_t = cos_ref[...], sin_ref[...]
    inv_head = 1.0 / QK_HEAD_DIM

    cq = z_ref[:, Z_CQ:Z_CKV]
    cqn = cq * lax.rsqrt(jnp.mean(cq * cq, axis=-1, keepdims=True) + EPS) * cqg_ref[...]
    qf = jnp.dot(cqn.astype(BF16), wuq_ref[...], preferred_element_type=F32)
    q_scale = QK_HEAD_DIM ** -0.5 * LOG2_E
    for h in range(MLA_HEADS):
        qh = qf[:, h * LANES:(h + 1) * LANES]
        ss = jnp.sum(qh * qh, axis=-1, keepdims=True) * inv_head
        qn = qh * lax.rsqrt(ss + EPS) * qg_ref[...]
        q_ref[0, h] = (_rope(qn, cos_t, sin_t) * q_scale).astype(q_ref.dtype)

    ckv = z_ref[:, Z_CKV:Z_ROPE]
    ckvn = (ckv * lax.rsqrt(jnp.mean(ckv * ckv, axis=-1, keepdims=True) + EPS) * ckvg_ref[...]).astype(BF16)
    kf = jnp.dot(ckvn, wuk_ref[...], preferred_element_type=F32)
    vf = jnp.dot(ckvn, wuv_ref[...], preferred_element_type=F32)
    k_rope = z_ref[:, Z_ROPE:Z_WIDTH]
    ones_lane = lax.broadcasted_iota(jnp.int32, (tm, LANES), 1) == V_HEAD_DIM
    for h in range(MLA_HEADS):
        kh = kf[:, h * LANES:(h + 1) * LANES] + k_rope
        ss = jnp.sum(kh * kh, axis=-1, keepdims=True) * inv_head
        kn = kh * lax.rsqrt(ss + EPS) * kg_ref[...]
        k_ref[0, h] = _rope(kn, cos_t, sin_t).astype(k_ref.dtype)
        v_ref[0, h] = jnp.where(ones_lane, 1.0, vf[:, h * LANES:(h + 1) * LANES]).astype(v_ref.dtype)


def mla_prep(z, pool_w, pool_scale, cq_g, wuq_pad, ckv_g, wuk_pad, wuv_pad, qg_pad, kg_pad,
             cos_t, sin_t, batch, seq, tm=512):
    n = z.shape[0]
    per = seq // tm
    hb = tm // POOL_HALO
    full = lambda a: pl.BlockSpec(a.shape, lambda i: (0,) * a.ndim)
    tab = pl.BlockSpec((tm, LANES), lambda i: (i, 0))
    head_out = pl.BlockSpec((1, MLA_HEADS, tm, LANES), lambda i: (i // per, 0, i % per, 0))
    head_shape = jax.ShapeDtypeStruct((batch, MLA_HEADS, seq, LANES), BF16)
    return pl.pallas_call(
        functools.partial(_mla_prep_kernel, tm=tm, per=per),
        grid=(n // tm,),
        in_specs=[pl.BlockSpec((tm, Z_WIDTH), lambda i: (i, 0)),
                  pl.BlockSpec((POOL_HALO, POOL_WIDTH), lambda i: (jnp.maximum(i * hb - 1, 0), 0)),
                  full(pool_w), full(pool_scale), full(cq_g), full(wuq_pad), full(ckv_g), full(wuk_pad),
                  full(wuv_pad), full(qg_pad), full(kg_pad), tab, tab],
        out_specs=[pl.BlockSpec((tm, POOL_WIDTH), lambda i: (i, 0)), head_out, head_out, head_out],
        out_shape=[jax.ShapeDtypeStruct((n, POOL_WIDTH), BF16), head_shape, head_shape, head_shape],
        scratch_shapes=[pltpu.VMEM((tm + POOL_HALO, POOL_WIDTH), F32)],
        compiler_params=_params("parallel"),
        name="mla_prep",
    )(z, z, pool_w, pool_scale, cq_g, wuq_pad, ckv_g, wuk_pad, wuv_pad, qg_pad, kg_pad, cos_t, sin_t)


def _flash_kernel(q_ref, k_ref, v_ref, o_ref, m_ref, acc_ref, *, tq, tk, heads):
    qi = pl.program_id(2)
    sub = tq // tk
    slabs = tk // LANES
    q_chunk = lax.broadcasted_iota(jnp.int32, (tq, tk), 0) // CHUNK
    k_chunk = lax.broadcasted_iota(jnp.int32, (tq, tk), 1) // CHUNK
    m_ref[...] = jnp.full(m_ref.shape, MASK_VALUE, F32)
    acc_ref[...] = jnp.zeros(acc_ref.shape, F32)

    def step(j, mask, row0=0):
        start = pl.multiple_of(j * tk, tk)
        for hh in range(heads):
            k = k_ref[0, hh, pl.ds(start, tk), :]
            v = v_ref[0, hh, pl.ds(start, tk), :]
            s = lax.dot_general(q_ref[0, hh, row0:, :], k, (((1,), (1,)), ((), ())), preferred_element_type=F32)
            if mask is not None:
                s = jnp.where(mask[row0:], s, MASK_VALUE)
            cols_s = [s[:, c * LANES:(c + 1) * LANES] for c in range(slabs)]
            s_max = cols_s[0]
            for sc in cols_s[1:]:
                s_max = jnp.maximum(s_max, sc)
            m_prev = m_ref[hh, row0:, :]
            m_new = jnp.maximum(m_prev, jnp.max(s_max, axis=-1, keepdims=True))
            alpha = jnp.exp2(m_prev - m_new)
            p = jnp.concatenate([jnp.exp2(sc - m_new) for sc in cols_s], axis=1).astype(v.dtype)
            acc_ref[hh, row0:, :] = alpha * acc_ref[hh, row0:, :] + jnp.dot(p, v, preferred_element_type=F32)
            m_ref[hh, row0:, :] = m_new

    def full_step(j, carry):
        step(j, None)
        return carry

    lax.fori_loop(0, sub * qi, full_step, 0)
    for dd in range(sub):
        step(sub * qi + dd, k_chunk + dd * (tk // CHUNK) <= q_chunk, row0=dd * tk)
    outs = []
    for hh in range(heads):
        acc = acc_ref[hh]
        outs.append((acc / acc[:, V_HEAD_DIM:V_HEAD_DIM + 1])[:, :V_HEAD_DIM])
    o_ref[0] = jnp.concatenate(outs, axis=-1).astype(o_ref.dtype)


def flash_attention(q, k, v, tq=1024, tk=512):
    b, h, s, _ = q.shape
    heads = LANES // V_HEAD_DIM
    return pl.pallas_call(
        functools.partial(_flash_kernel, tq=tq, tk=tk, heads=heads),
        grid=(b, h // heads, s // tq),
        in_specs=[pl.BlockSpec((1, heads, tq, LANES), lambda bi, hi, qi: (bi, hi, qi, 0)),
                  pl.BlockSpec((1, heads, s, LANES), lambda bi, hi, qi: (bi, hi, 0, 0)),
                  pl.BlockSpec((1, heads, s, LANES), lambda bi, hi, qi: (bi, hi, 0, 0))],
        out_specs=pl.BlockSpec((1, tq, LANES), lambda bi, hi, qi: (bi, qi, hi)),
        out_shape=jax.ShapeDtypeStruct((b, s, h * V_HEAD_DIM), BF16),
        scratch_shapes=[pltpu.VMEM((heads, tq, LANES), F32) for _ in range(2)],
        compiler_params=_params("parallel", "parallel", "parallel"),
        name="flash_attention",
    )(q, k, v)


def _dwconv_kernel(u_ref, halo_ref, w_ref, b_ref, g_ref, beta_ref, o_ref, ext_ref, sh_ref, conv_ref, *, tm, per):
    si = pl.program_id(0) % per
    d = u_ref.shape[1]
    ext_ref[0:CONV_HALO, :] = jnp.where(si == 0, 0.0, halo_ref[...])
    ext_ref[CONV_HALO:, :] = u_ref[...]
    span = sh_ref.shape[1]
    for b in range(1, SUBLANES):
        sh_ref[b - 1] = ext_ref[b:b + span, :]
    first = CONV_HALO - (CONV_WIDTH - 1)
    for r0 in range(0, tm, CONV_PATCH_ROWS):
        for c0 in range(0, d, CONV_PATCH_COLS):
            cols = slice(c0, c0 + CONV_PATCH_COLS)
            view = (CONV_PATCH_ROWS // SUBLANES, SUBLANES, CONV_PATCH_COLS)
            patch = jnp.broadcast_to(b_ref[:, cols], view)
            for j in range(CONV_WIDTH):
                a, b = divmod(first + j, SUBLANES)
                rows = slice(SUBLANES * a + r0, SUBLANES * a + r0 + CONV_PATCH_ROWS)
                win = ext_ref[rows, cols] if b == 0 else sh_ref[b - 1, rows, cols]
                tap = jnp.broadcast_to(w_ref[j:j + 1, cols], view[1:])
                patch = patch + tap[None] * win.reshape(view)
            conv_ref[r0:r0 + CONV_PATCH_ROWS, cols] = patch.reshape(CONV_PATCH_ROWS, CONV_PATCH_COLS)
    acc = conv_ref[...]
    mu = jnp.mean(acc, axis=-1, keepdims=True)
    cen = acc - mu
    var = jnp.mean(cen * cen, axis=-1, keepdims=True)
    y = cen * lax.rsqrt(var + EPS) * g_ref[...] + beta_ref[...]
    o_ref[...] = (y * jax.nn.sigmoid(y)).astype(o_ref.dtype)


def dwconv_ln_silu(u, w_pad, b, g, beta, seq, tm=256):
    n, d = u.shape
    per = seq // tm
    hb = tm // CONV_HALO
    full = lambda a: pl.BlockSpec(a.shape, lambda i: (0, 0))
    return pl.pallas_call(
        functools.partial(_dwconv_kernel, tm=tm, per=per),
        grid=(n // tm,),
        in_specs=[pl.BlockSpec((tm, d), lambda i: (i, 0)),
                  pl.BlockSpec((CONV_HALO, d), lambda i: (jnp.maximum(i * hb - 1, 0), 0)),
                  full(w_pad), full(b), full(g), full(beta)],
        out_specs=pl.BlockSpec((tm, d), lambda i: (i, 0)),
        out_shape=jax.ShapeDtypeStruct((n, d), BF16),
        scratch_shapes=[pltpu.VMEM((tm + CONV_HALO, d), F32),
                        pltpu.VMEM((SUBLANES - 1, tm + CONV_HALO - SUBLANES, d), F32),
                        pltpu.VMEM((tm, d), F32)],
        compiler_params=_params("parallel"),
        name="dwconv_ln_silu",
    )(u, u, w_pad, b, g, beta)


def _router_kernel(lg_ref, idx_ref, gate_ref, cnt_ref, carry_ref, *, tm):
    @pl.when(pl.program_id(0) == 0)
    def _():
        carry_ref[...] = jnp.zeros(carry_ref.shape, F32)

    lane = lax.broadcasted_iota(jnp.int32, (tm, LANES), 1)
    lane_f = lane.astype(F32)
    neg = -jnp.inf
    logits = jnp.where(lane < N_EXPERTS, lg_ref[...], neg)
    picks, vals, ids = [], [], []
    for _ in range(TOP_K):
        mx = jnp.max(logits, axis=-1, keepdims=True)
        idx = jnp.min(jnp.where(logits == mx, lane_f, float(LANES)), axis=-1, keepdims=True)
        pick = lane_f == idx
        picks.append(pick)
        vals.append(mx)
        ids.append(idx)
        logits = jnp.where(pick, neg, logits)
    exps = [jnp.exp(v - vals[0]) for v in vals]
    den = exps[0]
    for e in exps[1:]:
        den = den + e

    chosen = jnp.zeros((tm, LANES), F32)
    for pick in picks:
        chosen = chosen + pick.astype(F32)
    r_io = lax.broadcasted_iota(jnp.int32, (tm, tm), 0)
    c_io = lax.broadcasted_iota(jnp.int32, (tm, tm), 1)
    earlier = (c_io < r_io).astype(BF16)
    before = jnp.dot(earlier, chosen.astype(BF16), preferred_element_type=F32) + carry_ref[0:1, :]
    idx_out = jnp.zeros((tm, LANES), F32)
    gate_out = jnp.zeros((tm, LANES), F32)
    for k in range(TOP_K):
        rank = jnp.sum(jnp.where(picks[k], before, 0.0), axis=-1, keepdims=True)
        idx_out = jnp.where(lane == k, ids[k], idx_out)
        idx_out = jnp.where(lane == TOP_K + k, rank, idx_out)
        gate_out = jnp.where(lane == k, exps[k] / den, gate_out)
    idx_ref[...] = idx_out.astype(jnp.int32)
    gate_ref[...] = gate_out
    total = carry_ref[0:1, :] + jnp.sum(chosen, axis=0, keepdims=True)
    carry_ref[...] = jnp.broadcast_to(total, carry_ref.shape)
    cnt_ref[...] = jnp.broadcast_to(total, cnt_ref.shape).astype(jnp.int32)


def router(logits, tm=512):
    n = logits.shape[0]
    row = pl.BlockSpec((tm, LANES), lambda i: (i, 0))
    return pl.pallas_call(
        functools.partial(_router_kernel, tm=tm),
        grid=(n // tm,),
        in_specs=[row],
        out_specs=[row, row, pl.BlockSpec((8, LANES), lambda i: (0, 0))],
        out_shape=[jax.ShapeDtypeStruct((n, LANES), jnp.int32), jax.ShapeDtypeStruct((n, LANES), F32),
                   jax.ShapeDtypeStruct((8, LANES), jnp.int32)],
        scratch_shapes=[pltpu.VMEM((8, LANES), F32)],
        compiler_params=_params("arbitrary"),
        name="router",
    )(logits)


def _sc_worker_base(per_worker):
    return (lax.axis_index("s") * SC_CORES + lax.axis_index("c")) * per_worker


def sc_scatter_rows(src, dest_flat, n_rows):
    n, d = src.shape
    per_w = n // SC_WORKERS
    n_chunks = per_w // SC_CHUNK
    mesh = plsc.VectorSubcoreMesh(core_axis_name="c", subcore_axis_name="s")

    @functools.partial(
        pl.kernel, out_type=jax.ShapeDtypeStruct((n_rows, d), src.dtype), mesh=mesh,
        scratch_types=[pltpu.VMEM((per_w,), jnp.int32) for _ in range(TOP_K)]
        + [pltpu.VMEM((SC_CHUNK, d), src.dtype) for _ in range(2)] + [pltpu.SemaphoreType.DMA] * 4,
        name="sc_scatter_rows")
    def scatter(src_hbm, dest_hbm, out_hbm, *scratch):
        idx_refs = scratch[:TOP_K]
        bufs = scratch[TOP_K:TOP_K + 2]
        in_sems, out_sems = scratch[TOP_K + 2:TOP_K + 4], scratch[TOP_K + 4:TOP_K + 6]
        base = _sc_worker_base(per_w)
        for k, idx_ref in enumerate(idx_refs):
            pltpu.sync_copy(dest_hbm.at[pl.ds(k * n + base, per_w)], idx_ref)

        def load(j, slot):
            return pltpu.make_async_copy(src_hbm.at[pl.ds(base + j * SC_CHUNK, SC_CHUNK)], bufs[slot], in_sems[slot])

        def store_all(j, slot):
            copies = [pltpu.make_async_copy(bufs[slot], out_hbm.at[idx_ref.at[pl.ds(j * SC_CHUNK, SC_CHUNK)]],
                                            out_sems[slot]) for idx_ref in idx_refs]
            for cp in copies:
                cp.start()
            for cp in copies:
                cp.wait()

        load(0, 0).start()

        @pl.loop(0, n_chunks // 2)
        def _(p):
            j = 2 * p
            load(j + 1, 1).start()
            load(j, 0).wait()
            store_all(j, 0)

            @pl.when(j + 2 < n_chunks)
            def _():
                load(j + 2, 0).start()

            load(j + 1, 1).wait()
            store_all(j + 1, 1)

    return scatter(src, dest_flat)


def sc_gather_rows(table, idx):
    b = idx.shape[0]
    d = table.shape[1]
    per_w = b // SC_WORKERS
    n_chunks = per_w // SC_CHUNK
    mesh = plsc.VectorSubcoreMesh(core_axis_name="c", subcore_axis_name="s")

    @functools.partial(
        pl.kernel, out_type=jax.ShapeDtypeStruct((b, d), table.dtype), mesh=mesh,
        scratch_types=[pltpu.VMEM((per_w,), jnp.int32)] + [pltpu.VMEM((SC_CHUNK, d), table.dtype) for _ in range(2)]
        + [pltpu.SemaphoreType.DMA] * 4,
        name="sc_gather_rows")
    def gather(table_hbm, idx_hbm, out_hbm, idx_ref, buf0, buf1, gsem0, gsem1, osem0, osem1):
        bufs, in_sems, out_sems = (buf0, buf1), (gsem0, gsem1), (osem0, osem1)
        base = _sc_worker_base(per_w)
        pltpu.sync_copy(idx_hbm.at[pl.ds(base, per_w)], idx_ref)

        def fetch(j, slot):
            return pltpu.make_async_copy(table_hbm.at[idx_ref.at[pl.ds(j * SC_CHUNK, SC_CHUNK)]], bufs[slot],
                                         in_sems[slot])

        def store(j, slot):
            return pltpu.make_async_copy(bufs[slot], out_hbm.at[pl.ds(base + j * SC_CHUNK, SC_CHUNK)],
                                         out_sems[slot])

        fetch(0, 0).start()

        @pl.loop(0, n_chunks // 2)
        def _(p):
            j = 2 * p

            @pl.when(p > 0)
            def _():
                store(j - 1, 1).wait()

            fetch(j + 1, 1).start()
            fetch(j, 0).wait()
            store(j, 0).start()
            fetch(j + 1, 1).wait()
            store(j, 0).wait()

            @pl.when(j + 2 < n_chunks)
            def _():
                fetch(j + 2, 0).start()

            store(j + 1, 1).start()

        store(n_chunks - 1, 1).wait()

    return gather(table, idx)


def _ffn_kernel(be_ref, nv_ref, nu_ref, xs_ref, w1_ref, b1g_ref, b1u_ref, w2_ref, b2_ref, y_ref,
                wg_ref, wu_ref, w2s_ref):
    i = pl.program_id(0)
    pair = 2 * LANES

    @pl.when(i < nu_ref[0])
    def _():
        e = be_ref[i]
        prev = be_ref[jnp.maximum(i - 1, 0)]

        @pl.when((i == 0) | (e != prev))
        def _():
            r_io = lax.broadcasted_iota(jnp.int32, (pair, pair), 0)
            c_io = lax.broadcasted_iota(jnp.int32, (pair, pair), 1)
            want = jnp.where(c_io < LANES, 2 * c_io, 2 * (c_io - LANES) + 1)
            sel = (r_io == want).astype(BF16)
            for c in range(wg_ref.shape[1] // LANES):
                slab = w1_ref[0, 0, :, c * pair:(c + 1) * pair].astype(BF16)
                split = jnp.dot(slab, sel, preferred_element_type=F32)
                wg_ref[:, c * LANES:(c + 1) * LANES] = split[:, :LANES].astype(BF16)
                wu_ref[:, c * LANES:(c + 1) * LANES] = split[:, LANES:].astype(BF16)
            w2s_ref[...] = w2_ref[0, 0].astype(BF16)

        nvalid = nv_ref[i]

        def ffn_rows(rows):
            row = lax.broadcasted_iota(jnp.int32, (rows, xs_ref.shape[1]), 0)
            lo, hi = _unpack_halves(jnp.where(row < nvalid, xs_ref[0:rows, :], 0))
            x = jnp.concatenate([lo, hi], axis=1).astype(BF16)
            hg = jnp.dot(x, wg_ref[...], preferred_element_type=F32) + b1g_ref[0]
            hu = jnp.dot(x, wu_ref[...], preferred_element_type=F32) + b1u_ref[0]
            gate = jnp.minimum(hg, SWIGLU_LIMIT)
            up = jnp.clip(hu, -SWIGLU_LIMIT, SWIGLU_LIMIT)
            act = gate * jax.nn.sigmoid(SWIGLU_ALPHA * gate) * (up + 1.0)
            y = jnp.dot(act.astype(BF16), w2s_ref[...], preferred_element_type=F32) + b2_ref[0]
            y_ref[0:rows, :] = _pack_halves(y)
            if rows < y_ref.shape[0]:
                y_ref[rows:, :] = jnp.zeros((y_ref.shape[0] - rows, y_ref.shape[1]), y_ref.dtype)

        n_sub = y_ref.shape[0] // MOE_BLOCK
        for sub in range(1, n_sub + 1):
            lo_rows, hi_rows = (sub - 1) * MOE_BLOCK, sub * MOE_BLOCK
            pl.when((nvalid > lo_rows) & (nvalid <= hi_rows))(functools.partial(ffn_rows, hi_rows))

    @pl.when(i >= nu_ref[0])
    def _():
        y_ref[...] = jnp.zeros(y_ref.shape, y_ref.dtype)


def moe_ffn(xs, block_e, block_valid, n_used, layer, w1, b1g, b1u, w2, b2):
    n_rows, half = xs.shape
    d = 2 * half
    n_blocks = n_rows // MOE_STEP
    n_exp, f2 = w1.shape[1], w1.shape[3]
    f = f2 // 2
    rows = lambda i, be, nv, nu: (jnp.minimum(i, nu[0] - 1), 0)
    vec = lambda width: pl.BlockSpec((1, 1, width), lambda i, be, nv, nu: (be[i], 0, 0))
    grid_spec = pltpu.PrefetchScalarGridSpec(
        num_scalar_prefetch=3,
        grid=(n_blocks,),
        in_specs=[
            pl.BlockSpec((MOE_STEP, half), rows),
            pl.BlockSpec((1, 1, d, f2), lambda i, be, nv, nu: (layer, be[i], 0, 0)),
            vec(f), vec(f),
            pl.BlockSpec((1, 1, f, d), lambda i, be, nv, nu: (layer, be[i], 0, 0)),
            vec(d),
        ],
        out_specs=pl.BlockSpec((MOE_STEP, half), lambda i, be, nv, nu: (i, 0)),
        scratch_shapes=[pltpu.VMEM((d, f), BF16), pltpu.VMEM((d, f), BF16), pltpu.VMEM((f, d), BF16)],
    )
    return pl.pallas_call(
        _ffn_kernel,
        grid_spec=grid_spec,
        out_shape=jax.ShapeDtypeStruct((n_rows, half), jnp.int32),
        compiler_params=_params("arbitrary"),
        name="moe_ffn",
    )(block_e, block_valid, n_used, xs, w1, b1g.reshape(n_exp, 1, f), b1u.reshape(n_exp, 1, f), w2,
      b2.reshape(n_exp, 1, d))


def _combine_kernel(x_ref, y_ref, gates_ref, gate_ref, *rest, with_norm):
    g = gates_ref[...]
    half = y_ref.shape[2]
    acc_lo = acc_hi = None
    for k in range(TOP_K):
        lo, hi = _unpack_halves(y_ref[k])
        gk = g[:, k:k + 1]
        acc_lo = gk * lo if acc_lo is None else acc_lo + gk * lo
        acc_hi = gk * hi if acc_hi is None else acc_hi + gk * hi
    x_lo = x_ref[:, :half] + gate_ref[0, :, :half] * acc_lo
    x_hi = x_ref[:, half:] + gate_ref[0, :, half:] * acc_hi
    if with_norm:
        ng_ref, sc_ref, sh_ref, o_ref, h_ref = rest
        x_new = jnp.concatenate([x_lo, x_hi], axis=1)
        o_ref[...] = x_new
        h_ref[...] = _modulated_norm(x_new, ng_ref[...], sc_ref[0], sh_ref[0]).astype(h_ref.dtype)
    else:
        (o_ref,) = rest
        o_ref[:, :half] = x_lo
        o_ref[:, half:] = x_hi


def moe_combine(x, y, gates, gate_mod, seq, next_norm=None, tm=512):
    n, d = x.shape
    per = seq // tm
    rows = lambda width: pl.BlockSpec((tm, width), lambda i: (i, 0))
    vec = pl.BlockSpec((1, 1, d), lambda i: (i // per, 0, 0))
    in_specs = [rows(d), pl.BlockSpec((TOP_K, tm, d // 2), lambda i: (0, i, 0)), rows(LANES), vec]
    args = [x, y, gates, gate_mod]
    out_specs, out_shape = rows(d), jax.ShapeDtypeStruct((n, d), F32)
    if next_norm is not None:
        in_specs += [pl.BlockSpec((1, d), lambda i: (0, 0)), vec, vec]
        args += list(next_norm)
        out_specs, out_shape = [out_specs, rows(d)], [out_shape, jax.ShapeDtypeStruct((n, d), BF16)]
    return pl.pallas_call(
        functools.partial(_combine_kernel, with_norm=next_norm is not None),
        grid=(n // tm,),
        in_specs=in_specs,
        out_specs=out_specs,
        out_shape=out_shape,
        compiler_params=_params("parallel"),
        name="moe_combine",
    )(*args)


def moe_layer(x, mixer, mix_gate, mods, norm_g, router_w, router_b, layer, w1, b1, w2, b2, seq, next_norm):
    n, d = x.shape
    shift, scale, gate = mods
    rw_pad = jnp.pad(router_w, ((0, 0), (0, LANES - N_EXPERTS)))
    rw_hi = rw_pad.astype(BF16)
    rw_lo = (rw_pad - rw_hi.astype(F32)).astype(BF16)
    rb_pad = jnp.pad(router_b, (0, LANES - N_EXPERTS)).reshape(1, LANES)
    x, h, logits = proj_residual_router(*mixer, x, mix_gate, norm_g.reshape(1, d), scale, shift,
                                        jnp.concatenate([rw_hi, rw_lo], axis=1), rw_hi, rb_pad, seq)
    idx, gates, counts = router(logits)

    top_i = idx[:, :TOP_K]
    rank = idx[:, TOP_K:2 * TOP_K]
    counts = counts[0, :N_EXPERTS]
    experts = jnp.arange(N_EXPERTS, dtype=jnp.int32)
    padded = (counts + MOE_STEP - 1) // MOE_STEP * MOE_STEP
    pad_ends = jnp.sum(jnp.where(experts[:, None] >= experts[None, :], padded[None, :], 0), axis=1)
    pad_starts = pad_ends - padded
    dest = jnp.sum(jnp.where(top_i[..., None] == experts, pad_starts, 0), axis=-1) + rank
    n_blocks = -(-n * TOP_K // MOE_STEP) + N_EXPERTS
    block_start = jnp.arange(n_blocks, dtype=jnp.int32) * MOE_STEP
    block_e = jnp.minimum(jnp.sum((pad_ends[None, :] <= block_start[:, None]).astype(jnp.int32), axis=1),
                          N_EXPERTS - 1)
    n_used = (pad_ends[N_EXPERTS - 1:] // MOE_STEP).astype(jnp.int32)
    seg_end = jnp.sum(jnp.where(block_e[:, None] == experts, pad_starts + counts, 0), axis=1)
    block_valid = jnp.clip(seg_end - block_start, 0, MOE_STEP).astype(jnp.int32)
    dest_flat = dest.T.reshape(-1)

    xs = sc_scatter_rows(h, dest_flat, n_blocks * MOE_STEP)
    ys = moe_ffn(xs, block_e, block_valid, n_used, layer, w1, b1[layer][:, 0::2], b1[layer][:, 1::2], w2, b2[layer])
    y = sc_gather_rows(ys, dest_flat).reshape(TOP_K, n, d // 2)
    out = moe_combine(x, y, gates, gate, seq, next_norm)
    return out if next_norm is not None else (out, None)


def _pad_heads(w, width):
    k = w.shape[0]
    w = w.reshape(k, MLA_HEADS, width)
    return jnp.pad(w, ((0, 0), (0, 0), (0, LANES - width))).reshape(k, MLA_HEADS * LANES)


def _head_lane_source():
    half = QK_ROPE_DIM // 2
    first_nope = LANES // 2 - half
    lanes = (list(range(QK_NOPE_DIM, QK_NOPE_DIM + half)) + list(range(first_nope))
             + list(range(QK_NOPE_DIM + half, QK_HEAD_DIM)) + list(range(first_nope, QK_NOPE_DIM)))
    return jnp.array(lanes + [QK_HEAD_DIM] * (LANES - QK_HEAD_DIM), jnp.int32)


def _to_head_lanes(w):
    w = jnp.concatenate([w, jnp.zeros(w.shape[:-1] + (1,), w.dtype)], axis=-1)
    return jnp.take(w, _head_lane_source(), axis=-1)


def _rope_tables(positions):
    inv = 1.0 / (ROPE_THETA ** (jnp.arange(0, QK_ROPE_DIM, 2, dtype=F32) / QK_ROPE_DIM))
    ang = positions.reshape(-1).astype(F32)[:, None] * inv
    cos, sin = jnp.cos(ang), jnp.sin(ang)
    n = ang.shape[0]
    half = QK_ROPE_DIM // 2
    first_nope = LANES // 2 - half
    ones = lambda w: jnp.ones((n, w), F32)
    zeros = lambda w: jnp.zeros((n, w), F32)
    pad = LANES - QK_HEAD_DIM
    cos_t = jnp.concatenate([cos, ones(first_nope), cos, ones(QK_NOPE_DIM - first_nope), zeros(pad)], axis=1)
    sin_t = jnp.concatenate([-sin, zeros(first_nope), sin, zeros(QK_NOPE_DIM - first_nope + pad)], axis=1)
    return cos_t, sin_t


def _split_mods(m, batch):
    d = m.shape[-1] // 3
    m = m[:batch]
    return tuple(m[:, None, j * d:(j + 1) * d] for j in range(3))


def pool_mla_mixer(x, h, mods, norm_g, tables, w_in, pool_w, pool_scale, cq_norm_g, w_uq, ckv_norm_g, w_ukv,
                   q_norm_g, k_norm_g, w_out, batch, seq):
    n, d = x.shape
    shift, scale, _ = mods
    rope_cols = _to_head_lanes(jnp.pad(w_in[:, Z_ROPE:], ((0, 0), (QK_NOPE_DIM, 0))))
    w_in_pad = jnp.concatenate([w_in[:, :Z_ROPE], rope_cols], axis=1).astype(BF16)
    if h is None:
        z = norm_matmul(x, norm_g.reshape(1, d), scale, shift, w_in_pad, seq)
    else:
        z = matmul(h, w_in_pad)
    w_ukv_h = w_ukv.reshape(KV_LORA_RANK, MLA_HEADS, QK_NOPE_DIM + V_HEAD_DIM)
    k_nope = jnp.pad(w_ukv_h[:, :, :QK_NOPE_DIM], ((0, 0), (0, 0), (0, QK_ROPE_DIM)))
    wuk_pad = _to_head_lanes(k_nope).reshape(KV_LORA_RANK, MLA_HEADS * LANES).astype(BF16)
    wuv_pad = _pad_heads(w_ukv_h[:, :, QK_NOPE_DIM:].reshape(KV_LORA_RANK, -1), V_HEAD_DIM).astype(BF16)
    wuq_pad = _to_head_lanes(w_uq.reshape(Q_LORA_RANK, MLA_HEADS, QK_HEAD_DIM))
    wuq_pad = wuq_pad.reshape(Q_LORA_RANK, MLA_HEADS * LANES).astype(BF16)
    pad_g = lambda g: _to_head_lanes(g).reshape(1, LANES)
    y_pool, q, k, v = mla_prep(z, pool_w.astype(BF16), pool_scale.reshape(1, -1), cq_norm_g.reshape(1, -1), wuq_pad,
                               ckv_norm_g.reshape(1, -1), wuk_pad, wuv_pad, pad_g(q_norm_g), pad_g(k_norm_g),
                               *tables, batch, seq)
    y_att = flash_attention(q, k, v).reshape(n, MLA_HEADS * V_HEAD_DIM)
    w_out_b = w_out.astype(BF16)
    return [y_pool, y_att], [w_out_b[:POOL_WIDTH], w_out_b[POOL_WIDTH:]], jnp.zeros((1, d), F32)


def conformer_mixer(h, pw1_w, pw1_b, dw_w, dw_b, ln_g, ln_b, pw2_w, pw2_b, seq):
    d = pw2_w.shape[1]
    cd = pw1_w.shape[1] // 2
    pw1 = pw1_w.astype(BF16)
    u = glu_matmul(h, pw1[:, :cd], pw1[:, cd:], pw1_b[:cd].reshape(1, cd), pw1_b[cd:].reshape(1, cd))
    w_pad = jnp.pad(dw_w, ((0, CONV_HALO - CONV_WIDTH), (0, 0)))
    u = dwconv_ln_silu(u, w_pad, dw_b.reshape(1, cd), ln_g.reshape(1, cd), ln_b.reshape(1, cd), seq)
    return [u], [pw2_w.astype(BF16)], pw2_b.reshape(1, d)


def kernel(x, c, positions, ada_mix_w, ada_mix_b, norm_mix_g, w_in, pool_w, pool_scale, cq_norm_g, w_uq,
           ckv_norm_g, w_ukv, q_norm_g, k_norm_g, w_out, conv_pw1_w, conv_pw1_b, conv_dw_w, conv_dw_b,
           conv_ln_g, conv_ln_b, conv_pw2_w, conv_pw2_b, ada_ffn_w, ada_ffn_b, norm_ffn_g, router_w,
           router_b, moe_w1, moe_b1, moe_w2, moe_b2):
    batch, seq, d = x.shape
    depth = ada_mix_w.shape[0]
    c_pad = jnp.pad(c, ((0, 8 - batch), (0, 0)))
    mix_mods = adaln(c_pad, ada_mix_w, ada_mix_b)
    ffn_mods = adaln(c_pad, ada_ffn_w, ada_ffn_b)
    tables = _rope_tables(positions)
    xf = x.reshape(batch * seq, d)
    h = None
    for layer in range(depth):
        i = layer // 2
        mods = _split_mods(mix_mods[layer], batch)
        if layer % 2 == 0:
            mixer = pool_mla_mixer(xf, h, mods, norm_mix_g[layer], tables, w_in[i], pool_w[i], pool_scale[i],
                                   cq_norm_g[i], w_uq[i], ckv_norm_g[i], w_ukv[i], q_norm_g[i], k_norm_g[i],
                                   w_out[i], batch, seq)
        else:
            mixer = conformer_mixer(h, conv_pw1_w[i], conv_pw1_b[i], conv_dw_w[i], conv_dw_b[i],
                                    conv_ln_g[i], conv_ln_b[i], conv_pw2_w[i], conv_pw2_b[i], seq)
        next_norm = None
        if layer + 1 < depth:
            n_shift, n_scale, _ = _split_mods(mix_mods[layer + 1], batch)
            next_norm = (norm_mix_g[layer + 1].reshape(1, d), n_scale, n_shift)
        xf, h = moe_layer(xf, mixer, mods[2], _split_mods(ffn_mods[layer], batch), norm_ffn_g[layer],
                          router_w[layer], router_b[layer], layer, moe_w1, moe_b1, moe_w2, moe_b2, seq, next_norm)
    return xf.reshape(batch, seq, d)
```

```python
import functools

import jax
import jax.numpy as jnp
from jax import lax
from jax.experimental import pallas as pl
from jax.experimental.pallas import tpu as pltpu
from jax.experimental.pallas import tpu_sc as plsc

F32 = jnp.float32
BF16 = jnp.bfloat16
HIGHEST = lax.Precision.HIGHEST

EPS = 1e-6
POOL_WINDOWS = (2, 4, 8, 16)
POOL_GROUP_DIM = 128
POOL_WIDTH = POOL_GROUP_DIM * len(POOL_WINDOWS)
MLA_HEADS = 8
QK_NOPE_DIM = 64
QK_ROPE_DIM = 32
QK_HEAD_DIM = QK_NOPE_DIM + QK_ROPE_DIM
V_HEAD_DIM = 64
Q_LORA_RANK = 384
KV_LORA_RANK = 256
ROPE_THETA = 10000.0
CHUNK = 64
CONV_WIDTH = 31
N_EXPERTS = 32
TOP_K = 4
SWIGLU_ALPHA = 1.702
SWIGLU_LIMIT = 7.0
MOE_BLOCK = 256
MOE_STEP = 512

LANES = 128
SUBLANES = 8
CONV_PATCH_ROWS = 64
CONV_PATCH_COLS = 256
POOL_HALO = 16
CONV_HALO = 32
MASK_VALUE = -1e30
LOG2_E = 1.4426950408889634
VMEM_LIMIT = 52 * 1024 * 1024
SC_CORES = 2
SC_SUBCORES = 16
SC_WORKERS = SC_CORES * SC_SUBCORES
SC_CHUNK = 64
COMBINE_PARTS = 2

Z_CQ = POOL_WIDTH
Z_CKV = Z_CQ + Q_LORA_RANK
Z_ROPE = Z_CKV + KV_LORA_RANK
Z_WIDTH = Z_ROPE + LANES


def _params(*sem, vmem=None):
    return pltpu.CompilerParams(dimension_semantics=sem, vmem_limit_bytes=vmem or VMEM_LIMIT,
                                disable_bounds_checks=True)


def _adaln_kernel(c_ref, w_ref, b_ref, o_ref):
    c = c_ref[...]
    s = c * jax.nn.sigmoid(c)
    o_ref[0] = jnp.dot(s, w_ref[0], preferred_element_type=F32, precision=HIGHEST) + b_ref[0]


def adaln(c_pad, w, b):
    n_l, d, d3 = w.shape
    tn = 512
    return pl.pallas_call(
        _adaln_kernel,
        grid=(n_l, d3 // tn),
        in_specs=[pl.BlockSpec((8, d), lambda l, j: (0, 0)),
                  pl.BlockSpec((1, d, tn), lambda l, j: (l, 0, j)),
                  pl.BlockSpec((1, 1, tn), lambda l, j: (l, 0, j))],
        out_specs=pl.BlockSpec((1, 8, tn), lambda l, j: (l, 0, j)),
        out_shape=jax.ShapeDtypeStruct((n_l, 8, d3), F32),
        compiler_params=_params("parallel", "parallel"),
        name="adaln",
    )(c_pad, w, b.reshape(n_l, 1, d3))


def _modulated_norm(x, g, scale, shift):
    ms = jnp.mean(x * x, axis=-1, keepdims=True)
    return x * lax.rsqrt(ms + EPS) * g * (1.0 + scale) + shift


def _pack_halves(x):
    w = x.shape[1] // 2
    lo = lax.bitcast_convert_type(x[:, :w].astype(BF16).astype(F32), jnp.uint32)
    hi = lax.bitcast_convert_type(x[:, w:].astype(BF16).astype(F32), jnp.uint32)
    return lax.bitcast_convert_type((lo >> 16) | (hi & jnp.uint32(0xFFFF0000)), jnp.int32)


def _unpack_halves(p):
    u = lax.bitcast_convert_type(p, jnp.uint32)
    lo = lax.bitcast_convert_type(u << 16, F32)
    hi = lax.bitcast_convert_type(u & jnp.uint32(0xFFFF0000), F32)
    return lo, hi


def _norm_matmul_kernel(x_ref, g_ref, sc_ref, sh_ref, w_ref, o_ref):
    h = _modulated_norm(x_ref[...], g_ref[...], sc_ref[0], sh_ref[0])
    o_ref[...] = jnp.dot(h.astype(BF16), w_ref[...], preferred_element_type=F32)


def norm_matmul(x, g, scale, shift, w, seq, tm=512):
    n, d = x.shape
    m = w.shape[1]
    per = seq // tm
    vec = pl.BlockSpec((1, 1, d), lambda i: (i // per, 0, 0))
    return pl.pallas_call(
        _norm_matmul_kernel,
        grid=(n // tm,),
        in_specs=[pl.BlockSpec((tm, d), lambda i: (i, 0)), pl.BlockSpec((1, d), lambda i: (0, 0)), vec, vec,
                  pl.BlockSpec((d, m), lambda i: (0, 0))],
        out_specs=pl.BlockSpec((tm, m), lambda i: (i, 0)),
        out_shape=jax.ShapeDtypeStruct((n, m), F32),
        compiler_params=_params("parallel"),
        name="norm_matmul",
    )(x, g, scale, shift, w)


def _matmul_kernel(a_ref, w_ref, o_ref):
    o_ref[...] = jnp.dot(a_ref[...], w_ref[...], preferred_element_type=F32).astype(o_ref.dtype)


def matmul(a, w, out_dtype=F32, tm=512):
    n, k = a.shape
    m = w.shape[1]
    return pl.pallas_call(
        _matmul_kernel,
        grid=(n // tm,),
        in_specs=[pl.BlockSpec((tm, k), lambda i: (i, 0)), pl.BlockSpec((k, m), lambda i: (0, 0))],
        out_specs=pl.BlockSpec((tm, m), lambda i: (i, 0)),
        out_shape=jax.ShapeDtypeStruct((n, m), out_dtype),
        compiler_params=_params("parallel"),
        name="matmul",
    )(a, w)


def _glu_kernel(a_ref, wv_ref, wg_ref, bv_ref, bg_ref, o_ref):
    a = a_ref[...]
    val = jnp.dot(a, wv_ref[...], preferred_element_type=F32) + bv_ref[...]
    gt = jnp.dot(a, wg_ref[...], preferred_element_type=F32) + bg_ref[...]
    o_ref[...] = val * jax.nn.sigmoid(gt)


def glu_matmul(a, wv, wg, bv, bg, tm=512):
    n, k = a.shape
    m = wv.shape[1]
    full = lambda r, c: pl.BlockSpec((r, c), lambda i: (0, 0))
    return pl.pallas_call(
        _glu_kernel,
        grid=(n // tm,),
        in_specs=[pl.BlockSpec((tm, k), lambda i: (i, 0)), full(k, m), full(k, m), full(1, m), full(1, m)],
        out_specs=pl.BlockSpec((tm, m), lambda i: (i, 0)),
        out_shape=jax.ShapeDtypeStruct((n, m), F32),
        compiler_params=_params("parallel"),
        name="glu_matmul",
    )(a, wv, wg, bv, bg)


def _proj_residual_router_kernel(*refs, n_in, tm):
    a_refs, w_refs = refs[:n_in], refs[n_in:2 * n_in]
    (b_ref, x_ref, gate_ref, g_ref, sc_ref, sh_ref, rw2_ref, rwh_ref, rb_ref,
     o_ref, h_ref, idx_ref, gates_ref, cnt_ref, carry_ref) = refs[2 * n_in:]

    @pl.when(pl.program_id(0) == 0)
    def _():
        carry_ref[...] = jnp.zeros(carry_ref.shape, F32)

    acc = b_ref[...]
    for a_ref, w_ref in zip(a_refs, w_refs):
        acc = acc + jnp.dot(a_ref[...], w_ref[...], preferred_element_type=F32)
    x_new = x_ref[...] + gate_ref[0] * acc
    o_ref[...] = x_new
    h = _modulated_norm(x_new, g_ref[...], sc_ref[0], sh_ref[0])
    h_ref[...] = _pack_halves(h)
    h_hi = h.astype(BF16)
    h_lo = (h - h_hi.astype(F32)).astype(BF16)
    both = jnp.dot(h_hi, rw2_ref[...], preferred_element_type=F32)
    low = jnp.dot(h_lo, rwh_ref[...], preferred_element_type=F32)
    logits = both[:, :LANES] + both[:, LANES:] + low + rb_ref[...]
    idx, gates, total = _route_tile(logits, carry_ref[0:1, :], tm)
    idx_ref[...] = jnp.transpose(idx)[:2 * TOP_K, :].astype(jnp.int32)
    gates_ref[...] = gates
    carry_ref[...] = jnp.broadcast_to(total, carry_ref.shape)
    cnt_ref[...] = jnp.broadcast_to(total, cnt_ref.shape).astype(jnp.int32)


def proj_residual_router(a_list, w_list, bias, x, gate, g, scale, shift, rw_both, rw_hi, rb_pad, seq, tm=512):
    n, d = x.shape
    per = seq // tm
    n_in = len(a_list)
    rows = lambda width: pl.BlockSpec((tm, width), lambda i: (i, 0))
    const = lambda a: pl.BlockSpec(a.shape, lambda i: (0, 0))
    vec = pl.BlockSpec((1, 1, d), lambda i: (i // per, 0, 0))
    in_specs = [rows(a.shape[1]) for a in a_list] + [const(w) for w in w_list]
    in_specs += [const(bias), rows(d), vec, const(g), vec, vec, const(rw_both), const(rw_hi), const(rb_pad)]
    return pl.pallas_call(
        functools.partial(_proj_residual_router_kernel, n_in=n_in, tm=tm),
        grid=(n // tm,),
        in_specs=in_specs,
        out_specs=[rows(d), rows(d // 2), pl.BlockSpec((2 * TOP_K, tm), lambda i: (0, i)), rows(LANES),
                   pl.BlockSpec((8, LANES), lambda i: (0, 0))],
        out_shape=[jax.ShapeDtypeStruct((n, d), F32), jax.ShapeDtypeStruct((n, d // 2), jnp.int32),
                   jax.ShapeDtypeStruct((2 * TOP_K, n), jnp.int32), jax.ShapeDtypeStruct((n, LANES), F32),
                   jax.ShapeDtypeStruct((8, LANES), jnp.int32)],
        scratch_shapes=[pltpu.VMEM((8, LANES), F32)],
        compiler_params=_params("arbitrary"),
        name="proj_residual_router",
    )(*a_list, *w_list, bias, x, gate, g, scale, shift, rw_both, rw_hi, rb_pad)


def _rope(xn, cos_t, sin_t):
    return xn * cos_t + pltpu.roll(xn, LANES // 2, axis=1) * sin_t


def _mla_prep_kernel(z_ref, halo_ref, pw_ref, ps_ref, cqg_ref, wuq_ref, ckvg_ref, wuk_ref, wuv_ref,
                     qg_ref, kg_ref, cos_ref, sin_ref,
                     yp_ref, q_ref, k_ref, v_ref, ext_ref, *, tm, per):
    si = pl.program_id(0) % per
    u = z_ref[:, 0:POOL_WIDTH]
    ext_ref[0:POOL_HALO, :] = jnp.where(si == 0, 0.0, halo_ref[...])
    ext_ref[POOL_HALO:, :] = u
    t = si * tm + lax.broadcasted_iota(jnp.int32, (tm, 1), 0)
    for g, w in enumerate(POOL_WINDOWS):
        cols = slice(g * POOL_GROUP_DIM, (g + 1) * POOL_GROUP_DIM)
        ug = u[:, cols]
        s = ug
        for j in range(1, w):
            s = s + ext_ref[POOL_HALO - j:POOL_HALO - j + tm, cols]
        cnt = jnp.minimum(t + 1, w).astype(F32)
        pooled = s / cnt - ug
        yp = jnp.dot(pooled.astype(BF16), pw_ref[g], preferred_element_type=F32) * ps_ref[:, cols]
        yp_ref[:, cols] = yp.astype(yp_ref.dtype)

    cos_t, sin_t = cos_ref[...], sin_ref[...]
    inv_head = 1.0 / QK_HEAD_DIM

    cq = z_ref[:, Z_CQ:Z_CKV]
    cqn = cq * lax.rsqrt(jnp.mean(cq * cq, axis=-1, keepdims=True) + EPS) * cqg_ref[...]
    qf = jnp.dot(cqn.astype(BF16), wuq_ref[...], preferred_element_type=F32)
    q_scale = QK_HEAD_DIM ** -0.5 * LOG2_E
    for h in range(MLA_HEADS):
        qh = qf[:, h * LANES:(h + 1) * LANES]
        ss = jnp.sum(qh * qh, axis=-1, keepdims=True) * inv_head
        qn = qh * lax.rsqrt(ss + EPS) * qg_ref[...]
        q_ref[0, h] = (_rope(qn, cos_t, sin_t) * q_scale).astype(q_ref.dtype)

    ckv = z_ref[:, Z_CKV:Z_ROPE]
    ckvn = (ckv * lax.rsqrt(jnp.mean(ckv * ckv, axis=-1, keepdims=True) + EPS) * ckvg_ref[...]).astype(BF16)
    kf = jnp.dot(ckvn, wuk_ref[...], preferred_element_type=F32)
    vf = jnp.dot(ckvn, wuv_ref[...], preferred_element_type=F32)
    k_rope = z_ref[:, Z_ROPE:Z_WIDTH]
    ones_lane = lax.broadcasted_iota(jnp.int32, (tm, LANES), 1) == V_HEAD_DIM
    for h in range(MLA_HEADS):
        kh = kf[:, h * LANES:(h + 1) * LANES] + k_rope
        ss = jnp.sum(kh * kh, axis=-1, keepdims=True) * inv_head
        kn = kh * lax.rsqrt(ss + EPS) * kg_ref[...]
        k_ref[0, h] = _rope(kn, cos_t, sin_t).astype(k_ref.dtype)
        v_ref[0, h] = jnp.where(ones_lane, 1.0, vf[:, h * LANES:(h + 1) * LANES]).astype(v_ref.dtype)


def mla_prep(z, pool_w, pool_scale, cq_g, wuq_pad, ckv_g, wuk_pad, wuv_pad, qg_pad, kg_pad,
             cos_t, sin_t, batch, seq, tm=512):
    n = z.shape[0]
    per = seq // tm
    hb = tm // POOL_HALO
    full = lambda a: pl.BlockSpec(a.shape, lambda i: (0,) * a.ndim)
    tab = pl.BlockSpec((tm, LANES), lambda i: (i, 0))
    head_out = pl.BlockSpec((1, MLA_HEADS, tm, LANES), lambda i: (i // per, 0, i % per, 0))
    head_shape = jax.ShapeDtypeStruct((batch, MLA_HEADS, seq, LANES), BF16)
    return pl.pallas_call(
        functools.partial(_mla_prep_kernel, tm=tm, per=per),
        grid=(n // tm,),
        in_specs=[pl.BlockSpec((tm, Z_WIDTH), lambda i: (i, 0)),
                  pl.BlockSpec((POOL_HALO, POOL_WIDTH), lambda i: (jnp.maximum(i * hb - 1, 0), 0)),
                  full(pool_w), full(pool_scale), full(cq_g), full(wuq_pad), full(ckv_g), full(wuk_pad),
                  full(wuv_pad), full(qg_pad), full(kg_pad), tab, tab],
        out_specs=[pl.BlockSpec((tm, POOL_WIDTH), lambda i: (i, 0)), head_out, head_out, head_out],
        out_shape=[jax.ShapeDtypeStruct((n, POOL_WIDTH), BF16), head_shape, head_shape, head_shape],
        scratch_shapes=[pltpu.VMEM((tm + POOL_HALO, POOL_WIDTH), F32)],
        compiler_params=_params("parallel"),
        name="mla_prep",
    )(z, z, pool_w, pool_scale, cq_g, wuq_pad, ckv_g, wuk_pad, wuv_pad, qg_pad, kg_pad, cos_t, sin_t)


def _flash_kernel(q_ref, k_ref, v_ref, o_ref, m_ref, acc_ref, *, tq, tk, heads):
    qi = pl.program_id(2)
    sub = tq // tk
    slabs = tk // LANES
    q_chunk = lax.broadcasted_iota(jnp.int32, (tq, tk), 0) // CHUNK
    k_chunk = lax.broadcasted_iota(jnp.int32, (tq, tk), 1) // CHUNK
    m_ref[...] = jnp.full(m_ref.shape, MASK_VALUE, F32)
    acc_ref[...] = jnp.zeros(acc_ref.shape, F32)

    def step(j, mask, row0=0):
        start = pl.multiple_of(j * tk, tk)
        for hh in range(heads):
            k = k_ref[0, hh, pl.ds(start, tk), :]
            v = v_ref[0, hh, pl.ds(start, tk), :]
            s = lax.dot_general(q_ref[0, hh, row0:, :], k, (((1,), (1,)), ((), ())), preferred_element_type=F32)
            if mask is not None:
                s = jnp.where(mask[row0:], s, MASK_VALUE)
            cols_s = [s[:, c * LANES:(c + 1) * LANES] for c in range(slabs)]
            s_max = cols_s[0]
            for sc in cols_s[1:]:
                s_max = jnp.maximum(s_max, sc)
            m_prev = m_ref[hh, row0:, :]
            m_new = jnp.maximum(m_prev, jnp.max(s_max, axis=-1, keepdims=True))
            alpha = jnp.exp2(m_prev - m_new)
            p = jnp.concatenate([jnp.exp2(sc - m_new) for sc in cols_s], axis=1).astype(v.dtype)
            acc_ref[hh, row0:, :] = alpha * acc_ref[hh, row0:, :] + jnp.dot(p, v, preferred_element_type=F32)
            m_ref[hh, row0:, :] = m_new

    def full_step(j, carry):
        step(j, None)
        return carry

    lax.fori_loop(0, sub * qi, full_step, 0)
    for dd in range(sub):
        step(sub * qi + dd, k_chunk + dd * (tk // CHUNK) <= q_chunk, row0=dd * tk)
    outs = []
    for hh in range(heads):
        acc = acc_ref[hh]
        outs.append((acc / acc[:, V_HEAD_DIM:V_HEAD_DIM + 1])[:, :V_HEAD_DIM])
    o_ref[0] = jnp.concatenate(outs, axis=-1).astype(o_ref.dtype)


def flash_attention(q, k, v, tq=1024, tk=1024):
    b, h, s, _ = q.shape
    heads = LANES // V_HEAD_DIM
    return pl.pallas_call(
        functools.partial(_flash_kernel, tq=tq, tk=tk, heads=heads),
        grid=(b, h // heads, s // tq),
        in_specs=[pl.BlockSpec((1, heads, tq, LANES), lambda bi, hi, qi: (bi, hi, qi, 0)),
                  pl.BlockSpec((1, heads, s, LANES), lambda bi, hi, qi: (bi, hi, 0, 0)),
                  pl.BlockSpec((1, heads, s, LANES), lambda bi, hi, qi: (bi, hi, 0, 0))],
        out_specs=pl.BlockSpec((1, tq, LANES), lambda bi, hi, qi: (bi, qi, hi)),
        out_shape=jax.ShapeDtypeStruct((b, s, h * V_HEAD_DIM), BF16),
        scratch_shapes=[pltpu.VMEM((heads, tq, LANES), F32) for _ in range(2)],
        compiler_params=_params("parallel", "parallel", "parallel"),
        name="flash_attention",
    )(q, k, v)


def _dwconv_kernel(u_ref, halo_ref, w_ref, b_ref, g_ref, beta_ref, o_ref, ext_ref, sh_ref, conv_ref, *, tm, per):
    si = pl.program_id(0) % per
    d = u_ref.shape[1]
    ext_ref[0:CONV_HALO, :] = jnp.where(si == 0, 0.0, halo_ref[...])
    ext_ref[CONV_HALO:, :] = u_ref[...]
    span = sh_ref.shape[1]
    for b in range(1, SUBLANES):
        sh_ref[b - 1] = ext_ref[b:b + span, :]
    first = CONV_HALO - (CONV_WIDTH - 1)
    for r0 in range(0, tm, CONV_PATCH_ROWS):
        for c0 in range(0, d, CONV_PATCH_COLS):
            cols = slice(c0, c0 + CONV_PATCH_COLS)
            view = (CONV_PATCH_ROWS // SUBLANES, SUBLANES, CONV_PATCH_COLS)
            patch = jnp.broadcast_to(b_ref[:, cols], view)
            for j in range(CONV_WIDTH):
                a, b = divmod(first + j, SUBLANES)
                rows = slice(SUBLANES * a + r0, SUBLANES * a + r0 + CONV_PATCH_ROWS)
                win = ext_ref[rows, cols] if b == 0 else sh_ref[b - 1, rows, cols]
                tap = jnp.broadcast_to(w_ref[j:j + 1, cols], view[1:])
                patch = patch + tap[None] * win.reshape(view)
            conv_ref[r0:r0 + CONV_PATCH_ROWS, cols] = patch.reshape(CONV_PATCH_ROWS, CONV_PATCH_COLS)
    acc = conv_ref[...]
    mu = jnp.mean(acc, axis=-1, keepdims=True)
    cen = acc - mu
    var = jnp.mean(cen * cen, axis=-1, keepdims=True)
    y = cen * lax.rsqrt(var + EPS) * g_ref[...] + beta_ref[...]
    o_ref[...] = (y * jax.nn.sigmoid(y)).astype(o_ref.dtype)


def dwconv_ln_silu(u, w_pad, b, g, beta, seq, tm=256):
    n, d = u.shape
    per = seq // tm
    hb = tm // CONV_HALO
    full = lambda a: pl.BlockSpec(a.shape, lambda i: (0, 0))
    return pl.pallas_call(
        functools.partial(_dwconv_kernel, tm=tm, per=per),
        grid=(n // tm,),
        in_specs=[pl.BlockSpec((tm, d), lambda i: (i, 0)),
                  pl.BlockSpec((CONV_HALO, d), lambda i: (jnp.maximum(i * hb - 1, 0), 0)),
                  full(w_pad), full(b), full(g), full(beta)],
        out_specs=pl.BlockSpec((tm, d), lambda i: (i, 0)),
        out_shape=jax.ShapeDtypeStruct((n, d), BF16),
        scratch_shapes=[pltpu.VMEM((tm + CONV_HALO, d), F32),
                        pltpu.VMEM((SUBLANES - 1, tm + CONV_HALO - SUBLANES, d), F32),
                        pltpu.VMEM((tm, d), F32)],
        compiler_params=_params("parallel"),
        name="dwconv_ln_silu",
    )(u, u, w_pad, b, g, beta)


def _route_tile(logits, before_tile, tm):
    lane = lax.broadcasted_iota(jnp.int32, (tm, LANES), 1)
    lane_f = lane.astype(F32)
    neg = -jnp.inf
    logits = jnp.where(lane < N_EXPERTS, logits, neg)
    picks, vals, ids = [], [], []
    for _ in range(TOP_K):
        mx = jnp.max(logits, axis=-1, keepdims=True)
        idx = jnp.min(jnp.where(logits == mx, lane_f, float(LANES)), axis=-1, keepdims=True)
        pick = lane_f == idx
        picks.append(pick)
        vals.append(mx)
        ids.append(idx)
        logits = jnp.where(pick, neg, logits)
    exps = [jnp.exp(v - vals[0]) for v in vals]
    den = exps[0]
    for e in exps[1:]:
        den = den + e

    chosen = jnp.zeros((tm, LANES), F32)
    for pick in picks:
        chosen = chosen + pick.astype(F32)
    r_io = lax.broadcasted_iota(jnp.int32, (tm, tm), 0)
    c_io = lax.broadcasted_iota(jnp.int32, (tm, tm), 1)
    earlier = (c_io < r_io).astype(BF16)
    before = jnp.dot(earlier, chosen.astype(BF16), preferred_element_type=F32) + before_tile
    idx_out = jnp.zeros((tm, LANES), F32)
    gate_out = jnp.zeros((tm, LANES), F32)
    for k in range(TOP_K):
        rank = jnp.sum(jnp.where(picks[k], before, 0.0), axis=-1, keepdims=True)
        idx_out = jnp.where(lane == k, ids[k], idx_out)
        idx_out = jnp.where(lane == TOP_K + k, rank, idx_out)
        gate_out = jnp.where(lane == k, exps[k] / den, gate_out)
    total = before_tile + jnp.sum(chosen, axis=0, keepdims=True)
    return idx_out, gate_out, total


def _sc_worker_base(per_worker):
    return (lax.axis_index("s") * SC_CORES + lax.axis_index("c")) * per_worker


def sc_scatter_rows(src, dest_flat, n_rows):
    n, d = src.shape
    per_w = n // SC_WORKERS
    n_chunks = per_w // SC_CHUNK
    mesh = plsc.VectorSubcoreMesh(core_axis_name="c", subcore_axis_name="s")

    @functools.partial(
        pl.kernel, out_type=jax.ShapeDtypeStruct((n_rows, d), src.dtype), mesh=mesh,
        scratch_types=[pltpu.VMEM((per_w,), jnp.int32) for _ in range(TOP_K)]
        + [pltpu.VMEM((SC_CHUNK, d), src.dtype) for _ in range(2)] + [pltpu.SemaphoreType.DMA] * 4,
        name="sc_scatter_rows")
    def scatter(src_hbm, dest_hbm, out_hbm, *scratch):
        idx_refs = scratch[:TOP_K]
        bufs = scratch[TOP_K:TOP_K + 2]
        in_sems, out_sems = scratch[TOP_K + 2:TOP_K + 4], scratch[TOP_K + 4:TOP_K + 6]
        base = _sc_worker_base(per_w)
        for k, idx_ref in enumerate(idx_refs):
            pltpu.sync_copy(dest_hbm.at[pl.ds(k * n + base, per_w)], idx_ref)

        def load(j, slot):
            return pltpu.make_async_copy(src_hbm.at[pl.ds(base + j * SC_CHUNK, SC_CHUNK)], bufs[slot], in_sems[slot])

        def store_all(j, slot):
            copies = [pltpu.make_async_copy(bufs[slot], out_hbm.at[idx_ref.at[pl.ds(j * SC_CHUNK, SC_CHUNK)]],
                                            out_sems[slot]) for idx_ref in idx_refs]
            for cp in copies:
                cp.start()
            for cp in copies:
                cp.wait()

        load(0, 0).start()

        @pl.loop(0, n_chunks // 2)
        def _(p):
            j = 2 * p
            load(j + 1, 1).start()
            load(j, 0).wait()
            store_all(j, 0)

            @pl.when(j + 2 < n_chunks)
            def _():
                load(j + 2, 0).start()

            load(j + 1, 1).wait()
            store_all(j + 1, 1)

    return scatter(src, dest_flat)


def sc_gather_rows(table, idx):
    b = idx.shape[0]
    d = table.shape[1]
    per_w = b // SC_WORKERS
    n_chunks = per_w // SC_CHUNK
    mesh = plsc.VectorSubcoreMesh(core_axis_name="c", subcore_axis_name="s")

    @functools.partial(
        pl.kernel, out_type=jax.ShapeDtypeStruct((b, d), table.dtype), mesh=mesh,
        scratch_types=[pltpu.VMEM((per_w,), jnp.int32)] + [pltpu.VMEM((SC_CHUNK, d), table.dtype) for _ in range(2)]
        + [pltpu.SemaphoreType.DMA] * 4,
        name="sc_gather_rows")
    def gather(table_hbm, idx_hbm, out_hbm, idx_ref, buf0, buf1, gsem0, gsem1, osem0, osem1):
        bufs, in_sems, out_sems = (buf0, buf1), (gsem0, gsem1), (osem0, osem1)
        base = _sc_worker_base(per_w)
        pltpu.sync_copy(idx_hbm.at[pl.ds(base, per_w)], idx_ref)

        def fetch(j, slot):
            return pltpu.make_async_copy(table_hbm.at[idx_ref.at[pl.ds(j * SC_CHUNK, SC_CHUNK)]], bufs[slot],
                                         in_sems[slot])

        def store(j, slot):
            return pltpu.make_async_copy(bufs[slot], out_hbm.at[pl.ds(base + j * SC_CHUNK, SC_CHUNK)],
                                         out_sems[slot])

        fetch(0, 0).start()

        @pl.loop(0, n_chunks // 2)
        def _(p):
            j = 2 * p

            @pl.when(p > 0)
            def _():
                store(j - 1, 1).wait()

            fetch(j + 1, 1).start()
            fetch(j, 0).wait()
            store(j, 0).start()
            fetch(j + 1, 1).wait()
            store(j, 0).wait()

            @pl.when(j + 2 < n_chunks)
            def _():
                fetch(j + 2, 0).start()

            store(j + 1, 1).start()

        store(n_chunks - 1, 1).wait()

    return gather(table, idx)


def _ffn_kernel(be_ref, nv_ref, nu_ref, xs_ref, w1_ref, b1g_ref, b1u_ref, w2_ref, b2_ref, y_ref,
                wg_ref, wu_ref, w2s_ref):
    i = pl.program_id(0)
    pair = 2 * LANES

    @pl.when(i < nu_ref[0])
    def _():
        e = be_ref[i]
        prev = be_ref[jnp.maximum(i - 1, 0)]

        @pl.when((i == 0) | (e != prev))
        def _():
            r_io = lax.broadcasted_iota(jnp.int32, (pair, pair), 0)
            c_io = lax.broadcasted_iota(jnp.int32, (pair, pair), 1)
            want = jnp.where(c_io < LANES, 2 * c_io, 2 * (c_io - LANES) + 1)
            sel = (r_io == want).astype(BF16)
            for c in range(wg_ref.shape[1] // LANES):
                slab = w1_ref[0, 0, :, c * pair:(c + 1) * pair].astype(BF16)
                split = jnp.dot(slab, sel, preferred_element_type=F32)
                wg_ref[:, c * LANES:(c + 1) * LANES] = split[:, :LANES].astype(BF16)
                wu_ref[:, c * LANES:(c + 1) * LANES] = split[:, LANES:].astype(BF16)
            w2s_ref[...] = w2_ref[0, 0].astype(BF16)

        nvalid = nv_ref[i]

        def ffn_rows(rows):
            row = lax.broadcasted_iota(jnp.int32, (rows, xs_ref.shape[1]), 0)
            lo, hi = _unpack_halves(jnp.where(row < nvalid, xs_ref[0:rows, :], 0))
            x = jnp.concatenate([lo, hi], axis=1).astype(BF16)
            hg = jnp.dot(x, wg_ref[...], preferred_element_type=F32) + b1g_ref[0]
            hu = jnp.dot(x, wu_ref[...], preferred_element_type=F32) + b1u_ref[0]
            gate = jnp.minimum(hg, SWIGLU_LIMIT)
            up = jnp.clip(hu, -SWIGLU_LIMIT, SWIGLU_LIMIT)
            act = gate * jax.nn.sigmoid(SWIGLU_ALPHA * gate) * (up + 1.0)
            y = jnp.dot(act.astype(BF16), w2s_ref[...], preferred_element_type=F32) + b2_ref[0]
            y_ref[0:rows, :] = _pack_halves(y)
            if rows < y_ref.shape[0]:
                y_ref[rows:, :] = jnp.zeros((y_ref.shape[0] - rows, y_ref.shape[1]), y_ref.dtype)

        n_sub = y_ref.shape[0] // MOE_BLOCK
        for sub in range(1, n_sub + 1):
            lo_rows, hi_rows = (sub - 1) * MOE_BLOCK, sub * MOE_BLOCK
            pl.when((nvalid > lo_rows) & (nvalid <= hi_rows))(functools.partial(ffn_rows, hi_rows))

    @pl.when(i >= nu_ref[0])
    def _():
        y_ref[...] = jnp.zeros(y_ref.shape, y_ref.dtype)


def moe_ffn(xs, block_e, block_valid, n_used, layer, w1, b1g, b1u, w2, b2):
    n_rows, half = xs.shape
    d = 2 * half
    n_blocks = n_rows // MOE_STEP
    n_exp, f2 = w1.shape[1], w1.shape[3]
    f = f2 // 2
    rows = lambda i, be, nv, nu: (jnp.minimum(i, nu[0] - 1), 0)
    vec = lambda width: pl.BlockSpec((1, 1, width), lambda i, be, nv, nu: (be[i], 0, 0))
    grid_spec = pltpu.PrefetchScalarGridSpec(
        num_scalar_prefetch=3,
        grid=(n_blocks,),
        in_specs=[
            pl.BlockSpec((MOE_STEP, half), rows),
            pl.BlockSpec((1, 1, d, f2), lambda i, be, nv, nu: (layer, be[i], 0, 0)),
            vec(f), vec(f),
            pl.BlockSpec((1, 1, f, d), lambda i, be, nv, nu: (layer, be[i], 0, 0)),
            vec(d),
        ],
        out_specs=pl.BlockSpec((MOE_STEP, half), lambda i, be, nv, nu: (i, 0)),
        scratch_shapes=[pltpu.VMEM((d, f), BF16), pltpu.VMEM((d, f), BF16), pltpu.VMEM((f, d), BF16)],
    )
    return pl.pallas_call(
        _ffn_kernel,
        grid_spec=grid_spec,
        out_shape=jax.ShapeDtypeStruct((n_rows, half), jnp.int32),
        compiler_params=_params("arbitrary"),
        name="moe_ffn",
    )(block_e, block_valid, n_used, xs, w1, b1g.reshape(n_exp, 1, f), b1u.reshape(n_exp, 1, f), w2,
      b2.reshape(n_exp, 1, d))


def _combine_kernel(x_ref, y_ref, gates_ref, gate_ref, *rest, with_norm):
    g = gates_ref[...]
    half = y_ref.shape[2]
    acc_lo = acc_hi = None
    for k in range(TOP_K):
        lo, hi = _unpack_halves(y_ref[k])
        gk = g[:, k:k + 1]
        acc_lo = gk * lo if acc_lo is None else acc_lo + gk * lo
        acc_hi = gk * hi if acc_hi is None else acc_hi + gk * hi
    x_lo = x_ref[:, :half] + gate_ref[0, :, :half] * acc_lo
    x_hi = x_ref[:, half:] + gate_ref[0, :, half:] * acc_hi
    if with_norm:
        ng_ref, sc_ref, sh_ref = rest[:3]
        o_ref, h_ref = rest[-2:]
        x_new = jnp.concatenate([x_lo, x_hi], axis=1)
        o_ref[...] = x_new
        h_ref[...] = _modulated_norm(x_new, ng_ref[...], sc_ref[0], sh_ref[0]).astype(h_ref.dtype)
    else:
        o_ref = rest[-1]
        o_ref[:, :half] = x_lo
        o_ref[:, half:] = x_hi


def moe_combine(x, y_part, gates, gate_mod, seq, next_norm=None, part=0, n_parts=1, prev=(), tm=512):
    n, d = x.shape
    per = seq // tm
    steps = n // (tm * n_parts)
    first = part * steps
    rows = lambda width: pl.BlockSpec((tm, width), lambda i: (i + first, 0))
    vec = pl.BlockSpec((1, 1, d), lambda i: ((i + first) // per, 0, 0))
    in_specs = [rows(d), pl.BlockSpec((TOP_K, tm, d // 2), lambda i: (0, i, 0)), rows(LANES), vec]
    args = [x, y_part, gates, gate_mod]
    out_specs, out_shape = [rows(d)], [jax.ShapeDtypeStruct((n, d), F32)]
    if next_norm is not None:
        in_specs += [pl.BlockSpec((1, d), lambda i: (0, 0)), vec, vec]
        args += list(next_norm)
        out_specs, out_shape = out_specs + [rows(d)], out_shape + [jax.ShapeDtypeStruct((n, d), BF16)]
    aliases = {len(args) + j: j for j in range(len(prev))}
    in_specs += [pl.BlockSpec(memory_space=pl.ANY) for _ in prev]
    args += list(prev)
    return pl.pallas_call(
        functools.partial(_combine_kernel, with_norm=next_norm is not None),
        grid=(steps,),
        in_specs=in_specs,
        out_specs=out_specs,
        out_shape=out_shape,
        input_output_aliases=aliases,
        compiler_params=_params("parallel"),
        name="moe_combine",
    )(*args)


def moe_layer(x, mixer, mix_gate, mods, norm_g, router_w, router_b, layer, w1, b1, w2, b2, seq, next_norm):
    n, d = x.shape
    shift, scale, gate = mods
    rw_pad = jnp.pad(router_w, ((0, 0), (0, LANES - N_EXPERTS)))
    rw_hi = rw_pad.astype(BF16)
    rw_lo = (rw_pad - rw_hi.astype(F32)).astype(BF16)
    rb_pad = jnp.pad(router_b, (0, LANES - N_EXPERTS)).reshape(1, LANES)
    x, h, idx, gates, counts = proj_residual_router(
        *mixer, x, mix_gate, norm_g.reshape(1, d), scale, shift, jnp.concatenate([rw_hi, rw_lo], axis=1), rw_hi,
        rb_pad, seq)

    top_i = idx[:TOP_K]
    rank = idx[TOP_K:]
    counts = counts[0, :N_EXPERTS]
    experts = jnp.arange(N_EXPERTS, dtype=jnp.int32)
    padded = (counts + MOE_STEP - 1) // MOE_STEP * MOE_STEP
    pad_ends = jnp.sum(jnp.where(experts[:, None] >= experts[None, :], padded[None, :], 0), axis=1)
    pad_starts = pad_ends - padded
    dest = jnp.sum(jnp.where(top_i[..., None] == experts, pad_starts, 0), axis=-1) + rank
    n_blocks = -(-n * TOP_K // MOE_STEP) + N_EXPERTS
    block_start = jnp.arange(n_blocks, dtype=jnp.int32) * MOE_STEP
    block_e = jnp.minimum(jnp.sum((pad_ends[None, :] <= block_start[:, None]).astype(jnp.int32), axis=1),
                          N_EXPERTS - 1)
    n_used = (pad_ends[N_EXPERTS - 1:] // MOE_STEP).astype(jnp.int32)
    seg_end = jnp.sum(jnp.where(block_e[:, None] == experts, pad_starts + counts, 0), axis=1)
    block_valid = jnp.clip(seg_end - block_start, 0, MOE_STEP).astype(jnp.int32)
    dest_flat = dest.reshape(-1)

    xs = sc_scatter_rows(h, dest_flat, n_blocks * MOE_STEP)
    ys = moe_ffn(xs, block_e, block_valid, n_used, layer, w1, b1[layer][:, 0::2], b1[layer][:, 1::2], w2, b2[layer])
    outs = ()
    per_part = n // COMBINE_PARTS
    for part in range(COMBINE_PARTS):
        idx_part = dest[:, part * per_part:(part + 1) * per_part].reshape(-1)
        y = sc_gather_rows(ys, idx_part).reshape(TOP_K, per_part, d // 2)
        outs = moe_combine(x, y, gates, gate, seq, next_norm, part, COMBINE_PARTS, tuple(outs))
    return (outs[0], outs[1]) if next_norm is not None else (outs[0], None)


def _pad_heads(w, width):
    k = w.shape[0]
    w = w.reshape(k, MLA_HEADS, width)
    return jnp.pad(w, ((0, 0), (0, 0), (0, LANES - width))).reshape(k, MLA_HEADS * LANES)


def _head_lane_source():
    half = QK_ROPE_DIM // 2
    first_nope = LANES // 2 - half
    lanes = (list(range(QK_NOPE_DIM, QK_NOPE_DIM + half)) + list(range(first_nope))
             + list(range(QK_NOPE_DIM + half, QK_HEAD_DIM)) + list(range(first_nope, QK_NOPE_DIM)))
    return jnp.array(lanes + [QK_HEAD_DIM] * (LANES - QK_HEAD_DIM), jnp.int32)


def _to_head_lanes(w):
    w = jnp.concatenate([w, jnp.zeros(w.shape[:-1] + (1,), w.dtype)], axis=-1)
    return jnp.take(w, _head_lane_source(), axis=-1)


def _rope_tables(positions):
    inv = 1.0 / (ROPE_THETA ** (jnp.arange(0, QK_ROPE_DIM, 2, dtype=F32) / QK_ROPE_DIM))
    ang = inv[:, None] * positions.reshape(-1).astype(F32)[None, :]
    cos, sin = jnp.cos(ang).T, jnp.sin(ang).T
    n = cos.shape[0]
    half = QK_ROPE_DIM // 2
    first_nope = LANES // 2 - half
    ones = lambda w: jnp.ones((n, w), F32)
    zeros = lambda w: jnp.zeros((n, w), F32)
    pad = LANES - QK_HEAD_DIM
    cos_t = jnp.concatenate([cos, ones(first_nope), cos, ones(QK_NOPE_DIM - first_nope), zeros(pad)], axis=1)
    sin_t = jnp.concatenate([-sin, zeros(first_nope), sin, zeros(QK_NOPE_DIM - first_nope + pad)], axis=1)
    return cos_t, sin_t


def _split_mods(m, batch):
    d = m.shape[-1] // 3
    m = m[:batch]
    return tuple(m[:, None, j * d:(j + 1) * d] for j in range(3))


def pool_mla_mixer(x, h, mods, norm_g, tables, w_in, pool_w, pool_scale, cq_norm_g, w_uq, ckv_norm_g, w_ukv,
                   q_norm_g, k_norm_g, w_out, batch, seq):
    n, d = x.shape
    shift, scale, _ = mods
    rope_cols = _to_head_lanes(jnp.pad(w_in[:, Z_ROPE:], ((0, 0), (QK_NOPE_DIM, 0))))
    w_in_pad = jnp.concatenate([w_in[:, :Z_ROPE], rope_cols], axis=1).astype(BF16)
    if h is None:
        z = norm_matmul(x, norm_g.reshape(1, d), scale, shift, w_in_pad, seq)
    else:
        z = matmul(h, w_in_pad)
    w_ukv_h = w_ukv.reshape(KV_LORA_RANK, MLA_HEADS, QK_NOPE_DIM + V_HEAD_DIM)
    k_nope = jnp.pad(w_ukv_h[:, :, :QK_NOPE_DIM], ((0, 0), (0, 0), (0, QK_ROPE_DIM)))
    wuk_pad = _to_head_lanes(k_nope).reshape(KV_LORA_RANK, MLA_HEADS * LANES).astype(BF16)
    wuv_pad = _pad_heads(w_ukv_h[:, :, QK_NOPE_DIM:].reshape(KV_LORA_RANK, -1), V_HEAD_DIM).astype(BF16)
    wuq_pad = _to_head_lanes(w_uq.reshape(Q_LORA_RANK, MLA_HEADS, QK_HEAD_DIM))
    wuq_pad = wuq_pad.reshape(Q_LORA_RANK, MLA_HEADS * LANES).astype(BF16)
    pad_g = lambda g: _to_head_lanes(g).reshape(1, LANES)
    y_pool, q, k, v = mla_prep(z, pool_w.astype(BF16), pool_scale.reshape(1, -1), cq_norm_g.reshape(1, -1), wuq_pad,
                               ckv_norm_g.reshape(1, -1), wuk_pad, wuv_pad, pad_g(q_norm_g), pad_g(k_norm_g),
                               *tables, batch, seq)
    y_att = flash_attention(q, k, v).reshape(n, MLA_HEADS * V_HEAD_DIM)
    w_out_b = w_out.astype(BF16)
    return [y_pool, y_att], [w_out_b[:POOL_WIDTH], w_out_b[POOL_WIDTH:]], jnp.zeros((1, d), F32)


def conformer_mixer(h, pw1_w, pw1_b, dw_w, dw_b, ln_g, ln_b, pw2_w, pw2_b, seq):
    d = pw2_w.shape[1]
    cd = pw1_w.shape[1] // 2
    pw1 = pw1_w.astype(BF16)
    u = glu_matmul(h, pw1[:, :cd], pw1[:, cd:], pw1_b[:cd].reshape(1, cd), pw1_b[cd:].reshape(1, cd))
    w_pad = jnp.pad(dw_w, ((0, CONV_HALO - CONV_WIDTH), (0, 0)))
    u = dwconv_ln_silu(u, w_pad, dw_b.reshape(1, cd), ln_g.reshape(1, cd), ln_b.reshape(1, cd), seq)
    return [u], [pw2_w.astype(BF16)], pw2_b.reshape(1, d)


def kernel(x, c, positions, ada_mix_w, ada_mix_b, norm_mix_g, w_in, pool_w, pool_scale, cq_norm_g, w_uq,
           ckv_norm_g, w_ukv, q_norm_g, k_norm_g, w_out, conv_pw1_w, conv_pw1_b, conv_dw_w, conv_dw_b,
           conv_ln_g, conv_ln_b, conv_pw2_w, conv_pw2_b, ada_ffn_w, ada_ffn_b, norm_ffn_g, router_w,
           router_b, moe_w1, moe_b1, moe_w2, moe_b2):
    batch, seq, d = x.shape
    depth = ada_mix_w.shape[0]
    c_pad = jnp.pad(c, ((0, 8 - batch), (0, 0)))
    mix_mods = adaln(c_pad, ada_mix_w, ada_mix_b)
    ffn_mods = adaln(c_pad, ada_ffn_w, ada_ffn_b)
    tables = _rope_tables(positions)
    xf = x.reshape(batch * seq, d)
    h = None
    for layer in range(depth):
        i = layer // 2
        mods = _split_mods(mix_mods[layer], batch)
        if layer % 2 == 0:
            mixer = pool_mla_mixer(xf, h, mods, norm_mix_g[layer], tables, w_in[i], pool_w[i], pool_scale[i],
                                   cq_norm_g[i], w_uq[i], ckv_norm_g[i], w_ukv[i], q_norm_g[i], k_norm_g[i],
                                   w_out[i], batch, seq)
        else:
            mixer = conformer_mixer(h, conv_pw1_w[i], conv_pw1_b[i], conv_dw_w[i], conv_dw_b[i],
                                    conv_ln_g[i], conv_ln_b[i], conv_pw2_w[i], conv_pw2_b[i], seq)
        next_norm = None
        if layer + 1 < depth:
            n_shift, n_scale, _ = _split_mods(mix_mods[layer + 1], batch)
            next_norm = (norm_mix_g[layer + 1].reshape(1, d), n_scale, n_shift)
        xf, h = moe_layer(xf, mixer, mods[2], _split_mods(ffn_mods[layer], batch), norm_ffn_g[layer],
                          router_w[layer], router_b[layer], layer, moe_w1, moe_b1, moe_w2, moe_b2, seq, next_norm)
    return xf.reshape(batch, seq, d)
```

```python
import functools

import jax
import jax.numpy as jnp
from jax import lax
from jax.experimental import pallas as pl
from jax.experimental.pallas import tpu as pltpu
from jax.experimental.pallas import tpu_sc as plsc

F32 = jnp.float32
BF16 = jnp.bfloat16
HIGHEST = lax.Precision.HIGHEST

EPS = 1e-6
POOL_WINDOWS = (2, 4, 8, 16)
POOL_GROUP_DIM = 128
POOL_WIDTH = POOL_GROUP_DIM * len(POOL_WINDOWS)
MLA_HEADS = 8
QK_NOPE_DIM = 64
QK_ROPE_DIM = 32
QK_HEAD_DIM = QK_NOPE_DIM + QK_ROPE_DIM
V_HEAD_DIM = 64
Q_LORA_RANK = 384
KV_LORA_RANK = 256
ROPE_THETA = 10000.0
CHUNK = 64
CONV_WIDTH = 31
N_EXPERTS = 32
TOP_K = 4
SWIGLU_ALPHA = 1.702
SWIGLU_LIMIT = 7.0
MOE_BLOCK = 256
MOE_CHUNK = 512
MOE_STEP = 1024

LANES = 128
SUBLANES = 8
CONV_PATCH_ROWS = 64
CONV_PATCH_COLS = 256
POOL_HALO = 16
CONV_HALO = 32
MASK_VALUE = -1e30
LOG2_E = 1.4426950408889634
VMEM_LIMIT = 52 * 1024 * 1024
SC_CORES = 2
SC_SUBCORES = 16
SC_WORKERS = SC_CORES * SC_SUBCORES
SC_CHUNK = 64
COMBINE_PARTS = 2

Z_CQ = POOL_WIDTH
Z_CKV = Z_CQ + Q_LORA_RANK
Z_ROPE = Z_CKV + KV_LORA_RANK
Z_WIDTH = Z_ROPE + LANES


def _params(*sem, vmem=None):
    return pltpu.CompilerParams(dimension_semantics=sem, vmem_limit_bytes=vmem or VMEM_LIMIT,
                                disable_bounds_checks=True)


def _adaln_kernel(c_ref, w_ref, b_ref, o_ref):
    c = c_ref[...]
    s = c * jax.nn.sigmoid(c)
    o_ref[0] = jnp.dot(s, w_ref[0], preferred_element_type=F32, precision=HIGHEST) + b_ref[0]


def adaln(c_pad, w, b):
    n_l, d, d3 = w.shape
    tn = 512
    return pl.pallas_call(
        _adaln_kernel,
        grid=(n_l, d3 // tn),
        in_specs=[pl.BlockSpec((8, d), lambda l, j: (0, 0)),
                  pl.BlockSpec((1, d, tn), lambda l, j: (l, 0, j)),
                  pl.BlockSpec((1, 1, tn), lambda l, j: (l, 0, j))],
        out_specs=pl.BlockSpec((1, 8, tn), lambda l, j: (l, 0, j)),
        out_shape=jax.ShapeDtypeStruct((n_l, 8, d3), F32),
        compiler_params=_params("parallel", "parallel"),
        name="adaln",
    )(c_pad, w, b.reshape(n_l, 1, d3))


def _modulated_norm(x, g, scale, shift):
    ms = jnp.mean(x * x, axis=-1, keepdims=True)
    return x * lax.rsqrt(ms + EPS) * g * (1.0 + scale) + shift


def _pack_halves(x):
    w = x.shape[1] // 2
    lo = lax.bitcast_convert_type(x[:, :w].astype(BF16).astype(F32), jnp.uint32)
    hi = lax.bitcast_convert_type(x[:, w:].astype(BF16).astype(F32), jnp.uint32)
    return lax.bitcast_convert_type((lo >> 16) | (hi & jnp.uint32(0xFFFF0000)), jnp.int32)


def _unpack_halves(p):
    u = lax.bitcast_convert_type(p, jnp.uint32)
    lo = lax.bitcast_convert_type(u << 16, F32)
    hi = lax.bitcast_convert_type(u & jnp.uint32(0xFFFF0000), F32)
    return lo, hi


def _norm_matmul_kernel(x_ref, g_ref, sc_ref, sh_ref, w_ref, o_ref):
    h = _modulated_norm(x_ref[...], g_ref[...], sc_ref[0], sh_ref[0])
    o_ref[...] = jnp.dot(h.astype(BF16), w_ref[...], preferred_element_type=F32)


def norm_matmul(x, g, scale, shift, w, seq, tm=512):
    n, d = x.shape
    m = w.shape[1]
    per = seq // tm
    vec = pl.BlockSpec((1, 1, d), lambda i: (i // per, 0, 0))
    return pl.pallas_call(
        _norm_matmul_kernel,
        grid=(n // tm,),
        in_specs=[pl.BlockSpec((tm, d), lambda i: (i, 0)), pl.BlockSpec((1, d), lambda i: (0, 0)), vec, vec,
                  pl.BlockSpec((d, m), lambda i: (0, 0))],
        out_specs=pl.BlockSpec((tm, m), lambda i: (i, 0)),
        out_shape=jax.ShapeDtypeStruct((n, m), F32),
        compiler_params=_params("parallel"),
        name="norm_matmul",
    )(x, g, scale, shift, w)


def _matmul_kernel(a_ref, w_ref, o_ref):
    o_ref[...] = jnp.dot(a_ref[...], w_ref[...], preferred_element_type=F32).astype(o_ref.dtype)


def matmul(a, w, out_dtype=F32, tm=512):
    n, k = a.shape
    m = w.shape[1]
    return pl.pallas_call(
        _matmul_kernel,
        grid=(n // tm,),
        in_specs=[pl.BlockSpec((tm, k), lambda i: (i, 0)), pl.BlockSpec((k, m), lambda i: (0, 0))],
        out_specs=pl.BlockSpec((tm, m), lambda i: (i, 0)),
        out_shape=jax.ShapeDtypeStruct((n, m), out_dtype),
        compiler_params=_params("parallel"),
        name="matmul",
    )(a, w)


def _glu_kernel(a_ref, wv_ref, wg_ref, bv_ref, bg_ref, o_ref):
    a = a_ref[...]
    val = jnp.dot(a, wv_ref[...], preferred_element_type=F32) + bv_ref[...]
    gt = jnp.dot(a, wg_ref[...], preferred_element_type=F32) + bg_ref[...]
    o_ref[...] = val * jax.nn.sigmoid(gt)


def glu_matmul(a, wv, wg, bv, bg, tm=512):
    n, k = a.shape
    m = wv.shape[1]
    full = lambda r, c: pl.BlockSpec((r, c), lambda i: (0, 0))
    return pl.pallas_call(
        _glu_kernel,
        grid=(n // tm,),
        in_specs=[pl.BlockSpec((tm, k), lambda i: (i, 0)), full(k, m), full(k, m), full(1, m), full(1, m)],
        out_specs=pl.BlockSpec((tm, m), lambda i: (i, 0)),
        out_shape=jax.ShapeDtypeStruct((n, m), F32),
        compiler_params=_params("parallel"),
        name="glu_matmul",
    )(a, wv, wg, bv, bg)


def _proj_residual_router_kernel(*refs, n_in, tm):
    a_refs, w_refs = refs[:n_in], refs[n_in:2 * n_in]
    (b_ref, x_ref, gate_ref, g_ref, sc_ref, sh_ref, rw2_ref, rwh_ref, rb_ref,
     o_ref, h_ref, idx_ref, gates_ref, cnt_ref, carry_ref) = refs[2 * n_in:]

    @pl.when(pl.program_id(0) == 0)
    def _():
        carry_ref[...] = jnp.zeros(carry_ref.shape, F32)

    acc = b_ref[...]
    for a_ref, w_ref in zip(a_refs, w_refs):
        acc = acc + jnp.dot(a_ref[...], w_ref[...], preferred_element_type=F32)
    x_new = x_ref[...] + gate_ref[0] * acc
    o_ref[...] = x_new
    h = _modulated_norm(x_new, g_ref[...], sc_ref[0], sh_ref[0])
    h_ref[...] = _pack_halves(h)
    h_hi = h.astype(BF16)
    h_lo = (h - h_hi.astype(F32)).astype(BF16)
    both = jnp.dot(h_hi, rw2_ref[...], preferred_element_type=F32)
    low = jnp.dot(h_lo, rwh_ref[...], preferred_element_type=F32)
    logits = both[:, :LANES] + both[:, LANES:] + low + rb_ref[...]
    idx, gates, total = _route_tile(logits, carry_ref[0:1, :], tm)
    idx_ref[...] = jnp.transpose(idx)[:2 * TOP_K, :].astype(jnp.int32)
    gates_ref[...] = gates
    carry_ref[...] = jnp.broadcast_to(total, carry_ref.shape)
    cnt_ref[...] = jnp.broadcast_to(total, cnt_ref.shape).astype(jnp.int32)


def proj_residual_router(a_list, w_list, bias, x, gate, g, scale, shift, rw_both, rw_hi, rb_pad, seq, tm=512):
    n, d = x.shape
    per = seq // tm
    n_in = len(a_list)
    rows = lambda width: pl.BlockSpec((tm, width), lambda i: (i, 0))
    const = lambda a: pl.BlockSpec(a.shape, lambda i: (0, 0))
    vec = pl.BlockSpec((1, 1, d), lambda i: (i // per, 0, 0))
    in_specs = [rows(a.shape[1]) for a in a_list] + [const(w) for w in w_list]
    in_specs += [const(bias), rows(d), vec, const(g), vec, vec, const(rw_both), const(rw_hi), const(rb_pad)]
    return pl.pallas_call(
        functools.partial(_proj_residual_router_kernel, n_in=n_in, tm=tm),
        grid=(n // tm,),
        in_specs=in_specs,
        out_specs=[rows(d), rows(d // 2), pl.BlockSpec((2 * TOP_K, tm), lambda i: (0, i)), rows(LANES),
                   pl.BlockSpec((8, LANES), lambda i: (0, 0))],
        out_shape=[jax.ShapeDtypeStruct((n, d), F32), jax.ShapeDtypeStruct((n, d // 2), jnp.int32),
                   jax.ShapeDtypeStruct((2 * TOP_K, n), jnp.int32), jax.ShapeDtypeStruct((n, LANES), F32),
                   jax.ShapeDtypeStruct((8, LANES), jnp.int32)],
        scratch_shapes=[pltpu.VMEM((8, LANES), F32)],
        compiler_params=_params("arbitrary"),
        name="proj_residual_router",
    )(*a_list, *w_list, bias, x, gate, g, scale, shift, rw_both, rw_hi, rb_pad)


def _rope(xn, cos_t, sin_t):
    return xn * cos_t + pltpu.roll(xn, LANES // 2, axis=1) * sin_t


def _mla_prep_kernel(z_ref, halo_ref, pw_ref, ps_ref, cqg_ref, wuq_ref, ckvg_ref, wuk_ref, wuv_ref,
                     qg_ref, kg_ref, cos_ref, sin_ref,
                     yp_ref, q_ref, k_ref, v_ref, ext_ref, *, tm, per):
    si = pl.program_id(0) % per
    u = z_ref[:, 0:POOL_WIDTH]
    ext_ref[0:POOL_HALO, :] = jnp.where(si == 0, 0.0, halo_ref[...])
    ext_ref[POOL_HALO:, :] = u
    t = si * tm + lax.broadcasted_iota(jnp.int32, (tm, 1), 0)
    for g, w in enumerate(POOL_WINDOWS):
        cols = slice(g * POOL_GROUP_DIM, (g + 1) * POOL_GROUP_DIM)
        ug = u[:, cols]
        s = ug
        for j in range(1, w):
            s = s + ext_ref[POOL_HALO - j:POOL_HALO - j + tm, cols]
        cnt = jnp.minimum(t + 1, w).astype(F32)
        pooled = s / cnt - ug
        yp = jnp.dot(pooled.astype(BF16), pw_ref[g], preferred_element_type=F32) * ps_ref[:, cols]
        yp_ref[:, cols] = yp.astype(yp_ref.dtype)

    cos_t, sin_t = cos_ref[...], sin_ref[...]
    inv_head = 1.0 / QK_HEAD_DIM

    cq = z_ref[:, Z_CQ:Z_CKV]
    cqn = cq * lax.rsqrt(jnp.mean(cq * cq, axis=-1, keepdims=True) + EPS) * cqg_ref[...]
    qf = jnp.dot(cqn.astype(BF16), wuq_ref[...], preferred_element_type=F32)
    q_scale = QK_HEAD_DIM ** -0.5 * LOG2_E
    for h in range(MLA_HEADS):
        qh = qf[:, h * LANES:(h + 1) * LANES]
        ss = jnp.sum(qh * qh, axis=-1, keepdims=True) * inv_head
        qn = qh * lax.rsqrt(ss + EPS) * qg_ref[...]
        q_ref[0, h] = (_rope(qn, cos_t, sin_t) * q_scale).astype(q_ref.dtype)

    ckv = z_ref[:, Z_CKV:Z_ROPE]
    ckvn = (ckv * lax.rsqrt(jnp.mean(ckv * ckv, axis=-1, keepdims=True) + EPS) * ckvg_ref[...]).astype(BF16)
    kf = jnp.dot(ckvn, wuk_ref[...], preferred_element_type=F32)
    vf = jnp.dot(ckvn, wuv_ref[...], preferred_element_type=F32)
    k_rope = z_ref[:, Z_ROPE:Z_WIDTH]
    ones_lane = lax.broadcasted_iota(jnp.int32, (tm, LANES), 1) == V_HEAD_DIM
    for h in range(MLA_HEADS):
        kh = kf[:, h * LANES:(h + 1) * LANES] + k_rope
        ss = jnp.sum(kh * kh, axis=-1, keepdims=True) * inv_head
        kn = kh * lax.rsqrt(ss + EPS) * kg_ref[...]
        k_ref[0, h] = _rope(kn, cos_t, sin_t).astype(k_ref.dtype)
        v_ref[0, h] = jnp.where(ones_lane, 1.0, vf[:, h * LANES:(h + 1) * LANES]).astype(v_ref.dtype)


def mla_prep(z, pool_w, pool_scale, cq_g, wuq_pad, ckv_g, wuk_pad, wuv_pad, qg_pad, kg_pad,
             cos_t, sin_t, batch, seq, tm=512):
    n = z.shape[0]
    per = seq // tm
    hb = tm // POOL_HALO
    full = lambda a: pl.BlockSpec(a.shape, lambda i: (0,) * a.ndim)
    tab = pl.BlockSpec((tm, LANES), lambda i: (i, 0))
    head_out = pl.BlockSpec((1, MLA_HEADS, tm, LANES), lambda i: (i // per, 0, i % per, 0))
    head_shape = jax.ShapeDtypeStruct((batch, MLA_HEADS, seq, LANES), BF16)
    return pl.pallas_call(
        functools.partial(_mla_prep_kernel, tm=tm, per=per),
        grid=(n // tm,),
        in_specs=[pl.BlockSpec((tm, Z_WIDTH), lambda i: (i, 0)),
                  pl.BlockSpec((POOL_HALO, POOL_WIDTH), lambda i: (jnp.maximum(i * hb - 1, 0), 0)),
                  full(pool_w), full(pool_scale), full(cq_g), full(wuq_pad), full(ckv_g), full(wuk_pad),
                  full(wuv_pad), full(qg_pad), full(kg_pad), tab, tab],
        out_specs=[pl.BlockSpec((tm, POOL_WIDTH), lambda i: (i, 0)), head_out, head_out, head_out],
        out_shape=[jax.ShapeDtypeStruct((n, POOL_WIDTH), BF16), head_shape, head_shape, head_shape],
        scratch_shapes=[pltpu.VMEM((tm + POOL_HALO, POOL_WIDTH), F32)],
        compiler_params=_params("parallel"),
        name="mla_prep",
    )(z, z, pool_w, pool_scale, cq_g, wuq_pad, ckv_g, wuk_pad, wuv_pad, qg_pad, kg_pad, cos_t, sin_t)


def _flash_kernel(q_ref, k_ref, v_ref, o_ref, m_ref, acc_ref, *, tq, tk, heads):
    qi = pl.program_id(2)
    sub = tq // tk
    slabs = tk // LANES
    q_chunk = lax.broadcasted_iota(jnp.int32, (tq, tk), 0) // CHUNK
    k_chunk = lax.broadcasted_iota(jnp.int32, (tq, tk), 1) // CHUNK
    m_ref[...] = jnp.full(m_ref.shape, MASK_VALUE, F32)
    acc_ref[...] = jnp.zeros(acc_ref.shape, F32)

    def step(j, mask, row0=0):
        start = pl.multiple_of(j * tk, tk)
        for hh in range(heads):
            k = k_ref[0, hh, pl.ds(start, tk), :]
            v = v_ref[0, hh, pl.ds(start, tk), :]
            s = lax.dot_general(q_ref[0, hh, row0:, :], k, (((1,), (1,)), ((), ())), preferred_element_type=F32)
            if mask is not None:
                s = jnp.where(mask[row0:], s, MASK_VALUE)
            cols_s = [s[:, c * LANES:(c + 1) * LANES] for c in range(slabs)]
            s_max = cols_s[0]
            for sc in cols_s[1:]:
                s_max = jnp.maximum(s_max, sc)
            m_prev = m_ref[hh, row0:, :]
            m_new = jnp.maximum(m_prev, jnp.max(s_max, axis=-1, keepdims=True))
            alpha = jnp.exp2(m_prev - m_new)
            p = jnp.concatenate([jnp.exp2(sc - m_new) for sc in cols_s], axis=1).astype(v.dtype)
            acc_ref[hh, row0:, :] = alpha * acc_ref[hh, row0:, :] + jnp.dot(p, v, preferred_element_type=F32)
            m_ref[hh, row0:, :] = m_new

    def full_step(j, carry):
        step(j, None)
        return carry

    lax.fori_loop(0, sub * qi, full_step, 0)
    for dd in range(sub):
        step(sub * qi + dd, k_chunk + dd * (tk // CHUNK) <= q_chunk, row0=dd * tk)
    outs = []
    for hh in range(heads):
        acc = acc_ref[hh]
        outs.append((acc / acc[:, V_HEAD_DIM:V_HEAD_DIM + 1])[:, :V_HEAD_DIM])
    o_ref[0] = jnp.concatenate(outs, axis=-1).astype(o_ref.dtype)


def flash_attention(q, k, v, tq=1024, tk=1024):
    b, h, s, _ = q.shape
    heads = LANES // V_HEAD_DIM
    return pl.pallas_call(
        functools.partial(_flash_kernel, tq=tq, tk=tk, heads=heads),
        grid=(b, h // heads, s // tq),
        in_specs=[pl.BlockSpec((1, heads, tq, LANES), lambda bi, hi, qi: (bi, hi, qi, 0)),
                  pl.BlockSpec((1, heads, s, LANES), lambda bi, hi, qi: (bi, hi, 0, 0)),
                  pl.BlockSpec((1, heads, s, LANES), lambda bi, hi, qi: (bi, hi, 0, 0))],
        out_specs=pl.BlockSpec((1, tq, LANES), lambda bi, hi, qi: (bi, qi, hi)),
        out_shape=jax.ShapeDtypeStruct((b, s, h * V_HEAD_DIM), BF16),
        scratch_shapes=[pltpu.VMEM((heads, tq, LANES), F32) for _ in range(2)],
        compiler_params=_params("parallel", "parallel", "parallel"),
        name="flash_attention",
    )(q, k, v)


def _dwconv_kernel(u_ref, halo_ref, w_ref, b_ref, g_ref, beta_ref, o_ref, ext_ref, sh_ref, conv_ref, *, tm, per):
    si = pl.program_id(0) % per
    d = u_ref.shape[1]
    ext_ref[0:CONV_HALO, :] = jnp.where(si == 0, 0.0, halo_ref[...])
    ext_ref[CONV_HALO:, :] = u_ref[...]
    span = sh_ref.shape[1]
    for b in range(1, SUBLANES):
        sh_ref[b - 1] = ext_ref[b:b + span, :]
    first = CONV_HALO - (CONV_WIDTH - 1)
    for r0 in range(0, tm, CONV_PATCH_ROWS):
        for c0 in range(0, d, CONV_PATCH_COLS):
            cols = slice(c0, c0 + CONV_PATCH_COLS)
            view = (CONV_PATCH_ROWS // SUBLANES, SUBLANES, CONV_PATCH_COLS)
            patch = jnp.broadcast_to(b_ref[:, cols], view)
            for j in range(CONV_WIDTH):
                a, b = divmod(first + j, SUBLANES)
                rows = slice(SUBLANES * a + r0, SUBLANES * a + r0 + CONV_PATCH_ROWS)
                win = ext_ref[rows, cols] if b == 0 else sh_ref[b - 1, rows, cols]
                tap = jnp.broadcast_to(w_ref[j:j + 1, cols], view[1:])
                patch = patch + tap[None] * win.reshape(view)
            conv_ref[r0:r0 + CONV_PATCH_ROWS, cols] = patch.reshape(CONV_PATCH_ROWS, CONV_PATCH_COLS)
    acc = conv_ref[...]
    mu = jnp.mean(acc, axis=-1, keepdims=True)
    cen = acc - mu
    var = jnp.mean(cen * cen, axis=-1, keepdims=True)
    y = cen * lax.rsqrt(var + EPS) * g_ref[...] + beta_ref[...]
    o_ref[...] = (y * jax.nn.sigmoid(y)).astype(o_ref.dtype)


def dwconv_ln_silu(u, w_pad, b, g, beta, seq, tm=256):
    n, d = u.shape
    per = seq // tm
    hb = tm // CONV_HALO
    full = lambda a: pl.BlockSpec(a.shape, lambda i: (0, 0))
    return pl.pallas_call(
        functools.partial(_dwconv_kernel, tm=tm, per=per),
        grid=(n // tm,),
        in_specs=[pl.BlockSpec((tm, d), lambda i: (i, 0)),
                  pl.BlockSpec((CONV_HALO, d), lambda i: (jnp.maximum(i * hb - 1, 0), 0)),
                  full(w_pad), full(b), full(g), full(beta)],
        out_specs=pl.BlockSpec((tm, d), lambda i: (i, 0)),
        out_shape=jax.ShapeDtypeStruct((n, d), BF16),
        scratch_shapes=[pltpu.VMEM((tm + CONV_HALO, d), F32),
                        pltpu.VMEM((SUBLANES - 1, tm + CONV_HALO - SUBLANES, d), F32),
                        pltpu.VMEM((tm, d), F32)],
        compiler_params=_params("parallel"),
        name="dwconv_ln_silu",
    )(u, u, w_pad, b, g, beta)


def _route_tile(logits, before_tile, tm):
    lane = lax.broadcasted_iota(jnp.int32, (tm, LANES), 1)
    lane_f = lane.astype(F32)
    neg = -jnp.inf
    logits = jnp.where(lane < N_EXPERTS, logits, neg)
    picks, vals, ids = [], [], []
    for _ in range(TOP_K):
        mx = jnp.max(logits, axis=-1, keepdims=True)
        idx = jnp.min(jnp.where(logits == mx, lane_f, float(LANES)), axis=-1, keepdims=True)
        pick = lane_f == idx
        picks.append(pick)
        vals.append(mx)
        ids.append(idx)
        logits = jnp.where(pick, neg, logits)
    exps = [jnp.exp(v - vals[0]) for v in vals]
    den = exps[0]
    for e in exps[1:]:
        den = den + e

    chosen = jnp.zeros((tm, LANES), F32)
    for pick in picks:
        chosen = chosen + pick.astype(F32)
    r_io = lax.broadcasted_iota(jnp.int32, (tm, tm), 0)
    c_io = lax.broadcasted_iota(jnp.int32, (tm, tm), 1)
    earlier = (c_io < r_io).astype(BF16)
    before = jnp.dot(earlier, chosen.astype(BF16), preferred_element_type=F32) + before_tile
    idx_out = jnp.zeros((tm, LANES), F32)
    gate_out = jnp.zeros((tm, LANES), F32)
    for k in range(TOP_K):
        rank = jnp.sum(jnp.where(picks[k], before, 0.0), axis=-1, keepdims=True)
        idx_out = jnp.where(lane == k, ids[k], idx_out)
        idx_out = jnp.where(lane == TOP_K + k, rank, idx_out)
        gate_out = jnp.where(lane == k, exps[k] / den, gate_out)
    total = before_tile + jnp.sum(chosen, axis=0, keepdims=True)
    return idx_out, gate_out, total


def _sc_worker_base(per_worker):
    return (lax.axis_index("s") * SC_CORES + lax.axis_index("c")) * per_worker


def sc_scatter_rows(src, dest_flat, n_rows):
    n, d = src.shape
    per_w = n // SC_WORKERS
    n_chunks = per_w // SC_CHUNK
    mesh = plsc.VectorSubcoreMesh(core_axis_name="c", subcore_axis_name="s")

    @functools.partial(
        pl.kernel, out_type=jax.ShapeDtypeStruct((n_rows, d), src.dtype), mesh=mesh,
        scratch_types=[pltpu.VMEM((per_w,), jnp.int32) for _ in range(TOP_K)]
        + [pltpu.VMEM((SC_CHUNK, d), src.dtype) for _ in range(2)] + [pltpu.SemaphoreType.DMA] * 4,
        name="sc_scatter_rows")
    def scatter(src_hbm, dest_hbm, out_hbm, *scratch):
        idx_refs = scratch[:TOP_K]
        bufs = scratch[TOP_K:TOP_K + 2]
        in_sems, out_sems = scratch[TOP_K + 2:TOP_K + 4], scratch[TOP_K + 4:TOP_K + 6]
        base = _sc_worker_base(per_w)
        for k, idx_ref in enumerate(idx_refs):
            pltpu.sync_copy(dest_hbm.at[pl.ds(k * n + base, per_w)], idx_ref)

        def load(j, slot):
            return pltpu.make_async_copy(src_hbm.at[pl.ds(base + j * SC_CHUNK, SC_CHUNK)], bufs[slot], in_sems[slot])

        def store_all(j, slot):
            copies = [pltpu.make_async_copy(bufs[slot], out_hbm.at[idx_ref.at[pl.ds(j * SC_CHUNK, SC_CHUNK)]],
                                            out_sems[slot]) for idx_ref in idx_refs]
            for cp in copies:
                cp.start()
            for cp in copies:
                cp.wait()

        load(0, 0).start()

        @pl.loop(0, n_chunks // 2)
        def _(p):
            j = 2 * p
            load(j + 1, 1).start()
            load(j, 0).wait()
            store_all(j, 0)

            @pl.when(j + 2 < n_chunks)
            def _():
                load(j + 2, 0).start()

            load(j + 1, 1).wait()
            store_all(j + 1, 1)

    return scatter(src, dest_flat)


def sc_gather_rows(table, idx):
    b = idx.shape[0]
    d = table.shape[1]
    per_w = b // SC_WORKERS
    n_chunks = per_w // SC_CHUNK
    mesh = plsc.VectorSubcoreMesh(core_axis_name="c", subcore_axis_name="s")

    @functools.partial(
        pl.kernel, out_type=jax.ShapeDtypeStruct((b, d), table.dtype), mesh=mesh,
        scratch_types=[pltpu.VMEM((per_w,), jnp.int32)] + [pltpu.VMEM((SC_CHUNK, d), table.dtype) for _ in range(2)]
        + [pltpu.SemaphoreType.DMA] * 4,
        name="sc_gather_rows")
    def gather(table_hbm, idx_hbm, out_hbm, idx_ref, buf0, buf1, gsem0, gsem1, osem0, osem1):
        bufs, in_sems, out_sems = (buf0, buf1), (gsem0, gsem1), (osem0, osem1)
        base = _sc_worker_base(per_w)
        pltpu.sync_copy(idx_hbm.at[pl.ds(base, per_w)], idx_ref)

        def fetch(j, slot):
            return pltpu.make_async_copy(table_hbm.at[idx_ref.at[pl.ds(j * SC_CHUNK, SC_CHUNK)]], bufs[slot],
                                         in_sems[slot])

        def store(j, slot):
            return pltpu.make_async_copy(bufs[slot], out_hbm.at[pl.ds(base + j * SC_CHUNK, SC_CHUNK)],
                                         out_sems[slot])

        fetch(0, 0).start()

        @pl.loop(0, n_chunks // 2)
        def _(p):
            j = 2 * p

            @pl.when(p > 0)
            def _():
                store(j - 1, 1).wait()

            fetch(j + 1, 1).start()
            fetch(j, 0).wait()
            store(j, 0).start()
            fetch(j + 1, 1).wait()
            store(j, 0).wait()

            @pl.when(j + 2 < n_chunks)
            def _():
                fetch(j + 2, 0).start()

            store(j + 1, 1).start()

        store(n_chunks - 1, 1).wait()

    return gather(table, idx)


def _ffn_kernel(be_ref, nv_ref, nu_ref, xs_ref, w1_ref, b1g_ref, b1u_ref, w2_ref, b2_ref, y_ref,
                wg_ref, wu_ref, w2s_ref):
    i = pl.program_id(0)
    pair = 2 * LANES

    @pl.when(i < nu_ref[0])
    def _():
        e = be_ref[i]
        prev = be_ref[jnp.maximum(i - 1, 0)]

        @pl.when((i == 0) | (e != prev))
        def _():
            r_io = lax.broadcasted_iota(jnp.int32, (pair, pair), 0)
            c_io = lax.broadcasted_iota(jnp.int32, (pair, pair), 1)
            want = jnp.where(c_io < LANES, 2 * c_io, 2 * (c_io - LANES) + 1)
            sel = (r_io == want).astype(BF16)
            for c in range(wg_ref.shape[1] // LANES):
                slab = w1_ref[0, 0, :, c * pair:(c + 1) * pair].astype(BF16)
                split = jnp.dot(slab, sel, preferred_element_type=F32)
                wg_ref[:, c * LANES:(c + 1) * LANES] = split[:, :LANES].astype(BF16)
                wu_ref[:, c * LANES:(c + 1) * LANES] = split[:, LANES:].astype(BF16)
            w2s_ref[...] = w2_ref[0, 0].astype(BF16)

        nvalid = nv_ref[i]

        def ffn_rows(r0, rows):
            row = r0 + lax.broadcasted_iota(jnp.int32, (rows, xs_ref.shape[1]), 0)
            lo, hi = _unpack_halves(jnp.where(row < nvalid, xs_ref[r0:r0 + rows, :], 0))
            x = jnp.concatenate([lo, hi], axis=1).astype(BF16)
            hg = jnp.dot(x, wg_ref[...], preferred_element_type=F32) + b1g_ref[0]
            hu = jnp.dot(x, wu_ref[...], preferred_element_type=F32) + b1u_ref[0]
            gate = jnp.minimum(hg, SWIGLU_LIMIT)
            up = jnp.clip(hu, -SWIGLU_LIMIT, SWIGLU_LIMIT)
            act = gate * jax.nn.sigmoid(SWIGLU_ALPHA * gate) * (up + 1.0)
            y = jnp.dot(act.astype(BF16), w2s_ref[...], preferred_element_type=F32) + b2_ref[0]
            y_ref[r0:r0 + rows, :] = _pack_halves(y)

        def zero_rows(r0, rows):
            y_ref[r0:r0 + rows, :] = jnp.zeros((rows, y_ref.shape[1]), y_ref.dtype)

        for r0 in range(0, y_ref.shape[0], MOE_CHUNK):
            left = nvalid - r0

            @pl.when(left > MOE_BLOCK)
            def _(r0=r0):
                ffn_rows(r0, MOE_CHUNK)

            @pl.when((left > 0) & (left <= MOE_BLOCK))
            def _(r0=r0):
                ffn_rows(r0, MOE_BLOCK)
                zero_rows(r0 + MOE_BLOCK, MOE_CHUNK - MOE_BLOCK)

            @pl.when(left <= 0)
            def _(r0=r0):
                zero_rows(r0, MOE_CHUNK)

    @pl.when(i >= nu_ref[0])
    def _():
        y_ref[...] = jnp.zeros(y_ref.shape, y_ref.dtype)


def moe_ffn(xs, block_e, block_valid, n_used, layer, w1, b1g, b1u, w2, b2):
    n_rows, half = xs.shape
    d = 2 * half
    n_blocks = n_rows // MOE_STEP
    n_exp, f2 = w1.shape[1], w1.shape[3]
    f = f2 // 2
    rows = lambda i, be, nv, nu: (jnp.minimum(i, nu[0] - 1), 0)
    vec = lambda width: pl.BlockSpec((1, 1, width), lambda i, be, nv, nu: (be[i], 0, 0))
    grid_spec = pltpu.PrefetchScalarGridSpec(
        num_scalar_prefetch=3,
        grid=(n_blocks,),
        in_specs=[
            pl.BlockSpec((MOE_STEP, half), rows),
            pl.BlockSpec((1, 1, d, f2), lambda i, be, nv, nu: (layer, be[i], 0, 0)),
            vec(f), vec(f),
            pl.BlockSpec((1, 1, f, d), lambda i, be, nv, nu: (layer, be[i], 0, 0)),
            vec(d),
        ],
        out_specs=pl.BlockSpec((MOE_STEP, half), lambda i, be, nv, nu: (i, 0)),
        scratch_shapes=[pltpu.VMEM((d, f), BF16), pltpu.VMEM((d, f), BF16), pltpu.VMEM((f, d), BF16)],
    )
    return pl.pallas_call(
        _ffn_kernel,
        grid_spec=grid_spec,
        out_shape=jax.ShapeDtypeStruct((n_rows, half), jnp.int32),
        compiler_params=_params("arbitrary"),
        name="moe_ffn",
    )(block_e, block_valid, n_used, xs, w1, b1g.reshape(n_exp, 1, f), b1u.reshape(n_exp, 1, f), w2,
      b2.reshape(n_exp, 1, d))


def _combine_kernel(x_ref, y_ref, gates_ref, gate_ref, *rest, with_norm):
    g = gates_ref[...]
    half = y_ref.shape[2]
    acc_lo = acc_hi = None
    for k in range(TOP_K):
        lo, hi = _unpack_halves(y_ref[k])
        gk = g[:, k:k + 1]
        acc_lo = gk * lo if acc_lo is None else acc_lo + gk * lo
        acc_hi = gk * hi if acc_hi is None else acc_hi + gk * hi
    x_lo = x_ref[:, :half] + gate_ref[0, :, :half] * acc_lo
    x_hi = x_ref[:, half:] + gate_ref[0, :, half:] * acc_hi
    if with_norm:
        ng_ref, sc_ref, sh_ref = rest[:3]
        o_ref, h_ref = rest[-2:]
        x_new = jnp.concatenate([x_lo, x_hi], axis=1)
        o_ref[...] = x_new
        h_ref[...] = _modulated_norm(x_new, ng_ref[...], sc_ref[0], sh_ref[0]).astype(h_ref.dtype)
    else:
        o_ref = rest[-1]
        o_ref[:, :half] = x_lo
        o_ref[:, half:] = x_hi


def moe_combine(x, y_part, gates, gate_mod, seq, next_norm=None, part=0, n_parts=1, prev=(), tm=512):
    n, d = x.shape
    per = seq // tm
    steps = n // (tm * n_parts)
    first = part * steps
    rows = lambda width: pl.BlockSpec((tm, width), lambda i: (i + first, 0))
    vec = pl.BlockSpec((1, 1, d), lambda i: ((i + first) // per, 0, 0))
    in_specs = [rows(d), pl.BlockSpec((TOP_K, tm, d // 2), lambda i: (0, i, 0)), rows(LANES), vec]
    args = [x, y_part, gates, gate_mod]
    out_specs, out_shape = [rows(d)], [jax.ShapeDtypeStruct((n, d), F32)]
    if next_norm is not None:
        in_specs += [pl.BlockSpec((1, d), lambda i: (0, 0)), vec, vec]
        args += list(next_norm)
        out_specs, out_shape = out_specs + [rows(d)], out_shape + [jax.ShapeDtypeStruct((n, d), BF16)]
    aliases = {len(args) + j: j for j in range(len(prev))}
    in_specs += [pl.BlockSpec(memory_space=pl.ANY) for _ in prev]
    args += list(prev)
    return pl.pallas_call(
        functools.partial(_combine_kernel, with_norm=next_norm is not None),
        grid=(steps,),
        in_specs=in_specs,
        out_specs=out_specs,
        out_shape=out_shape,
        input_output_aliases=aliases,
        compiler_params=_params("parallel"),
        name="moe_combine",
    )(*args)


def moe_layer(x, mixer, mix_gate, mods, norm_g, router_w, router_b, layer, w1, b1, w2, b2, seq, next_norm):
    n, d = x.shape
    shift, scale, gate = mods
    rw_pad = jnp.pad(router_w, ((0, 0), (0, LANES - N_EXPERTS)))
    rw_hi = rw_pad.astype(BF16)
    rw_lo = (rw_pad - rw_hi.astype(F32)).astype(BF16)
    rb_pad = jnp.pad(router_b, (0, LANES - N_EXPERTS)).reshape(1, LANES)
    x, h, idx, gates, counts = proj_residual_router(
        *mixer, x, mix_gate, norm_g.reshape(1, d), scale, shift, jnp.concatenate([rw_hi, rw_lo], axis=1), rw_hi,
        rb_pad, seq)

    top_i = idx[:TOP_K]
    rank = idx[TOP_K:]
    counts = counts[0, :N_EXPERTS]
    experts = jnp.arange(N_EXPERTS, dtype=jnp.int32)
    padded = (counts + MOE_STEP - 1) // MOE_STEP * MOE_STEP
    pad_ends = jnp.sum(jnp.where(experts[:, None] >= experts[None, :], padded[None, :], 0), axis=1)
    pad_starts = pad_ends - padded
    dest = jnp.sum(jnp.where(top_i[..., None] == experts, pad_starts, 0), axis=-1) + rank
    n_blocks = -(-n * TOP_K // MOE_STEP) + N_EXPERTS
    block_start = jnp.arange(n_blocks, dtype=jnp.int32) * MOE_STEP
    block_e = jnp.minimum(jnp.sum((pad_ends[None, :] <= block_start[:, None]).astype(jnp.int32), axis=1),
                          N_EXPERTS - 1)
    n_used = (pad_ends[N_EXPERTS - 1:] // MOE_STEP).astype(jnp.int32)
    seg_end = jnp.sum(jnp.where(block_e[:, None] == experts, pad_starts + counts, 0), axis=1)
    block_valid = jnp.clip(seg_end - block_start, 0, MOE_STEP).astype(jnp.int32)
    dest_flat = dest.reshape(-1)

    xs = sc_scatter_rows(h, dest_flat, n_blocks * MOE_STEP)
    ys = moe_ffn(xs, block_e, block_valid, n_used, layer, w1, b1[layer][:, 0::2], b1[layer][:, 1::2], w2, b2[layer])
    outs = ()
    per_part = n // COMBINE_PARTS
    for part in range(COMBINE_PARTS):
        idx_part = dest[:, part * per_part:(part + 1) * per_part].reshape(-1)
        y = sc_gather_rows(ys, idx_part).reshape(TOP_K, per_part, d // 2)
        outs = moe_combine(x, y, gates, gate, seq, next_norm, part, COMBINE_PARTS, tuple(outs))
    return (outs[0], outs[1]) if next_norm is not None else (outs[0], None)


def _pad_heads(w, width):
    k = w.shape[0]
    w = w.reshape(k, MLA_HEADS, width)
    return jnp.pad(w, ((0, 0), (0, 0), (0, LANES - width))).reshape(k, MLA_HEADS * LANES)


def _head_lane_source():
    half = QK_ROPE_DIM // 2
    first_nope = LANES // 2 - half
    lanes = (list(range(QK_NOPE_DIM, QK_NOPE_DIM + half)) + list(range(first_nope))
             + list(range(QK_NOPE_DIM + half, QK_HEAD_DIM)) + list(range(first_nope, QK_NOPE_DIM)))
    return jnp.array(lanes + [QK_HEAD_DIM] * (LANES - QK_HEAD_DIM), jnp.int32)


def _to_head_lanes(w):
    w = jnp.concatenate([w, jnp.zeros(w.shape[:-1] + (1,), w.dtype)], axis=-1)
    return jnp.take(w, _head_lane_source(), axis=-1)


def _rope_tables(positions):
    inv = 1.0 / (ROPE_THETA ** (jnp.arange(0, QK_ROPE_DIM, 2, dtype=F32) / QK_ROPE_DIM))
    ang = inv[:, None] * positions.reshape(-1).astype(F32)[None, :]
    cos, sin = jnp.cos(ang).T, jnp.sin(ang).T
    n = cos.shape[0]
    half = QK_ROPE_DIM // 2
    first_nope = LANES // 2 - half
    ones = lambda w: jnp.ones((n, w), F32)
    zeros = lambda w: jnp.zeros((n, w), F32)
    pad = LANES - QK_HEAD_DIM
    cos_t = jnp.concatenate([cos, ones(first_nope), cos, ones(QK_NOPE_DIM - first_nope), zeros(pad)], axis=1)
    sin_t = jnp.concatenate([-sin, zeros(first_nope), sin, zeros(QK_NOPE_DIM - first_nope + pad)], axis=1)
    return cos_t, sin_t


def _split_mods(m, batch):
    d = m.shape[-1] // 3
    m = m[:batch]
    return tuple(m[:, None, j * d:(j + 1) * d] for j in range(3))


def pool_mla_mixer(x, h, mods, norm_g, tables, w_in, pool_w, pool_scale, cq_norm_g, w_uq, ckv_norm_g, w_ukv,
                   q_norm_g, k_norm_g, w_out, batch, seq):
    n, d = x.shape
    shift, scale, _ = mods
    rope_cols = _to_head_lanes(jnp.pad(w_in[:, Z_ROPE:], ((0, 0), (QK_NOPE_DIM, 0))))
    w_in_pad = jnp.concatenate([w_in[:, :Z_ROPE], rope_cols], axis=1).astype(BF16)
    if h is None:
        z = norm_matmul(x, norm_g.reshape(1, d), scale, shift, w_in_pad, seq)
    else:
        z = matmul(h, w_in_pad)
    w_ukv_h = w_ukv.reshape(KV_LORA_RANK, MLA_HEADS, QK_NOPE_DIM + V_HEAD_DIM)
    k_nope = jnp.pad(w_ukv_h[:, :, :QK_NOPE_DIM], ((0, 0), (0, 0), (0, QK_ROPE_DIM)))
    wuk_pad = _to_head_lanes(k_nope).reshape(KV_LORA_RANK, MLA_HEADS * LANES).astype(BF16)
    wuv_pad = _pad_heads(w_ukv_h[:, :, QK_NOPE_DIM:].reshape(KV_LORA_RANK, -1), V_HEAD_DIM).astype(BF16)
    wuq_pad = _to_head_lanes(w_uq.reshape(Q_LORA_RANK, MLA_HEADS, QK_HEAD_DIM))
    wuq_pad = wuq_pad.reshape(Q_LORA_RANK, MLA_HEADS * LANES).astype(BF16)
    pad_g = lambda g: _to_head_lanes(g).reshape(1, LANES)
    y_pool, q, k, v = mla_prep(z, pool_w.astype(BF16), pool_scale.reshape(1, -1), cq_norm_g.reshape(1, -1), wuq_pad,
                               ckv_norm_g.reshape(1, -1), wuk_pad, wuv_pad, pad_g(q_norm_g), pad_g(k_norm_g),
                               *tables, batch, seq)
    y_att = flash_attention(q, k, v).reshape(n, MLA_HEADS * V_HEAD_DIM)
    w_out_b = w_out.astype(BF16)
    return [y_pool, y_att], [w_out_b[:POOL_WIDTH], w_out_b[POOL_WIDTH:]], jnp.zeros((1, d), F32)


def conformer_mixer(h, pw1_w, pw1_b, dw_w, dw_b, ln_g, ln_b, pw2_w, pw2_b, seq):
    d = pw2_w.shape[1]
    cd = pw1_w.shape[1] // 2
    pw1 = pw1_w.astype(BF16)
    u = glu_matmul(h, pw1[:, :cd], pw1[:, cd:], pw1_b[:cd].reshape(1, cd), pw1_b[cd:].reshape(1, cd))
    w_pad = jnp.pad(dw_w, ((0, CONV_HALO - CONV_WIDTH), (0, 0)))
    u = dwconv_ln_silu(u, w_pad, dw_b.reshape(1, cd), ln_g.reshape(1, cd), ln_b.reshape(1, cd), seq)
    return [u], [pw2_w.astype(BF16)], pw2_b.reshape(1, d)


def kernel(x, c, positions, ada_mix_w, ada_mix_b, norm_mix_g, w_in, pool_w, pool_scale, cq_norm_g, w_uq,
           ckv_norm_g, w_ukv, q_norm_g, k_norm_g, w_out, conv_pw1_w, conv_pw1_b, conv_dw_w, conv_dw_b,
           conv_ln_g, conv_ln_b, conv_pw2_w, conv_pw2_b, ada_ffn_w, ada_ffn_b, norm_ffn_g, router_w,
           router_b, moe_w1, moe_b1, moe_w2, moe_b2):
    batch, seq, d = x.shape
    depth = ada_mix_w.shape[0]
    c_pad = jnp.pad(c, ((0, 8 - batch), (0, 0)))
    mix_mods = adaln(c_pad, ada_mix_w, ada_mix_b)
    ffn_mods = adaln(c_pad, ada_ffn_w, ada_ffn_b)
    tables = _rope_tables(positions)
    xf = x.reshape(batch * seq, d)
    h = None
    for layer in range(depth):
        i = layer // 2
        mods = _split_mods(mix_mods[layer], batch)
        if layer % 2 == 0:
            mixer = pool_mla_mixer(xf, h, mods, norm_mix_g[layer], tables, w_in[i], pool_w[i], pool_scale[i],
                                   cq_norm_g[i], w_uq[i], ckv_norm_g[i], w_ukv[i], q_norm_g[i], k_norm_g[i],
                                   w_out[i], batch, seq)
        else:
            mixer = conformer_mixer(h, conv_pw1_w[i], conv_pw1_b[i], conv_dw_w[i], conv_dw_b[i],
                                    conv_ln_g[i], conv_ln_b[i], conv_pw2_w[i], conv_pw2_b[i], seq)
        next_norm = None
        if layer + 1 < depth:
            n_shift, n_scale, _ = _split_mods(mix_mods[layer + 1], batch)
            next_norm = (norm_mix_g[layer + 1].reshape(1, d), n_scale, n_shift)
        xf, h = moe_layer(xf, mixer, mods[2], _split_mods(ffn_mods[layer], batch), norm_ffn_g[layer],
                          router_w[layer], router_b[layer], layer, moe_w1, moe_b1, moe_w2, moe_b2, seq, next_norm)
    return xf.reshape(batch, seq, d)
```

```python
import functools

import jax
import jax.numpy as jnp
from jax import lax
from jax.experimental import pallas as pl
from jax.experimental.pallas import tpu as pltpu
from jax.experimental.pallas import tpu_sc as plsc

F32 = jnp.float32
BF16 = jnp.bfloat16
HIGHEST = lax.Precision.HIGHEST

EPS = 1e-6
POOL_WINDOWS = (2, 4, 8, 16)
POOL_GROUP_DIM = 128
POOL_WIDTH = POOL_GROUP_DIM * len(POOL_WINDOWS)
MLA_HEADS = 8
QK_NOPE_DIM = 64
QK_ROPE_DIM = 32
QK_HEAD_DIM = QK_NOPE_DIM + QK_ROPE_DIM
V_HEAD_DIM = 64
Q_LORA_RANK = 384
KV_LORA_RANK = 256
ROPE_THETA = 10000.0
CHUNK = 64
CONV_WIDTH = 31
N_EXPERTS = 32
TOP_K = 4
SWIGLU_ALPHA = 1.702
SWIGLU_LIMIT = 7.0
MOE_BLOCK = 256
MOE_CHUNK = 512
MOE_STEP = 1024

LANES = 128
SUBLANES = 8
CONV_PATCH_ROWS = 64
CONV_PATCH_COLS = 256
POOL_HALO = 16
CONV_HALO = 32
MASK_VALUE = -1e30
LOG2_E = 1.4426950408889634
VMEM_LIMIT = 52 * 1024 * 1024
SC_CORES = 2
SC_SUBCORES = 16
SC_WORKERS = SC_CORES * SC_SUBCORES
SC_CHUNK = 64
COMBINE_PARTS = 2

Z_CQ = POOL_WIDTH
Z_CKV = Z_CQ + Q_LORA_RANK
Z_ROPE = Z_CKV + KV_LORA_RANK
Z_WIDTH = Z_ROPE + LANES


def _params(*sem, vmem=None):
    return pltpu.CompilerParams(dimension_semantics=sem, vmem_limit_bytes=vmem or VMEM_LIMIT,
                                disable_bounds_checks=True)


def _adaln_kernel(c_ref, w_ref, b_ref, o_ref):
    c = c_ref[...]
    s = c * jax.nn.sigmoid(c)
    o_ref[0] = jnp.dot(s, w_ref[0], preferred_element_type=F32, precision=HIGHEST) + b_ref[0]


def adaln(c_pad, w, b):
    n_l, d, d3 = w.shape
    tn = 512
    return pl.pallas_call(
        _adaln_kernel,
        grid=(n_l, d3 // tn),
        in_specs=[pl.BlockSpec((8, d), lambda l, j: (0, 0)),
                  pl.BlockSpec((1, d, tn), lambda l, j: (l, 0, j)),
                  pl.BlockSpec((1, 1, tn), lambda l, j: (l, 0, j))],
        out_specs=pl.BlockSpec((1, 8, tn), lambda l, j: (l, 0, j)),
        out_shape=jax.ShapeDtypeStruct((n_l, 8, d3), F32),
        compiler_params=_params("parallel", "parallel"),
        name="adaln",
    )(c_pad, w, b.reshape(n_l, 1, d3))


def _modulated_norm(x, g, scale, shift):
    ms = jnp.mean(x * x, axis=-1, keepdims=True)
    return x * lax.rsqrt(ms + EPS) * g * (1.0 + scale) + shift


def _pack_halves(x):
    w = x.shape[1] // 2
    lo = lax.bitcast_convert_type(x[:, :w].astype(BF16).astype(F32), jnp.uint32)
    hi = lax.bitcast_convert_type(x[:, w:].astype(BF16).astype(F32), jnp.uint32)
    return lax.bitcast_convert_type((lo >> 16) | (hi & jnp.uint32(0xFFFF0000)), jnp.int32)


def _unpack_halves(p):
    u = lax.bitcast_convert_type(p, jnp.uint32)
    lo = lax.bitcast_convert_type(u << 16, F32)
    hi = lax.bitcast_convert_type(u & jnp.uint32(0xFFFF0000), F32)
    return lo, hi


def _norm_matmul_kernel(x_ref, g_ref, sc_ref, sh_ref, w_ref, o_ref):
    h = _modulated_norm(x_ref[...], g_ref[...], sc_ref[0], sh_ref[0])
    o_ref[...] = jnp.dot(h.astype(BF16), w_ref[...], preferred_element_type=F32)


def norm_matmul(x, g, scale, shift, w, seq, tm=512):
    n, d = x.shape
    m = w.shape[1]
    per = seq // tm
    vec = pl.BlockSpec((1, 1, d), lambda i: (i // per, 0, 0))
    return pl.pallas_call(
        _norm_matmul_kernel,
        grid=(n // tm,),
        in_specs=[pl.BlockSpec((tm, d), lambda i: (i, 0)), pl.BlockSpec((1, d), lambda i: (0, 0)), vec, vec,
                  pl.BlockSpec((d, m), lambda i: (0, 0))],
        out_specs=pl.BlockSpec((tm, m), lambda i: (i, 0)),
        out_shape=jax.ShapeDtypeStruct((n, m), F32),
        compiler_params=_params("parallel"),
        name="norm_matmul",
    )(x, g, scale, shift, w)


def _matmul_kernel(a_ref, w_ref, o_ref):
    o_ref[...] = jnp.dot(a_ref[...], w_ref[...], preferred_element_type=F32).astype(o_ref.dtype)


def matmul(a, w, out_dtype=F32, tm=512):
    n, k = a.shape
    m = w.shape[1]
    return pl.pallas_call(
        _matmul_kernel,
        grid=(n // tm,),
        in_specs=[pl.BlockSpec((tm, k), lambda i: (i, 0)), pl.BlockSpec((k, m), lambda i: (0, 0))],
        out_specs=pl.BlockSpec((tm, m), lambda i: (i, 0)),
        out_shape=jax.ShapeDtypeStruct((n, m), out_dtype),
        compiler_params=_params("parallel"),
        name="matmul",
    )(a, w)


def _glu_kernel(a_ref, wv_ref, wg_ref, bv_ref, bg_ref, o_ref):
    a = a_ref[...]
    val = jnp.dot(a, wv_ref[...], preferred_element_type=F32) + bv_ref[...]
    gt = jnp.dot(a, wg_ref[...], preferred_element_type=F32) + bg_ref[...]
    o_ref[...] = val * jax.nn.sigmoid(gt)


def glu_matmul(a, wv, wg, bv, bg, tm=512):
    n, k = a.shape
    m = wv.shape[1]
    full = lambda r, c: pl.BlockSpec((r, c), lambda i: (0, 0))
    return pl.pallas_call(
        _glu_kernel,
        grid=(n // tm,),
        in_specs=[pl.BlockSpec((tm, k), lambda i: (i, 0)), full(k, m), full(k, m), full(1, m), full(1, m)],
        out_specs=pl.BlockSpec((tm, m), lambda i: (i, 0)),
        out_shape=jax.ShapeDtypeStruct((n, m), F32),
        compiler_params=_params("parallel"),
        name="glu_matmul",
    )(a, wv, wg, bv, bg)


def _proj_residual_router_kernel(*refs, n_in, tm):
    a_refs, w_refs = refs[:n_in], refs[n_in:2 * n_in]
    (b_ref, x_ref, gate_ref, g_ref, sc_ref, sh_ref, rw2_ref, rwh_ref, rb_ref,
     o_ref, h_ref, idx_ref, gates_ref, cnt_ref, carry_ref) = refs[2 * n_in:]

    @pl.when(pl.program_id(0) == 0)
    def _():
        carry_ref[...] = jnp.zeros(carry_ref.shape, F32)

    acc = b_ref[...]
    for a_ref, w_ref in zip(a_refs, w_refs):
        acc = acc + jnp.dot(a_ref[...], w_ref[...], preferred_element_type=F32)
    x_new = x_ref[...] + gate_ref[0] * acc
    o_ref[...] = x_new
    h = _modulated_norm(x_new, g_ref[...], sc_ref[0], sh_ref[0])
    h_ref[...] = _pack_halves(h)
    h_hi = h.astype(BF16)
    h_lo = (h - h_hi.astype(F32)).astype(BF16)
    contract_features = (((1,), (1,)), ((), ()))
    both = lax.dot_general(rw2_ref[...], h_hi, contract_features, preferred_element_type=F32)
    low = lax.dot_general(rwh_ref[...], h_lo, contract_features, preferred_element_type=F32)
    logits = (both[:LANES] + both[LANES:] + low)[:N_EXPERTS] + rb_ref[...]
    idx, gates, total = _route_tile(logits, carry_ref[:, 0:1], tm)
    idx_ref[...] = idx.astype(jnp.int32)
    gate_rows = jnp.concatenate([gates, jnp.zeros((LANES - TOP_K, tm), F32)], axis=0)
    gates_ref[...] = jnp.transpose(gate_rows)
    carry_ref[...] = jnp.broadcast_to(total, carry_ref.shape)
    cnt_ref[...] = jnp.broadcast_to(total, cnt_ref.shape).astype(jnp.int32)


def proj_residual_router(a_list, w_list, bias, x, gate, g, scale, shift, rw_both, rw_hi, rb_pad, seq, tm=512):
    n, d = x.shape
    per = seq // tm
    n_in = len(a_list)
    rows = lambda width: pl.BlockSpec((tm, width), lambda i: (i, 0))
    const = lambda a: pl.BlockSpec(a.shape, lambda i: (0, 0))
    vec = pl.BlockSpec((1, 1, d), lambda i: (i // per, 0, 0))
    in_specs = [rows(a.shape[1]) for a in a_list] + [const(w) for w in w_list]
    in_specs += [const(bias), rows(d), vec, const(g), vec, vec, const(rw_both), const(rw_hi), const(rb_pad)]
    return pl.pallas_call(
        functools.partial(_proj_residual_router_kernel, n_in=n_in, tm=tm),
        grid=(n // tm,),
        in_specs=in_specs,
        out_specs=[rows(d), rows(d // 2), pl.BlockSpec((2 * TOP_K, tm), lambda i: (0, i)), rows(LANES),
                   pl.BlockSpec((N_EXPERTS, LANES), lambda i: (0, 0))],
        out_shape=[jax.ShapeDtypeStruct((n, d), F32), jax.ShapeDtypeStruct((n, d // 2), jnp.int32),
                   jax.ShapeDtypeStruct((2 * TOP_K, n), jnp.int32), jax.ShapeDtypeStruct((n, LANES), F32),
                   jax.ShapeDtypeStruct((N_EXPERTS, LANES), jnp.int32)],
        scratch_shapes=[pltpu.VMEM((N_EXPERTS, LANES), F32)],
        compiler_params=_params("arbitrary"),
        name="proj_residual_router",
    )(*a_list, *w_list, bias, x, gate, g, scale, shift, rw_both, rw_hi, rb_pad)


def _rope(xn, cos_t, sin_t):
    return xn * cos_t + pltpu.roll(xn, LANES // 2, axis=1) * sin_t


def _mla_prep_kernel(z_ref, halo_ref, pw_ref, ps_ref, cqg_ref, wuq_ref, ckvg_ref, wuk_ref, wuv_ref,
                     qg_ref, kg_ref, cos_ref, sin_ref,
                     yp_ref, q_ref, k_ref, v_ref, ext_ref, *, tm, per):
    si = pl.program_id(0) % per
    u = z_ref[:, 0:POOL_WIDTH]
    ext_ref[0:POOL_HALO, :] = jnp.where(si == 0, 0.0, halo_ref[...])
    ext_ref[POOL_HALO:, :] = u
    t = si * tm + lax.broadcasted_iota(jnp.int32, (tm, 1), 0)
    for g, w in enumerate(POOL_WINDOWS):
        cols = slice(g * POOL_GROUP_DIM, (g + 1) * POOL_GROUP_DIM)
        ug = u[:, cols]
        s = ug
        for j in range(1, w):
            s = s + ext_ref[POOL_HALO - j:POOL_HALO - j + tm, cols]
        cnt = jnp.minimum(t + 1, w).astype(F32)
        pooled = s / cnt - ug
        yp = jnp.dot(pooled.astype(BF16), pw_ref[g], preferred_element_type=F32) * ps_ref[:, cols]
        yp_ref[:, cols] = yp.astype(yp_ref.dtype)

    cos_t, sin_t = cos_ref[...], sin_ref[...]
    inv_head = 1.0 / QK_HEAD_DIM

    cq = z_ref[:, Z_CQ:Z_CKV]
    cqn = cq * lax.rsqrt(jnp.mean(cq * cq, axis=-1, keepdims=True) + EPS) * cqg_ref[...]
    qf = jnp.dot(cqn.astype(BF16), wuq_ref[...], preferred_element_type=F32)
    q_scale = QK_HEAD_DIM ** -0.5 * LOG2_E
    for h in range(MLA_HEADS):
        qh = qf[:, h * LANES:(h + 1) * LANES]
        ss = jnp.sum(qh * qh, axis=-1, keepdims=True) * inv_head
        qn = qh * lax.rsqrt(ss + EPS) * qg_ref[...]
        q_ref[0, h] = (_rope(qn, cos_t, sin_t) * q_scale).astype(q_ref.dtype)

    ckv = z_ref[:, Z_CKV:Z_ROPE]
    ckvn = (ckv * lax.rsqrt(jnp.mean(ckv * ckv, axis=-1, keepdims=True) + EPS) * ckvg_ref[...]).astype(BF16)
    kf = jnp.dot(ckvn, wuk_ref[...], preferred_element_type=F32)
    vf = jnp.dot(ckvn, wuv_ref[...], preferred_element_type=F32)
    k_rope = z_ref[:, Z_ROPE:Z_WIDTH]
    ones_lane = lax.broadcasted_iota(jnp.int32, (tm, LANES), 1) == V_HEAD_DIM
    for h in range(MLA_HEADS):
        kh = kf[:, h * LANES:(h + 1) * LANES] + k_rope
        ss = jnp.sum(kh * kh, axis=-1, keepdims=True) * inv_head
        kn = kh * lax.rsqrt(ss + EPS) * kg_ref[...]
        k_ref[0, h] = _rope(kn, cos_t, sin_t).astype(k_ref.dtype)
        v_ref[0, h] = jnp.where(ones_lane, 1.0, vf[:, h * LANES:(h + 1) * LANES]).astype(v_ref.dtype)


def mla_prep(z, pool_w, pool_scale, cq_g, wuq_pad, ckv_g, wuk_pad, wuv_pad, qg_pad, kg_pad,
             cos_t, sin_t, batch, seq, tm=512):
    n = z.shape[0]
    per = seq // tm
    hb = tm // POOL_HALO
    full = lambda a: pl.BlockSpec(a.shape, lambda i: (0,) * a.ndim)
    tab = pl.BlockSpec((tm, LANES), lambda i: (i, 0))
    head_out = pl.BlockSpec((1, MLA_HEADS, tm, LANES), lambda i: (i // per, 0, i % per, 0))
    head_shape = jax.ShapeDtypeStruct((batch, MLA_HEADS, seq, LANES), BF16)
    return pl.pallas_call(
        functools.partial(_mla_prep_kernel, tm=tm, per=per),
        grid=(n // tm,),
        in_specs=[pl.BlockSpec((tm, Z_WIDTH), lambda i: (i, 0)),
                  pl.BlockSpec((POOL_HALO, POOL_WIDTH), lambda i: (jnp.maximum(i * hb - 1, 0), 0)),
                  full(pool_w), full(pool_scale), full(cq_g), full(wuq_pad), full(ckv_g), full(wuk_pad),
                  full(wuv_pad), full(qg_pad), full(kg_pad), tab, tab],
        out_specs=[pl.BlockSpec((tm, POOL_WIDTH), lambda i: (i, 0)), head_out, head_out, head_out],
        out_shape=[jax.ShapeDtypeStruct((n, POOL_WIDTH), BF16), head_shape, head_shape, head_shape],
        scratch_shapes=[pltpu.VMEM((tm + POOL_HALO, POOL_WIDTH), F32)],
        compiler_params=_params("parallel"),
        name="mla_prep",
    )(z, z, pool_w, pool_scale, cq_g, wuq_pad, ckv_g, wuk_pad, wuv_pad, qg_pad, kg_pad, cos_t, sin_t)


def _flash_kernel(q_ref, k_ref, v_ref, o_ref, m_ref, acc_ref, *, tq, tk, td, heads):
    qi = pl.program_id(2)
    q_chunk = lax.broadcasted_iota(jnp.int32, (tq, td), 0) // CHUNK
    k_chunk = lax.broadcasted_iota(jnp.int32, (tq, td), 1) // CHUNK
    m_ref[...] = jnp.full(m_ref.shape, MASK_VALUE, F32)
    acc_ref[...] = jnp.zeros(acc_ref.shape, F32)

    def step(start, width, mask, row0=0):
        for hh in range(heads):
            k = k_ref[0, hh, pl.ds(start, width), :]
            v = v_ref[0, hh, pl.ds(start, width), :]
            s = lax.dot_general(q_ref[0, hh, row0:, :], k, (((1,), (1,)), ((), ())), preferred_element_type=F32)
            if mask is not None:
                s = jnp.where(mask[row0:], s, MASK_VALUE)
            cols_s = [s[:, c * LANES:(c + 1) * LANES] for c in range(width // LANES)]
            s_max = cols_s[0]
            for sc in cols_s[1:]:
                s_max = jnp.maximum(s_max, sc)
            m_prev = m_ref[hh, row0:, :]
            m_new = jnp.maximum(m_prev, jnp.max(s_max, axis=-1, keepdims=True))
            alpha = jnp.exp2(m_prev - m_new)
            p = jnp.concatenate([jnp.exp2(sc - m_new) for sc in cols_s], axis=1).astype(v.dtype)
            acc_ref[hh, row0:, :] = alpha * acc_ref[hh, row0:, :] + jnp.dot(p, v, preferred_element_type=F32)
            m_ref[hh, row0:, :] = m_new

    def full_step(j, carry):
        step(pl.multiple_of(j * tk, tk), tk, None)
        return carry

    lax.fori_loop(0, (tq // tk) * qi, full_step, 0)
    for dd in range(tq // td):
        step(pl.multiple_of(qi * tq + dd * td, td), td, k_chunk + dd * (td // CHUNK) <= q_chunk, row0=dd * td)
    outs = []
    for hh in range(heads):
        acc = acc_ref[hh]
        outs.append((acc / acc[:, V_HEAD_DIM:V_HEAD_DIM + 1])[:, :V_HEAD_DIM])
    o_ref[0] = jnp.concatenate(outs, axis=-1).astype(o_ref.dtype)


def flash_attention(q, k, v, tq=1024, tk=1024, td=512):
    b, h, s, _ = q.shape
    heads = LANES // V_HEAD_DIM
    return pl.pallas_call(
        functools.partial(_flash_kernel, tq=tq, tk=tk, td=td, heads=heads),
        grid=(b, h // heads, s // tq),
        in_specs=[pl.BlockSpec((1, heads, tq, LANES), lambda bi, hi, qi: (bi, hi, qi, 0)),
                  pl.BlockSpec((1, heads, s, LANES), lambda bi, hi, qi: (bi, hi, 0, 0)),
                  pl.BlockSpec((1, heads, s, LANES), lambda bi, hi, qi: (bi, hi, 0, 0))],
        out_specs=pl.BlockSpec((1, tq, LANES), lambda bi, hi, qi: (bi, qi, hi)),
        out_shape=jax.ShapeDtypeStruct((b, s, h * V_HEAD_DIM), BF16),
        scratch_shapes=[pltpu.VMEM((heads, tq, LANES), F32) for _ in range(2)],
        compiler_params=_params("parallel", "parallel", "parallel"),
        name="flash_attention",
    )(q, k, v)


def _dwconv_kernel(u_ref, halo_ref, w_ref, b_ref, g_ref, beta_ref, o_ref, ext_ref, sh_ref, conv_ref, *, tm, per):
    si = pl.program_id(0) % per
    d = u_ref.shape[1]
    ext_ref[0:CONV_HALO, :] = jnp.where(si == 0, 0.0, halo_ref[...])
    ext_ref[CONV_HALO:, :] = u_ref[...]
    span = sh_ref.shape[1]
    for b in range(1, SUBLANES):
        sh_ref[b - 1] = ext_ref[b:b + span, :]
    first = CONV_HALO - (CONV_WIDTH - 1)
    for r0 in range(0, tm, CONV_PATCH_ROWS):
        for c0 in range(0, d, CONV_PATCH_COLS):
            cols = slice(c0, c0 + CONV_PATCH_COLS)
            view = (CONV_PATCH_ROWS // SUBLANES, SUBLANES, CONV_PATCH_COLS)
            patch = jnp.broadcast_to(b_ref[:, cols], view)
            for j in range(CONV_WIDTH):
                a, b = divmod(first + j, SUBLANES)
                rows = slice(SUBLANES * a + r0, SUBLANES * a + r0 + CONV_PATCH_ROWS)
                win = ext_ref[rows, cols] if b == 0 else sh_ref[b - 1, rows, cols]
                tap = jnp.broadcast_to(w_ref[j:j + 1, cols], view[1:])
                patch = patch + tap[None] * win.reshape(view)
            conv_ref[r0:r0 + CONV_PATCH_ROWS, cols] = patch.reshape(CONV_PATCH_ROWS, CONV_PATCH_COLS)
    acc = conv_ref[...]
    mu = jnp.mean(acc, axis=-1, keepdims=True)
    cen = acc - mu
    var = jnp.mean(cen * cen, axis=-1, keepdims=True)
    y = cen * lax.rsqrt(var + EPS) * g_ref[...] + beta_ref[...]
    o_ref[...] = (y * jax.nn.sigmoid(y)).astype(o_ref.dtype)


def dwconv_ln_silu(u, w_pad, b, g, beta, seq, tm=256):
    n, d = u.shape
    per = seq // tm
    hb = tm // CONV_HALO
    full = lambda a: pl.BlockSpec(a.shape, lambda i: (0, 0))
    return pl.pallas_call(
        functools.partial(_dwconv_kernel, tm=tm, per=per),
        grid=(n // tm,),
        in_specs=[pl.BlockSpec((tm, d), lambda i: (i, 0)),
                  pl.BlockSpec((CONV_HALO, d), lambda i: (jnp.maximum(i * hb - 1, 0), 0)),
                  full(w_pad), full(b), full(g), full(beta)],
        out_specs=pl.BlockSpec((tm, d), lambda i: (i, 0)),
        out_shape=jax.ShapeDtypeStruct((n, d), BF16),
        scratch_shapes=[pltpu.VMEM((tm + CONV_HALO, d), F32),
                        pltpu.VMEM((SUBLANES - 1, tm + CONV_HALO - SUBLANES, d), F32),
                        pltpu.VMEM((tm, d), F32)],
        compiler_params=_params("parallel"),
        name="dwconv_ln_silu",
    )(u, u, w_pad, b, g, beta)


def _route_tile(logits, before_tile, tm):
    row_f = lax.broadcasted_iota(jnp.int32, (N_EXPERTS, tm), 0).astype(F32)
    neg = -jnp.inf
    picks, vals, ids = [], [], []
    for _ in range(TOP_K):
        mx = jnp.max(logits, axis=0, keepdims=True)
        idx = jnp.min(jnp.where(logits == mx, row_f, float(N_EXPERTS)), axis=0, keepdims=True)
        pick = row_f == idx
        picks.append(pick)
        vals.append(mx)
        ids.append(idx)
        logits = jnp.where(pick, neg, logits)
    exps = [jnp.exp(v - vals[0]) for v in vals]
    den = exps[0]
    for e in exps[1:]:
        den = den + e

    chosen = jnp.zeros((N_EXPERTS, tm), F32)
    for pick in picks:
        chosen = chosen + pick.astype(F32)
    r_io = lax.broadcasted_iota(jnp.int32, (tm, tm), 0)
    c_io = lax.broadcasted_iota(jnp.int32, (tm, tm), 1)
    later = (r_io < c_io).astype(BF16)
    before = jnp.dot(chosen.astype(BF16), later, preferred_element_type=F32) + before_tile
    ranks = [jnp.sum(jnp.where(pick, before, 0.0), axis=0, keepdims=True) for pick in picks]
    idx_out = jnp.concatenate(ids + ranks, axis=0)
    gate_out = jnp.concatenate([e / den for e in exps], axis=0)
    total = before_tile + jnp.sum(chosen, axis=1, keepdims=True)
    return idx_out, gate_out, total


def _sc_worker_base(per_worker):
    return (lax.axis_index("s") * SC_CORES + lax.axis_index("c")) * per_worker


def sc_scatter_rows(src, dest_flat, n_rows):
    n, d = src.shape
    per_w = n // SC_WORKERS
    n_chunks = per_w // SC_CHUNK
    mesh = plsc.VectorSubcoreMesh(core_axis_name="c", subcore_axis_name="s")

    @functools.partial(
        pl.kernel, out_type=jax.ShapeDtypeStruct((n_rows, d), src.dtype), mesh=mesh,
        scratch_types=[pltpu.VMEM((per_w,), jnp.int32) for _ in range(TOP_K)]
        + [pltpu.VMEM((SC_CHUNK, d), src.dtype) for _ in range(2)] + [pltpu.SemaphoreType.DMA] * 4,
        name="sc_scatter_rows")
    def scatter(src_hbm, dest_hbm, out_hbm, *scratch):
        idx_refs = scratch[:TOP_K]
        bufs = scratch[TOP_K:TOP_K + 2]
        in_sems, out_sems = scratch[TOP_K + 2:TOP_K + 4], scratch[TOP_K + 4:TOP_K + 6]
        base = _sc_worker_base(per_w)
        for k, idx_ref in enumerate(idx_refs):
            pltpu.sync_copy(dest_hbm.at[pl.ds(k * n + base, per_w)], idx_ref)

        def load(j, slot):
            return pltpu.make_async_copy(src_hbm.at[pl.ds(base + j * SC_CHUNK, SC_CHUNK)], bufs[slot], in_sems[slot])

        def store_all(j, slot):
            copies = [pltpu.make_async_copy(bufs[slot], out_hbm.at[idx_ref.at[pl.ds(j * SC_CHUNK, SC_CHUNK)]],
                                            out_sems[slot]) for idx_ref in idx_refs]
            for cp in copies:
                cp.start()
            for cp in copies:
                cp.wait()

        load(0, 0).start()

        @pl.loop(0, n_chunks // 2)
        def _(p):
            j = 2 * p
            load(j + 1, 1).start()
            load(j, 0).wait()
            store_all(j, 0)

            @pl.when(j + 2 < n_chunks)
            def _():
                load(j + 2, 0).start()

            load(j + 1, 1).wait()
            store_all(j + 1, 1)

    return scatter(src, dest_flat)


def sc_gather_rows(table, idx):
    b = idx.shape[0]
    d = table.shape[1]
    per_w = b // SC_WORKERS
    n_chunks = per_w // SC_CHUNK
    mesh = plsc.VectorSubcoreMesh(core_axis_name="c", subcore_axis_name="s")

    @functools.partial(
        pl.kernel, out_type=jax.ShapeDtypeStruct((b, d), table.dtype), mesh=mesh,
        scratch_types=[pltpu.VMEM((per_w,), jnp.int32)] + [pltpu.VMEM((SC_CHUNK, d), table.dtype) for _ in range(2)]
        + [pltpu.SemaphoreType.DMA] * 4,
        name="sc_gather_rows")
    def gather(table_hbm, idx_hbm, out_hbm, idx_ref, buf0, buf1, gsem0, gsem1, osem0, osem1):
        bufs, in_sems, out_sems = (buf0, buf1), (gsem0, gsem1), (osem0, osem1)
        base = _sc_worker_base(per_w)
        pltpu.sync_copy(idx_hbm.at[pl.ds(base, per_w)], idx_ref)

        def fetch(j, slot):
            return pltpu.make_async_copy(table_hbm.at[idx_ref.at[pl.ds(j * SC_CHUNK, SC_CHUNK)]], bufs[slot],
                                         in_sems[slot])

        def store(j, slot):
            return pltpu.make_async_copy(bufs[slot], out_hbm.at[pl.ds(base + j * SC_CHUNK, SC_CHUNK)],
                                         out_sems[slot])

        fetch(0, 0).start()

        @pl.loop(0, n_chunks // 2)
        def _(p):
            j = 2 * p

            @pl.when(p > 0)
            def _():
                store(j - 1, 1).wait()

            fetch(j + 1, 1).start()
            fetch(j, 0).wait()
            store(j, 0).start()
            fetch(j + 1, 1).wait()
            store(j, 0).wait()

            @pl.when(j + 2 < n_chunks)
            def _():
                fetch(j + 2, 0).start()

            store(j + 1, 1).start()

        store(n_chunks - 1, 1).wait()

    return gather(table, idx)


def _ffn_kernel(be_ref, nv_ref, nu_ref, xs_ref, w1_ref, b1g_ref, b1u_ref, w2_ref, b2_ref, y_ref,
                wg_ref, wu_ref, w2s_ref):
    i = pl.program_id(0)
    pair = 2 * LANES

    @pl.when(i < nu_ref[0])
    def _():
        e = be_ref[i]
        prev = be_ref[jnp.maximum(i - 1, 0)]

        @pl.when((i == 0) | (e != prev))
        def _():
            r_io = lax.broadcasted_iota(jnp.int32, (pair, pair), 0)
            c_io = lax.broadcasted_iota(jnp.int32, (pair, pair), 1)
            want = jnp.where(c_io < LANES, 2 * c_io, 2 * (c_io - LANES) + 1)
            sel = (r_io == want).astype(BF16)
            for c in range(wg_ref.shape[1] // LANES):
                slab = w1_ref[0, 0, :, c * pair:(c + 1) * pair].astype(BF16)
                split = jnp.dot(slab, sel, preferred_element_type=F32)
                wg_ref[:, c * LANES:(c + 1) * LANES] = split[:, :LANES].astype(BF16)
                wu_ref[:, c * LANES:(c + 1) * LANES] = split[:, LANES:].astype(BF16)
            w2s_ref[...] = w2_ref[0, 0].astype(BF16)

        nvalid = nv_ref[i]

        def ffn_rows(r0, rows):
            row = r0 + lax.broadcasted_iota(jnp.int32, (rows, xs_ref.shape[1]), 0)
            lo, hi = _unpack_halves(jnp.where(row < nvalid, xs_ref[r0:r0 + rows, :], 0))
            x = jnp.concatenate([lo, hi], axis=1).astype(BF16)
            hg = jnp.dot(x, wg_ref[...], preferred_element_type=F32) + b1g_ref[0]
            hu = jnp.dot(x, wu_ref[...], preferred_element_type=F32) + b1u_ref[0]
            gate = jnp.minimum(hg, SWIGLU_LIMIT)
            up = jnp.clip(hu, -SWIGLU_LIMIT, SWIGLU_LIMIT)
            act = gate * jax.nn.sigmoid(SWIGLU_ALPHA * gate) * (up + 1.0)
            y = jnp.dot(act.astype(BF16), w2s_ref[...], preferred_element_type=F32) + b2_ref[0]
            y_ref[r0:r0 + rows, :] = _pack_halves(y)

        def zero_rows(r0, rows):
            y_ref[r0:r0 + rows, :] = jnp.zeros((rows, y_ref.shape[1]), y_ref.dtype)

        for r0 in range(0, y_ref.shape[0], MOE_CHUNK):
            left = nvalid - r0

            @pl.when(left > MOE_BLOCK)
            def _(r0=r0):
                ffn_rows(r0, MOE_CHUNK)

            @pl.when((left > 0) & (left <= MOE_BLOCK))
            def _(r0=r0):
                ffn_rows(r0, MOE_BLOCK)
                zero_rows(r0 + MOE_BLOCK, MOE_CHUNK - MOE_BLOCK)

            @pl.when(left <= 0)
            def _(r0=r0):
                zero_rows(r0, MOE_CHUNK)

    @pl.when(i >= nu_ref[0])
    def _():
        y_ref[...] = jnp.zeros(y_ref.shape, y_ref.dtype)


def moe_ffn(xs, block_e, block_valid, n_used, layer, w1, b1g, b1u, w2, b2):
    n_rows, half = xs.shape
    d = 2 * half
    n_blocks = n_rows // MOE_STEP
    n_exp, f2 = w1.shape[1], w1.shape[3]
    f = f2 // 2
    rows = lambda i, be, nv, nu: (jnp.minimum(i, nu[0] - 1), 0)
    vec = lambda width: pl.BlockSpec((1, 1, width), lambda i, be, nv, nu: (be[i], 0, 0))
    grid_spec = pltpu.PrefetchScalarGridSpec(
        num_scalar_prefetch=3,
        grid=(n_blocks,),
        in_specs=[
            pl.BlockSpec((MOE_STEP, half), rows),
            pl.BlockSpec((1, 1, d, f2), lambda i, be, nv, nu: (layer, be[i], 0, 0)),
            vec(f), vec(f),
            pl.BlockSpec((1, 1, f, d), lambda i, be, nv, nu: (layer, be[i], 0, 0)),
            vec(d),
        ],
        out_specs=pl.BlockSpec((MOE_STEP, half), lambda i, be, nv, nu: (i, 0)),
        scratch_shapes=[pltpu.VMEM((d, f), BF16), pltpu.VMEM((d, f), BF16), pltpu.VMEM((f, d), BF16)],
    )
    return pl.pallas_call(
        _ffn_kernel,
        grid_spec=grid_spec,
        out_shape=jax.ShapeDtypeStruct((n_rows, half), jnp.int32),
        compiler_params=_params("arbitrary"),
        name="moe_ffn",
    )(block_e, block_valid, n_used, xs, w1, b1g.reshape(n_exp, 1, f), b1u.reshape(n_exp, 1, f), w2,
      b2.reshape(n_exp, 1, d))


def _combine_kernel(x_ref, y_ref, gates_ref, gate_ref, *rest, with_norm):
    g = gates_ref[...]
    half = y_ref.shape[2]
    acc_lo = acc_hi = None
    for k in range(TOP_K):
        lo, hi = _unpack_halves(y_ref[k])
        gk = g[:, k:k + 1]
        acc_lo = gk * lo if acc_lo is None else acc_lo + gk * lo
        acc_hi = gk * hi if acc_hi is None else acc_hi + gk * hi
    x_lo = x_ref[:, :half] + gate_ref[0, :, :half] * acc_lo
    x_hi = x_ref[:, half:] + gate_ref[0, :, half:] * acc_hi
    if with_norm:
        ng_ref, sc_ref, sh_ref = rest[:3]
        o_ref, h_ref = rest[-2:]
        x_new = jnp.concatenate([x_lo, x_hi], axis=1)
        o_ref[...] = x_new
        h_ref[...] = _modulated_norm(x_new, ng_ref[...], sc_ref[0], sh_ref[0]).astype(h_ref.dtype)
    else:
        o_ref = rest[-1]
        o_ref[:, :half] = x_lo
        o_ref[:, half:] = x_hi


def moe_combine(x, y_part, gates, gate_mod, seq, next_norm=None, part=0, n_parts=1, prev=(), tm=512):
    n, d = x.shape
    per = seq // tm
    steps = n // (tm * n_parts)
    first = part * steps
    rows = lambda width: pl.BlockSpec((tm, width), lambda i: (i + first, 0))
    vec = pl.BlockSpec((1, 1, d), lambda i: ((i + first) // per, 0, 0))
    in_specs = [rows(d), pl.BlockSpec((TOP_K, tm, d // 2), lambda i: (0, i, 0)), rows(LANES), vec]
    args = [x, y_part, gates, gate_mod]
    out_specs, out_shape = [rows(d)], [jax.ShapeDtypeStruct((n, d), F32)]
    if next_norm is not None:
        in_specs += [pl.BlockSpec((1, d), lambda i: (0, 0)), vec, vec]
        args += list(next_norm)
        out_specs, out_shape = out_specs + [rows(d)], out_shape + [jax.ShapeDtypeStruct((n, d), BF16)]
    aliases = {len(args) + j: j for j in range(len(prev))}
    in_specs += [pl.BlockSpec(memory_space=pl.ANY) for _ in prev]
    args += list(prev)
    return pl.pallas_call(
        functools.partial(_combine_kernel, with_norm=next_norm is not None),
        grid=(steps,),
        in_specs=in_specs,
        out_specs=out_specs,
        out_shape=out_shape,
        input_output_aliases=aliases,
        compiler_params=_params("parallel"),
        name="moe_combine",
    )(*args)


def moe_layer(x, mixer, mix_gate, mods, norm_g, router_w, router_b, layer, w1, b1, w2, b2, seq, next_norm):
    n, d = x.shape
    shift, scale, gate = mods
    rw_rows = jnp.pad(router_w.T, ((0, LANES - N_EXPERTS), (0, 0)))
    rw_hi = rw_rows.astype(BF16)
    rw_lo = (rw_rows - rw_hi.astype(F32)).astype(BF16)
    x, h, idx, gates, counts = proj_residual_router(
        *mixer, x, mix_gate, norm_g.reshape(1, d), scale, shift, jnp.concatenate([rw_hi, rw_lo], axis=0), rw_hi,
        router_b.reshape(N_EXPERTS, 1), seq)

    top_i = idx[:TOP_K]
    rank = idx[TOP_K:]
    counts = counts[:, 0]
    experts = jnp.arange(N_EXPERTS, dtype=jnp.int32)
    padded = (counts + MOE_STEP - 1) // MOE_STEP * MOE_STEP
    pad_ends = jnp.sum(jnp.where(experts[:, None] >= experts[None, :], padded[None, :], 0), axis=1)
    pad_starts = pad_ends - padded
    dest = jnp.sum(jnp.where(top_i[..., None] == experts, pad_starts, 0), axis=-1) + rank
    n_blocks = -(-n * TOP_K // MOE_STEP) + N_EXPERTS
    block_start = jnp.arange(n_blocks, dtype=jnp.int32) * MOE_STEP
    block_e = jnp.minimum(jnp.sum((pad_ends[None, :] <= block_start[:, None]).astype(jnp.int32), axis=1),
                          N_EXPERTS - 1)
    n_used = (pad_ends[N_EXPERTS - 1:] // MOE_STEP).astype(jnp.int32)
    seg_end = jnp.sum(jnp.where(block_e[:, None] == experts, pad_starts + counts, 0), axis=1)
    block_valid = jnp.clip(seg_end - block_start, 0, MOE_STEP).astype(jnp.int32)
    dest_flat = dest.reshape(-1)

    xs = sc_scatter_rows(h, dest_flat, n_blocks * MOE_STEP)
    ys = moe_ffn(xs, block_e, block_valid, n_used, layer, w1, b1[layer][:, 0::2], b1[layer][:, 1::2], w2, b2[layer])
    outs = ()
    per_part = n // COMBINE_PARTS
    for part in range(COMBINE_PARTS):
        idx_part = dest[:, part * per_part:(part + 1) * per_part].reshape(-1)
        y = sc_gather_rows(ys, idx_part).reshape(TOP_K, per_part, d // 2)
        outs = moe_combine(x, y, gates, gate, seq, next_norm, part, COMBINE_PARTS, tuple(outs))
    return (outs[0], outs[1]) if next_norm is not None else (outs[0], None)


def _pad_heads(w, width):
    k = w.shape[0]
    w = w.reshape(k, MLA_HEADS, width)
    return jnp.pad(w, ((0, 0), (0, 0), (0, LANES - width))).reshape(k, MLA_HEADS * LANES)


def _head_lane_source():
    half = QK_ROPE_DIM // 2
    first_nope = LANES // 2 - half
    lanes = (list(range(QK_NOPE_DIM, QK_NOPE_DIM + half)) + list(range(first_nope))
             + list(range(QK_NOPE_DIM + half, QK_HEAD_DIM)) + list(range(first_nope, QK_NOPE_DIM)))
    return jnp.array(lanes + [QK_HEAD_DIM] * (LANES - QK_HEAD_DIM), jnp.int32)


def _to_head_lanes(w):
    w = jnp.concatenate([w, jnp.zeros(w.shape[:-1] + (1,), w.dtype)], axis=-1)
    return jnp.take(w, _head_lane_source(), axis=-1)


def _rope_tables(positions):
    inv = 1.0 / (ROPE_THETA ** (jnp.arange(0, QK_ROPE_DIM, 2, dtype=F32) / QK_ROPE_DIM))
    ang = inv[:, None] * positions.reshape(-1).astype(F32)[None, :]
    cos, sin = jnp.cos(ang).T, jnp.sin(ang).T
    n = cos.shape[0]
    half = QK_ROPE_DIM // 2
    first_nope = LANES // 2 - half
    ones = lambda w: jnp.ones((n, w), F32)
    zeros = lambda w: jnp.zeros((n, w), F32)
    pad = LANES - QK_HEAD_DIM
    cos_t = jnp.concatenate([cos, ones(first_nope), cos, ones(QK_NOPE_DIM - first_nope), zeros(pad)], axis=1)
    sin_t = jnp.concatenate([-sin, zeros(first_nope), sin, zeros(QK_NOPE_DIM - first_nope + pad)], axis=1)
    return cos_t, sin_t


def _split_mods(m, batch):
    d = m.shape[-1] // 3
    m = m[:batch]
    return tuple(m[:, None, j * d:(j + 1) * d] for j in range(3))


def pool_mla_mixer(x, h, mods, norm_g, tables, w_in, pool_w, pool_scale, cq_norm_g, w_uq, ckv_norm_g, w_ukv,
                   q_norm_g, k_norm_g, w_out, batch, seq):
    n, d = x.shape
    shift, scale, _ = mods
    rope_cols = _to_head_lanes(jnp.pad(w_in[:, Z_ROPE:], ((0, 0), (QK_NOPE_DIM, 0))))
    w_in_pad = jnp.concatenate([w_in[:, :Z_ROPE], rope_cols], axis=1).astype(BF16)
    if h is None:
        z = norm_matmul(x, norm_g.reshape(1, d), scale, shift, w_in_pad, seq)
    else:
        z = matmul(h, w_in_pad)
    w_ukv_h = w_ukv.reshape(KV_LORA_RANK, MLA_HEADS, QK_NOPE_DIM + V_HEAD_DIM)
    k_nope = jnp.pad(w_ukv_h[:, :, :QK_NOPE_DIM], ((0, 0), (0, 0), (0, QK_ROPE_DIM)))
    wuk_pad = _to_head_lanes(k_nope).reshape(KV_LORA_RANK, MLA_HEADS * LANES).astype(BF16)
    wuv_pad = _pad_heads(w_ukv_h[:, :, QK_NOPE_DIM:].reshape(KV_LORA_RANK, -1), V_HEAD_DIM).astype(BF16)
    wuq_pad = _to_head_lanes(w_uq.reshape(Q_LORA_RANK, MLA_HEADS, QK_HEAD_DIM))
    wuq_pad = wuq_pad.reshape(Q_LORA_RANK, MLA_HEADS * LANES).astype(BF16)
    pad_g = lambda g: _to_head_lanes(g).reshape(1, LANES)
    y_pool, q, k, v = mla_prep(z, pool_w.astype(BF16), pool_scale.reshape(1, -1), cq_norm_g.reshape(1, -1), wuq_pad,
                               ckv_norm_g.reshape(1, -1), wuk_pad, wuv_pad, pad_g(q_norm_g), pad_g(k_norm_g),
                               *tables, batch, seq)
    y_att = flash_attention(q, k, v).reshape(n, MLA_HEADS * V_HEAD_DIM)
    w_out_b = w_out.astype(BF16)
    return [y_pool, y_att], [w_out_b[:POOL_WIDTH], w_out_b[POOL_WIDTH:]], jnp.zeros((1, d), F32)


def conformer_mixer(h, pw1_w, pw1_b, dw_w, dw_b, ln_g, ln_b, pw2_w, pw2_b, seq):
    d = pw2_w.shape[1]
    cd = pw1_w.shape[1] // 2
    pw1 = pw1_w.astype(BF16)
    u = glu_matmul(h, pw1[:, :cd], pw1[:, cd:], pw1_b[:cd].reshape(1, cd), pw1_b[cd:].reshape(1, cd))
    w_pad = jnp.pad(dw_w, ((0, CONV_HALO - CONV_WIDTH), (0, 0)))
    u = dwconv_ln_silu(u, w_pad, dw_b.reshape(1, cd), ln_g.reshape(1, cd), ln_b.reshape(1, cd), seq)
    return [u], [pw2_w.astype(BF16)], pw2_b.reshape(1, d)


def kernel(x, c, positions, ada_mix_w, ada_mix_b, norm_mix_g, w_in, pool_w, pool_scale, cq_norm_g, w_uq,
           ckv_norm_g, w_ukv, q_norm_g, k_norm_g, w_out, conv_pw1_w, conv_pw1_b, conv_dw_w, conv_dw_b,
           conv_ln_g, conv_ln_b, conv_pw2_w, conv_pw2_b, ada_ffn_w, ada_ffn_b, norm_ffn_g, router_w,
           router_b, moe_w1, moe_b1, moe_w2, moe_b2):
    batch, seq, d = x.shape
    depth = ada_mix_w.shape[0]
    c_pad = jnp.pad(c, ((0, 8 - batch), (0, 0)))
    mix_mods = adaln(c_pad, ada_mix_w, ada_mix_b)
    ffn_mods = adaln(c_pad, ada_ffn_w, ada_ffn_b)
    tables = _rope_tables(positions)
    xf = x.reshape(batch * seq, d)
    h = None
    for layer in range(depth):
        i = layer // 2
        mods = _split_mods(mix_mods[layer], batch)
        if layer % 2 == 0:
            mixer = pool_mla_mixer(xf, h, mods, norm_mix_g[layer], tables, w_in[i], pool_w[i], pool_scale[i],
                                   cq_norm_g[i], w_uq[i], ckv_norm_g[i], w_ukv[i], q_norm_g[i], k_norm_g[i],
                                   w_out[i], batch, seq)
        else:
            mixer = conformer_mixer(h, conv_pw1_w[i], conv_pw1_b[i], conv_dw_w[i], conv_dw_b[i],
                                    conv_ln_g[i], conv_ln_b[i], conv_pw2_w[i], conv_pw2_b[i], seq)
        next_norm = None
        if layer + 1 < depth:
            n_shift, n_scale, _ = _split_mods(mix_mods[layer + 1], batch)
            next_norm = (norm_mix_g[layer + 1].reshape(1, d), n_scale, n_shift)
        xf, h = moe_layer(xf, mixer, mods[2], _split_mods(ffn_mods[layer], batch), norm_ffn_g[layer],
                          router_w[layer], router_b[layer], layer, moe_w1, moe_b1, moe_w2, moe_b2, seq, next_norm)
    return xf.reshape(batch, seq, d)
```

```python
import functools

import jax
import jax.numpy as jnp
from jax import lax
from jax.experimental import pallas as pl
from jax.experimental.pallas import tpu as pltpu
from jax.experimental.pallas import tpu_sc as plsc

F32 = jnp.float32
BF16 = jnp.bfloat16
HIGHEST = lax.Precision.HIGHEST

EPS = 1e-6
POOL_WINDOWS = (2, 4, 8, 16)
POOL_GROUP_DIM = 128
POOL_WIDTH = POOL_GROUP_DIM * len(POOL_WINDOWS)
MLA_HEADS = 8
QK_NOPE_DIM = 64
QK_ROPE_DIM = 32
QK_HEAD_DIM = QK_NOPE_DIM + QK_ROPE_DIM
V_HEAD_DIM = 64
Q_LORA_RANK = 384
KV_LORA_RANK = 256
ROPE_THETA = 10000.0
CHUNK = 64
CONV_WIDTH = 31
N_EXPERTS = 32
TOP_K = 4
SWIGLU_ALPHA = 1.702
SWIGLU_LIMIT = 7.0
MOE_BLOCK = 256
MOE_CHUNK = 512
MOE_STEP = 1024

LANES = 128
SUBLANES = 8
CONV_PATCH_ROWS = 64
CONV_PATCH_COLS = 256
POOL_HALO = 16
CONV_HALO = 32
MASK_VALUE = -1e30
LOG2_E = 1.4426950408889634
VMEM_LIMIT = 52 * 1024 * 1024
SC_CORES = 2
SC_SUBCORES = 16
SC_WORKERS = SC_CORES * SC_SUBCORES
SC_CHUNK = 64
COMBINE_PARTS = 2

Z_CQ = POOL_WIDTH
Z_CKV = Z_CQ + Q_LORA_RANK
Z_ROPE = Z_CKV + KV_LORA_RANK
Z_WIDTH = Z_ROPE + LANES


def _params(*sem, vmem=None):
    return pltpu.CompilerParams(dimension_semantics=sem, vmem_limit_bytes=vmem or VMEM_LIMIT,
                                disable_bounds_checks=True)


def _adaln_kernel(c_ref, w_ref, b_ref, o_ref):
    c = c_ref[...]
    s = c * jax.nn.sigmoid(c)
    o_ref[0] = jnp.dot(s, w_ref[0], preferred_element_type=F32, precision=HIGHEST) + b_ref[0]


def adaln(c_pad, w, b):
    n_l, d, d3 = w.shape
    tn = 512
    return pl.pallas_call(
        _adaln_kernel,
        grid=(n_l, d3 // tn),
        in_specs=[pl.BlockSpec((8, d), lambda l, j: (0, 0)),
                  pl.BlockSpec((1, d, tn), lambda l, j: (l, 0, j)),
                  pl.BlockSpec((1, 1, tn), lambda l, j: (l, 0, j))],
        out_specs=pl.BlockSpec((1, 8, tn), lambda l, j: (l, 0, j)),
        out_shape=jax.ShapeDtypeStruct((n_l, 8, d3), F32),
        compiler_params=_params("parallel", "parallel"),
        name="adaln",
    )(c_pad, w, b.reshape(n_l, 1, d3))


def _modulated_norm(x, g, scale, shift):
    ms = jnp.mean(x * x, axis=-1, keepdims=True)
    return x * lax.rsqrt(ms + EPS) * g * (1.0 + scale) + shift


def _pack_halves(x):
    w = x.shape[1] // 2
    lo = lax.bitcast_convert_type(x[:, :w].astype(BF16).astype(F32), jnp.uint32)
    hi = lax.bitcast_convert_type(x[:, w:].astype(BF16).astype(F32), jnp.uint32)
    return lax.bitcast_convert_type((lo >> 16) | (hi & jnp.uint32(0xFFFF0000)), jnp.int32)


def _unpack_halves(p):
    u = lax.bitcast_convert_type(p, jnp.uint32)
    lo = lax.bitcast_convert_type(u << 16, F32)
    hi = lax.bitcast_convert_type(u & jnp.uint32(0xFFFF0000), F32)
    return lo, hi


def _norm_matmul_kernel(x_ref, g_ref, sc_ref, sh_ref, w_ref, o_ref):
    h = _modulated_norm(x_ref[...], g_ref[...], sc_ref[0], sh_ref[0])
    o_ref[...] = jnp.dot(h.astype(BF16), w_ref[...], preferred_element_type=F32)


def norm_matmul(x, g, scale, shift, w, seq, tm=512):
    n, d = x.shape
    m = w.shape[1]
    per = seq // tm
    vec = pl.BlockSpec((1, 1, d), lambda i: (i // per, 0, 0))
    return pl.pallas_call(
        _norm_matmul_kernel,
        grid=(n // tm,),
        in_specs=[pl.BlockSpec((tm, d), lambda i: (i, 0)), pl.BlockSpec((1, d), lambda i: (0, 0)), vec, vec,
                  pl.BlockSpec((d, m), lambda i: (0, 0))],
        out_specs=pl.BlockSpec((tm, m), lambda i: (i, 0)),
        out_shape=jax.ShapeDtypeStruct((n, m), F32),
        compiler_params=_params("parallel"),
        name="norm_matmul",
    )(x, g, scale, shift, w)


def _matmul_kernel(a_ref, w_ref, o_ref):
    o_ref[...] = jnp.dot(a_ref[...], w_ref[...], preferred_element_type=F32).astype(o_ref.dtype)


def matmul(a, w, out_dtype=F32, tm=512):
    n, k = a.shape
    m = w.shape[1]
    return pl.pallas_call(
        _matmul_kernel,
        grid=(n // tm,),
        in_specs=[pl.BlockSpec((tm, k), lambda i: (i, 0)), pl.BlockSpec((k, m), lambda i: (0, 0))],
        out_specs=pl.BlockSpec((tm, m), lambda i: (i, 0)),
        out_shape=jax.ShapeDtypeStruct((n, m), out_dtype),
        compiler_params=_params("parallel"),
        name="matmul",
    )(a, w)


def _proj_residual_router_kernel(*refs, n_in, tm):
    a_refs, w_refs = refs[:n_in], refs[n_in:2 * n_in]
    (b_ref, x_ref, gate_ref, g_ref, sc_ref, sh_ref, rw2_ref, rwh_ref, rb_ref,
     o_ref, h_ref, idx_ref, gates_ref, cnt_ref, carry_ref) = refs[2 * n_in:]

    @pl.when(pl.program_id(0) == 0)
    def _():
        carry_ref[...] = jnp.zeros(carry_ref.shape, F32)

    acc = b_ref[...]
    for a_ref, w_ref in zip(a_refs, w_refs):
        acc = acc + jnp.dot(a_ref[...], w_ref[...], preferred_element_type=F32)
    x_new = x_ref[...] + gate_ref[0] * acc
    o_ref[...] = x_new
    h = _modulated_norm(x_new, g_ref[...], sc_ref[0], sh_ref[0])
    h_ref[...] = _pack_halves(h)
    h_hi = h.astype(BF16)
    h_lo = (h - h_hi.astype(F32)).astype(BF16)
    contract_features = (((1,), (1,)), ((), ()))
    both = lax.dot_general(rw2_ref[...], h_hi, contract_features, preferred_element_type=F32)
    low = lax.dot_general(rwh_ref[...], h_lo, contract_features, preferred_element_type=F32)
    logits = (both[:LANES] + both[LANES:] + low)[:N_EXPERTS] + rb_ref[...]
    idx, gates, total = _route_tile(logits, carry_ref[:, 0:1], tm)
    idx_ref[...] = idx.astype(jnp.int32)
    gate_rows = jnp.concatenate([gates, jnp.zeros((LANES - TOP_K, tm), F32)], axis=0)
    gates_ref[...] = jnp.transpose(gate_rows)
    carry_ref[...] = jnp.broadcast_to(total, carry_ref.shape)
    cnt_ref[...] = jnp.broadcast_to(total, cnt_ref.shape).astype(jnp.int32)


def proj_residual_router(a_list, w_list, bias, x, gate, g, scale, shift, rw_both, rw_hi, rb_pad, seq, tm=512):
    n, d = x.shape
    per = seq // tm
    n_in = len(a_list)
    rows = lambda width: pl.BlockSpec((tm, width), lambda i: (i, 0))
    const = lambda a: pl.BlockSpec(a.shape, lambda i: (0, 0))
    vec = pl.BlockSpec((1, 1, d), lambda i: (i // per, 0, 0))
    in_specs = [rows(a.shape[1]) for a in a_list] + [const(w) for w in w_list]
    in_specs += [const(bias), rows(d), vec, const(g), vec, vec, const(rw_both), const(rw_hi), const(rb_pad)]
    return pl.pallas_call(
        functools.partial(_proj_residual_router_kernel, n_in=n_in, tm=tm),
        grid=(n // tm,),
        in_specs=in_specs,
        out_specs=[rows(d), rows(d // 2), pl.BlockSpec((2 * TOP_K, tm), lambda i: (0, i)), rows(LANES),
                   pl.BlockSpec((N_EXPERTS, LANES), lambda i: (0, 0))],
        out_shape=[jax.ShapeDtypeStruct((n, d), F32), jax.ShapeDtypeStruct((n, d // 2), jnp.int32),
                   jax.ShapeDtypeStruct((2 * TOP_K, n), jnp.int32), jax.ShapeDtypeStruct((n, LANES), F32),
                   jax.ShapeDtypeStruct((N_EXPERTS, LANES), jnp.int32)],
        scratch_shapes=[pltpu.VMEM((N_EXPERTS, LANES), F32)],
        compiler_params=_params("arbitrary"),
        name="proj_residual_router",
    )(*a_list, *w_list, bias, x, gate, g, scale, shift, rw_both, rw_hi, rb_pad)


def _rope(xn, cos_t, sin_t):
    return xn * cos_t + pltpu.roll(xn, LANES // 2, axis=1) * sin_t


def _mla_prep_kernel(z_ref, halo_ref, pw_ref, ps_ref, cqg_ref, wuq_ref, ckvg_ref, wuk_ref, wuv_ref,
                     qg_ref, kg_ref, cos_ref, sin_ref,
                     yp_ref, q_ref, k_ref, v_ref, ext_ref, *, tm, per):
    si = pl.program_id(0) % per
    u = z_ref[:, 0:POOL_WIDTH]
    ext_ref[0:POOL_HALO, :] = jnp.where(si == 0, 0.0, halo_ref[...])
    ext_ref[POOL_HALO:, :] = u
    t = si * tm + lax.broadcasted_iota(jnp.int32, (tm, 1), 0)
    for g, w in enumerate(POOL_WINDOWS):
        cols = slice(g * POOL_GROUP_DIM, (g + 1) * POOL_GROUP_DIM)
        ug = u[:, cols]
        s = ug
        for j in range(1, w):
            s = s + ext_ref[POOL_HALO - j:POOL_HALO - j + tm, cols]
        cnt = jnp.minimum(t + 1, w).astype(F32)
        pooled = s / cnt - ug
        yp = jnp.dot(pooled.astype(BF16), pw_ref[g], preferred_element_type=F32) * ps_ref[:, cols]
        yp_ref[:, cols] = yp.astype(yp_ref.dtype)

    cos_t, sin_t = cos_ref[...], sin_ref[...]
    inv_head = 1.0 / QK_HEAD_DIM

    cq = z_ref[:, Z_CQ:Z_CKV]
    cqn = cq * lax.rsqrt(jnp.mean(cq * cq, axis=-1, keepdims=True) + EPS) * cqg_ref[...]
    qf = jnp.dot(cqn.astype(BF16), wuq_ref[...], preferred_element_type=F32)
    q_scale = QK_HEAD_DIM ** -0.5 * LOG2_E
    for h in range(MLA_HEADS):
        qh = qf[:, h * LANES:(h + 1) * LANES]
        ss = jnp.sum(qh * qh, axis=-1, keepdims=True) * inv_head
        qn = qh * lax.rsqrt(ss + EPS) * qg_ref[...]
        q_ref[0, h] = (_rope(qn, cos_t, sin_t) * q_scale).astype(q_ref.dtype)

    ckv = z_ref[:, Z_CKV:Z_ROPE]
    ckvn = (ckv * lax.rsqrt(jnp.mean(ckv * ckv, axis=-1, keepdims=True) + EPS) * ckvg_ref[...]).astype(BF16)
    kf = jnp.dot(ckvn, wuk_ref[...], preferred_element_type=F32)
    vf = jnp.dot(ckvn, wuv_ref[...], preferred_element_type=F32)
    k_rope = z_ref[:, Z_ROPE:Z_WIDTH]
    ones_lane = lax.broadcasted_iota(jnp.int32, (tm, LANES), 1) == V_HEAD_DIM
    for h in range(MLA_HEADS):
        kh = kf[:, h * LANES:(h + 1) * LANES] + k_rope
        ss = jnp.sum(kh * kh, axis=-1, keepdims=True) * inv_head
        kn = kh * lax.rsqrt(ss + EPS) * kg_ref[...]
        k_ref[0, h] = _rope(kn, cos_t, sin_t).astype(k_ref.dtype)
        v_ref[0, h] = jnp.where(ones_lane, 1.0, vf[:, h * LANES:(h + 1) * LANES]).astype(v_ref.dtype)


def mla_prep(z, pool_w, pool_scale, cq_g, wuq_pad, ckv_g, wuk_pad, wuv_pad, qg_pad, kg_pad,
             cos_t, sin_t, batch, seq, tm=512):
    n = z.shape[0]
    per = seq // tm
    hb = tm // POOL_HALO
    full = lambda a: pl.BlockSpec(a.shape, lambda i: (0,) * a.ndim)
    tab = pl.BlockSpec((tm, LANES), lambda i: (i, 0))
    head_out = pl.BlockSpec((1, MLA_HEADS, tm, LANES), lambda i: (i // per, 0, i % per, 0))
    head_shape = jax.ShapeDtypeStruct((batch, MLA_HEADS, seq, LANES), BF16)
    return pl.pallas_call(
        functools.partial(_mla_prep_kernel, tm=tm, per=per),
        grid=(n // tm,),
        in_specs=[pl.BlockSpec((tm, Z_WIDTH), lambda i: (i, 0)),
                  pl.BlockSpec((POOL_HALO, POOL_WIDTH), lambda i: (jnp.maximum(i * hb - 1, 0), 0)),
                  full(pool_w), full(pool_scale), full(cq_g), full(wuq_pad), full(ckv_g), full(wuk_pad),
                  full(wuv_pad), full(qg_pad), full(kg_pad), tab, tab],
        out_specs=[pl.BlockSpec((tm, POOL_WIDTH), lambda i: (i, 0)), head_out, head_out, head_out],
        out_shape=[jax.ShapeDtypeStruct((n, POOL_WIDTH), BF16), head_shape, head_shape, head_shape],
        scratch_shapes=[pltpu.VMEM((tm + POOL_HALO, POOL_WIDTH), F32)],
        compiler_params=_params("parallel"),
        name="mla_prep",
    )(z, z, pool_w, pool_scale, cq_g, wuq_pad, ckv_g, wuk_pad, wuv_pad, qg_pad, kg_pad, cos_t, sin_t)


def _flash_kernel(q_ref, k_ref, v_ref, o_ref, m_ref, acc_ref, *, tq, tk, td, heads):
    qi = pl.program_id(2)
    q_chunk = lax.broadcasted_iota(jnp.int32, (tq, td), 0) // CHUNK
    k_chunk = lax.broadcasted_iota(jnp.int32, (tq, td), 1) // CHUNK
    m_ref[...] = jnp.full(m_ref.shape, MASK_VALUE, F32)
    acc_ref[...] = jnp.zeros(acc_ref.shape, F32)

    def step(start, width, mask, row0=0):
        for hh in range(heads):
            k = k_ref[0, hh, pl.ds(start, width), :]
            v = v_ref[0, hh, pl.ds(start, width), :]
            s = lax.dot_general(q_ref[0, hh, row0:, :], k, (((1,), (1,)), ((), ())), preferred_element_type=F32)
            if mask is not None:
                s = jnp.where(mask[row0:], s, MASK_VALUE)
            cols_s = [s[:, c * LANES:(c + 1) * LANES] for c in range(width // LANES)]
            s_max = cols_s[0]
            for sc in cols_s[1:]:
                s_max = jnp.maximum(s_max, sc)
            m_prev = m_ref[hh, row0:, :]
            m_new = jnp.maximum(m_prev, jnp.max(s_max, axis=-1, keepdims=True))
            alpha = jnp.exp2(m_prev - m_new)
            p = jnp.concatenate([jnp.exp2(sc - m_new) for sc in cols_s], axis=1).astype(v.dtype)
            acc_ref[hh, row0:, :] = alpha * acc_ref[hh, row0:, :] + jnp.dot(p, v, preferred_element_type=F32)
            m_ref[hh, row0:, :] = m_new

    def full_step(j, carry):
        step(pl.multiple_of(j * tk, tk), tk, None)
        return carry

    lax.fori_loop(0, (tq // tk) * qi, full_step, 0)
    for dd in range(tq // td):
        step(pl.multiple_of(qi * tq + dd * td, td), td, k_chunk + dd * (td // CHUNK) <= q_chunk, row0=dd * td)
    outs = []
    for hh in range(heads):
        acc = acc_ref[hh]
        outs.append((acc / acc[:, V_HEAD_DIM:V_HEAD_DIM + 1])[:, :V_HEAD_DIM])
    o_ref[0] = jnp.concatenate(outs, axis=-1).astype(o_ref.dtype)


def flash_attention(q, k, v, tq=1024, tk=1024, td=512):
    b, h, s, _ = q.shape
    heads = LANES // V_HEAD_DIM
    return pl.pallas_call(
        functools.partial(_flash_kernel, tq=tq, tk=tk, td=td, heads=heads),
        grid=(b, h // heads, s // tq),
        in_specs=[pl.BlockSpec((1, heads, tq, LANES), lambda bi, hi, qi: (bi, hi, qi, 0)),
                  pl.BlockSpec((1, heads, s, LANES), lambda bi, hi, qi: (bi, hi, 0, 0)),
                  pl.BlockSpec((1, heads, s, LANES), lambda bi, hi, qi: (bi, hi, 0, 0))],
        out_specs=pl.BlockSpec((1, tq, LANES), lambda bi, hi, qi: (bi, qi, hi)),
        out_shape=jax.ShapeDtypeStruct((b, s, h * V_HEAD_DIM), BF16),
        scratch_shapes=[pltpu.VMEM((heads, tq, LANES), F32) for _ in range(2)],
        compiler_params=_params("parallel", "parallel", "parallel"),
        name="flash_attention",
    )(q, k, v)


def _glu_conv_kernel(h0_ref, hn_ref, wv_ref, wg_ref, bv_ref, bg_ref, w_ref, b_ref, g_ref, beta_ref, o_ref,
                     ext_ref, sh_ref, conv_ref, nxt_ref, *, tm, per):
    i = pl.program_id(0)
    d = o_ref.shape[1]

    def glu(h):
        val = jnp.dot(h, wv_ref[...], preferred_element_type=F32) + bv_ref[...]
        gt = jnp.dot(h, wg_ref[...], preferred_element_type=F32) + bg_ref[...]
        return val * jax.nn.sigmoid(gt)

    @pl.when(i == 0)
    def _():
        ext_ref[CONV_HALO:, :] = glu(h0_ref[...])

    @pl.when(i > 0)
    def _():
        ext_ref[0:CONV_HALO, :] = ext_ref[tm:tm + CONV_HALO, :]
        ext_ref[CONV_HALO:, :] = nxt_ref[...]

    @pl.when(i % per == 0)
    def _():
        ext_ref[0:CONV_HALO, :] = jnp.zeros((CONV_HALO, d), F32)

    nxt_ref[...] = glu(hn_ref[...])
    span = sh_ref.shape[1]
    for b in range(1, SUBLANES):
        sh_ref[b - 1] = ext_ref[b:b + span, :]
    first = CONV_HALO - (CONV_WIDTH - 1)
    for r0 in range(0, tm, CONV_PATCH_ROWS):
        for c0 in range(0, d, CONV_PATCH_COLS):
            cols = slice(c0, c0 + CONV_PATCH_COLS)
            view = (CONV_PATCH_ROWS // SUBLANES, SUBLANES, CONV_PATCH_COLS)
            patch = jnp.broadcast_to(b_ref[:, cols], view)
            for j in range(CONV_WIDTH):
                a, b = divmod(first + j, SUBLANES)
                rows = slice(SUBLANES * a + r0, SUBLANES * a + r0 + CONV_PATCH_ROWS)
                win = ext_ref[rows, cols] if b == 0 else sh_ref[b - 1, rows, cols]
                tap = jnp.broadcast_to(w_ref[j:j + 1, cols], view[1:])
                patch = patch + tap[None] * win.reshape(view)
            conv_ref[r0:r0 + CONV_PATCH_ROWS, cols] = patch.reshape(CONV_PATCH_ROWS, CONV_PATCH_COLS)
    acc = conv_ref[...]
    mu = jnp.mean(acc, axis=-1, keepdims=True)
    cen = acc - mu
    var = jnp.mean(cen * cen, axis=-1, keepdims=True)
    y = cen * lax.rsqrt(var + EPS) * g_ref[...] + beta_ref[...]
    o_ref[...] = (y * jax.nn.sigmoid(y)).astype(o_ref.dtype)


def glu_dwconv_ln_silu(h, wv, wg, bv, bg, w_pad, b, g, beta, seq, tm=256):
    n, k = h.shape
    d = wv.shape[1]
    per = seq // tm
    last = n // tm - 1
    full = lambda a: pl.BlockSpec(a.shape, lambda i: (0, 0))
    return pl.pallas_call(
        functools.partial(_glu_conv_kernel, tm=tm, per=per),
        grid=(n // tm,),
        in_specs=[pl.BlockSpec((tm, k), lambda i: (0, 0)),
                  pl.BlockSpec((tm, k), lambda i: (jnp.minimum(i + 1, last), 0)),
                  full(wv), full(wg), full(bv), full(bg), full(w_pad), full(b), full(g), full(beta)],
        out_specs=pl.BlockSpec((tm, d), lambda i: (i, 0)),
        out_shape=jax.ShapeDtypeStruct((n, d), BF16),
        scratch_shapes=[pltpu.VMEM((tm + CONV_HALO, d), F32),
                        pltpu.VMEM((SUBLANES - 1, tm + CONV_HALO - SUBLANES, d), F32),
                        pltpu.VMEM((tm, d), F32), pltpu.VMEM((tm, d), F32)],
        compiler_params=_params("arbitrary"),
        name="glu_dwconv_ln_silu",
    )(h, h, wv, wg, bv, bg, w_pad, b, g, beta)


def _route_tile(logits, before_tile, tm):
    row_f = lax.broadcasted_iota(jnp.int32, (N_EXPERTS, tm), 0).astype(F32)
    neg = -jnp.inf
    picks, vals, ids = [], [], []
    for _ in range(TOP_K):
        mx = jnp.max(logits, axis=0, keepdims=True)
        idx = jnp.min(jnp.where(logits == mx, row_f, float(N_EXPERTS)), axis=0, keepdims=True)
        pick = row_f == idx
        picks.append(pick)
        vals.append(mx)
        ids.append(idx)
        logits = jnp.where(pick, neg, logits)
    exps = [jnp.exp(v - vals[0]) for v in vals]
    den = exps[0]
    for e in exps[1:]:
        den = den + e

    chosen = jnp.zeros((N_EXPERTS, tm), F32)
    for pick in picks:
        chosen = chosen + pick.astype(F32)
    r_io = lax.broadcasted_iota(jnp.int32, (tm, tm), 0)
    c_io = lax.broadcasted_iota(jnp.int32, (tm, tm), 1)
    later = (r_io < c_io).astype(BF16)
    before = jnp.dot(chosen.astype(BF16), later, preferred_element_type=F32) + before_tile
    ranks = [jnp.sum(jnp.where(pick, before, 0.0), axis=0, keepdims=True) for pick in picks]
    idx_out = jnp.concatenate(ids + ranks, axis=0)
    gate_out = jnp.concatenate([e / den for e in exps], axis=0)
    total = before_tile + jnp.sum(chosen, axis=1, keepdims=True)
    return idx_out, gate_out, total


def _sc_worker_base(per_worker):
    return (lax.axis_index("s") * SC_CORES + lax.axis_index("c")) * per_worker


def sc_scatter_rows(src, dest_flat, n_rows):
    n, d = src.shape
    per_w = n // SC_WORKERS
    n_chunks = per_w // SC_CHUNK
    mesh = plsc.VectorSubcoreMesh(core_axis_name="c", subcore_axis_name="s")

    @functools.partial(
        pl.kernel, out_type=jax.ShapeDtypeStruct((n_rows, d), src.dtype), mesh=mesh,
        scratch_types=[pltpu.VMEM((per_w,), jnp.int32) for _ in range(TOP_K)]
        + [pltpu.VMEM((SC_CHUNK, d), src.dtype) for _ in range(2)] + [pltpu.SemaphoreType.DMA] * 4,
        name="sc_scatter_rows")
    def scatter(src_hbm, dest_hbm, out_hbm, *scratch):
        idx_refs = scratch[:TOP_K]
        bufs = scratch[TOP_K:TOP_K + 2]
        in_sems, out_sems = scratch[TOP_K + 2:TOP_K + 4], scratch[TOP_K + 4:TOP_K + 6]
        base = _sc_worker_base(per_w)
        for k, idx_ref in enumerate(idx_refs):
            pltpu.sync_copy(dest_hbm.at[pl.ds(k * n + base, per_w)], idx_ref)

        def load(j, slot):
            return pltpu.make_async_copy(src_hbm.at[pl.ds(base + j * SC_CHUNK, SC_CHUNK)], bufs[slot], in_sems[slot])

        def store_all(j, slot):
            copies = [pltpu.make_async_copy(bufs[slot], out_hbm.at[idx_ref.at[pl.ds(j * SC_CHUNK, SC_CHUNK)]],
                                            out_sems[slot]) for idx_ref in idx_refs]
            for cp in copies:
                cp.start()
            for cp in copies:
                cp.wait()

        load(0, 0).start()

        @pl.loop(0, n_chunks // 2)
        def _(p):
            j = 2 * p
            load(j + 1, 1).start()
            load(j, 0).wait()
            store_all(j, 0)

            @pl.when(j + 2 < n_chunks)
            def _():
                load(j + 2, 0).start()

            load(j + 1, 1).wait()
            store_all(j + 1, 1)

    return scatter(src, dest_flat)


def sc_gather_rows(table, idx):
    b = idx.shape[0]
    d = table.shape[1]
    per_w = b // SC_WORKERS
    n_chunks = per_w // SC_CHUNK
    mesh = plsc.VectorSubcoreMesh(core_axis_name="c", subcore_axis_name="s")

    @functools.partial(
        pl.kernel, out_type=jax.ShapeDtypeStruct((b, d), table.dtype), mesh=mesh,
        scratch_types=[pltpu.VMEM((per_w,), jnp.int32)] + [pltpu.VMEM((SC_CHUNK, d), table.dtype) for _ in range(2)]
        + [pltpu.SemaphoreType.DMA] * 4,
        name="sc_gather_rows")
    def gather(table_hbm, idx_hbm, out_hbm, idx_ref, buf0, buf1, gsem0, gsem1, osem0, osem1):
        bufs, in_sems, out_sems = (buf0, buf1), (gsem0, gsem1), (osem0, osem1)
        base = _sc_worker_base(per_w)
        pltpu.sync_copy(idx_hbm.at[pl.ds(base, per_w)], idx_ref)

        def fetch(j, slot):
            return pltpu.make_async_copy(table_hbm.at[idx_ref.at[pl.ds(j * SC_CHUNK, SC_CHUNK)]], bufs[slot],
                                         in_sems[slot])

        def store(j, slot):
            return pltpu.make_async_copy(bufs[slot], out_hbm.at[pl.ds(base + j * SC_CHUNK, SC_CHUNK)],
                                         out_sems[slot])

        fetch(0, 0).start()

        @pl.loop(0, n_chunks // 2)
        def _(p):
            j = 2 * p

            @pl.when(p > 0)
            def _():
                store(j - 1, 1).wait()

            fetch(j + 1, 1).start()
            fetch(j, 0).wait()
            store(j, 0).start()
            fetch(j + 1, 1).wait()
            store(j, 0).wait()

            @pl.when(j + 2 < n_chunks)
            def _():
                fetch(j + 2, 0).start()

            store(j + 1, 1).start()

        store(n_chunks - 1, 1).wait()

    return gather(table, idx)


def _ffn_kernel(be_ref, nv_ref, nu_ref, xs_ref, w1_ref, b1g_ref, b1u_ref, w2_ref, b2_ref, y_ref,
                wg_ref, wu_ref, w2s_ref):
    i = pl.program_id(0)
    pair = 2 * LANES

    @pl.when(i < nu_ref[0])
    def _():
        e = be_ref[i]
        prev = be_ref[jnp.maximum(i - 1, 0)]

        @pl.when((i == 0) | (e != prev))
        def _():
            r_io = lax.broadcasted_iota(jnp.int32, (pair, pair), 0)
            c_io = lax.broadcasted_iota(jnp.int32, (pair, pair), 1)
            want = jnp.where(c_io < LANES, 2 * c_io, 2 * (c_io - LANES) + 1)
            sel = (r_io == want).astype(BF16)
            for c in range(wg_ref.shape[1] // LANES):
                slab = w1_ref[0, 0, :, c * pair:(c + 1) * pair].astype(BF16)
                split = jnp.dot(slab, sel, preferred_element_type=F32)
                wg_ref[:, c * LANES:(c + 1) * LANES] = split[:, :LANES].astype(BF16)
                wu_ref[:, c * LANES:(c + 1) * LANES] = split[:, LANES:].astype(BF16)
            w2s_ref[...] = w2_ref[0, 0].astype(BF16)

        nvalid = nv_ref[i]

        def ffn_rows(r0, rows):
            row = r0 + lax.broadcasted_iota(jnp.int32, (rows, xs_ref.shape[1]), 0)
            lo, hi = _unpack_halves(jnp.where(row < nvalid, xs_ref[r0:r0 + rows, :], 0))
            x = jnp.concatenate([lo, hi], axis=1).astype(BF16)
            hg = jnp.dot(x, wg_ref[...], preferred_element_type=F32) + b1g_ref[0]
            hu = jnp.dot(x, wu_ref[...], preferred_element_type=F32) + b1u_ref[0]
            gate = jnp.minimum(hg, SWIGLU_LIMIT)
            up = jnp.clip(hu, -SWIGLU_LIMIT, SWIGLU_LIMIT)
            act = gate * jax.nn.sigmoid(SWIGLU_ALPHA * gate) * (up + 1.0)
            y = jnp.dot(act.astype(BF16), w2s_ref[...], preferred_element_type=F32) + b2_ref[0]
            y_ref[r0:r0 + rows, :] = _pack_halves(y)

        def zero_rows(r0, rows):
            y_ref[r0:r0 + rows, :] = jnp.zeros((rows, y_ref.shape[1]), y_ref.dtype)

        for r0 in range(0, y_ref.shape[0], MOE_CHUNK):
            left = nvalid - r0

            @pl.when(left > MOE_BLOCK)
            def _(r0=r0):
                ffn_rows(r0, MOE_CHUNK)

            @pl.when((left > 0) & (left <= MOE_BLOCK))
            def _(r0=r0):
                ffn_rows(r0, MOE_BLOCK)
                zero_rows(r0 + MOE_BLOCK, MOE_CHUNK - MOE_BLOCK)

            @pl.when(left <= 0)
            def _(r0=r0):
                zero_rows(r0, MOE_CHUNK)

    @pl.when(i >= nu_ref[0])
    def _():
        y_ref[...] = jnp.zeros(y_ref.shape, y_ref.dtype)


def moe_ffn(xs, block_e, block_valid, n_used, layer, w1, b1g, b1u, w2, b2):
    n_rows, half = xs.shape
    d = 2 * half
    n_blocks = n_rows // MOE_STEP
    n_exp, f2 = w1.shape[1], w1.shape[3]
    f = f2 // 2
    rows = lambda i, be, nv, nu: (jnp.minimum(i, nu[0] - 1), 0)
    vec = lambda width: pl.BlockSpec((1, 1, width), lambda i, be, nv, nu: (be[i], 0, 0))
    grid_spec = pltpu.PrefetchScalarGridSpec(
        num_scalar_prefetch=3,
        grid=(n_blocks,),
        in_specs=[
            pl.BlockSpec((MOE_STEP, half), rows),
            pl.BlockSpec((1, 1, d, f2), lambda i, be, nv, nu: (layer, be[i], 0, 0)),
            vec(f), vec(f),
            pl.BlockSpec((1, 1, f, d), lambda i, be, nv, nu: (layer, be[i], 0, 0)),
            vec(d),
        ],
        out_specs=pl.BlockSpec((MOE_STEP, half), lambda i, be, nv, nu: (i, 0)),
        scratch_shapes=[pltpu.VMEM((d, f), BF16), pltpu.VMEM((d, f), BF16), pltpu.VMEM((f, d), BF16)],
    )
    return pl.pallas_call(
        _ffn_kernel,
        grid_spec=grid_spec,
        out_shape=jax.ShapeDtypeStruct((n_rows, half), jnp.int32),
        compiler_params=_params("arbitrary"),
        name="moe_ffn",
    )(block_e, block_valid, n_used, xs, w1, b1g.reshape(n_exp, 1, f), b1u.reshape(n_exp, 1, f), w2,
      b2.reshape(n_exp, 1, d))


def _combine_kernel(x_ref, y_ref, gates_ref, gate_ref, *rest, with_norm):
    g = gates_ref[...]
    half = y_ref.shape[2]
    acc_lo = acc_hi = None
    for k in range(TOP_K):
        lo, hi = _unpack_halves(y_ref[k])
        gk = g[:, k:k + 1]
        acc_lo = gk * lo if acc_lo is None else acc_lo + gk * lo
        acc_hi = gk * hi if acc_hi is None else acc_hi + gk * hi
    x_lo = x_ref[:, :half] + gate_ref[0, :, :half] * acc_lo
    x_hi = x_ref[:, half:] + gate_ref[0, :, half:] * acc_hi
    if with_norm:
        ng_ref, sc_ref, sh_ref = rest[:3]
        o_ref, h_ref = rest[-2:]
        x_new = jnp.concatenate([x_lo, x_hi], axis=1)
        o_ref[...] = x_new
        h_ref[...] = _modulated_norm(x_new, ng_ref[...], sc_ref[0], sh_ref[0]).astype(h_ref.dtype)
    else:
        o_ref = rest[-1]
        o_ref[:, :half] = x_lo
        o_ref[:, half:] = x_hi


def moe_combine(x, y_part, gates, gate_mod, seq, next_norm=None, part=0, n_parts=1, prev=(), tm=512):
    n, d = x.shape
    per = seq // tm
    steps = n // (tm * n_parts)
    first = part * steps
    rows = lambda width: pl.BlockSpec((tm, width), lambda i: (i + first, 0))
    vec = pl.BlockSpec((1, 1, d), lambda i: ((i + first) // per, 0, 0))
    in_specs = [rows(d), pl.BlockSpec((TOP_K, tm, d // 2), lambda i: (0, i, 0)), rows(LANES), vec]
    args = [x, y_part, gates, gate_mod]
    out_specs, out_shape = [rows(d)], [jax.ShapeDtypeStruct((n, d), F32)]
    if next_norm is not None:
        in_specs += [pl.BlockSpec((1, d), lambda i: (0, 0)), vec, vec]
        args += list(next_norm)
        out_specs, out_shape = out_specs + [rows(d)], out_shape + [jax.ShapeDtypeStruct((n, d), BF16)]
    aliases = {len(args) + j: j for j in range(len(prev))}
    in_specs += [pl.BlockSpec(memory_space=pl.ANY) for _ in prev]
    args += list(prev)
    return pl.pallas_call(
        functools.partial(_combine_kernel, with_norm=next_norm is not None),
        grid=(steps,),
        in_specs=in_specs,
        out_specs=out_specs,
        out_shape=out_shape,
        input_output_aliases=aliases,
        compiler_params=_params("parallel"),
        name="moe_combine",
    )(*args)


def moe_layer(x, mixer, mix_gate, mods, norm_g, router_w, router_b, layer, w1, b1, w2, b2, seq, next_norm):
    n, d = x.shape
    shift, scale, gate = mods
    rw_rows = jnp.pad(router_w.T, ((0, LANES - N_EXPERTS), (0, 0)))
    rw_hi = rw_rows.astype(BF16)
    rw_lo = (rw_rows - rw_hi.astype(F32)).astype(BF16)
    x, h, idx, gates, counts = proj_residual_router(
        *mixer, x, mix_gate, norm_g.reshape(1, d), scale, shift, jnp.concatenate([rw_hi, rw_lo], axis=0), rw_hi,
        router_b.reshape(N_EXPERTS, 1), seq)

    top_i = idx[:TOP_K]
    rank = idx[TOP_K:]
    counts = counts[:, 0]
    experts = jnp.arange(N_EXPERTS, dtype=jnp.int32)
    padded = (counts + MOE_STEP - 1) // MOE_STEP * MOE_STEP
    pad_ends = jnp.sum(jnp.where(experts[:, None] >= experts[None, :], padded[None, :], 0), axis=1)
    pad_starts = pad_ends - padded
    dest = jnp.sum(jnp.where(top_i[..., None] == experts, pad_starts, 0), axis=-1) + rank
    n_blocks = -(-n * TOP_K // MOE_STEP) + N_EXPERTS
    block_start = jnp.arange(n_blocks, dtype=jnp.int32) * MOE_STEP
    block_e = jnp.minimum(jnp.sum((pad_ends[None, :] <= block_start[:, None]).astype(jnp.int32), axis=1),
                          N_EXPERTS - 1)
    n_used = (pad_ends[N_EXPERTS - 1:] // MOE_STEP).astype(jnp.int32)
    seg_end = jnp.sum(jnp.where(block_e[:, None] == experts, pad_starts + counts, 0), axis=1)
    block_valid = jnp.clip(seg_end - block_start, 0, MOE_STEP).astype(jnp.int32)
    dest_flat = dest.reshape(-1)

    xs = sc_scatter_rows(h, dest_flat, n_blocks * MOE_STEP)
    ys = moe_ffn(xs, block_e, block_valid, n_used, layer, w1, b1[layer][:, 0::2], b1[layer][:, 1::2], w2, b2[layer])
    outs = ()
    per_part = n // COMBINE_PARTS
    for part in range(COMBINE_PARTS):
        idx_part = dest[:, part * per_part:(part + 1) * per_part].reshape(-1)
        y = sc_gather_rows(ys, idx_part).reshape(TOP_K, per_part, d // 2)
        outs = moe_combine(x, y, gates, gate, seq, next_norm, part, COMBINE_PARTS, tuple(outs))
    return (outs[0], outs[1]) if next_norm is not None else (outs[0], None)


def _pad_heads(w, width):
    k = w.shape[0]
    w = w.reshape(k, MLA_HEADS, width)
    return jnp.pad(w, ((0, 0), (0, 0), (0, LANES - width))).reshape(k, MLA_HEADS * LANES)


def _head_lane_source():
    half = QK_ROPE_DIM // 2
    first_nope = LANES // 2 - half
    lanes = (list(range(QK_NOPE_DIM, QK_NOPE_DIM + half)) + list(range(first_nope))
             + list(range(QK_NOPE_DIM + half, QK_HEAD_DIM)) + list(range(first_nope, QK_NOPE_DIM)))
    return jnp.array(lanes + [QK_HEAD_DIM] * (LANES - QK_HEAD_DIM), jnp.int32)


def _to_head_lanes(w):
    w = jnp.concatenate([w, jnp.zeros(w.shape[:-1] + (1,), w.dtype)], axis=-1)
    return jnp.take(w, _head_lane_source(), axis=-1)


def _rope_tables(positions):
    inv = 1.0 / (ROPE_THETA ** (jnp.arange(0, QK_ROPE_DIM, 2, dtype=F32) / QK_ROPE_DIM))
    ang = inv[:, None] * positions.reshape(-1).astype(F32)[None, :]
    cos, sin = jnp.cos(ang).T, jnp.sin(ang).T
    n = cos.shape[0]
    half = QK_ROPE_DIM // 2
    first_nope = LANES // 2 - half
    ones = lambda w: jnp.ones((n, w), F32)
    zeros = lambda w: jnp.zeros((n, w), F32)
    pad = LANES - QK_HEAD_DIM
    cos_t = jnp.concatenate([cos, ones(first_nope), cos, ones(QK_NOPE_DIM - first_nope), zeros(pad)], axis=1)
    sin_t = jnp.concatenate([-sin, zeros(first_nope), sin, zeros(QK_NOPE_DIM - first_nope + pad)], axis=1)
    return cos_t, sin_t


def _split_mods(m, batch):
    d = m.shape[-1] // 3
    m = m[:batch]
    return tuple(m[:, None, j * d:(j + 1) * d] for j in range(3))


def pool_mla_mixer(x, h, mods, norm_g, tables, w_in, pool_w, pool_scale, cq_norm_g, w_uq, ckv_norm_g, w_ukv,
                   q_norm_g, k_norm_g, w_out, batch, seq):
    n, d = x.shape
    shift, scale, _ = mods
    rope_cols = _to_head_lanes(jnp.pad(w_in[:, Z_ROPE:], ((0, 0), (QK_NOPE_DIM, 0))))
    w_in_pad = jnp.concatenate([w_in[:, :Z_ROPE], rope_cols], axis=1).astype(BF16)
    if h is None:
        z = norm_matmul(x, norm_g.reshape(1, d), scale, shift, w_in_pad, seq)
    else:
        z = matmul(h, w_in_pad)
    w_ukv_h = w_ukv.reshape(KV_LORA_RANK, MLA_HEADS, QK_NOPE_DIM + V_HEAD_DIM)
    k_nope = jnp.pad(w_ukv_h[:, :, :QK_NOPE_DIM], ((0, 0), (0, 0), (0, QK_ROPE_DIM)))
    wuk_pad = _to_head_lanes(k_nope).reshape(KV_LORA_RANK, MLA_HEADS * LANES).astype(BF16)
    wuv_pad = _pad_heads(w_ukv_h[:, :, QK_NOPE_DIM:].reshape(KV_LORA_RANK, -1), V_HEAD_DIM).astype(BF16)
    wuq_pad = _to_head_lanes(w_uq.reshape(Q_LORA_RANK, MLA_HEADS, QK_HEAD_DIM))
    wuq_pad = wuq_pad.reshape(Q_LORA_RANK, MLA_HEADS * LANES).astype(BF16)
    pad_g = lambda g: _to_head_lanes(g).reshape(1, LANES)
    y_pool, q, k, v = mla_prep(z, pool_w.astype(BF16), pool_scale.reshape(1, -1), cq_norm_g.reshape(1, -1), wuq_pad,
                               ckv_norm_g.reshape(1, -1), wuk_pad, wuv_pad, pad_g(q_norm_g), pad_g(k_norm_g),
                               *tables, batch, seq)
    y_att = flash_attention(q, k, v).reshape(n, MLA_HEADS * V_HEAD_DIM)
    w_out_b = w_out.astype(BF16)
    return [y_pool, y_att], [w_out_b[:POOL_WIDTH], w_out_b[POOL_WIDTH:]], jnp.zeros((1, d), F32)


def conformer_mixer(h, pw1_w, pw1_b, dw_w, dw_b, ln_g, ln_b, pw2_w, pw2_b, seq):
    d = pw2_w.shape[1]
    cd = pw1_w.shape[1] // 2
    pw1 = pw1_w.astype(BF16)
    w_pad = jnp.pad(dw_w, ((0, CONV_HALO - CONV_WIDTH), (0, 0)))
    u = glu_dwconv_ln_silu(h, pw1[:, :cd], pw1[:, cd:], pw1_b[:cd].reshape(1, cd), pw1_b[cd:].reshape(1, cd),
                           w_pad, dw_b.reshape(1, cd), ln_g.reshape(1, cd), ln_b.reshape(1, cd), seq)
    return [u], [pw2_w.astype(BF16)], pw2_b.reshape(1, d)


def kernel(x, c, positions, ada_mix_w, ada_mix_b, norm_mix_g, w_in, pool_w, pool_scale, cq_norm_g, w_uq,
           ckv_norm_g, w_ukv, q_norm_g, k_norm_g, w_out, conv_pw1_w, conv_pw1_b, conv_dw_w, conv_dw_b,
           conv_ln_g, conv_ln_b, conv_pw2_w, conv_pw2_b, ada_ffn_w, ada_ffn_b, norm_ffn_g, router_w,
           router_b, moe_w1, moe_b1, moe_w2, moe_b2):
    batch, seq, d = x.shape
    depth = ada_mix_w.shape[0]
    c_pad = jnp.pad(c, ((0, 8 - batch), (0, 0)))
    mix_mods = adaln(c_pad, ada_mix_w, ada_mix_b)
    ffn_mods = adaln(c_pad, ada_ffn_w, ada_ffn_b)
    tables = _rope_tables(positions)
    xf = x.reshape(batch * seq, d)
    h = None
    for layer in range(depth):
        i = layer // 2
        mods = _split_mods(mix_mods[layer], batch)
        if layer % 2 == 0:
            mixer = pool_mla_mixer(xf, h, mods, norm_mix_g[layer], tables, w_in[i], pool_w[i], pool_scale[i],
                                   cq_norm_g[i], w_uq[i], ckv_norm_g[i], w_ukv[i], q_norm_g[i], k_norm_g[i],
                                   w_out[i], batch, seq)
        else:
            mixer = conformer_mixer(h, conv_pw1_w[i], conv_pw1_b[i], conv_dw_w[i], conv_dw_b[i],
                                    conv_ln_g[i], conv_ln_b[i], conv_pw2_w[i], conv_pw2_b[i], seq)
        next_norm = None
        if layer + 1 < depth:
            n_shift, n_scale, _ = _split_mods(mix_mods[layer + 1], batch)
            next_norm = (norm_mix_g[layer + 1].reshape(1, d), n_scale, n_shift)
        xf, h = moe_layer(xf, mixer, mods[2], _split_mods(ffn_mods[layer], batch), norm_ffn_g[layer],
                          router_w[layer], router_b[layer], layer, moe_w1, moe_b1, moe_w2, moe_b2, seq, next_norm)
    return xf.reshape(batch, seq, d)
```

```python
import functools

import jax
import jax.numpy as jnp
from jax import lax
from jax.experimental import pallas as pl
from jax.experimental.pallas import tpu as pltpu
from jax.experimental.pallas import tpu_sc as plsc

F32 = jnp.float32
BF16 = jnp.bfloat16
HIGHEST = lax.Precision.HIGHEST

EPS = 1e-6
POOL_WINDOWS = (2, 4, 8, 16)
POOL_GROUP_DIM = 128
POOL_WIDTH = POOL_GROUP_DIM * len(POOL_WINDOWS)
MLA_HEADS = 8
QK_NOPE_DIM = 64
QK_ROPE_DIM = 32
QK_HEAD_DIM = QK_NOPE_DIM + QK_ROPE_DIM
V_HEAD_DIM = 64
Q_LORA_RANK = 384
KV_LORA_RANK = 256
ROPE_THETA = 10000.0
CHUNK = 64
CONV_WIDTH = 31
N_EXPERTS = 32
TOP_K = 4
SWIGLU_ALPHA = 1.702
SWIGLU_LIMIT = 7.0
MOE_BLOCK = 256
MOE_CHUNK = 512
MOE_STEP = 1024

LANES = 128
SUBLANES = 8
CONV_PATCH_ROWS = 64
CONV_PATCH_COLS = 256
POOL_HALO = 16
CONV_HALO = 32
MASK_VALUE = -1e30
LOG2_E = 1.4426950408889634
VMEM_LIMIT = 52 * 1024 * 1024
SC_CORES = 2
SC_SUBCORES = 16
SC_WORKERS = SC_CORES * SC_SUBCORES
SC_CHUNK = 64
COMBINE_PARTS = 2

Z_CQ = POOL_WIDTH
Z_CKV = Z_CQ + Q_LORA_RANK
Z_ROPE = Z_CKV + KV_LORA_RANK
Z_WIDTH = Z_ROPE + LANES


def _params(*sem, vmem=None):
    return pltpu.CompilerParams(dimension_semantics=sem, vmem_limit_bytes=vmem or VMEM_LIMIT,
                                disable_bounds_checks=True)


def _adaln_kernel(c_ref, w_ref, b_ref, o_ref):
    c = c_ref[...]
    s = c * jax.nn.sigmoid(c)
    o_ref[0] = jnp.dot(s, w_ref[0], preferred_element_type=F32, precision=HIGHEST) + b_ref[0]


def adaln(c_pad, w, b):
    n_l, d, d3 = w.shape
    tn = d3 // 2
    return pl.pallas_call(
        _adaln_kernel,
        grid=(n_l, d3 // tn),
        in_specs=[pl.BlockSpec((8, d), lambda l, j: (0, 0)),
                  pl.BlockSpec((1, d, tn), lambda l, j: (l, 0, j)),
                  pl.BlockSpec((1, 1, tn), lambda l, j: (l, 0, j))],
        out_specs=pl.BlockSpec((1, 8, tn), lambda l, j: (l, 0, j)),
        out_shape=jax.ShapeDtypeStruct((n_l, 8, d3), F32),
        compiler_params=_params("parallel", "parallel"),
        name="adaln",
    )(c_pad, w, b.reshape(n_l, 1, d3))


def _modulated_norm(x, g, scale, shift):
    ms = jnp.mean(x * x, axis=-1, keepdims=True)
    return x * lax.rsqrt(ms + EPS) * g * (1.0 + scale) + shift


def _pack_halves(x):
    w = x.shape[1] // 2
    lo = lax.bitcast_convert_type(x[:, :w].astype(BF16).astype(F32), jnp.uint32)
    hi = lax.bitcast_convert_type(x[:, w:].astype(BF16).astype(F32), jnp.uint32)
    return lax.bitcast_convert_type((lo >> 16) | (hi & jnp.uint32(0xFFFF0000)), jnp.int32)


def _unpack_halves(p):
    u = lax.bitcast_convert_type(p, jnp.uint32)
    lo = lax.bitcast_convert_type(u << 16, F32)
    hi = lax.bitcast_convert_type(u & jnp.uint32(0xFFFF0000), F32)
    return lo, hi


def _norm_matmul_kernel(x_ref, g_ref, sc_ref, sh_ref, w_ref, o_ref):
    h = _modulated_norm(x_ref[...], g_ref[...], sc_ref[0], sh_ref[0])
    o_ref[...] = jnp.dot(h.astype(BF16), w_ref[...], preferred_element_type=F32)


def norm_matmul(x, g, scale, shift, w, seq, tm=512):
    n, d = x.shape
    m = w.shape[1]
    per = seq // tm
    vec = pl.BlockSpec((1, 1, d), lambda i: (i // per, 0, 0))
    return pl.pallas_call(
        _norm_matmul_kernel,
        grid=(n // tm,),
        in_specs=[pl.BlockSpec((tm, d), lambda i: (i, 0)), pl.BlockSpec((1, d), lambda i: (0, 0)), vec, vec,
                  pl.BlockSpec((d, m), lambda i: (0, 0))],
        out_specs=pl.BlockSpec((tm, m), lambda i: (i, 0)),
        out_shape=jax.ShapeDtypeStruct((n, m), F32),
        compiler_params=_params("parallel"),
        name="norm_matmul",
    )(x, g, scale, shift, w)


def _matmul_kernel(a_ref, w_ref, o_ref):
    o_ref[...] = jnp.dot(a_ref[...], w_ref[...], preferred_element_type=F32).astype(o_ref.dtype)


def matmul(a, w, out_dtype=F32, tm=512):
    n, k = a.shape
    m = w.shape[1]
    return pl.pallas_call(
        _matmul_kernel,
        grid=(n // tm,),
        in_specs=[pl.BlockSpec((tm, k), lambda i: (i, 0)), pl.BlockSpec((k, m), lambda i: (0, 0))],
        out_specs=pl.BlockSpec((tm, m), lambda i: (i, 0)),
        out_shape=jax.ShapeDtypeStruct((n, m), out_dtype),
        compiler_params=_params("parallel"),
        name="matmul",
    )(a, w)


def _proj_residual_router_kernel(*refs, n_in, tm):
    a_refs, w_refs = refs[:n_in], refs[n_in:2 * n_in]
    (b_ref, x_ref, gate_ref, g_ref, sc_ref, sh_ref, rw2_ref, rwh_ref, rb_ref,
     o_ref, h_ref, idx_ref, gates_ref, cnt_ref, carry_ref) = refs[2 * n_in:]

    @pl.when(pl.program_id(0) == 0)
    def _():
        carry_ref[...] = jnp.zeros(carry_ref.shape, F32)

    acc = b_ref[...]
    for a_ref, w_ref in zip(a_refs, w_refs):
        acc = acc + jnp.dot(a_ref[...], w_ref[...], preferred_element_type=F32)
    x_new = x_ref[...] + gate_ref[0] * acc
    o_ref[...] = x_new
    h = _modulated_norm(x_new, g_ref[...], sc_ref[0], sh_ref[0])
    h_ref[...] = _pack_halves(h)
    h_hi = h.astype(BF16)
    h_lo = (h - h_hi.astype(F32)).astype(BF16)
    contract_features = (((1,), (1,)), ((), ()))
    both = lax.dot_general(rw2_ref[...], h_hi, contract_features, preferred_element_type=F32)
    low = lax.dot_general(rwh_ref[...], h_lo, contract_features, preferred_element_type=F32)
    logits = (both[:LANES] + both[LANES:] + low)[:N_EXPERTS] + rb_ref[...]
    idx, gates, total = _route_tile(logits, carry_ref[:, 0:1], tm)
    idx_ref[...] = idx.astype(jnp.int32)
    gate_rows = jnp.concatenate([gates, jnp.zeros((LANES - TOP_K, tm), F32)], axis=0)
    gates_ref[...] = jnp.transpose(gate_rows)
    carry_ref[...] = jnp.broadcast_to(total, carry_ref.shape)
    cnt_ref[...] = jnp.broadcast_to(total, cnt_ref.shape).astype(jnp.int32)


def proj_residual_router(a_list, w_list, bias, x, gate, g, scale, shift, rw_both, rw_hi, rb_pad, seq, tm=512):
    n, d = x.shape
    per = seq // tm
    n_in = len(a_list)
    rows = lambda width: pl.BlockSpec((tm, width), lambda i: (i, 0))
    const = lambda a: pl.BlockSpec(a.shape, lambda i: (0, 0))
    vec = pl.BlockSpec((1, 1, d), lambda i: (i // per, 0, 0))
    in_specs = [rows(a.shape[1]) for a in a_list] + [const(w) for w in w_list]
    in_specs += [const(bias), rows(d), vec, const(g), vec, vec, const(rw_both), const(rw_hi), const(rb_pad)]
    return pl.pallas_call(
        functools.partial(_proj_residual_router_kernel, n_in=n_in, tm=tm),
        grid=(n // tm,),
        in_specs=in_specs,
        out_specs=[rows(d), rows(d // 2), pl.BlockSpec((2 * TOP_K, tm), lambda i: (0, i)), rows(LANES),
                   pl.BlockSpec((N_EXPERTS, LANES), lambda i: (0, 0))],
        out_shape=[jax.ShapeDtypeStruct((n, d), F32), jax.ShapeDtypeStruct((n, d // 2), jnp.int32),
                   jax.ShapeDtypeStruct((2 * TOP_K, n), jnp.int32), jax.ShapeDtypeStruct((n, LANES), F32),
                   jax.ShapeDtypeStruct((N_EXPERTS, LANES), jnp.int32)],
        scratch_shapes=[pltpu.VMEM((N_EXPERTS, LANES), F32)],
        compiler_params=_params("arbitrary"),
        name="proj_residual_router",
    )(*a_list, *w_list, bias, x, gate, g, scale, shift, rw_both, rw_hi, rb_pad)


def _rope(xn, cos_t, sin_t):
    return xn * cos_t + pltpu.roll(xn, LANES // 2, axis=1) * sin_t


def _mla_prep_kernel(z_ref, halo_ref, pw_ref, ps_ref, cqg_ref, wuq_ref, ckvg_ref, wuk_ref, wuv_ref,
                     qg_ref, kg_ref, cos_ref, sin_ref,
                     yp_ref, q_ref, k_ref, v_ref, ext_ref, *, tm, per):
    si = pl.program_id(0) % per
    u = z_ref[:, 0:POOL_WIDTH]
    ext_ref[0:POOL_HALO, :] = jnp.where(si == 0, 0.0, halo_ref[...])
    ext_ref[POOL_HALO:, :] = u
    t = si * tm + lax.broadcasted_iota(jnp.int32, (tm, 1), 0)
    for g, w in enumerate(POOL_WINDOWS):
        cols = slice(g * POOL_GROUP_DIM, (g + 1) * POOL_GROUP_DIM)
        ug = u[:, cols]
        s = ug
        for j in range(1, w):
            s = s + ext_ref[POOL_HALO - j:POOL_HALO - j + tm, cols]
        cnt = jnp.minimum(t + 1, w).astype(F32)
        pooled = s / cnt - ug
        yp = jnp.dot(pooled.astype(BF16), pw_ref[g], preferred_element_type=F32) * ps_ref[:, cols]
        yp_ref[:, cols] = yp.astype(yp_ref.dtype)

    cos_t, sin_t = cos_ref[...], sin_ref[...]
    inv_head = 1.0 / QK_HEAD_DIM

    cq = z_ref[:, Z_CQ:Z_CKV]
    cqn = cq * lax.rsqrt(jnp.mean(cq * cq, axis=-1, keepdims=True) + EPS) * cqg_ref[...]
    qf = jnp.dot(cqn.astype(BF16), wuq_ref[...], preferred_element_type=F32)
    q_scale = QK_HEAD_DIM ** -0.5 * LOG2_E
    for h in range(MLA_HEADS):
        qh = qf[:, h * LANES:(h + 1) * LANES]
        ss = jnp.sum(qh * qh, axis=-1, keepdims=True) * inv_head
        qn = qh * lax.rsqrt(ss + EPS) * qg_ref[...]
        q_ref[0, h] = (_rope(qn, cos_t, sin_t) * q_scale).astype(q_ref.dtype)

    ckv = z_ref[:, Z_CKV:Z_ROPE]
    ckvn = (ckv * lax.rsqrt(jnp.mean(ckv * ckv, axis=-1, keepdims=True) + EPS) * ckvg_ref[...]).astype(BF16)
    kf = jnp.dot(ckvn, wuk_ref[...], preferred_element_type=F32)
    vf = jnp.dot(ckvn, wuv_ref[...], preferred_element_type=F32)
    k_rope = z_ref[:, Z_ROPE:Z_WIDTH]
    ones_lane = lax.broadcasted_iota(jnp.int32, (tm, LANES), 1) == V_HEAD_DIM
    for h in range(MLA_HEADS):
        kh = kf[:, h * LANES:(h + 1) * LANES] + k_rope
        ss = jnp.sum(kh * kh, axis=-1, keepdims=True) * inv_head
        kn = kh * lax.rsqrt(ss + EPS) * kg_ref[...]
        k_ref[0, h] = _rope(kn, cos_t, sin_t).astype(k_ref.dtype)
        v_ref[0, h] = jnp.where(ones_lane, 1.0, vf[:, h * LANES:(h + 1) * LANES]).astype(v_ref.dtype)


def mla_prep(z, pool_w, pool_scale, cq_g, wuq_pad, ckv_g, wuk_pad, wuv_pad, qg_pad, kg_pad,
             cos_t, sin_t, batch, seq, tm=512):
    n = z.shape[0]
    per = seq // tm
    hb = tm // POOL_HALO
    full = lambda a: pl.BlockSpec(a.shape, lambda i: (0,) * a.ndim)
    tab = pl.BlockSpec((tm, LANES), lambda i: (i, 0))
    head_out = pl.BlockSpec((1, MLA_HEADS, tm, LANES), lambda i: (i // per, 0, i % per, 0))
    head_shape = jax.ShapeDtypeStruct((batch, MLA_HEADS, seq, LANES), BF16)
    return pl.pallas_call(
        functools.partial(_mla_prep_kernel, tm=tm, per=per),
        grid=(n // tm,),
        in_specs=[pl.BlockSpec((tm, Z_WIDTH), lambda i: (i, 0)),
                  pl.BlockSpec((POOL_HALO, POOL_WIDTH), lambda i: (jnp.maximum(i * hb - 1, 0), 0)),
                  full(pool_w), full(pool_scale), full(cq_g), full(wuq_pad), full(ckv_g), full(wuk_pad),
                  full(wuv_pad), full(qg_pad), full(kg_pad), tab, tab],
        out_specs=[pl.BlockSpec((tm, POOL_WIDTH), lambda i: (i, 0)), head_out, head_out, head_out],
        out_shape=[jax.ShapeDtypeStruct((n, POOL_WIDTH), BF16), head_shape, head_shape, head_shape],
        scratch_shapes=[pltpu.VMEM((tm + POOL_HALO, POOL_WIDTH), F32)],
        compiler_params=_params("parallel"),
        name="mla_prep",
    )(z, z, pool_w, pool_scale, cq_g, wuq_pad, ckv_g, wuk_pad, wuv_pad, qg_pad, kg_pad, cos_t, sin_t)


def _flash_kernel(q_ref, k_ref, v_ref, o_ref, m_ref, acc_ref, *, tq, tk, td, heads):
    qi = pl.program_id(2)
    q_chunk = lax.broadcasted_iota(jnp.int32, (tq, td), 0) // CHUNK
    k_chunk = lax.broadcasted_iota(jnp.int32, (tq, td), 1) // CHUNK
    m_ref[...] = jnp.full(m_ref.shape, MASK_VALUE, F32)
    acc_ref[...] = jnp.zeros(acc_ref.shape, F32)

    def step(start, width, mask, row0=0):
        for hh in range(heads):
            k = k_ref[0, hh, pl.ds(start, width), :]
            v = v_ref[0, hh, pl.ds(start, width), :]
            s = lax.dot_general(q_ref[0, hh, row0:, :], k, (((1,), (1,)), ((), ())), preferred_element_type=F32)
            if mask is not None:
                s = jnp.where(mask[row0:], s, MASK_VALUE)
            cols_s = [s[:, c * LANES:(c + 1) * LANES] for c in range(width // LANES)]
            s_max = cols_s[0]
            for sc in cols_s[1:]:
                s_max = jnp.maximum(s_max, sc)
            m_prev = m_ref[hh, row0:, :]
            m_new = jnp.maximum(m_prev, jnp.max(s_max, axis=-1, keepdims=True))
            alpha = jnp.exp2(m_prev - m_new)
            p = jnp.concatenate([jnp.exp2(sc - m_new) for sc in cols_s], axis=1).astype(v.dtype)
            acc_ref[hh, row0:, :] = alpha * acc_ref[hh, row0:, :] + jnp.dot(p, v, preferred_element_type=F32)
            m_ref[hh, row0:, :] = m_new

    def full_step(j, carry):
        step(pl.multiple_of(j * tk, tk), tk, None)
        return carry

    lax.fori_loop(0, (tq // tk) * qi, full_step, 0)
    for dd in range(tq // td):
        step(pl.multiple_of(qi * tq + dd * td, td), td, k_chunk + dd * (td // CHUNK) <= q_chunk, row0=dd * td)
    outs = []
    for hh in range(heads):
        acc = acc_ref[hh]
        outs.append((acc / acc[:, V_HEAD_DIM:V_HEAD_DIM + 1])[:, :V_HEAD_DIM])
    o_ref[0] = jnp.concatenate(outs, axis=-1).astype(o_ref.dtype)


def flash_attention(q, k, v, tq=1024, tk=1024, td=512):
    b, h, s, _ = q.shape
    heads = LANES // V_HEAD_DIM
    return pl.pallas_call(
        functools.partial(_flash_kernel, tq=tq, tk=tk, td=td, heads=heads),
        grid=(b, h // heads, s // tq),
        in_specs=[pl.BlockSpec((1, heads, tq, LANES), lambda bi, hi, qi: (bi, hi, qi, 0)),
                  pl.BlockSpec((1, heads, s, LANES), lambda bi, hi, qi: (bi, hi, 0, 0)),
                  pl.BlockSpec((1, heads, s, LANES), lambda bi, hi, qi: (bi, hi, 0, 0))],
        out_specs=pl.BlockSpec((1, tq, LANES), lambda bi, hi, qi: (bi, qi, hi)),
        out_shape=jax.ShapeDtypeStruct((b, s, h * V_HEAD_DIM), BF16),
        scratch_shapes=[pltpu.VMEM((heads, tq, LANES), F32) for _ in range(2)],
        compiler_params=_params("parallel", "parallel", "parallel"),
        name="flash_attention",
    )(q, k, v)


def _glu_conv_kernel(h0_ref, hn_ref, wv_ref, wg_ref, bv_ref, bg_ref, w_ref, b_ref, g_ref, beta_ref, o_ref,
                     ext_ref, sh_ref, conv_ref, nxt_ref, *, tm, per):
    i = pl.program_id(0)
    d = o_ref.shape[1]

    def glu(h):
        val = jnp.dot(h, wv_ref[...], preferred_element_type=F32) + bv_ref[...]
        gt = jnp.dot(h, wg_ref[...], preferred_element_type=F32) + bg_ref[...]
        return val * jax.nn.sigmoid(gt)

    @pl.when(i == 0)
    def _():
        ext_ref[CONV_HALO:, :] = glu(h0_ref[...])

    @pl.when(i > 0)
    def _():
        ext_ref[0:CONV_HALO, :] = ext_ref[tm:tm + CONV_HALO, :]
        ext_ref[CONV_HALO:, :] = nxt_ref[...]

    @pl.when(i % per == 0)
    def _():
        ext_ref[0:CONV_HALO, :] = jnp.zeros((CONV_HALO, d), F32)

    nxt_ref[...] = glu(hn_ref[...])
    span = sh_ref.shape[1]
    for b in range(1, SUBLANES):
        sh_ref[b - 1] = ext_ref[b:b + span, :]
    first = CONV_HALO - (CONV_WIDTH - 1)
    for r0 in range(0, tm, CONV_PATCH_ROWS):
        for c0 in range(0, d, CONV_PATCH_COLS):
            cols = slice(c0, c0 + CONV_PATCH_COLS)
            view = (CONV_PATCH_ROWS // SUBLANES, SUBLANES, CONV_PATCH_COLS)
            patch = jnp.broadcast_to(b_ref[:, cols], view)
            for j in range(CONV_WIDTH):
                a, b = divmod(first + j, SUBLANES)
                rows = slice(SUBLANES * a + r0, SUBLANES * a + r0 + CONV_PATCH_ROWS)
                win = ext_ref[rows, cols] if b == 0 else sh_ref[b - 1, rows, cols]
                tap = jnp.broadcast_to(w_ref[j:j + 1, cols], view[1:])
                patch = patch + tap[None] * win.reshape(view)
            conv_ref[r0:r0 + CONV_PATCH_ROWS, cols] = patch.reshape(CONV_PATCH_ROWS, CONV_PATCH_COLS)
    acc = conv_ref[...]
    mu = jnp.mean(acc, axis=-1, keepdims=True)
    cen = acc - mu
    var = jnp.mean(cen * cen, axis=-1, keepdims=True)
    y = cen * lax.rsqrt(var + EPS) * g_ref[...] + beta_ref[...]
    o_ref[...] = (y * jax.nn.sigmoid(y)).astype(o_ref.dtype)


def glu_dwconv_ln_silu(h, wv, wg, bv, bg, w_pad, b, g, beta, seq, tm=256):
    n, k = h.shape
    d = wv.shape[1]
    per = seq // tm
    last = n // tm - 1
    full = lambda a: pl.BlockSpec(a.shape, lambda i: (0, 0))
    return pl.pallas_call(
        functools.partial(_glu_conv_kernel, tm=tm, per=per),
        grid=(n // tm,),
        in_specs=[pl.BlockSpec((tm, k), lambda i: (0, 0)),
                  pl.BlockSpec((tm, k), lambda i: (jnp.minimum(i + 1, last), 0)),
                  full(wv), full(wg), full(bv), full(bg), full(w_pad), full(b), full(g), full(beta)],
        out_specs=pl.BlockSpec((tm, d), lambda i: (i, 0)),
        out_shape=jax.ShapeDtypeStruct((n, d), BF16),
        scratch_shapes=[pltpu.VMEM((tm + CONV_HALO, d), F32),
                        pltpu.VMEM((SUBLANES - 1, tm + CONV_HALO - SUBLANES, d), F32),
                        pltpu.VMEM((tm, d), F32), pltpu.VMEM((tm, d), F32)],
        compiler_params=_params("arbitrary"),
        name="glu_dwconv_ln_silu",
    )(h, h, wv, wg, bv, bg, w_pad, b, g, beta)


def _route_tile(logits, before_tile, tm):
    row_f = lax.broadcasted_iota(jnp.int32, (N_EXPERTS, tm), 0).astype(F32)
    neg = -jnp.inf
    picks, vals, ids = [], [], []
    for _ in range(TOP_K):
        mx = jnp.max(logits, axis=0, keepdims=True)
        idx = jnp.min(jnp.where(logits == mx, row_f, float(N_EXPERTS)), axis=0, keepdims=True)
        pick = row_f == idx
        picks.append(pick)
        vals.append(mx)
        ids.append(idx)
        logits = jnp.where(pick, neg, logits)
    exps = [jnp.exp(v - vals[0]) for v in vals]
    den = exps[0]
    for e in exps[1:]:
        den = den + e

    chosen = jnp.zeros((N_EXPERTS, tm), F32)
    for pick in picks:
        chosen = chosen + pick.astype(F32)
    r_io = lax.broadcasted_iota(jnp.int32, (tm, tm), 0)
    c_io = lax.broadcasted_iota(jnp.int32, (tm, tm), 1)
    later = (r_io < c_io).astype(BF16)
    before = jnp.dot(chosen.astype(BF16), later, preferred_element_type=F32) + before_tile
    ranks = [jnp.sum(jnp.where(pick, before, 0.0), axis=0, keepdims=True) for pick in picks]
    idx_out = jnp.concatenate(ids + ranks, axis=0)
    gate_out = jnp.concatenate([e / den for e in exps], axis=0)
    total = before_tile + jnp.sum(chosen, axis=1, keepdims=True)
    return idx_out, gate_out, total


def _sc_worker_base(per_worker):
    return (lax.axis_index("s") * SC_CORES + lax.axis_index("c")) * per_worker


def sc_scatter_rows(src, dest_flat, n_rows):
    n, d = src.shape
    per_w = n // SC_WORKERS
    n_chunks = per_w // SC_CHUNK
    mesh = plsc.VectorSubcoreMesh(core_axis_name="c", subcore_axis_name="s")

    @functools.partial(
        pl.kernel, out_type=jax.ShapeDtypeStruct((n_rows, d), src.dtype), mesh=mesh,
        scratch_types=[pltpu.VMEM((per_w,), jnp.int32) for _ in range(TOP_K)]
        + [pltpu.VMEM((SC_CHUNK, d), src.dtype) for _ in range(2)] + [pltpu.SemaphoreType.DMA] * 4,
        name="sc_scatter_rows")
    def scatter(src_hbm, dest_hbm, out_hbm, *scratch):
        idx_refs = scratch[:TOP_K]
        bufs = scratch[TOP_K:TOP_K + 2]
        in_sems, out_sems = scratch[TOP_K + 2:TOP_K + 4], scratch[TOP_K + 4:TOP_K + 6]
        base = _sc_worker_base(per_w)
        for k, idx_ref in enumerate(idx_refs):
            pltpu.sync_copy(dest_hbm.at[pl.ds(k * n + base, per_w)], idx_ref)

        def load(j, slot):
            return pltpu.make_async_copy(src_hbm.at[pl.ds(base + j * SC_CHUNK, SC_CHUNK)], bufs[slot], in_sems[slot])

        def store_all(j, slot):
            copies = [pltpu.make_async_copy(bufs[slot], out_hbm.at[idx_ref.at[pl.ds(j * SC_CHUNK, SC_CHUNK)]],
                                            out_sems[slot]) for idx_ref in idx_refs]
            for cp in copies:
                cp.start()
            for cp in copies:
                cp.wait()

        load(0, 0).start()

        @pl.loop(0, n_chunks // 2)
        def _(p):
            j = 2 * p
            load(j + 1, 1).start()
            load(j, 0).wait()
            store_all(j, 0)

            @pl.when(j + 2 < n_chunks)
            def _():
                load(j + 2, 0).start()

            load(j + 1, 1).wait()
            store_all(j + 1, 1)

    return scatter(src, dest_flat)


def sc_gather_rows(table, idx):
    b = idx.shape[0]
    d = table.shape[1]
    per_w = b // SC_WORKERS
    n_chunks = per_w // SC_CHUNK
    mesh = plsc.VectorSubcoreMesh(core_axis_name="c", subcore_axis_name="s")

    @functools.partial(
        pl.kernel, out_type=jax.ShapeDtypeStruct((b, d), table.dtype), mesh=mesh,
        scratch_types=[pltpu.VMEM((per_w,), jnp.int32)] + [pltpu.VMEM((SC_CHUNK, d), table.dtype) for _ in range(2)]
        + [pltpu.SemaphoreType.DMA] * 4,
        name="sc_gather_rows")
    def gather(table_hbm, idx_hbm, out_hbm, idx_ref, buf0, buf1, gsem0, gsem1, osem0, osem1):
        bufs, in_sems, out_sems = (buf0, buf1), (gsem0, gsem1), (osem0, osem1)
        base = _sc_worker_base(per_w)
        pltpu.sync_copy(idx_hbm.at[pl.ds(base, per_w)], idx_ref)

        def fetch(j, slot):
            return pltpu.make_async_copy(table_hbm.at[idx_ref.at[pl.ds(j * SC_CHUNK, SC_CHUNK)]], bufs[slot],
                                         in_sems[slot])

        def store(j, slot):
            return pltpu.make_async_copy(bufs[slot], out_hbm.at[pl.ds(base + j * SC_CHUNK, SC_CHUNK)],
                                         out_sems[slot])

        fetch(0, 0).start()

        @pl.loop(0, n_chunks // 2)
        def _(p):
            j = 2 * p

            @pl.when(p > 0)
            def _():
                store(j - 1, 1).wait()

            fetch(j + 1, 1).start()
            fetch(j, 0).wait()
            store(j, 0).start()
            fetch(j + 1, 1).wait()
            store(j, 0).wait()

            @pl.when(j + 2 < n_chunks)
            def _():
                fetch(j + 2, 0).start()

            store(j + 1, 1).start()

        store(n_chunks - 1, 1).wait()

    return gather(table, idx)


def _ffn_kernel(be_ref, nv_ref, nu_ref, xs_ref, w1_ref, b1g_ref, b1u_ref, w2_ref, b2_ref, y_ref,
                wg_ref, wu_ref, w2s_ref):
    i = pl.program_id(0)
    pair = 2 * LANES

    @pl.when(i < nu_ref[0])
    def _():
        e = be_ref[i]
        prev = be_ref[jnp.maximum(i - 1, 0)]

        @pl.when((i == 0) | (e != prev))
        def _():
            r_io = lax.broadcasted_iota(jnp.int32, (pair, pair), 0)
            c_io = lax.broadcasted_iota(jnp.int32, (pair, pair), 1)
            want = jnp.where(c_io < LANES, 2 * c_io, 2 * (c_io - LANES) + 1)
            sel = (r_io == want).astype(BF16)
            for c in range(wg_ref.shape[1] // LANES):
                slab = w1_ref[0, 0, :, c * pair:(c + 1) * pair].astype(BF16)
                split = jnp.dot(slab, sel, preferred_element_type=F32)
                wg_ref[:, c * LANES:(c + 1) * LANES] = split[:, :LANES].astype(BF16)
                wu_ref[:, c * LANES:(c + 1) * LANES] = split[:, LANES:].astype(BF16)
            w2s_ref[...] = w2_ref[0, 0].astype(BF16)

        nvalid = nv_ref[i]

        def ffn_rows(r0, rows):
            row = r0 + lax.broadcasted_iota(jnp.int32, (rows, xs_ref.shape[1]), 0)
            lo, hi = _unpack_halves(jnp.where(row < nvalid, xs_ref[r0:r0 + rows, :], 0))
            x = jnp.concatenate([lo, hi], axis=1).astype(BF16)
            hg = jnp.dot(x, wg_ref[...], preferred_element_type=F32) + b1g_ref[0]
            hu = jnp.dot(x, wu_ref[...], preferred_element_type=F32) + b1u_ref[0]
            gate = jnp.minimum(hg, SWIGLU_LIMIT)
            up = jnp.clip(hu, -SWIGLU_LIMIT, SWIGLU_LIMIT)
            act = gate * jax.nn.sigmoid(SWIGLU_ALPHA * gate) * (up + 1.0)
            y = jnp.dot(act.astype(BF16), w2s_ref[...], preferred_element_type=F32) + b2_ref[0]
            y_ref[r0:r0 + rows, :] = _pack_halves(y)

        def zero_rows(r0, rows):
            y_ref[r0:r0 + rows, :] = jnp.zeros((rows, y_ref.shape[1]), y_ref.dtype)

        for r0 in range(0, y_ref.shape[0], MOE_CHUNK):
            left = nvalid - r0

            @pl.when(left > MOE_BLOCK)
            def _(r0=r0):
                ffn_rows(r0, MOE_CHUNK)

            @pl.when((left > 0) & (left <= MOE_BLOCK))
            def _(r0=r0):
                ffn_rows(r0, MOE_BLOCK)
                zero_rows(r0 + MOE_BLOCK, MOE_CHUNK - MOE_BLOCK)

            @pl.when(left <= 0)
            def _(r0=r0):
                zero_rows(r0, MOE_CHUNK)

    @pl.when(i >= nu_ref[0])
    def _():
        y_ref[...] = jnp.zeros(y_ref.shape, y_ref.dtype)


def moe_ffn(xs, block_e, block_valid, n_used, layer, w1, b1g, b1u, w2, b2):
    n_rows, half = xs.shape
    d = 2 * half
    n_blocks = n_rows // MOE_STEP
    n_exp, f2 = w1.shape[1], w1.shape[3]
    f = f2 // 2
    rows = lambda i, be, nv, nu: (jnp.minimum(i, nu[0] - 1), 0)
    vec = lambda width: pl.BlockSpec((1, 1, width), lambda i, be, nv, nu: (be[i], 0, 0))
    grid_spec = pltpu.PrefetchScalarGridSpec(
        num_scalar_prefetch=3,
        grid=(n_blocks,),
        in_specs=[
            pl.BlockSpec((MOE_STEP, half), rows),
            pl.BlockSpec((1, 1, d, f2), lambda i, be, nv, nu: (layer, be[i], 0, 0)),
            vec(f), vec(f),
            pl.BlockSpec((1, 1, f, d), lambda i, be, nv, nu: (layer, be[i], 0, 0)),
            vec(d),
        ],
        out_specs=pl.BlockSpec((MOE_STEP, half), lambda i, be, nv, nu: (i, 0)),
        scratch_shapes=[pltpu.VMEM((d, f), BF16), pltpu.VMEM((d, f), BF16), pltpu.VMEM((f, d), BF16)],
    )
    return pl.pallas_call(
        _ffn_kernel,
        grid_spec=grid_spec,
        out_shape=jax.ShapeDtypeStruct((n_rows, half), jnp.int32),
        compiler_params=_params("arbitrary"),
        name="moe_ffn",
    )(block_e, block_valid, n_used, xs, w1, b1g.reshape(n_exp, 1, f), b1u.reshape(n_exp, 1, f), w2,
      b2.reshape(n_exp, 1, d))


def _combine_kernel(x_ref, y_ref, gates_ref, gate_ref, *rest, with_norm):
    g = gates_ref[...]
    half = y_ref.shape[2]
    acc_lo = acc_hi = None
    for k in range(TOP_K):
        lo, hi = _unpack_halves(y_ref[k])
        gk = g[:, k:k + 1]
        acc_lo = gk * lo if acc_lo is None else acc_lo + gk * lo
        acc_hi = gk * hi if acc_hi is None else acc_hi + gk * hi
    x_lo = x_ref[:, :half] + gate_ref[0, :, :half] * acc_lo
    x_hi = x_ref[:, half:] + gate_ref[0, :, half:] * acc_hi
    if with_norm:
        ng_ref, sc_ref, sh_ref = rest[:3]
        o_ref, h_ref = rest[-2:]
        x_new = jnp.concatenate([x_lo, x_hi], axis=1)
        o_ref[...] = x_new
        h_ref[...] = _modulated_norm(x_new, ng_ref[...], sc_ref[0], sh_ref[0]).astype(h_ref.dtype)
    else:
        o_ref = rest[-1]
        o_ref[:, :half] = x_lo
        o_ref[:, half:] = x_hi


def moe_combine(x, y_part, gates, gate_mod, seq, next_norm=None, part=0, n_parts=1, prev=(), tm=512):
    n, d = x.shape
    per = seq // tm
    steps = n // (tm * n_parts)
    first = part * steps
    rows = lambda width: pl.BlockSpec((tm, width), lambda i: (i + first, 0))
    vec = pl.BlockSpec((1, 1, d), lambda i: ((i + first) // per, 0, 0))
    in_specs = [rows(d), pl.BlockSpec((TOP_K, tm, d // 2), lambda i: (0, i, 0)), rows(LANES), vec]
    args = [x, y_part, gates, gate_mod]
    out_specs, out_shape = [rows(d)], [jax.ShapeDtypeStruct((n, d), F32)]
    if next_norm is not None:
        in_specs += [pl.BlockSpec((1, d), lambda i: (0, 0)), vec, vec]
        args += list(next_norm)
        out_specs, out_shape = out_specs + [rows(d)], out_shape + [jax.ShapeDtypeStruct((n, d), BF16)]
    aliases = {len(args) + j: j for j in range(len(prev))}
    in_specs += [pl.BlockSpec(memory_space=pl.ANY) for _ in prev]
    args += list(prev)
    return pl.pallas_call(
        functools.partial(_combine_kernel, with_norm=next_norm is not None),
        grid=(steps,),
        in_specs=in_specs,
        out_specs=out_specs,
        out_shape=out_shape,
        input_output_aliases=aliases,
        compiler_params=_params("parallel"),
        name="moe_combine",
    )(*args)


def moe_layer(x, mixer, mix_gate, mods, norm_g, router_w, router_b, layer, w1, b1, w2, b2, seq, next_norm):
    n, d = x.shape
    shift, scale, gate = mods
    rw_rows = jnp.pad(router_w.T, ((0, LANES - N_EXPERTS), (0, 0)))
    rw_hi = rw_rows.astype(BF16)
    rw_lo = (rw_rows - rw_hi.astype(F32)).astype(BF16)
    x, h, idx, gates, counts = proj_residual_router(
        *mixer, x, mix_gate, norm_g.reshape(1, d), scale, shift, jnp.concatenate([rw_hi, rw_lo], axis=0), rw_hi,
        router_b.reshape(N_EXPERTS, 1), seq)

    top_i = idx[:TOP_K]
    rank = idx[TOP_K:]
    counts = counts[:, 0]
    experts = jnp.arange(N_EXPERTS, dtype=jnp.int32)
    padded = (counts + MOE_STEP - 1) // MOE_STEP * MOE_STEP
    pad_ends = jnp.sum(jnp.where(experts[:, None] >= experts[None, :], padded[None, :], 0), axis=1)
    pad_starts = pad_ends - padded
    dest = jnp.sum(jnp.where(top_i[..., None] == experts, pad_starts, 0), axis=-1) + rank
    n_blocks = -(-n * TOP_K // MOE_STEP) + N_EXPERTS
    block_start = jnp.arange(n_blocks, dtype=jnp.int32) * MOE_STEP
    block_e = jnp.minimum(jnp.sum((pad_ends[None, :] <= block_start[:, None]).astype(jnp.int32), axis=1),
                          N_EXPERTS - 1)
    n_used = (pad_ends[N_EXPERTS - 1:] // MOE_STEP).astype(jnp.int32)
    seg_end = jnp.sum(jnp.where(block_e[:, None] == experts, pad_starts + counts, 0), axis=1)
    block_valid = jnp.clip(seg_end - block_start, 0, MOE_STEP).astype(jnp.int32)
    dest_flat = dest.reshape(-1)

    xs = sc_scatter_rows(h, dest_flat, n_blocks * MOE_STEP)
    ys = moe_ffn(xs, block_e, block_valid, n_used, layer, w1, b1[layer][:, 0::2], b1[layer][:, 1::2], w2, b2[layer])
    outs = ()
    per_part = n // COMBINE_PARTS
    for part in range(COMBINE_PARTS):
        idx_part = dest[:, part * per_part:(part + 1) * per_part].reshape(-1)
        y = sc_gather_rows(ys, idx_part).reshape(TOP_K, per_part, d // 2)
        outs = moe_combine(x, y, gates, gate, seq, next_norm, part, COMBINE_PARTS, tuple(outs))
    return (outs[0], outs[1]) if next_norm is not None else (outs[0], None)


def _pad_heads(w, width):
    k = w.shape[0]
    w = w.reshape(k, MLA_HEADS, width)
    return jnp.pad(w, ((0, 0), (0, 0), (0, LANES - width))).reshape(k, MLA_HEADS * LANES)


def _to_head_lanes(w):
    half = QK_ROPE_DIM // 2
    first_nope = LANES // 2 - half
    x1, x2 = w[..., QK_NOPE_DIM:QK_NOPE_DIM + half], w[..., QK_NOPE_DIM + half:QK_HEAD_DIM]
    pad = jnp.zeros(w.shape[:-1] + (LANES - QK_HEAD_DIM,), w.dtype)
    return jnp.concatenate([x1, w[..., :first_nope], x2, w[..., first_nope:QK_NOPE_DIM], pad], axis=-1)


def _rope_tables(positions):
    inv = 1.0 / (ROPE_THETA ** (jnp.arange(0, QK_ROPE_DIM, 2, dtype=F32) / QK_ROPE_DIM))
    ang = inv[:, None] * positions.reshape(-1).astype(F32)[None, :]
    cos, sin = jnp.cos(ang).T, jnp.sin(ang).T
    n = cos.shape[0]
    half = QK_ROPE_DIM // 2
    first_nope = LANES // 2 - half
    ones = lambda w: jnp.ones((n, w), F32)
    zeros = lambda w: jnp.zeros((n, w), F32)
    pad = LANES - QK_HEAD_DIM
    cos_t = jnp.concatenate([cos, ones(first_nope), cos, ones(QK_NOPE_DIM - first_nope), zeros(pad)], axis=1)
    sin_t = jnp.concatenate([-sin, zeros(first_nope), sin, zeros(QK_NOPE_DIM - first_nope + pad)], axis=1)
    return cos_t, sin_t


def _split_mods(m, batch):
    d = m.shape[-1] // 3
    m = m[:batch]
    return tuple(m[:, None, j * d:(j + 1) * d] for j in range(3))


def pool_mla_mixer(x, h, mods, norm_g, tables, w_in, pool_w, pool_scale, cq_norm_g, w_uq, ckv_norm_g, w_ukv,
                   q_norm_g, k_norm_g, w_out, batch, seq):
    n, d = x.shape
    shift, scale, _ = mods
    rope_cols = _to_head_lanes(jnp.pad(w_in[:, Z_ROPE:], ((0, 0), (QK_NOPE_DIM, 0))))
    w_in_pad = jnp.concatenate([w_in[:, :Z_ROPE], rope_cols], axis=1).astype(BF16)
    if h is None:
        z = norm_matmul(x, norm_g.reshape(1, d), scale, shift, w_in_pad, seq)
    else:
        z = matmul(h, w_in_pad)
    w_ukv_h = w_ukv.reshape(KV_LORA_RANK, MLA_HEADS, QK_NOPE_DIM + V_HEAD_DIM)
    k_nope = jnp.pad(w_ukv_h[:, :, :QK_NOPE_DIM], ((0, 0), (0, 0), (0, QK_ROPE_DIM)))
    wuk_pad = _to_head_lanes(k_nope).reshape(KV_LORA_RANK, MLA_HEADS * LANES).astype(BF16)
    wuv_pad = _pad_heads(w_ukv_h[:, :, QK_NOPE_DIM:].reshape(KV_LORA_RANK, -1), V_HEAD_DIM).astype(BF16)
    wuq_pad = _to_head_lanes(w_uq.reshape(Q_LORA_RANK, MLA_HEADS, QK_HEAD_DIM))
    wuq_pad = wuq_pad.reshape(Q_LORA_RANK, MLA_HEADS * LANES).astype(BF16)
    pad_g = lambda g: _to_head_lanes(g).reshape(1, LANES)
    y_pool, q, k, v = mla_prep(z, pool_w.astype(BF16), pool_scale.reshape(1, -1), cq_norm_g.reshape(1, -1), wuq_pad,
                               ckv_norm_g.reshape(1, -1), wuk_pad, wuv_pad, pad_g(q_norm_g), pad_g(k_norm_g),
                               *tables, batch, seq)
    y_att = flash_attention(q, k, v).reshape(n, MLA_HEADS * V_HEAD_DIM)
    w_out_b = w_out.astype(BF16)
    return [y_pool, y_att], [w_out_b[:POOL_WIDTH], w_out_b[POOL_WIDTH:]], jnp.zeros((1, d), F32)


def conformer_mixer(h, pw1_w, pw1_b, dw_w, dw_b, ln_g, ln_b, pw2_w, pw2_b, seq):
    d = pw2_w.shape[1]
    cd = pw1_w.shape[1] // 2
    pw1 = pw1_w.astype(BF16)
    w_pad = jnp.pad(dw_w, ((0, CONV_HALO - CONV_WIDTH), (0, 0)))
    u = glu_dwconv_ln_silu(h, pw1[:, :cd], pw1[:, cd:], pw1_b[:cd].reshape(1, cd), pw1_b[cd:].reshape(1, cd),
                           w_pad, dw_b.reshape(1, cd), ln_g.reshape(1, cd), ln_b.reshape(1, cd), seq)
    return [u], [pw2_w.astype(BF16)], pw2_b.reshape(1, d)


def kernel(x, c, positions, ada_mix_w, ada_mix_b, norm_mix_g, w_in, pool_w, pool_scale, cq_norm_g, w_uq,
           ckv_norm_g, w_ukv, q_norm_g, k_norm_g, w_out, conv_pw1_w, conv_pw1_b, conv_dw_w, conv_dw_b,
           conv_ln_g, conv_ln_b, conv_pw2_w, conv_pw2_b, ada_ffn_w, ada_ffn_b, norm_ffn_g, router_w,
           router_b, moe_w1, moe_b1, moe_w2, moe_b2):
    batch, seq, d = x.shape
    depth = ada_mix_w.shape[0]
    c_pad = jnp.pad(c, ((0, 8 - batch), (0, 0)))
    mix_mods = adaln(c_pad, ada_mix_w, ada_mix_b)
    ffn_mods = adaln(c_pad, ada_ffn_w, ada_ffn_b)
    tables = _rope_tables(positions)
    xf = x.reshape(batch * seq, d)
    h = None
    for layer in range(depth):
        i = layer // 2
        mods = _split_mods(mix_mods[layer], batch)
        if layer % 2 == 0:
            mixer = pool_mla_mixer(xf, h, mods, norm_mix_g[layer], tables, w_in[i], pool_w[i], pool_scale[i],
                                   cq_norm_g[i], w_uq[i], ckv_norm_g[i], w_ukv[i], q_norm_g[i], k_norm_g[i],
                                   w_out[i], batch, seq)
        else:
            mixer = conformer_mixer(h, conv_pw1_w[i], conv_pw1_b[i], conv_dw_w[i], conv_dw_b[i],
                                    conv_ln_g[i], conv_ln_b[i], conv_pw2_w[i], conv_pw2_b[i], seq)
        next_norm = None
        if layer + 1 < depth:
            n_shift, n_scale, _ = _split_mods(mix_mods[layer + 1], batch)
            next_norm = (norm_mix_g[layer + 1].reshape(1, d), n_scale, n_shift)
        xf, h = moe_layer(xf, mixer, mods[2], _split_mods(ffn_mods[layer], batch), norm_ffn_g[layer],
                          router_w[layer], router_b[layer], layer, moe_w1, moe_b1, moe_w2, moe_b2, seq, next_norm)
    return xf.reshape(batch, seq, d)
```

```python
import functools

import jax
import jax.numpy as jnp
from jax import lax
from jax.experimental import pallas as pl
from jax.experimental.pallas import tpu as pltpu
from jax.experimental.pallas import tpu_sc as plsc

F32 = jnp.float32
BF16 = jnp.bfloat16
HIGHEST = lax.Precision.HIGHEST

EPS = 1e-6
POOL_WINDOWS = (2, 4, 8, 16)
POOL_GROUP_DIM = 128
POOL_WIDTH = POOL_GROUP_DIM * len(POOL_WINDOWS)
MLA_HEADS = 8
QK_NOPE_DIM = 64
QK_ROPE_DIM = 32
QK_HEAD_DIM = QK_NOPE_DIM + QK_ROPE_DIM
V_HEAD_DIM = 64
Q_LORA_RANK = 384
KV_LORA_RANK = 256
ROPE_THETA = 10000.0
CHUNK = 64
CONV_WIDTH = 31
N_EXPERTS = 32
TOP_K = 4
SWIGLU_ALPHA = 1.702
SWIGLU_LIMIT = 7.0
MOE_BLOCK = 256
MOE_CHUNK = 512
MOE_STEP = 1024

LANES = 128
SUBLANES = 8
CONV_PATCH_ROWS = 64
CONV_PATCH_COLS = 256
POOL_HALO = 16
CONV_HALO = 32
MASK_VALUE = -1e30
LOG2_E = 1.4426950408889634
VMEM_LIMIT = 52 * 1024 * 1024
SC_CORES = 2
SC_SUBCORES = 16
SC_WORKERS = SC_CORES * SC_SUBCORES
SC_CHUNK = 64
COMBINE_PARTS = 2

Z_CQ = POOL_WIDTH
Z_CKV = Z_CQ + Q_LORA_RANK
Z_ROPE = Z_CKV + KV_LORA_RANK
Z_WIDTH = Z_ROPE + LANES


def _params(*sem, vmem=None):
    return pltpu.CompilerParams(dimension_semantics=sem, vmem_limit_bytes=vmem or VMEM_LIMIT,
                                disable_bounds_checks=True)


def _adaln_kernel(c_ref, w_ref, b_ref, o_ref):
    c = c_ref[...]
    s = c * jax.nn.sigmoid(c)
    o_ref[0] = jnp.dot(s, w_ref[0], preferred_element_type=F32, precision=HIGHEST) + b_ref[0]


def adaln(c_pad, w, b):
    n_l, d, d3 = w.shape
    tn = 512
    return pl.pallas_call(
        _adaln_kernel,
        grid=(n_l, d3 // tn),
        in_specs=[pl.BlockSpec((8, d), lambda l, j: (0, 0)),
                  pl.BlockSpec((1, d, tn), lambda l, j: (l, 0, j)),
                  pl.BlockSpec((1, 1, tn), lambda l, j: (l, 0, j))],
        out_specs=pl.BlockSpec((1, 8, tn), lambda l, j: (l, 0, j)),
        out_shape=jax.ShapeDtypeStruct((n_l, 8, d3), F32),
        compiler_params=_params("parallel", "parallel"),
        name="adaln",
    )(c_pad, w, b.reshape(n_l, 1, d3))


def _modulated_norm(x, g, scale, shift):
    ms = jnp.mean(x * x, axis=-1, keepdims=True)
    return x * lax.rsqrt(ms + EPS) * g * (1.0 + scale) + shift


def _pack_halves(x):
    w = x.shape[1] // 2
    lo = lax.bitcast_convert_type(x[:, :w].astype(BF16).astype(F32), jnp.uint32)
    hi = lax.bitcast_convert_type(x[:, w:].astype(BF16).astype(F32), jnp.uint32)
    return lax.bitcast_convert_type((lo >> 16) | (hi & jnp.uint32(0xFFFF0000)), jnp.int32)


def _unpack_halves(p):
    u = lax.bitcast_convert_type(p, jnp.uint32)
    lo = lax.bitcast_convert_type(u << 16, F32)
    hi = lax.bitcast_convert_type(u & jnp.uint32(0xFFFF0000), F32)
    return lo, hi


def _norm_matmul_kernel(x_ref, g_ref, sc_ref, sh_ref, w_ref, o_ref):
    h = _modulated_norm(x_ref[...], g_ref[...], sc_ref[0], sh_ref[0])
    o_ref[...] = jnp.dot(h.astype(BF16), w_ref[...], preferred_element_type=F32)


def norm_matmul(x, g, scale, shift, w, seq, tm=512):
    n, d = x.shape
    m = w.shape[1]
    per = seq // tm
    vec = pl.BlockSpec((1, 1, d), lambda i: (i // per, 0, 0))
    return pl.pallas_call(
        _norm_matmul_kernel,
        grid=(n // tm,),
        in_specs=[pl.BlockSpec((tm, d), lambda i: (i, 0)), pl.BlockSpec((1, d), lambda i: (0, 0)), vec, vec,
                  pl.BlockSpec((d, m), lambda i: (0, 0))],
        out_specs=pl.BlockSpec((tm, m), lambda i: (i, 0)),
        out_shape=jax.ShapeDtypeStruct((n, m), F32),
        compiler_params=_params("parallel"),
        name="norm_matmul",
    )(x, g, scale, shift, w)


def _matmul_kernel(a_ref, w_ref, o_ref):
    o_ref[...] = jnp.dot(a_ref[...], w_ref[...], preferred_element_type=F32).astype(o_ref.dtype)


def matmul(a, w, out_dtype=F32, tm=512):
    n, k = a.shape
    m = w.shape[1]
    return pl.pallas_call(
        _matmul_kernel,
        grid=(n // tm,),
        in_specs=[pl.BlockSpec((tm, k), lambda i: (i, 0)), pl.BlockSpec((k, m), lambda i: (0, 0))],
        out_specs=pl.BlockSpec((tm, m), lambda i: (i, 0)),
        out_shape=jax.ShapeDtypeStruct((n, m), out_dtype),
        compiler_params=_params("parallel"),
        name="matmul",
    )(a, w)


def _proj_residual_router_kernel(*refs, n_in, tm):
    a_refs, w_refs = refs[:n_in], refs[n_in:2 * n_in]
    (b_ref, x_ref, gate_ref, g_ref, sc_ref, sh_ref, rw2_ref, rwh_ref, rb_ref,
     o_ref, h_ref, idx_ref, gates_ref, cnt_ref, carry_ref) = refs[2 * n_in:]

    @pl.when(pl.program_id(0) == 0)
    def _():
        carry_ref[...] = jnp.zeros(carry_ref.shape, F32)

    acc = b_ref[...]
    for a_ref, w_ref in zip(a_refs, w_refs):
        acc = acc + jnp.dot(a_ref[...], w_ref[...], preferred_element_type=F32)
    x_new = x_ref[...] + gate_ref[0] * acc
    o_ref[...] = x_new
    h = _modulated_norm(x_new, g_ref[...], sc_ref[0], sh_ref[0])
    h_ref[...] = _pack_halves(h)
    h_hi = h.astype(BF16)
    h_lo = (h - h_hi.astype(F32)).astype(BF16)
    contract_features = (((1,), (1,)), ((), ()))
    both = lax.dot_general(rw2_ref[...], h_hi, contract_features, preferred_element_type=F32)
    low = lax.dot_general(rwh_ref[...], h_lo, contract_features, preferred_element_type=F32)
    logits = (both[:LANES] + both[LANES:] + low)[:N_EXPERTS] + rb_ref[...]
    idx, gates, total = _route_tile(logits, carry_ref[:, 0:1], tm)
    idx_ref[...] = idx.astype(jnp.int32)
    gate_rows = jnp.concatenate([gates, jnp.zeros((LANES - TOP_K, tm), F32)], axis=0)
    gates_ref[...] = jnp.transpose(gate_rows)
    carry_ref[...] = jnp.broadcast_to(total, carry_ref.shape)
    cnt_ref[...] = jnp.broadcast_to(total, cnt_ref.shape).astype(jnp.int32)


def proj_residual_router(a_list, w_list, bias, x, gate, g, scale, shift, rw_both, rw_hi, rb_pad, seq, tm=512):
    n, d = x.shape
    per = seq // tm
    n_in = len(a_list)
    rows = lambda width: pl.BlockSpec((tm, width), lambda i: (i, 0))
    const = lambda a: pl.BlockSpec(a.shape, lambda i: (0, 0))
    vec = pl.BlockSpec((1, 1, d), lambda i: (i // per, 0, 0))
    in_specs = [rows(a.shape[1]) for a in a_list] + [const(w) for w in w_list]
    in_specs += [const(bias), rows(d), vec, const(g), vec, vec, const(rw_both), const(rw_hi), const(rb_pad)]
    return pl.pallas_call(
        functools.partial(_proj_residual_router_kernel, n_in=n_in, tm=tm),
        grid=(n // tm,),
        in_specs=in_specs,
        out_specs=[rows(d), rows(d // 2), pl.BlockSpec((2 * TOP_K, tm), lambda i: (0, i)), rows(LANES),
                   pl.BlockSpec((N_EXPERTS, LANES), lambda i: (0, 0))],
        out_shape=[jax.ShapeDtypeStruct((n, d), F32), jax.ShapeDtypeStruct((n, d // 2), jnp.int32),
                   jax.ShapeDtypeStruct((2 * TOP_K, n), jnp.int32), jax.ShapeDtypeStruct((n, LANES), F32),
                   jax.ShapeDtypeStruct((N_EXPERTS, LANES), jnp.int32)],
        scratch_shapes=[pltpu.VMEM((N_EXPERTS, LANES), F32)],
        compiler_params=_params("arbitrary"),
        name="proj_residual_router",
    )(*a_list, *w_list, bias, x, gate, g, scale, shift, rw_both, rw_hi, rb_pad)


def _rope(xn, cos_t, sin_t):
    return xn * cos_t + pltpu.roll(xn, LANES // 2, axis=1) * sin_t


def _mla_prep_kernel(z_ref, halo_ref, pw_ref, ps_ref, cqg_ref, wuq_ref, ckvg_ref, wuk_ref, wuv_ref,
                     qg_ref, kg_ref, cos_ref, sin_ref,
                     yp_ref, q_ref, k_ref, v_ref, ext_ref, *, tm, per):
    si = pl.program_id(0) % per
    u = z_ref[:, 0:POOL_WIDTH]
    ext_ref[0:POOL_HALO, :] = jnp.where(si == 0, 0.0, halo_ref[...])
    ext_ref[POOL_HALO:, :] = u
    t = si * tm + lax.broadcasted_iota(jnp.int32, (tm, 1), 0)
    for g, w in enumerate(POOL_WINDOWS):
        cols = slice(g * POOL_GROUP_DIM, (g + 1) * POOL_GROUP_DIM)
        ug = u[:, cols]
        s = ug
        for j in range(1, w):
            s = s + ext_ref[POOL_HALO - j:POOL_HALO - j + tm, cols]
        cnt = jnp.minimum(t + 1, w).astype(F32)
        pooled = s / cnt - ug
        yp = jnp.dot(pooled.astype(BF16), pw_ref[g], preferred_element_type=F32) * ps_ref[:, cols]
        yp_ref[:, cols] = yp.astype(yp_ref.dtype)

    half = QK_ROPE_DIM // 2

    def to_lanes(t_ref):
        rows = jnp.concatenate([t_ref[...], jnp.zeros((LANES - half, tm), F32)], axis=0)
        return jnp.transpose(rows)

    cos_lo, sin_lo = to_lanes(cos_ref), to_lanes(sin_ref)
    lane = lax.broadcasted_iota(jnp.int32, (tm, LANES), 1)
    nope = ((lane >= half) & (lane < LANES // 2)) | ((lane >= LANES // 2 + half) & (lane < QK_HEAD_DIM))
    cos_t = cos_lo + pltpu.roll(cos_lo, LANES // 2, axis=1) + jnp.where(nope, 1.0, 0.0)
    sin_t = pltpu.roll(sin_lo, LANES // 2, axis=1) - sin_lo
    inv_head = 1.0 / QK_HEAD_DIM

    cq = z_ref[:, Z_CQ:Z_CKV]
    cqn = cq * lax.rsqrt(jnp.mean(cq * cq, axis=-1, keepdims=True) + EPS) * cqg_ref[...]
    qf = jnp.dot(cqn.astype(BF16), wuq_ref[...], preferred_element_type=F32)
    q_scale = QK_HEAD_DIM ** -0.5 * LOG2_E
    for h in range(MLA_HEADS):
        qh = qf[:, h * LANES:(h + 1) * LANES]
        ss = jnp.sum(qh * qh, axis=-1, keepdims=True) * inv_head
        qn = qh * lax.rsqrt(ss + EPS) * qg_ref[...]
        q_ref[0, h] = (_rope(qn, cos_t, sin_t) * q_scale).astype(q_ref.dtype)

    ckv = z_ref[:, Z_CKV:Z_ROPE]
    ckvn = (ckv * lax.rsqrt(jnp.mean(ckv * ckv, axis=-1, keepdims=True) + EPS) * ckvg_ref[...]).astype(BF16)
    kf = jnp.dot(ckvn, wuk_ref[...], preferred_element_type=F32)
    vf = jnp.dot(ckvn, wuv_ref[...], preferred_element_type=F32)
    k_rope = z_ref[:, Z_ROPE:Z_WIDTH]
    ones_lane = lax.broadcasted_iota(jnp.int32, (tm, LANES), 1) == V_HEAD_DIM
    for h in range(MLA_HEADS):
        kh = kf[:, h * LANES:(h + 1) * LANES] + k_rope
        ss = jnp.sum(kh * kh, axis=-1, keepdims=True) * inv_head
        kn = kh * lax.rsqrt(ss + EPS) * kg_ref[...]
        k_ref[0, h] = _rope(kn, cos_t, sin_t).astype(k_ref.dtype)
        v_ref[0, h] = jnp.where(ones_lane, 1.0, vf[:, h * LANES:(h + 1) * LANES]).astype(v_ref.dtype)


def mla_prep(z, pool_w, pool_scale, cq_g, wuq_pad, ckv_g, wuk_pad, wuv_pad, qg_pad, kg_pad,
             cos_t, sin_t, batch, seq, tm=512):
    n = z.shape[0]
    per = seq // tm
    hb = tm // POOL_HALO
    full = lambda a: pl.BlockSpec(a.shape, lambda i: (0,) * a.ndim)
    tab = pl.BlockSpec((QK_ROPE_DIM // 2, tm), lambda i: (0, i))
    head_out = pl.BlockSpec((1, MLA_HEADS, tm, LANES), lambda i: (i // per, 0, i % per, 0))
    head_shape = jax.ShapeDtypeStruct((batch, MLA_HEADS, seq, LANES), BF16)
    return pl.pallas_call(
        functools.partial(_mla_prep_kernel, tm=tm, per=per),
        grid=(n // tm,),
        in_specs=[pl.BlockSpec((tm, Z_WIDTH), lambda i: (i, 0)),
                  pl.BlockSpec((POOL_HALO, POOL_WIDTH), lambda i: (jnp.maximum(i * hb - 1, 0), 0)),
                  full(pool_w), full(pool_scale), full(cq_g), full(wuq_pad), full(ckv_g), full(wuk_pad),
                  full(wuv_pad), full(qg_pad), full(kg_pad), tab, tab],
        out_specs=[pl.BlockSpec((tm, POOL_WIDTH), lambda i: (i, 0)), head_out, head_out, head_out],
        out_shape=[jax.ShapeDtypeStruct((n, POOL_WIDTH), BF16), head_shape, head_shape, head_shape],
        scratch_shapes=[pltpu.VMEM((tm + POOL_HALO, POOL_WIDTH), F32)],
        compiler_params=_params("parallel"),
        name="mla_prep",
    )(z, z, pool_w, pool_scale, cq_g, wuq_pad, ckv_g, wuk_pad, wuv_pad, qg_pad, kg_pad, cos_t, sin_t)


def _flash_kernel(q_ref, k_ref, v_ref, o_ref, m_ref, acc_ref, *, tq, tk, td, heads):
    qi = pl.program_id(2)
    q_chunk = lax.broadcasted_iota(jnp.int32, (tq, td), 0) // CHUNK
    k_chunk = lax.broadcasted_iota(jnp.int32, (tq, td), 1) // CHUNK
    m_ref[...] = jnp.full(m_ref.shape, MASK_VALUE, F32)
    acc_ref[...] = jnp.zeros(acc_ref.shape, F32)

    def step(start, width, mask, row0=0):
        for hh in range(heads):
            k = k_ref[0, hh, pl.ds(start, width), :]
            v = v_ref[0, hh, pl.ds(start, width), :]
            s = lax.dot_general(q_ref[0, hh, row0:, :], k, (((1,), (1,)), ((), ())), preferred_element_type=F32)
            if mask is not None:
                s = jnp.where(mask[row0:], s, MASK_VALUE)
            cols_s = [s[:, c * LANES:(c + 1) * LANES] for c in range(width // LANES)]
            s_max = cols_s[0]
            for sc in cols_s[1:]:
                s_max = jnp.maximum(s_max, sc)
            m_prev = m_ref[hh, row0:, :]
            m_new = jnp.maximum(m_prev, jnp.max(s_max, axis=-1, keepdims=True))
            alpha = jnp.exp2(m_prev - m_new)
            p = jnp.concatenate([jnp.exp2(sc - m_new) for sc in cols_s], axis=1).astype(v.dtype)
            acc_ref[hh, row0:, :] = alpha * acc_ref[hh, row0:, :] + jnp.dot(p, v, preferred_element_type=F32)
            m_ref[hh, row0:, :] = m_new

    def full_step(j, carry):
        step(pl.multiple_of(j * tk, tk), tk, None)
        return carry

    lax.fori_loop(0, (tq // tk) * qi, full_step, 0)
    for dd in range(tq // td):
        step(pl.multiple_of(qi * tq + dd * td, td), td, k_chunk + dd * (td // CHUNK) <= q_chunk, row0=dd * td)
    outs = []
    for hh in range(heads):
        acc = acc_ref[hh]
        outs.append((acc / acc[:, V_HEAD_DIM:V_HEAD_DIM + 1])[:, :V_HEAD_DIM])
    o_ref[0] = jnp.concatenate(outs, axis=-1).astype(o_ref.dtype)


def flash_attention(q, k, v, tq=1024, tk=1024, td=512):
    b, h, s, _ = q.shape
    heads = LANES // V_HEAD_DIM
    return pl.pallas_call(
        functools.partial(_flash_kernel, tq=tq, tk=tk, td=td, heads=heads),
        grid=(b, h // heads, s // tq),
        in_specs=[pl.BlockSpec((1, heads, tq, LANES), lambda bi, hi, qi: (bi, hi, qi, 0)),
                  pl.BlockSpec((1, heads, s, LANES), lambda bi, hi, qi: (bi, hi, 0, 0)),
                  pl.BlockSpec((1, heads, s, LANES), lambda bi, hi, qi: (bi, hi, 0, 0))],
        out_specs=pl.BlockSpec((1, tq, LANES), lambda bi, hi, qi: (bi, qi, hi)),
        out_shape=jax.ShapeDtypeStruct((b, s, h * V_HEAD_DIM), BF16),
        scratch_shapes=[pltpu.VMEM((heads, tq, LANES), F32) for _ in range(2)],
        compiler_params=_params("parallel", "parallel", "parallel"),
        name="flash_attention",
    )(q, k, v)


def _glu_conv_kernel(h0_ref, hn_ref, wv_ref, wg_ref, bv_ref, bg_ref, w_ref, b_ref, g_ref, beta_ref, o_ref,
                     ext_ref, sh_ref, conv_ref, nxt_ref, *, tm, per):
    i = pl.program_id(0)
    d = o_ref.shape[1]

    def glu(h):
        val = jnp.dot(h, wv_ref[...], preferred_element_type=F32) + bv_ref[...]
        gt = jnp.dot(h, wg_ref[...], preferred_element_type=F32) + bg_ref[...]
        return val * jax.nn.sigmoid(gt)

    @pl.when(i == 0)
    def _():
        ext_ref[CONV_HALO:, :] = glu(h0_ref[...])

    @pl.when(i > 0)
    def _():
        ext_ref[0:CONV_HALO, :] = ext_ref[tm:tm + CONV_HALO, :]
        ext_ref[CONV_HALO:, :] = nxt_ref[...]

    @pl.when(i % per == 0)
    def _():
        ext_ref[0:CONV_HALO, :] = jnp.zeros((CONV_HALO, d), F32)

    nxt_ref[...] = glu(hn_ref[...])
    span = sh_ref.shape[1]
    for b in range(1, SUBLANES):
        sh_ref[b - 1] = ext_ref[b:b + span, :]
    first = CONV_HALO - (CONV_WIDTH - 1)
    for r0 in range(0, tm, CONV_PATCH_ROWS):
        for c0 in range(0, d, CONV_PATCH_COLS):
            cols = slice(c0, c0 + CONV_PATCH_COLS)
            view = (CONV_PATCH_ROWS // SUBLANES, SUBLANES, CONV_PATCH_COLS)
            patch = jnp.broadcast_to(b_ref[:, cols], view)
            for j in range(CONV_WIDTH):
                a, b = divmod(first + j, SUBLANES)
                rows = slice(SUBLANES * a + r0, SUBLANES * a + r0 + CONV_PATCH_ROWS)
                win = ext_ref[rows, cols] if b == 0 else sh_ref[b - 1, rows, cols]
                tap = jnp.broadcast_to(w_ref[j:j + 1, cols], view[1:])
                patch = patch + tap[None] * win.reshape(view)
            conv_ref[r0:r0 + CONV_PATCH_ROWS, cols] = patch.reshape(CONV_PATCH_ROWS, CONV_PATCH_COLS)
    acc = conv_ref[...]
    mu = jnp.mean(acc, axis=-1, keepdims=True)
    cen = acc - mu
    var = jnp.mean(cen * cen, axis=-1, keepdims=True)
    y = cen * lax.rsqrt(var + EPS) * g_ref[...] + beta_ref[...]
    o_ref[...] = (y * jax.nn.sigmoid(y)).astype(o_ref.dtype)


def glu_dwconv_ln_silu(h, wv, wg, bv, bg, w_pad, b, g, beta, seq, tm=256):
    n, k = h.shape
    d = wv.shape[1]
    per = seq // tm
    last = n // tm - 1
    full = lambda a: pl.BlockSpec(a.shape, lambda i: (0, 0))
    return pl.pallas_call(
        functools.partial(_glu_conv_kernel, tm=tm, per=per),
        grid=(n // tm,),
        in_specs=[pl.BlockSpec((tm, k), lambda i: (0, 0)),
                  pl.BlockSpec((tm, k), lambda i: (jnp.minimum(i + 1, last), 0)),
                  full(wv), full(wg), full(bv), full(bg), full(w_pad), full(b), full(g), full(beta)],
        out_specs=pl.BlockSpec((tm, d), lambda i: (i, 0)),
        out_shape=jax.ShapeDtypeStruct((n, d), BF16),
        scratch_shapes=[pltpu.VMEM((tm + CONV_HALO, d), F32),
                        pltpu.VMEM((SUBLANES - 1, tm + CONV_HALO - SUBLANES, d), F32),
                        pltpu.VMEM((tm, d), F32), pltpu.VMEM((tm, d), F32)],
        compiler_params=_params("arbitrary"),
        name="glu_dwconv_ln_silu",
    )(h, h, wv, wg, bv, bg, w_pad, b, g, beta)


def _route_tile(logits, before_tile, tm):
    row_f = lax.broadcasted_iota(jnp.int32, (N_EXPERTS, tm), 0).astype(F32)
    neg = -jnp.inf
    picks, vals, ids = [], [], []
    for _ in range(TOP_K):
        mx = jnp.max(logits, axis=0, keepdims=True)
        idx = jnp.min(jnp.where(logits == mx, row_f, float(N_EXPERTS)), axis=0, keepdims=True)
        pick = row_f == idx
        picks.append(pick)
        vals.append(mx)
        ids.append(idx)
        logits = jnp.where(pick, neg, logits)
    exps = [jnp.exp(v - vals[0]) for v in vals]
    den = exps[0]
    for e in exps[1:]:
        den = den + e

    chosen = jnp.zeros((N_EXPERTS, tm), F32)
    for pick in picks:
        chosen = chosen + pick.astype(F32)
    r_io = lax.broadcasted_iota(jnp.int32, (tm, tm), 0)
    c_io = lax.broadcasted_iota(jnp.int32, (tm, tm), 1)
    later = (r_io < c_io).astype(BF16)
    before = jnp.dot(chosen.astype(BF16), later, preferred_element_type=F32) + before_tile
    ranks = [jnp.sum(jnp.where(pick, before, 0.0), axis=0, keepdims=True) for pick in picks]
    idx_out = jnp.concatenate(ids + ranks, axis=0)
    gate_out = jnp.concatenate([e / den for e in exps], axis=0)
    total = before_tile + jnp.sum(chosen, axis=1, keepdims=True)
    return idx_out, gate_out, total


def _sc_worker_base(per_worker):
    return (lax.axis_index("s") * SC_CORES + lax.axis_index("c")) * per_worker


def sc_scatter_rows(src, dest_flat, n_rows):
    n, d = src.shape
    per_w = n // SC_WORKERS
    n_chunks = per_w // SC_CHUNK
    mesh = plsc.VectorSubcoreMesh(core_axis_name="c", subcore_axis_name="s")

    @functools.partial(
        pl.kernel, out_type=jax.ShapeDtypeStruct((n_rows, d), src.dtype), mesh=mesh,
        scratch_types=[pltpu.VMEM((per_w,), jnp.int32) for _ in range(TOP_K)]
        + [pltpu.VMEM((SC_CHUNK, d), src.dtype) for _ in range(2)] + [pltpu.SemaphoreType.DMA] * 4,
        name="sc_scatter_rows")
    def scatter(src_hbm, dest_hbm, out_hbm, *scratch):
        idx_refs = scratch[:TOP_K]
        bufs = scratch[TOP_K:TOP_K + 2]
        in_sems, out_sems = scratch[TOP_K + 2:TOP_K + 4], scratch[TOP_K + 4:TOP_K + 6]
        base = _sc_worker_base(per_w)
        for k, idx_ref in enumerate(idx_refs):
            pltpu.sync_copy(dest_hbm.at[pl.ds(k * n + base, per_w)], idx_ref)

        def load(j, slot):
            return pltpu.make_async_copy(src_hbm.at[pl.ds(base + j * SC_CHUNK, SC_CHUNK)], bufs[slot], in_sems[slot])

        def store_all(j, slot):
            copies = [pltpu.make_async_copy(bufs[slot], out_hbm.at[idx_ref.at[pl.ds(j * SC_CHUNK, SC_CHUNK)]],
                                            out_sems[slot]) for idx_ref in idx_refs]
            for cp in copies:
                cp.start()
            for cp in copies:
                cp.wait()

        load(0, 0).start()

        @pl.loop(0, n_chunks // 2)
        def _(p):
            j = 2 * p
            load(j + 1, 1).start()
            load(j, 0).wait()
            store_all(j, 0)

            @pl.when(j + 2 < n_chunks)
            def _():
                load(j + 2, 0).start()

            load(j + 1, 1).wait()
            store_all(j + 1, 1)

    return scatter(src, dest_flat)


def sc_gather_rows(table, idx):
    b = idx.shape[0]
    d = table.shape[1]
    per_w = b // SC_WORKERS
    n_chunks = per_w // SC_CHUNK
    mesh = plsc.VectorSubcoreMesh(core_axis_name="c", subcore_axis_name="s")

    @functools.partial(
        pl.kernel, out_type=jax.ShapeDtypeStruct((b, d), table.dtype), mesh=mesh,
        scratch_types=[pltpu.VMEM((per_w,), jnp.int32)] + [pltpu.VMEM((SC_CHUNK, d), table.dtype) for _ in range(2)]
        + [pltpu.SemaphoreType.DMA] * 4,
        name="sc_gather_rows")
    def gather(table_hbm, idx_hbm, out_hbm, idx_ref, buf0, buf1, gsem0, gsem1, osem0, osem1):
        bufs, in_sems, out_sems = (buf0, buf1), (gsem0, gsem1), (osem0, osem1)
        base = _sc_worker_base(per_w)
        pltpu.sync_copy(idx_hbm.at[pl.ds(base, per_w)], idx_ref)

        def fetch(j, slot):
            return pltpu.make_async_copy(table_hbm.at[idx_ref.at[pl.ds(j * SC_CHUNK, SC_CHUNK)]], bufs[slot],
                                         in_sems[slot])

        def store(j, slot):
            return pltpu.make_async_copy(bufs[slot], out_hbm.at[pl.ds(base + j * SC_CHUNK, SC_CHUNK)],
                                         out_sems[slot])

        fetch(0, 0).start()

        @pl.loop(0, n_chunks // 2)
        def _(p):
            j = 2 * p

            @pl.when(p > 0)
            def _():
                store(j - 1, 1).wait()

            fetch(j + 1, 1).start()
            fetch(j, 0).wait()
            store(j, 0).start()
            fetch(j + 1, 1).wait()
            store(j, 0).wait()

            @pl.when(j + 2 < n_chunks)
            def _():
                fetch(j + 2, 0).start()

            store(j + 1, 1).start()

        store(n_chunks - 1, 1).wait()

    return gather(table, idx)


def _ffn_kernel(be_ref, nv_ref, nu_ref, xs_ref, w1_ref, b1g_ref, b1u_ref, w2_ref, b2_ref, y_ref,
                wg_ref, wu_ref, w2s_ref):
    i = pl.program_id(0)
    pair = 2 * LANES

    @pl.when(i < nu_ref[0])
    def _():
        e = be_ref[i]
        prev = be_ref[jnp.maximum(i - 1, 0)]

        @pl.when((i == 0) | (e != prev))
        def _():
            r_io = lax.broadcasted_iota(jnp.int32, (pair, pair), 0)
            c_io = lax.broadcasted_iota(jnp.int32, (pair, pair), 1)
            want = jnp.where(c_io < LANES, 2 * c_io, 2 * (c_io - LANES) + 1)
            sel = (r_io == want).astype(BF16)
            for c in range(wg_ref.shape[1] // LANES):
                slab = w1_ref[0, 0, :, c * pair:(c + 1) * pair].astype(BF16)
                split = jnp.dot(slab, sel, preferred_element_type=F32)
                wg_ref[:, c * LANES:(c + 1) * LANES] = split[:, :LANES].astype(BF16)
                wu_ref[:, c * LANES:(c + 1) * LANES] = split[:, LANES:].astype(BF16)
            w2s_ref[...] = w2_ref[0, 0].astype(BF16)

        nvalid = nv_ref[i]

        def ffn_rows(r0, rows):
            row = r0 + lax.broadcasted_iota(jnp.int32, (rows, xs_ref.shape[1]), 0)
            lo, hi = _unpack_halves(jnp.where(row < nvalid, xs_ref[r0:r0 + rows, :], 0))
            x = jnp.concatenate([lo, hi], axis=1).astype(BF16)
            hg = jnp.dot(x, wg_ref[...], preferred_element_type=F32) + b1g_ref[0]
            hu = jnp.dot(x, wu_ref[...], preferred_element_type=F32) + b1u_ref[0]
            gate = jnp.minimum(hg, SWIGLU_LIMIT)
            up = jnp.clip(hu, -SWIGLU_LIMIT, SWIGLU_LIMIT)
            act = gate * jax.nn.sigmoid(SWIGLU_ALPHA * gate) * (up + 1.0)
            y = jnp.dot(act.astype(BF16), w2s_ref[...], preferred_element_type=F32) + b2_ref[0]
            y_ref[r0:r0 + rows, :] = _pack_halves(y)

        def zero_rows(r0, rows):
            y_ref[r0:r0 + rows, :] = jnp.zeros((rows, y_ref.shape[1]), y_ref.dtype)

        for r0 in range(0, y_ref.shape[0], MOE_CHUNK):
            left = nvalid - r0

            @pl.when(left > MOE_BLOCK)
            def _(r0=r0):
                ffn_rows(r0, MOE_CHUNK)

            @pl.when((left > 0) & (left <= MOE_BLOCK))
            def _(r0=r0):
                ffn_rows(r0, MOE_BLOCK)
                zero_rows(r0 + MOE_BLOCK, MOE_CHUNK - MOE_BLOCK)

            @pl.when(left <= 0)
            def _(r0=r0):
                zero_rows(r0, MOE_CHUNK)

    @pl.when(i >= nu_ref[0])
    def _():
        y_ref[...] = jnp.zeros(y_ref.shape, y_ref.dtype)


def moe_ffn(xs, block_e, block_valid, n_used, layer, w1, b1g, b1u, w2, b2):
    n_rows, half = xs.shape
    d = 2 * half
    n_blocks = n_rows // MOE_STEP
    n_exp, f2 = w1.shape[1], w1.shape[3]
    f = f2 // 2
    rows = lambda i, be, nv, nu: (jnp.minimum(i, nu[0] - 1), 0)
    vec = lambda width: pl.BlockSpec((1, 1, width), lambda i, be, nv, nu: (be[i], 0, 0))
    grid_spec = pltpu.PrefetchScalarGridSpec(
        num_scalar_prefetch=3,
        grid=(n_blocks,),
        in_specs=[
            pl.BlockSpec((MOE_STEP, half), rows),
            pl.BlockSpec((1, 1, d, f2), lambda i, be, nv, nu: (layer, be[i], 0, 0)),
            vec(f), vec(f),
            pl.BlockSpec((1, 1, f, d), lambda i, be, nv, nu: (layer, be[i], 0, 0)),
            vec(d),
        ],
        out_specs=pl.BlockSpec((MOE_STEP, half), lambda i, be, nv, nu: (i, 0)),
        scratch_shapes=[pltpu.VMEM((d, f), BF16), pltpu.VMEM((d, f), BF16), pltpu.VMEM((f, d), BF16)],
    )
    return pl.pallas_call(
        _ffn_kernel,
        grid_spec=grid_spec,
        out_shape=jax.ShapeDtypeStruct((n_rows, half), jnp.int32),
        compiler_params=_params("arbitrary"),
        name="moe_ffn",
    )(block_e, block_valid, n_used, xs, w1, b1g.reshape(n_exp, 1, f), b1u.reshape(n_exp, 1, f), w2,
      b2.reshape(n_exp, 1, d))


def _combine_kernel(x_ref, y_ref, gates_ref, gate_ref, *rest, with_norm):
    g = gates_ref[...]
    half = y_ref.shape[2]
    acc_lo = acc_hi = None
    for k in range(TOP_K):
        lo, hi = _unpack_halves(y_ref[k])
        gk = g[:, k:k + 1]
        acc_lo = gk * lo if acc_lo is None else acc_lo + gk * lo
        acc_hi = gk * hi if acc_hi is None else acc_hi + gk * hi
    x_lo = x_ref[:, :half] + gate_ref[0, :, :half] * acc_lo
    x_hi = x_ref[:, half:] + gate_ref[0, :, half:] * acc_hi
    if with_norm:
        ng_ref, sc_ref, sh_ref = rest[:3]
        o_ref, h_ref = rest[-2:]
        x_new = jnp.concatenate([x_lo, x_hi], axis=1)
        o_ref[...] = x_new
        h_ref[...] = _modulated_norm(x_new, ng_ref[...], sc_ref[0], sh_ref[0]).astype(h_ref.dtype)
    else:
        o_ref = rest[-1]
        o_ref[:, :half] = x_lo
        o_ref[:, half:] = x_hi


def moe_combine(x, y_part, gates, gate_mod, seq, next_norm=None, part=0, n_parts=1, prev=(), tm=512):
    n, d = x.shape
    per = seq // tm
    steps = n // (tm * n_parts)
    first = part * steps
    rows = lambda width: pl.BlockSpec((tm, width), lambda i: (i + first, 0))
    vec = pl.BlockSpec((1, 1, d), lambda i: ((i + first) // per, 0, 0))
    in_specs = [rows(d), pl.BlockSpec((TOP_K, tm, d // 2), lambda i: (0, i, 0)), rows(LANES), vec]
    args = [x, y_part, gates, gate_mod]
    out_specs, out_shape = [rows(d)], [jax.ShapeDtypeStruct((n, d), F32)]
    if next_norm is not None:
        in_specs += [pl.BlockSpec((1, d), lambda i: (0, 0)), vec, vec]
        args += list(next_norm)
        out_specs, out_shape = out_specs + [rows(d)], out_shape + [jax.ShapeDtypeStruct((n, d), BF16)]
    aliases = {len(args) + j: j for j in range(len(prev))}
    in_specs += [pl.BlockSpec(memory_space=pl.ANY) for _ in prev]
    args += list(prev)
    return pl.pallas_call(
        functools.partial(_combine_kernel, with_norm=next_norm is not None),
        grid=(steps,),
        in_specs=in_specs,
        out_specs=out_specs,
        out_shape=out_shape,
        input_output_aliases=aliases,
        compiler_params=_params("parallel"),
        name="moe_combine",
    )(*args)


def moe_layer(x, mixer, mix_gate, mods, norm_g, router_w, router_b, layer, w1, b1, w2, b2, seq, next_norm):
    n, d = x.shape
    shift, scale, gate = mods
    rw_rows = jnp.pad(router_w.T, ((0, LANES - N_EXPERTS), (0, 0)))
    rw_hi = rw_rows.astype(BF16)
    rw_lo = (rw_rows - rw_hi.astype(F32)).astype(BF16)
    x, h, idx, gates, counts = proj_residual_router(
        *mixer, x, mix_gate, norm_g.reshape(1, d), scale, shift, jnp.concatenate([rw_hi, rw_lo], axis=0), rw_hi,
        router_b.reshape(N_EXPERTS, 1), seq)

    top_i = idx[:TOP_K]
    rank = idx[TOP_K:]
    counts = counts[:, 0]
    experts = jnp.arange(N_EXPERTS, dtype=jnp.int32)
    padded = (counts + MOE_STEP - 1) // MOE_STEP * MOE_STEP
    pad_ends = jnp.sum(jnp.where(experts[:, None] >= experts[None, :], padded[None, :], 0), axis=1)
    pad_starts = pad_ends - padded
    dest = jnp.sum(jnp.where(top_i[..., None] == experts, pad_starts, 0), axis=-1) + rank
    n_blocks = -(-n * TOP_K // MOE_STEP) + N_EXPERTS
    block_start = jnp.arange(n_blocks, dtype=jnp.int32) * MOE_STEP
    block_e = jnp.minimum(jnp.sum((pad_ends[None, :] <= block_start[:, None]).astype(jnp.int32), axis=1),
                          N_EXPERTS - 1)
    n_used = (pad_ends[N_EXPERTS - 1:] // MOE_STEP).astype(jnp.int32)
    seg_end = jnp.sum(jnp.where(block_e[:, None] == experts, pad_starts + counts, 0), axis=1)
    block_valid = jnp.clip(seg_end - block_start, 0, MOE_STEP).astype(jnp.int32)
    dest_flat = dest.reshape(-1)

    xs = sc_scatter_rows(h, dest_flat, n_blocks * MOE_STEP)
    ys = moe_ffn(xs, block_e, block_valid, n_used, layer, w1, b1[layer][:, 0::2], b1[layer][:, 1::2], w2, b2[layer])
    outs = ()
    per_part = n // COMBINE_PARTS
    for part in range(COMBINE_PARTS):
        idx_part = dest[:, part * per_part:(part + 1) * per_part].reshape(-1)
        y = sc_gather_rows(ys, idx_part).reshape(TOP_K, per_part, d // 2)
        outs = moe_combine(x, y, gates, gate, seq, next_norm, part, COMBINE_PARTS, tuple(outs))
    return (outs[0], outs[1]) if next_norm is not None else (outs[0], None)


def _pad_heads(w, width):
    k = w.shape[0]
    w = w.reshape(k, MLA_HEADS, width)
    return jnp.pad(w, ((0, 0), (0, 0), (0, LANES - width))).reshape(k, MLA_HEADS * LANES)


def _head_lane_source():
    half = QK_ROPE_DIM // 2
    first_nope = LANES // 2 - half
    lanes = (list(range(QK_NOPE_DIM, QK_NOPE_DIM + half)) + list(range(first_nope))
             + list(range(QK_NOPE_DIM + half, QK_HEAD_DIM)) + list(range(first_nope, QK_NOPE_DIM)))
    return jnp.array(lanes + [QK_HEAD_DIM] * (LANES - QK_HEAD_DIM), jnp.int32)


def _to_head_lanes(w):
    w = jnp.concatenate([w, jnp.zeros(w.shape[:-1] + (1,), w.dtype)], axis=-1)
    return jnp.take(w, _head_lane_source(), axis=-1)


def _rope_tables(positions):
    inv = 1.0 / (ROPE_THETA ** (jnp.arange(0, QK_ROPE_DIM, 2, dtype=F32) / QK_ROPE_DIM))
    ang = inv[:, None] * positions.reshape(-1).astype(F32)[None, :]
    return jnp.cos(ang), jnp.sin(ang)


def _split_mods(m, batch):
    d = m.shape[-1] // 3
    m = m[:batch]
    return tuple(m[:, None, j * d:(j + 1) * d] for j in range(3))


def pool_mla_mixer(x, h, mods, norm_g, tables, w_in, pool_w, pool_scale, cq_norm_g, w_uq, ckv_norm_g, w_ukv,
                   q_norm_g, k_norm_g, w_out, batch, seq):
    n, d = x.shape
    shift, scale, _ = mods
    rope_cols = _to_head_lanes(jnp.pad(w_in[:, Z_ROPE:], ((0, 0), (QK_NOPE_DIM, 0))))
    w_in_pad = jnp.concatenate([w_in[:, :Z_ROPE], rope_cols], axis=1).astype(BF16)
    if h is None:
        z = norm_matmul(x, norm_g.reshape(1, d), scale, shift, w_in_pad, seq)
    else:
        z = matmul(h, w_in_pad)
    w_ukv_h = w_ukv.reshape(KV_LORA_RANK, MLA_HEADS, QK_NOPE_DIM + V_HEAD_DIM)
    k_nope = jnp.pad(w_ukv_h[:, :, :QK_NOPE_DIM], ((0, 0), (0, 0), (0, QK_ROPE_DIM)))
    wuk_pad = _to_head_lanes(k_nope).reshape(KV_LORA_RANK, MLA_HEADS * LANES).astype(BF16)
    wuv_pad = _pad_heads(w_ukv_h[:, :, QK_NOPE_DIM:].reshape(KV_LORA_RANK, -1), V_HEAD_DIM).astype(BF16)
    wuq_pad = _to_head_lanes(w_uq.reshape(Q_LORA_RANK, MLA_HEADS, QK_HEAD_DIM))
    wuq_pad = wuq_pad.reshape(Q_LORA_RANK, MLA_HEADS * LANES).astype(BF16)
    pad_g = lambda g: _to_head_lanes(g).reshape(1, LANES)
    y_pool, q, k, v = mla_prep(z, pool_w.astype(BF16), pool_scale.reshape(1, -1), cq_norm_g.reshape(1, -1), wuq_pad,
                               ckv_norm_g.reshape(1, -1), wuk_pad, wuv_pad, pad_g(q_norm_g), pad_g(k_norm_g),
                               *tables, batch, seq)
    y_att = flash_attention(q, k, v).reshape(n, MLA_HEADS * V_HEAD_DIM)
    w_out_b = w_out.astype(BF16)
    return [y_pool, y_att], [w_out_b[:POOL_WIDTH], w_out_b[POOL_WIDTH:]], jnp.zeros((1, d), F32)


def conformer_mixer(h, pw1_w, pw1_b, dw_w, dw_b, ln_g, ln_b, pw2_w, pw2_b, seq):
    d = pw2_w.shape[1]
    cd = pw1_w.shape[1] // 2
    pw1 = pw1_w.astype(BF16)
    w_pad = jnp.pad(dw_w, ((0, CONV_HALO - CONV_WIDTH), (0, 0)))
    u = glu_dwconv_ln_silu(h, pw1[:, :cd], pw1[:, cd:], pw1_b[:cd].reshape(1, cd), pw1_b[cd:].reshape(1, cd),
                           w_pad, dw_b.reshape(1, cd), ln_g.reshape(1, cd), ln_b.reshape(1, cd), seq)
    return [u], [pw2_w.astype(BF16)], pw2_b.reshape(1, d)


def kernel(x, c, positions, ada_mix_w, ada_mix_b, norm_mix_g, w_in, pool_w, pool_scale, cq_norm_g, w_uq,
           ckv_norm_g, w_ukv, q_norm_g, k_norm_g, w_out, conv_pw1_w, conv_pw1_b, conv_dw_w, conv_dw_b,
           conv_ln_g, conv_ln_b, conv_pw2_w, conv_pw2_b, ada_ffn_w, ada_ffn_b, norm_ffn_g, router_w,
           router_b, moe_w1, moe_b1, moe_w2, moe_b2):
    batch, seq, d = x.shape
    depth = ada_mix_w.shape[0]
    c_pad = jnp.pad(c, ((0, 8 - batch), (0, 0)))
    mix_mods = adaln(c_pad, ada_mix_w, ada_mix_b)
    ffn_mods = adaln(c_pad, ada_ffn_w, ada_ffn_b)
    tables = _rope_tables(positions)
    xf = x.reshape(batch * seq, d)
    h = None
    for layer in range(depth):
        i = layer // 2
        mods = _split_mods(mix_mods[layer], batch)
        if layer % 2 == 0:
            mixer = pool_mla_mixer(xf, h, mods, norm_mix_g[layer], tables, w_in[i], pool_w[i], pool_scale[i],
                                   cq_norm_g[i], w_uq[i], ckv_norm_g[i], w_ukv[i], q_norm_g[i], k_norm_g[i],
                                   w_out[i], batch, seq)
        else:
            mixer = conformer_mixer(h, conv_pw1_w[i], conv_pw1_b[i], conv_dw_w[i], conv_dw_b[i],
                                    conv_ln_g[i], conv_ln_b[i], conv_pw2_w[i], conv_pw2_b[i], seq)
        next_norm = None
        if layer + 1 < depth:
            n_shift, n_scale, _ = _split_mods(mix_mods[layer + 1], batch)
            next_norm = (norm_mix_g[layer + 1].reshape(1, d), n_scale, n_shift)
        xf, h = moe_layer(xf, mixer, mods[2], _split_mods(ffn_mods[layer], batch), norm_ffn_g[layer],
                          router_w[layer], router_b[layer], layer, moe_w1, moe_b1, moe_w2, moe_b2, seq, next_norm)
    return xf.reshape(batch, seq, d)
```

```python
import functools

import jax
import jax.numpy as jnp
from jax import lax
from jax.experimental import pallas as pl
from jax.experimental.pallas import tpu as pltpu
from jax.experimental.pallas import tpu_sc as plsc

F32 = jnp.float32
BF16 = jnp.bfloat16
HIGHEST = lax.Precision.HIGHEST

EPS = 1e-6
POOL_WINDOWS = (2, 4, 8, 16)
POOL_GROUP_DIM = 128
POOL_WIDTH = POOL_GROUP_DIM * len(POOL_WINDOWS)
MLA_HEADS = 8
QK_NOPE_DIM = 64
QK_ROPE_DIM = 32
QK_HEAD_DIM = QK_NOPE_DIM + QK_ROPE_DIM
V_HEAD_DIM = 64
Q_LORA_RANK = 384
KV_LORA_RANK = 256
ROPE_THETA = 10000.0
CHUNK = 64
CONV_WIDTH = 31
N_EXPERTS = 32
TOP_K = 4
SWIGLU_ALPHA = 1.702
SWIGLU_LIMIT = 7.0
MOE_BLOCK = 256
MOE_CHUNK = 512
MOE_STEP = 1024

LANES = 128
SUBLANES = 8
CONV_PATCH_ROWS = 64
CONV_PATCH_COLS = 256
POOL_HALO = 16
CONV_HALO = 32
MASK_VALUE = -1e30
LOG2_E = 1.4426950408889634
VMEM_LIMIT = 52 * 1024 * 1024
SC_CORES = 2
SC_SUBCORES = 16
SC_WORKERS = SC_CORES * SC_SUBCORES
SC_CHUNK = 64
COMBINE_PARTS = 2

Z_CQ = POOL_WIDTH
Z_CKV = Z_CQ + Q_LORA_RANK
Z_ROPE = Z_CKV + KV_LORA_RANK
Z_WIDTH = Z_ROPE + LANES


def _params(*sem, vmem=None):
    return pltpu.CompilerParams(dimension_semantics=sem, vmem_limit_bytes=vmem or VMEM_LIMIT,
                                disable_bounds_checks=True)


def _adaln_kernel(c_ref, w_ref, b_ref, o_ref):
    c = c_ref[...]
    s = c * jax.nn.sigmoid(c)
    o_ref[0] = jnp.dot(s, w_ref[0], preferred_element_type=F32, precision=HIGHEST) + b_ref[0]


def adaln(c_pad, w, b):
    n_l, d, d3 = w.shape
    tn = d3 // 2
    return pl.pallas_call(
        _adaln_kernel,
        grid=(n_l, d3 // tn),
        in_specs=[pl.BlockSpec((8, d), lambda l, j: (0, 0)),
                  pl.BlockSpec((1, d, tn), lambda l, j: (l, 0, j)),
                  pl.BlockSpec((1, 1, tn), lambda l, j: (l, 0, j))],
        out_specs=pl.BlockSpec((1, 8, tn), lambda l, j: (l, 0, j)),
        out_shape=jax.ShapeDtypeStruct((n_l, 8, d3), F32),
        compiler_params=_params("parallel", "parallel"),
        name="adaln",
    )(c_pad, w, b.reshape(n_l, 1, d3))


def _modulated_norm(x, g, scale, shift):
    ms = jnp.mean(x * x, axis=-1, keepdims=True)
    return x * lax.rsqrt(ms + EPS) * g * (1.0 + scale) + shift


def _pack_halves(x):
    w = x.shape[1] // 2
    lo = lax.bitcast_convert_type(x[:, :w].astype(BF16).astype(F32), jnp.uint32)
    hi = lax.bitcast_convert_type(x[:, w:].astype(BF16).astype(F32), jnp.uint32)
    return lax.bitcast_convert_type((lo >> 16) | (hi & jnp.uint32(0xFFFF0000)), jnp.int32)


def _unpack_halves(p):
    u = lax.bitcast_convert_type(p, jnp.uint32)
    lo = lax.bitcast_convert_type(u << 16, F32)
    hi = lax.bitcast_convert_type(u & jnp.uint32(0xFFFF0000), F32)
    return lo, hi


def _norm_matmul_kernel(x_ref, g_ref, sc_ref, sh_ref, w_ref, o_ref):
    h = _modulated_norm(x_ref[...], g_ref[...], sc_ref[0], sh_ref[0])
    o_ref[...] = jnp.dot(h.astype(BF16), w_ref[...], preferred_element_type=F32)


def norm_matmul(x, g, scale, shift, w, seq, tm=512):
    n, d = x.shape
    m = w.shape[1]
    per = seq // tm
    vec = pl.BlockSpec((1, 1, d), lambda i: (i // per, 0, 0))
    return pl.pallas_call(
        _norm_matmul_kernel,
        grid=(n // tm,),
        in_specs=[pl.BlockSpec((tm, d), lambda i: (i, 0)), pl.BlockSpec((1, d), lambda i: (0, 0)), vec, vec,
                  pl.BlockSpec((d, m), lambda i: (0, 0))],
        out_specs=pl.BlockSpec((tm, m), lambda i: (i, 0)),
        out_shape=jax.ShapeDtypeStruct((n, m), F32),
        compiler_params=_params("parallel"),
        name="norm_matmul",
    )(x, g, scale, shift, w)


def _matmul_kernel(a_ref, w_ref, o_ref):
    o_ref[...] = jnp.dot(a_ref[...], w_ref[...], preferred_element_type=F32).astype(o_ref.dtype)


def matmul(a, w, out_dtype=F32, tm=512):
    n, k = a.shape
    m = w.shape[1]
    return pl.pallas_call(
        _matmul_kernel,
        grid=(n // tm,),
        in_specs=[pl.BlockSpec((tm, k), lambda i: (i, 0)), pl.BlockSpec((k, m), lambda i: (0, 0))],
        out_specs=pl.BlockSpec((tm, m), lambda i: (i, 0)),
        out_shape=jax.ShapeDtypeStruct((n, m), out_dtype),
        compiler_params=_params("parallel"),
        name="matmul",
    )(a, w)


def _proj_residual_router_kernel(*refs, n_in, tm):
    a_refs, w_refs = refs[:n_in], refs[n_in:2 * n_in]
    (b_ref, x_ref, gate_ref, g_ref, sc_ref, sh_ref, rw2_ref, rwh_ref, rb_ref,
     o_ref, h_ref, idx_ref, gates_ref, cnt_ref, carry_ref) = refs[2 * n_in:]

    @pl.when(pl.program_id(0) == 0)
    def _():
        carry_ref[...] = jnp.zeros(carry_ref.shape, F32)

    acc = b_ref[...]
    for a_ref, w_ref in zip(a_refs, w_refs):
        acc = acc + jnp.dot(a_ref[...], w_ref[...], preferred_element_type=F32)
    x_new = x_ref[...] + gate_ref[0] * acc
    o_ref[...] = x_new
    h = _modulated_norm(x_new, g_ref[...], sc_ref[0], sh_ref[0])
    h_ref[...] = _pack_halves(h)
    h_hi = h.astype(BF16)
    h_lo = (h - h_hi.astype(F32)).astype(BF16)
    contract_features = (((1,), (1,)), ((), ()))
    both = lax.dot_general(rw2_ref[...], h_hi, contract_features, preferred_element_type=F32)
    low = lax.dot_general(rwh_ref[...], h_lo, contract_features, preferred_element_type=F32)
    logits = (both[:LANES] + both[LANES:] + low)[:N_EXPERTS] + rb_ref[...]
    idx, gates, total = _route_tile(logits, carry_ref[:, 0:1], tm)
    idx_ref[...] = idx.astype(jnp.int32)
    gate_rows = jnp.concatenate([gates, jnp.zeros((LANES - TOP_K, tm), F32)], axis=0)
    gates_ref[...] = jnp.transpose(gate_rows)
    carry_ref[...] = jnp.broadcast_to(total, carry_ref.shape)
    cnt_ref[...] = jnp.broadcast_to(total, cnt_ref.shape).astype(jnp.int32)


def proj_residual_router(a_list, w_list, bias, x, gate, g, scale, shift, rw_both, rw_hi, rb_pad, seq, tm=512):
    n, d = x.shape
    per = seq // tm
    n_in = len(a_list)
    rows = lambda width: pl.BlockSpec((tm, width), lambda i: (i, 0))
    const = lambda a: pl.BlockSpec(a.shape, lambda i: (0, 0))
    vec = pl.BlockSpec((1, 1, d), lambda i: (i // per, 0, 0))
    in_specs = [rows(a.shape[1]) for a in a_list] + [const(w) for w in w_list]
    in_specs += [const(bias), rows(d), vec, const(g), vec, vec, const(rw_both), const(rw_hi), const(rb_pad)]
    return pl.pallas_call(
        functools.partial(_proj_residual_router_kernel, n_in=n_in, tm=tm),
        grid=(n // tm,),
        in_specs=in_specs,
        out_specs=[rows(d), rows(d // 2), pl.BlockSpec((2 * TOP_K, tm), lambda i: (0, i)), rows(LANES),
                   pl.BlockSpec((N_EXPERTS, LANES), lambda i: (0, 0))],
        out_shape=[jax.ShapeDtypeStruct((n, d), F32), jax.ShapeDtypeStruct((n, d // 2), jnp.int32),
                   jax.ShapeDtypeStruct((2 * TOP_K, n), jnp.int32), jax.ShapeDtypeStruct((n, LANES), F32),
                   jax.ShapeDtypeStruct((N_EXPERTS, LANES), jnp.int32)],
        scratch_shapes=[pltpu.VMEM((N_EXPERTS, LANES), F32)],
        compiler_params=_params("arbitrary"),
        name="proj_residual_router",
    )(*a_list, *w_list, bias, x, gate, g, scale, shift, rw_both, rw_hi, rb_pad)


def _rope(xn, cos_t, sin_t):
    return xn * cos_t + pltpu.roll(xn, LANES // 2, axis=1) * sin_t


def _mla_prep_kernel(z_ref, halo_ref, pw_ref, ps_ref, cqg_ref, wuq_ref, ckvg_ref, wuk_ref, wuv_ref,
                     qg_ref, kg_ref, cos_ref, sin_ref,
                     yp_ref, q_ref, k_ref, v_ref, ext_ref, *, tm, per):
    si = pl.program_id(0) % per
    u = z_ref[:, 0:POOL_WIDTH]
    ext_ref[0:POOL_HALO, :] = jnp.where(si == 0, 0.0, halo_ref[...])
    ext_ref[POOL_HALO:, :] = u
    t = si * tm + lax.broadcasted_iota(jnp.int32, (tm, 1), 0)
    for g, w in enumerate(POOL_WINDOWS):
        cols = slice(g * POOL_GROUP_DIM, (g + 1) * POOL_GROUP_DIM)
        ug = u[:, cols]
        s = ug
        for j in range(1, w):
            s = s + ext_ref[POOL_HALO - j:POOL_HALO - j + tm, cols]
        cnt = jnp.minimum(t + 1, w).astype(F32)
        pooled = s / cnt - ug
        yp = jnp.dot(pooled.astype(BF16), pw_ref[g], preferred_element_type=F32) * ps_ref[:, cols]
        yp_ref[:, cols] = yp.astype(yp_ref.dtype)

    half = QK_ROPE_DIM // 2

    def to_lanes(t_ref):
        rows = jnp.concatenate([t_ref[...], jnp.zeros((LANES - half, tm), F32)], axis=0)
        return jnp.transpose(rows)

    cos_lo, sin_lo = to_lanes(cos_ref), to_lanes(sin_ref)
    lane = lax.broadcasted_iota(jnp.int32, (tm, LANES), 1)
    nope = ((lane >= half) & (lane < LANES // 2)) | ((lane >= LANES // 2 + half) & (lane < QK_HEAD_DIM))
    cos_t = cos_lo + pltpu.roll(cos_lo, LANES // 2, axis=1) + jnp.where(nope, 1.0, 0.0)
    sin_t = pltpu.roll(sin_lo, LANES // 2, axis=1) - sin_lo
    inv_head = 1.0 / QK_HEAD_DIM
    ones_block = jnp.ones((LANES, LANES), BF16)

    def head_sum_sq(xh):
        sq = xh * xh
        hi = sq.astype(BF16)
        lo = (sq - hi.astype(F32)).astype(BF16)
        return (jnp.dot(hi, ones_block, preferred_element_type=F32)
                + jnp.dot(lo, ones_block, preferred_element_type=F32))

    cq = z_ref[:, Z_CQ:Z_CKV]
    cqn = cq * lax.rsqrt(jnp.mean(cq * cq, axis=-1, keepdims=True) + EPS) * cqg_ref[...]
    qf = jnp.dot(cqn.astype(BF16), wuq_ref[...], preferred_element_type=F32)
    q_scale = QK_HEAD_DIM ** -0.5 * LOG2_E
    for h in range(MLA_HEADS):
        qh = qf[:, h * LANES:(h + 1) * LANES]
        ss = head_sum_sq(qh) * inv_head
        qn = qh * lax.rsqrt(ss + EPS) * qg_ref[...]
        q_ref[0, h] = (_rope(qn, cos_t, sin_t) * q_scale).astype(q_ref.dtype)

    ckv = z_ref[:, Z_CKV:Z_ROPE]
    ckvn = (ckv * lax.rsqrt(jnp.mean(ckv * ckv, axis=-1, keepdims=True) + EPS) * ckvg_ref[...]).astype(BF16)
    kf = jnp.dot(ckvn, wuk_ref[...], preferred_element_type=F32)
    vf = jnp.dot(ckvn, wuv_ref[...], preferred_element_type=F32)
    k_rope = z_ref[:, Z_ROPE:Z_WIDTH]
    ones_lane = lax.broadcasted_iota(jnp.int32, (tm, LANES), 1) == V_HEAD_DIM
    for h in range(MLA_HEADS):
        kh = kf[:, h * LANES:(h + 1) * LANES] + k_rope
        ss = head_sum_sq(kh) * inv_head
        kn = kh * lax.rsqrt(ss + EPS) * kg_ref[...]
        k_ref[0, h] = _rope(kn, cos_t, sin_t).astype(k_ref.dtype)
        v_ref[0, h] = jnp.where(ones_lane, 1.0, vf[:, h * LANES:(h + 1) * LANES]).astype(v_ref.dtype)


def mla_prep(z, pool_w, pool_scale, cq_g, wuq_pad, ckv_g, wuk_pad, wuv_pad, qg_pad, kg_pad,
             cos_t, sin_t, batch, seq, tm=512):
    n = z.shape[0]
    per = seq // tm
    hb = tm // POOL_HALO
    full = lambda a: pl.BlockSpec(a.shape, lambda i: (0,) * a.ndim)
    tab = pl.BlockSpec((QK_ROPE_DIM // 2, tm), lambda i: (0, i))
    head_out = pl.BlockSpec((1, MLA_HEADS, tm, LANES), lambda i: (i // per, 0, i % per, 0))
    head_shape = jax.ShapeDtypeStruct((batch, MLA_HEADS, seq, LANES), BF16)
    return pl.pallas_call(
        functools.partial(_mla_prep_kernel, tm=tm, per=per),
        grid=(n // tm,),
        in_specs=[pl.BlockSpec((tm, Z_WIDTH), lambda i: (i, 0)),
                  pl.BlockSpec((POOL_HALO, POOL_WIDTH), lambda i: (jnp.maximum(i * hb - 1, 0), 0)),
                  full(pool_w), full(pool_scale), full(cq_g), full(wuq_pad), full(ckv_g), full(wuk_pad),
                  full(wuv_pad), full(qg_pad), full(kg_pad), tab, tab],
        out_specs=[pl.BlockSpec((tm, POOL_WIDTH), lambda i: (i, 0)), head_out, head_out, head_out],
        out_shape=[jax.ShapeDtypeStruct((n, POOL_WIDTH), BF16), head_shape, head_shape, head_shape],
        scratch_shapes=[pltpu.VMEM((tm + POOL_HALO, POOL_WIDTH), F32)],
        compiler_params=_params("parallel"),
        name="mla_prep",
    )(z, z, pool_w, pool_scale, cq_g, wuq_pad, ckv_g, wuk_pad, wuv_pad, qg_pad, kg_pad, cos_t, sin_t)


def _flash_kernel(q_ref, k_ref, v_ref, o_ref, m_ref, acc_ref, *, tq, tk, td, heads):
    qi = pl.program_id(2)
    q_chunk = lax.broadcasted_iota(jnp.int32, (tq, td), 0) // CHUNK
    k_chunk = lax.broadcasted_iota(jnp.int32, (tq, td), 1) // CHUNK
    m_ref[...] = jnp.full(m_ref.shape, MASK_VALUE, F32)
    acc_ref[...] = jnp.zeros(acc_ref.shape, F32)

    def step(start, width, mask, row0=0):
        for hh in range(heads):
            k = k_ref[0, hh, pl.ds(start, width), :]
            v = v_ref[0, hh, pl.ds(start, width), :]
            s = lax.dot_general(q_ref[0, hh, row0:, :], k, (((1,), (1,)), ((), ())), preferred_element_type=F32)
            if mask is not None:
                s = jnp.where(mask[row0:], s, MASK_VALUE)
            cols_s = [s[:, c * LANES:(c + 1) * LANES] for c in range(width // LANES)]
            s_max = cols_s[0]
            for sc in cols_s[1:]:
                s_max = jnp.maximum(s_max, sc)
            m_prev = m_ref[hh, row0:, :]
            m_new = jnp.maximum(m_prev, jnp.max(s_max, axis=-1, keepdims=True))
            alpha = jnp.exp2(m_prev - m_new)
            p = jnp.concatenate([jnp.exp2(sc - m_new) for sc in cols_s], axis=1).astype(v.dtype)
            acc_ref[hh, row0:, :] = alpha * acc_ref[hh, row0:, :] + jnp.dot(p, v, preferred_element_type=F32)
            m_ref[hh, row0:, :] = m_new

    def full_step(j, carry):
        step(pl.multiple_of(j * tk, tk), tk, None)
        return carry

    lax.fori_loop(0, (tq // tk) * qi, full_step, 0)
    for dd in range(tq // td):
        step(pl.multiple_of(qi * tq + dd * td, td), td, k_chunk + dd * (td // CHUNK) <= q_chunk, row0=dd * td)
    outs = []
    for hh in range(heads):
        acc = acc_ref[hh]
        outs.append((acc / acc[:, V_HEAD_DIM:V_HEAD_DIM + 1])[:, :V_HEAD_DIM])
    o_ref[0] = jnp.concatenate(outs, axis=-1).astype(o_ref.dtype)


def flash_attention(q, k, v, tq=1024, tk=1024, td=512):
    b, h, s, _ = q.shape
    heads = LANES // V_HEAD_DIM
    return pl.pallas_call(
        functools.partial(_flash_kernel, tq=tq, tk=tk, td=td, heads=heads),
        grid=(b, h // heads, s // tq),
        in_specs=[pl.BlockSpec((1, heads, tq, LANES), lambda bi, hi, qi: (bi, hi, qi, 0)),
                  pl.BlockSpec((1, heads, s, LANES), lambda bi, hi, qi: (bi, hi, 0, 0)),
                  pl.BlockSpec((1, heads, s, LANES), lambda bi, hi, qi: (bi, hi, 0, 0))],
        out_specs=pl.BlockSpec((1, tq, LANES), lambda bi, hi, qi: (bi, qi, hi)),
        out_shape=jax.ShapeDtypeStruct((b, s, h * V_HEAD_DIM), BF16),
        scratch_shapes=[pltpu.VMEM((heads, tq, LANES), F32) for _ in range(2)],
        compiler_params=_params("parallel", "parallel", "parallel"),
        name="flash_attention",
    )(q, k, v)


def _glu_conv_kernel(h0_ref, hn_ref, wv_ref, wg_ref, bv_ref, bg_ref, w_ref, b_ref, g_ref, beta_ref, o_ref,
                     ext_ref, sh_ref, conv_ref, nxt_ref, *, tm, per):
    i = pl.program_id(0)
    d = o_ref.shape[1]

    def glu(h):
        val = jnp.dot(h, wv_ref[...], preferred_element_type=F32) + bv_ref[...]
        gt = jnp.dot(h, wg_ref[...], preferred_element_type=F32) + bg_ref[...]
        return val * jax.nn.sigmoid(gt)

    @pl.when(i == 0)
    def _():
        ext_ref[CONV_HALO:, :] = glu(h0_ref[...])

    @pl.when(i > 0)
    def _():
        ext_ref[0:CONV_HALO, :] = ext_ref[tm:tm + CONV_HALO, :]
        ext_ref[CONV_HALO:, :] = nxt_ref[...]

    @pl.when(i % per == 0)
    def _():
        ext_ref[0:CONV_HALO, :] = jnp.zeros((CONV_HALO, d), F32)

    nxt_ref[...] = glu(hn_ref[...])
    span = sh_ref.shape[1]
    for b in range(1, SUBLANES):
        sh_ref[b - 1] = ext_ref[b:b + span, :]
    first = CONV_HALO - (CONV_WIDTH - 1)
    for r0 in range(0, tm, CONV_PATCH_ROWS):
        for c0 in range(0, d, CONV_PATCH_COLS):
            cols = slice(c0, c0 + CONV_PATCH_COLS)
            view = (CONV_PATCH_ROWS // SUBLANES, SUBLANES, CONV_PATCH_COLS)
            patch = jnp.broadcast_to(b_ref[:, cols], view)
            for j in range(CONV_WIDTH):
                a, b = divmod(first + j, SUBLANES)
                rows = slice(SUBLANES * a + r0, SUBLANES * a + r0 + CONV_PATCH_ROWS)
                win = ext_ref[rows, cols] if b == 0 else sh_ref[b - 1, rows, cols]
                tap = jnp.broadcast_to(w_ref[j:j + 1, cols], view[1:])
                patch = patch + tap[None] * win.reshape(view)
            conv_ref[r0:r0 + CONV_PATCH_ROWS, cols] = patch.reshape(CONV_PATCH_ROWS, CONV_PATCH_COLS)
    acc = conv_ref[...]
    mu = jnp.mean(acc, axis=-1, keepdims=True)
    cen = acc - mu
    var = jnp.mean(cen * cen, axis=-1, keepdims=True)
    y = cen * lax.rsqrt(var + EPS) * g_ref[...] + beta_ref[...]
    o_ref[...] = (y * jax.nn.sigmoid(y)).astype(o_ref.dtype)


def glu_dwconv_ln_silu(h, wv, wg, bv, bg, w_pad, b, g, beta, seq, tm=256):
    n, k = h.shape
    d = wv.shape[1]
    per = seq // tm
    last = n // tm - 1
    full = lambda a: pl.BlockSpec(a.shape, lambda i: (0, 0))
    return pl.pallas_call(
        functools.partial(_glu_conv_kernel, tm=tm, per=per),
        grid=(n // tm,),
        in_specs=[pl.BlockSpec((tm, k), lambda i: (0, 0)),
                  pl.BlockSpec((tm, k), lambda i: (jnp.minimum(i + 1, last), 0)),
                  full(wv), full(wg), full(bv), full(bg), full(w_pad), full(b), full(g), full(beta)],
        out_specs=pl.BlockSpec((tm, d), lambda i: (i, 0)),
        out_shape=jax.ShapeDtypeStruct((n, d), BF16),
        scratch_shapes=[pltpu.VMEM((tm + CONV_HALO, d), F32),
                        pltpu.VMEM((SUBLANES - 1, tm + CONV_HALO - SUBLANES, d), F32),
                        pltpu.VMEM((tm, d), F32), pltpu.VMEM((tm, d), F32)],
        compiler_params=_params("arbitrary"),
        name="glu_dwconv_ln_silu",
    )(h, h, wv, wg, bv, bg, w_pad, b, g, beta)


def _route_tile(logits, before_tile, tm):
    row_f = lax.broadcasted_iota(jnp.int32, (N_EXPERTS, tm), 0).astype(F32)
    neg = -jnp.inf
    picks, vals, ids = [], [], []
    for _ in range(TOP_K):
        mx = jnp.max(logits, axis=0, keepdims=True)
        idx = jnp.min(jnp.where(logits == mx, row_f, float(N_EXPERTS)), axis=0, keepdims=True)
        pick = row_f == idx
        picks.append(pick)
        vals.append(mx)
        ids.append(idx)
        logits = jnp.where(pick, neg, logits)
    exps = [jnp.exp(v - vals[0]) for v in vals]
    den = exps[0]
    for e in exps[1:]:
        den = den + e

    chosen = jnp.zeros((N_EXPERTS, tm), F32)
    for pick in picks:
        chosen = chosen + pick.astype(F32)
    r_io = lax.broadcasted_iota(jnp.int32, (tm, tm), 0)
    c_io = lax.broadcasted_iota(jnp.int32, (tm, tm), 1)
    later = (r_io < c_io).astype(BF16)
    before = jnp.dot(chosen.astype(BF16), later, preferred_element_type=F32) + before_tile
    ranks = [jnp.sum(jnp.where(pick, before, 0.0), axis=0, keepdims=True) for pick in picks]
    idx_out = jnp.concatenate(ids + ranks, axis=0)
    gate_out = jnp.concatenate([e / den for e in exps], axis=0)
    total = before_tile + jnp.sum(chosen, axis=1, keepdims=True)
    return idx_out, gate_out, total


def _sc_worker_base(per_worker):
    return (lax.axis_index("s") * SC_CORES + lax.axis_index("c")) * per_worker


def sc_scatter_rows(src, dest_flat, n_rows):
    n, d = src.shape
    per_w = n // SC_WORKERS
    n_chunks = per_w // SC_CHUNK
    mesh = plsc.VectorSubcoreMesh(core_axis_name="c", subcore_axis_name="s")

    @functools.partial(
        pl.kernel, out_type=jax.ShapeDtypeStruct((n_rows, d), src.dtype), mesh=mesh,
        scratch_types=[pltpu.VMEM((per_w,), jnp.int32) for _ in range(TOP_K)]
        + [pltpu.VMEM((SC_CHUNK, d), src.dtype) for _ in range(2)] + [pltpu.SemaphoreType.DMA] * 4,
        name="sc_scatter_rows")
    def scatter(src_hbm, dest_hbm, out_hbm, *scratch):
        idx_refs = scratch[:TOP_K]
        bufs = scratch[TOP_K:TOP_K + 2]
        in_sems, out_sems = scratch[TOP_K + 2:TOP_K + 4], scratch[TOP_K + 4:TOP_K + 6]
        base = _sc_worker_base(per_w)
        for k, idx_ref in enumerate(idx_refs):
            pltpu.sync_copy(dest_hbm.at[pl.ds(k * n + base, per_w)], idx_ref)

        def load(j, slot):
            return pltpu.make_async_copy(src_hbm.at[pl.ds(base + j * SC_CHUNK, SC_CHUNK)], bufs[slot], in_sems[slot])

        def store_all(j, slot):
            copies = [pltpu.make_async_copy(bufs[slot], out_hbm.at[idx_ref.at[pl.ds(j * SC_CHUNK, SC_CHUNK)]],
                                            out_sems[slot]) for idx_ref in idx_refs]
            for cp in copies:
                cp.start()
            for cp in copies:
                cp.wait()

        load(0, 0).start()

        @pl.loop(0, n_chunks // 2)
        def _(p):
            j = 2 * p
            load(j + 1, 1).start()
            load(j, 0).wait()
            store_all(j, 0)

            @pl.when(j + 2 < n_chunks)
            def _():
                load(j + 2, 0).start()

            load(j + 1, 1).wait()
            store_all(j + 1, 1)

    return scatter(src, dest_flat)


def sc_gather_rows(table, idx):
    b = idx.shape[0]
    d = table.shape[1]
    per_w = b // SC_WORKERS
    n_chunks = per_w // SC_CHUNK
    mesh = plsc.VectorSubcoreMesh(core_axis_name="c", subcore_axis_name="s")

    @functools.partial(
        pl.kernel, out_type=jax.ShapeDtypeStruct((b, d), table.dtype), mesh=mesh,
        scratch_types=[pltpu.VMEM((per_w,), jnp.int32)] + [pltpu.VMEM((SC_CHUNK, d), table.dtype) for _ in range(2)]
        + [pltpu.SemaphoreType.DMA] * 4,
        name="sc_gather_rows")
    def gather(table_hbm, idx_hbm, out_hbm, idx_ref, buf0, buf1, gsem0, gsem1, osem0, osem1):
        bufs, in_sems, out_sems = (buf0, buf1), (gsem0, gsem1), (osem0, osem1)
        base = _sc_worker_base(per_w)
        pltpu.sync_copy(idx_hbm.at[pl.ds(base, per_w)], idx_ref)

        def fetch(j, slot):
            return pltpu.make_async_copy(table_hbm.at[idx_ref.at[pl.ds(j * SC_CHUNK, SC_CHUNK)]], bufs[slot],
                                         in_sems[slot])

        def store(j, slot):
            return pltpu.make_async_copy(bufs[slot], out_hbm.at[pl.ds(base + j * SC_CHUNK, SC_CHUNK)],
                                         out_sems[slot])

        fetch(0, 0).start()

        @pl.loop(0, n_chunks // 2)
        def _(p):
            j = 2 * p

            @pl.when(p > 0)
            def _():
                store(j - 1, 1).wait()

            fetch(j + 1, 1).start()
            fetch(j, 0).wait()
            store(j, 0).start()
            fetch(j + 1, 1).wait()
            store(j, 0).wait()

            @pl.when(j + 2 < n_chunks)
            def _():
                fetch(j + 2, 0).start()

            store(j + 1, 1).start()

        store(n_chunks - 1, 1).wait()

    return gather(table, idx)


def _ffn_kernel(be_ref, nv_ref, nu_ref, xs_ref, w1_ref, b1g_ref, b1u_ref, w2_ref, b2_ref, y_ref,
                wg_ref, wu_ref, w2s_ref):
    i = pl.program_id(0)
    pair = 2 * LANES

    @pl.when(i < nu_ref[0])
    def _():
        e = be_ref[i]
        prev = be_ref[jnp.maximum(i - 1, 0)]

        @pl.when((i == 0) | (e != prev))
        def _():
            r_io = lax.broadcasted_iota(jnp.int32, (pair, pair), 0)
            c_io = lax.broadcasted_iota(jnp.int32, (pair, pair), 1)
            want = jnp.where(c_io < LANES, 2 * c_io, 2 * (c_io - LANES) + 1)
            sel = (r_io == want).astype(BF16)
            for c in range(wg_ref.shape[1] // LANES):
                slab = w1_ref[0, 0, :, c * pair:(c + 1) * pair].astype(BF16)
                split = jnp.dot(slab, sel, preferred_element_type=F32)
                wg_ref[:, c * LANES:(c + 1) * LANES] = split[:, :LANES].astype(BF16)
                wu_ref[:, c * LANES:(c + 1) * LANES] = split[:, LANES:].astype(BF16)
            w2s_ref[...] = w2_ref[0, 0].astype(BF16)

        nvalid = nv_ref[i]

        def ffn_rows(r0, rows):
            row = r0 + lax.broadcasted_iota(jnp.int32, (rows, xs_ref.shape[1]), 0)
            lo, hi = _unpack_halves(jnp.where(row < nvalid, xs_ref[r0:r0 + rows, :], 0))
            x = jnp.concatenate([lo, hi], axis=1).astype(BF16)
            hg = jnp.dot(x, wg_ref[...], preferred_element_type=F32) + b1g_ref[0]
            hu = jnp.dot(x, wu_ref[...], preferred_element_type=F32) + b1u_ref[0]
            gate = jnp.minimum(hg, SWIGLU_LIMIT)
            up = jnp.clip(hu, -SWIGLU_LIMIT, SWIGLU_LIMIT)
            act = gate * jax.nn.sigmoid(SWIGLU_ALPHA * gate) * (up + 1.0)
            y = jnp.dot(act.astype(BF16), w2s_ref[...], preferred_element_type=F32) + b2_ref[0]
            y_ref[r0:r0 + rows, :] = _pack_halves(y)

        def zero_rows(r0, rows):
            y_ref[r0:r0 + rows, :] = jnp.zeros((rows, y_ref.shape[1]), y_ref.dtype)

        for r0 in range(0, y_ref.shape[0], MOE_CHUNK):
            left = nvalid - r0

            @pl.when(left > MOE_BLOCK)
            def _(r0=r0):
                ffn_rows(r0, MOE_CHUNK)

            @pl.when((left > 0) & (left <= MOE_BLOCK))
            def _(r0=r0):
                ffn_rows(r0, MOE_BLOCK)
                zero_rows(r0 + MOE_BLOCK, MOE_CHUNK - MOE_BLOCK)

            @pl.when(left <= 0)
            def _(r0=r0):
                zero_rows(r0, MOE_CHUNK)

    @pl.when(i >= nu_ref[0])
    def _():
        y_ref[...] = jnp.zeros(y_ref.shape, y_ref.dtype)


def moe_ffn(xs, block_e, block_valid, n_used, layer, w1, b1g, b1u, w2, b2):
    n_rows, half = xs.shape
    d = 2 * half
    n_blocks = n_rows // MOE_STEP
    n_exp, f2 = w1.shape[1], w1.shape[3]
    f = f2 // 2
    rows = lambda i, be, nv, nu: (jnp.minimum(i, nu[0] - 1), 0)
    vec = lambda width: pl.BlockSpec((1, 1, width), lambda i, be, nv, nu: (be[i], 0, 0))
    grid_spec = pltpu.PrefetchScalarGridSpec(
        num_scalar_prefetch=3,
        grid=(n_blocks,),
        in_specs=[
            pl.BlockSpec((MOE_STEP, half), rows),
            pl.BlockSpec((1, 1, d, f2), lambda i, be, nv, nu: (layer, be[i], 0, 0)),
            vec(f), vec(f),
            pl.BlockSpec((1, 1, f, d), lambda i, be, nv, nu: (layer, be[i], 0, 0)),
            vec(d),
        ],
        out_specs=pl.BlockSpec((MOE_STEP, half), lambda i, be, nv, nu: (i, 0)),
        scratch_shapes=[pltpu.VMEM((d, f), BF16), pltpu.VMEM((d, f), BF16), pltpu.VMEM((f, d), BF16)],
    )
    return pl.pallas_call(
        _ffn_kernel,
        grid_spec=grid_spec,
        out_shape=jax.ShapeDtypeStruct((n_rows, half), jnp.int32),
        compiler_params=_params("arbitrary"),
        name="moe_ffn",
    )(block_e, block_valid, n_used, xs, w1, b1g.reshape(n_exp, 1, f), b1u.reshape(n_exp, 1, f), w2,
      b2.reshape(n_exp, 1, d))


def _combine_kernel(x_ref, y_ref, gates_ref, gate_ref, *rest, with_norm):
    g = gates_ref[...]
    half = y_ref.shape[2]
    acc_lo = acc_hi = None
    for k in range(TOP_K):
        lo, hi = _unpack_halves(y_ref[k])
        gk = g[:, k:k + 1]
        acc_lo = gk * lo if acc_lo is None else acc_lo + gk * lo
        acc_hi = gk * hi if acc_hi is None else acc_hi + gk * hi
    x_lo = x_ref[:, :half] + gate_ref[0, :, :half] * acc_lo
    x_hi = x_ref[:, half:] + gate_ref[0, :, half:] * acc_hi
    if with_norm:
        ng_ref, sc_ref, sh_ref = rest[:3]
        o_ref, h_ref = rest[-2:]
        x_new = jnp.concatenate([x_lo, x_hi], axis=1)
        o_ref[...] = x_new
        h_ref[...] = _modulated_norm(x_new, ng_ref[...], sc_ref[0], sh_ref[0]).astype(h_ref.dtype)
    else:
        o_ref = rest[-1]
        o_ref[:, :half] = x_lo
        o_ref[:, half:] = x_hi


def moe_combine(x, y_part, gates, gate_mod, seq, next_norm=None, part=0, n_parts=1, prev=(), tm=512):
    n, d = x.shape
    per = seq // tm
    steps = n // (tm * n_parts)
    first = part * steps
    rows = lambda width: pl.BlockSpec((tm, width), lambda i: (i + first, 0))
    vec = pl.BlockSpec((1, 1, d), lambda i: ((i + first) // per, 0, 0))
    in_specs = [rows(d), pl.BlockSpec((TOP_K, tm, d // 2), lambda i: (0, i, 0)), rows(LANES), vec]
    args = [x, y_part, gates, gate_mod]
    out_specs, out_shape = [rows(d)], [jax.ShapeDtypeStruct((n, d), F32)]
    if next_norm is not None:
        in_specs += [pl.BlockSpec((1, d), lambda i: (0, 0)), vec, vec]
        args += list(next_norm)
        out_specs, out_shape = out_specs + [rows(d)], out_shape + [jax.ShapeDtypeStruct((n, d), BF16)]
    aliases = {len(args) + j: j for j in range(len(prev))}
    in_specs += [pl.BlockSpec(memory_space=pl.ANY) for _ in prev]
    args += list(prev)
    return pl.pallas_call(
        functools.partial(_combine_kernel, with_norm=next_norm is not None),
        grid=(steps,),
        in_specs=in_specs,
        out_specs=out_specs,
        out_shape=out_shape,
        input_output_aliases=aliases,
        compiler_params=_params("parallel"),
        name="moe_combine",
    )(*args)


def moe_layer(x, mixer, mix_gate, mods, norm_g, router_w, router_b, layer, w1, b1, w2, b2, seq, next_norm):
    n, d = x.shape
    shift, scale, gate = mods
    rw_rows = jnp.pad(router_w.T, ((0, LANES - N_EXPERTS), (0, 0)))
    rw_hi = rw_rows.astype(BF16)
    rw_lo = (rw_rows - rw_hi.astype(F32)).astype(BF16)
    x, h, idx, gates, counts = proj_residual_router(
        *mixer, x, mix_gate, norm_g.reshape(1, d), scale, shift, jnp.concatenate([rw_hi, rw_lo], axis=0), rw_hi,
        router_b.reshape(N_EXPERTS, 1), seq)

    top_i = idx[:TOP_K]
    rank = idx[TOP_K:]
    counts = counts[:, 0]
    experts = jnp.arange(N_EXPERTS, dtype=jnp.int32)
    padded = (counts + MOE_STEP - 1) // MOE_STEP * MOE_STEP
    pad_ends = jnp.sum(jnp.where(experts[:, None] >= experts[None, :], padded[None, :], 0), axis=1)
    pad_starts = pad_ends - padded
    dest = jnp.sum(jnp.where(top_i[..., None] == experts, pad_starts, 0), axis=-1) + rank
    n_blocks = -(-n * TOP_K // MOE_STEP) + N_EXPERTS
    block_start = jnp.arange(n_blocks, dtype=jnp.int32) * MOE_STEP
    block_e = jnp.minimum(jnp.sum((pad_ends[None, :] <= block_start[:, None]).astype(jnp.int32), axis=1),
                          N_EXPERTS - 1)
    n_used = (pad_ends[N_EXPERTS - 1:] // MOE_STEP).astype(jnp.int32)
    seg_end = jnp.sum(jnp.where(block_e[:, None] == experts, pad_starts + counts, 0), axis=1)
    block_valid = jnp.clip(seg_end - block_start, 0, MOE_STEP).astype(jnp.int32)
    dest_flat = dest.reshape(-1)

    xs = sc_scatter_rows(h, dest_flat, n_blocks * MOE_STEP)
    ys = moe_ffn(xs, block_e, block_valid, n_used, layer, w1, b1[layer][:, 0::2], b1[layer][:, 1::2], w2, b2[layer])
    outs = ()
    per_part = n // COMBINE_PARTS
    for part in range(COMBINE_PARTS):
        idx_part = dest[:, part * per_part:(part + 1) * per_part].reshape(-1)
        y = sc_gather_rows(ys, idx_part).reshape(TOP_K, per_part, d // 2)
        outs = moe_combine(x, y, gates, gate, seq, next_norm, part, COMBINE_PARTS, tuple(outs))
    return (outs[0], outs[1]) if next_norm is not None else (outs[0], None)


def _pad_heads(w, width):
    k = w.shape[0]
    w = w.reshape(k, MLA_HEADS, width)
    return jnp.pad(w, ((0, 0), (0, 0), (0, LANES - width))).reshape(k, MLA_HEADS * LANES)


def _head_lane_source():
    half = QK_ROPE_DIM // 2
    first_nope = LANES // 2 - half
    lanes = (list(range(QK_NOPE_DIM, QK_NOPE_DIM + half)) + list(range(first_nope))
             + list(range(QK_NOPE_DIM + half, QK_HEAD_DIM)) + list(range(first_nope, QK_NOPE_DIM)))
    return jnp.array(lanes + [QK_HEAD_DIM] * (LANES - QK_HEAD_DIM), jnp.int32)


def _to_head_lanes(w):
    w = jnp.concatenate([w, jnp.zeros(w.shape[:-1] + (1,), w.dtype)], axis=-1)
    return jnp.take(w, _head_lane_source(), axis=-1)


def _rope_tables(positions):
    inv = 1.0 / (ROPE_THETA ** (jnp.arange(0, QK_ROPE_DIM, 2, dtype=F32) / QK_ROPE_DIM))
    ang = inv[:, None] * positions.reshape(-1).astype(F32)[None, :]
    return jnp.cos(ang), jnp.sin(ang)


def _split_mods(m, batch):
    d = m.shape[-1] // 3
    m = m[:batch]
    return tuple(m[:, None, j * d:(j + 1) * d] for j in range(3))


def pool_mla_mixer(x, h, mods, norm_g, tables, w_in, pool_w, pool_scale, cq_norm_g, w_uq, ckv_norm_g, w_ukv,
                   q_norm_g, k_norm_g, w_out, batch, seq):
    n, d = x.shape
    shift, scale, _ = mods
    rope_cols = _to_head_lanes(jnp.pad(w_in[:, Z_ROPE:], ((0, 0), (QK_NOPE_DIM, 0))))
    w_in_pad = jnp.concatenate([w_in[:, :Z_ROPE], rope_cols], axis=1).astype(BF16)
    if h is None:
        z = norm_matmul(x, norm_g.reshape(1, d), scale, shift, w_in_pad, seq)
    else:
        z = matmul(h, w_in_pad)
    w_ukv_h = w_ukv.reshape(KV_LORA_RANK, MLA_HEADS, QK_NOPE_DIM + V_HEAD_DIM)
    k_nope = jnp.pad(w_ukv_h[:, :, :QK_NOPE_DIM], ((0, 0), (0, 0), (0, QK_ROPE_DIM)))
    wuk_pad = _to_head_lanes(k_nope).reshape(KV_LORA_RANK, MLA_HEADS * LANES).astype(BF16)
    wuv_pad = _pad_heads(w_ukv_h[:, :, QK_NOPE_DIM:].reshape(KV_LORA_RANK, -1), V_HEAD_DIM).astype(BF16)
    wuq_pad = _to_head_lanes(w_uq.reshape(Q_LORA_RANK, MLA_HEADS, QK_HEAD_DIM))
    wuq_pad = wuq_pad.reshape(Q_LORA_RANK, MLA_HEADS * LANES).astype(BF16)
    pad_g = lambda g: _to_head_lanes(g).reshape(1, LANES)
    y_pool, q, k, v = mla_prep(z, pool_w.astype(BF16), pool_scale.reshape(1, -1), cq_norm_g.reshape(1, -1), wuq_pad,
                               ckv_norm_g.reshape(1, -1), wuk_pad, wuv_pad, pad_g(q_norm_g), pad_g(k_norm_g),
                               *tables, batch, seq)
    y_att = flash_attention(q, k, v).reshape(n, MLA_HEADS * V_HEAD_DIM)
    w_out_b = w_out.astype(BF16)
    return [y_pool, y_att], [w_out_b[:POOL_WIDTH], w_out_b[POOL_WIDTH:]], jnp.zeros((1, d), F32)


def conformer_mixer(h, pw1_w, pw1_b, dw_w, dw_b, ln_g, ln_b, pw2_w, pw2_b, seq):
    d = pw2_w.shape[1]
    cd = pw1_w.shape[1] // 2
    pw1 = pw1_w.astype(BF16)
    w_pad = jnp.pad(dw_w, ((0, CONV_HALO - CONV_WIDTH), (0, 0)))
    u = glu_dwconv_ln_silu(h, pw1[:, :cd], pw1[:, cd:], pw1_b[:cd].reshape(1, cd), pw1_b[cd:].reshape(1, cd),
                           w_pad, dw_b.reshape(1, cd), ln_g.reshape(1, cd), ln_b.reshape(1, cd), seq)
    return [u], [pw2_w.astype(BF16)], pw2_b.reshape(1, d)


def kernel(x, c, positions, ada_mix_w, ada_mix_b, norm_mix_g, w_in, pool_w, pool_scale, cq_norm_g, w_uq,
           ckv_norm_g, w_ukv, q_norm_g, k_norm_g, w_out, conv_pw1_w, conv_pw1_b, conv_dw_w, conv_dw_b,
           conv_ln_g, conv_ln_b, conv_pw2_w, conv_pw2_b, ada_ffn_w, ada_ffn_b, norm_ffn_g, router_w,
           router_b, moe_w1, moe_b1, moe_w2, moe_b2):
    batch, seq, d = x.shape
    depth = ada_mix_w.shape[0]
    c_pad = jnp.pad(c, ((0, 8 - batch), (0, 0)))
    mix_mods = adaln(c_pad, ada_mix_w, ada_mix_b)
    ffn_mods = adaln(c_pad, ada_ffn_w, ada_ffn_b)
    tables = _rope_tables(positions)
    xf = x.reshape(batch * seq, d)
    h = None
    for layer in range(depth):
        i = layer // 2
        mods = _split_mods(mix_mods[layer], batch)
        if layer % 2 == 0:
            mixer = pool_mla_mixer(xf, h, mods, norm_mix_g[layer], tables, w_in[i], pool_w[i], pool_scale[i],
                                   cq_norm_g[i], w_uq[i], ckv_norm_g[i], w_ukv[i], q_norm_g[i], k_norm_g[i],
                                   w_out[i], batch, seq)
        else:
            mixer = conformer_mixer(h, conv_pw1_w[i], conv_pw1_b[i], conv_dw_w[i], conv_dw_b[i],
                                    conv_ln_g[i], conv_ln_b[i], conv_pw2_w[i], conv_pw2_b[i], seq)
        next_norm = None
        if layer + 1 < depth:
            n_shift, n_scale, _ = _split_mods(mix_mods[layer + 1], batch)
            next_norm = (norm_mix_g[layer + 1].reshape(1, d), n_scale, n_shift)
        xf, h = moe_layer(xf, mixer, mods[2], _split_mods(ffn_mods[layer], batch), norm_ffn_g[layer],
                          router_w[layer], router_b[layer], layer, moe_w1, moe_b1, moe_w2, moe_b2, seq, next_norm)
    return xf.reshape(batch, seq, d)
```

```python
import functools

import jax
import jax.numpy as jnp
from jax import lax
from jax.experimental import pallas as pl
from jax.experimental.pallas import tpu as pltpu
from jax.experimental.pallas import tpu_sc as plsc

F32 = jnp.float32
BF16 = jnp.bfloat16
HIGHEST = lax.Precision.HIGHEST

EPS = 1e-6
POOL_WINDOWS = (2, 4, 8, 16)
POOL_GROUP_DIM = 128
POOL_WIDTH = POOL_GROUP_DIM * len(POOL_WINDOWS)
MLA_HEADS = 8
QK_NOPE_DIM = 64
QK_ROPE_DIM = 32
QK_HEAD_DIM = QK_NOPE_DIM + QK_ROPE_DIM
V_HEAD_DIM = 64
Q_LORA_RANK = 384
KV_LORA_RANK = 256
ROPE_THETA = 10000.0
CHUNK = 64
CONV_WIDTH = 31
N_EXPERTS = 32
TOP_K = 4
SWIGLU_ALPHA = 1.702
SWIGLU_LIMIT = 7.0
MOE_BLOCK = 256
MOE_CHUNK = 512
MOE_STEP = 1024

LANES = 128
SUBLANES = 8
CONV_PATCH_ROWS = 64
CONV_PATCH_COLS = 256
POOL_HALO = 16
CONV_HALO = 32
MASK_VALUE = -1e30
LOG2_E = 1.4426950408889634
VMEM_LIMIT = 52 * 1024 * 1024
SC_CORES = 2
SC_SUBCORES = 16
SC_WORKERS = SC_CORES * SC_SUBCORES
SC_CHUNK = 64
COMBINE_PARTS = 2

Z_CQ = POOL_WIDTH
Z_CKV = Z_CQ + Q_LORA_RANK
Z_ROPE = Z_CKV + KV_LORA_RANK
Z_WIDTH = Z_ROPE + LANES


def _params(*sem, vmem=None):
    return pltpu.CompilerParams(dimension_semantics=sem, vmem_limit_bytes=vmem or VMEM_LIMIT,
                                disable_bounds_checks=True)


def _adaln_kernel(c_ref, w_ref, b_ref, o_ref):
    c = c_ref[...]
    s = c * jax.nn.sigmoid(c)
    o_ref[0] = jnp.dot(s, w_ref[0], preferred_element_type=F32, precision=HIGHEST) + b_ref[0]


def adaln(c_pad, w, b):
    n_l, d, d3 = w.shape
    tn = d3 // 2
    return pl.pallas_call(
        _adaln_kernel,
        grid=(n_l, d3 // tn),
        in_specs=[pl.BlockSpec((8, d), lambda l, j: (0, 0)),
                  pl.BlockSpec((1, d, tn), lambda l, j: (l, 0, j)),
                  pl.BlockSpec((1, 1, tn), lambda l, j: (l, 0, j))],
        out_specs=pl.BlockSpec((1, 8, tn), lambda l, j: (l, 0, j)),
        out_shape=jax.ShapeDtypeStruct((n_l, 8, d3), F32),
        compiler_params=_params("parallel", "parallel"),
        name="adaln",
    )(c_pad, w, b.reshape(n_l, 1, d3))


def _modulated_norm(x, g, scale, shift):
    ms = jnp.mean(x * x, axis=-1, keepdims=True)
    return x * lax.rsqrt(ms + EPS) * g * (1.0 + scale) + shift


def _pack_halves(x):
    w = x.shape[1] // 2
    lo = lax.bitcast_convert_type(x[:, :w].astype(BF16).astype(F32), jnp.uint32)
    hi = lax.bitcast_convert_type(x[:, w:].astype(BF16).astype(F32), jnp.uint32)
    return lax.bitcast_convert_type((lo >> 16) | (hi & jnp.uint32(0xFFFF0000)), jnp.int32)


def _unpack_halves(p):
    u = lax.bitcast_convert_type(p, jnp.uint32)
    lo = lax.bitcast_convert_type(u << 16, F32)
    hi = lax.bitcast_convert_type(u & jnp.uint32(0xFFFF0000), F32)
    return lo, hi


def _norm_matmul_kernel(x_ref, g_ref, sc_ref, sh_ref, w_ref, o_ref):
    h = _modulated_norm(x_ref[...], g_ref[...], sc_ref[0], sh_ref[0])
    o_ref[...] = jnp.dot(h.astype(BF16), w_ref[...], preferred_element_type=F32)


def norm_matmul(x, g, scale, shift, w, seq, tm=512):
    n, d = x.shape
    m = w.shape[1]
    per = seq // tm
    vec = pl.BlockSpec((1, 1, d), lambda i: (i // per, 0, 0))
    return pl.pallas_call(
        _norm_matmul_kernel,
        grid=(n // tm,),
        in_specs=[pl.BlockSpec((tm, d), lambda i: (i, 0)), pl.BlockSpec((1, d), lambda i: (0, 0)), vec, vec,
                  pl.BlockSpec((d, m), lambda i: (0, 0))],
        out_specs=pl.BlockSpec((tm, m), lambda i: (i, 0)),
        out_shape=jax.ShapeDtypeStruct((n, m), F32),
        compiler_params=_params("parallel"),
        name="norm_matmul",
    )(x, g, scale, shift, w)


def _matmul_kernel(a_ref, w_ref, o_ref):
    o_ref[...] = jnp.dot(a_ref[...], w_ref[...], preferred_element_type=F32).astype(o_ref.dtype)


def matmul(a, w, out_dtype=F32, tm=512):
    n, k = a.shape
    m = w.shape[1]
    return pl.pallas_call(
        _matmul_kernel,
        grid=(n // tm,),
        in_specs=[pl.BlockSpec((tm, k), lambda i: (i, 0)), pl.BlockSpec((k, m), lambda i: (0, 0))],
        out_specs=pl.BlockSpec((tm, m), lambda i: (i, 0)),
        out_shape=jax.ShapeDtypeStruct((n, m), out_dtype),
        compiler_params=_params("parallel"),
        name="matmul",
    )(a, w)


def _proj_residual_router_kernel(*refs, n_in, tm):
    a_refs, w_refs = refs[:n_in], refs[n_in:2 * n_in]
    (b_ref, x_ref, gate_ref, g_ref, sc_ref, sh_ref, rw2_ref, rwh_ref, rb_ref,
     o_ref, h_ref, idx_ref, gates_ref, cnt_ref, carry_ref) = refs[2 * n_in:]

    @pl.when(pl.program_id(0) == 0)
    def _():
        carry_ref[...] = jnp.zeros(carry_ref.shape, F32)

    acc = b_ref[...]
    for a_ref, w_ref in zip(a_refs, w_refs):
        acc = acc + jnp.dot(a_ref[...], w_ref[...], preferred_element_type=F32)
    x_new = x_ref[...] + gate_ref[0] * acc
    o_ref[...] = x_new
    h = _modulated_norm(x_new, g_ref[...], sc_ref[0], sh_ref[0])
    h_ref[...] = _pack_halves(h)
    h_hi = h.astype(BF16)
    h_lo = (h - h_hi.astype(F32)).astype(BF16)
    contract_features = (((1,), (1,)), ((), ()))
    both = lax.dot_general(rw2_ref[...], h_hi, contract_features, preferred_element_type=F32)
    low = lax.dot_general(rwh_ref[...], h_lo, contract_features, preferred_element_type=F32)
    logits = (both[:LANES] + both[LANES:] + low)[:N_EXPERTS] + rb_ref[...]
    idx, gates, total = _route_tile(logits, carry_ref[:, 0:1], tm)
    idx_ref[...] = idx.astype(jnp.int32)
    gate_rows = jnp.concatenate([gates, jnp.zeros((LANES - TOP_K, tm), F32)], axis=0)
    gates_ref[...] = jnp.transpose(gate_rows)
    carry_ref[...] = jnp.broadcast_to(total, carry_ref.shape)
    cnt_ref[...] = jnp.broadcast_to(total, cnt_ref.shape).astype(jnp.int32)


def proj_residual_router(a_list, w_list, bias, x, gate, g, scale, shift, rw_both, rw_hi, rb_pad, seq, tm=1024):
    n, d = x.shape
    per = seq // tm
    n_in = len(a_list)
    rows = lambda width: pl.BlockSpec((tm, width), lambda i: (i, 0))
    const = lambda a: pl.BlockSpec(a.shape, lambda i: (0, 0))
    vec = pl.BlockSpec((1, 1, d), lambda i: (i // per, 0, 0))
    in_specs = [rows(a.shape[1]) for a in a_list] + [const(w) for w in w_list]
    in_specs += [const(bias), rows(d), vec, const(g), vec, vec, const(rw_both), const(rw_hi), const(rb_pad)]
    return pl.pallas_call(
        functools.partial(_proj_residual_router_kernel, n_in=n_in, tm=tm),
        grid=(n // tm,),
        in_specs=in_specs,
        out_specs=[rows(d), rows(d // 2), pl.BlockSpec((2 * TOP_K, tm), lambda i: (0, i)), rows(LANES),
                   pl.BlockSpec((N_EXPERTS, LANES), lambda i: (0, 0))],
        out_shape=[jax.ShapeDtypeStruct((n, d), F32), jax.ShapeDtypeStruct((n, d // 2), jnp.int32),
                   jax.ShapeDtypeStruct((2 * TOP_K, n), jnp.int32), jax.ShapeDtypeStruct((n, LANES), F32),
                   jax.ShapeDtypeStruct((N_EXPERTS, LANES), jnp.int32)],
        scratch_shapes=[pltpu.VMEM((N_EXPERTS, LANES), F32)],
        compiler_params=_params("arbitrary"),
        name="proj_residual_router",
    )(*a_list, *w_list, bias, x, gate, g, scale, shift, rw_both, rw_hi, rb_pad)


def _rope(xn, cos_t, sin_t):
    return xn * cos_t + pltpu.roll(xn, LANES // 2, axis=1) * sin_t


def _mla_prep_kernel(z_ref, halo_ref, pw_ref, ps_ref, cqg_ref, wuq_ref, ckvg_ref, wuk_ref, wuv_ref,
                     qg_ref, kg_ref, cos_ref, sin_ref,
                     yp_ref, q_ref, k_ref, v_ref, ext_ref, *, tm, per):
    si = pl.program_id(0) % per
    u = z_ref[:, 0:POOL_WIDTH]
    ext_ref[0:POOL_HALO, :] = jnp.where(si == 0, 0.0, halo_ref[...])
    ext_ref[POOL_HALO:, :] = u
    t = si * tm + lax.broadcasted_iota(jnp.int32, (tm, 1), 0)
    for g, w in enumerate(POOL_WINDOWS):
        cols = slice(g * POOL_GROUP_DIM, (g + 1) * POOL_GROUP_DIM)
        ug = u[:, cols]
        s = ug
        for j in range(1, w):
            s = s + ext_ref[POOL_HALO - j:POOL_HALO - j + tm, cols]
        cnt = jnp.minimum(t + 1, w).astype(F32)
        pooled = s / cnt - ug
        yp = jnp.dot(pooled.astype(BF16), pw_ref[g], preferred_element_type=F32) * ps_ref[:, cols]
        yp_ref[:, cols] = yp.astype(yp_ref.dtype)

    half = QK_ROPE_DIM // 2

    def to_lanes(t_ref):
        rows = jnp.concatenate([t_ref[...], jnp.zeros((LANES - half, tm), F32)], axis=0)
        return jnp.transpose(rows)

    cos_lo, sin_lo = to_lanes(cos_ref), to_lanes(sin_ref)
    lane = lax.broadcasted_iota(jnp.int32, (tm, LANES), 1)
    nope = ((lane >= half) & (lane < LANES // 2)) | ((lane >= LANES // 2 + half) & (lane < QK_HEAD_DIM))
    cos_t = cos_lo + pltpu.roll(cos_lo, LANES // 2, axis=1) + jnp.where(nope, 1.0, 0.0)
    sin_t = pltpu.roll(sin_lo, LANES // 2, axis=1) - sin_lo
    inv_head = 1.0 / QK_HEAD_DIM
    ones_block = jnp.ones((LANES, LANES), BF16)

    def head_sum_sq(xh):
        sq = xh * xh
        hi = sq.astype(BF16)
        lo = (sq - hi.astype(F32)).astype(BF16)
        return (jnp.dot(hi, ones_block, preferred_element_type=F32)
                + jnp.dot(lo, ones_block, preferred_element_type=F32))

    cq = z_ref[:, Z_CQ:Z_CKV]
    cqn = cq * lax.rsqrt(jnp.mean(cq * cq, axis=-1, keepdims=True) + EPS) * cqg_ref[...]
    qf = jnp.dot(cqn.astype(BF16), wuq_ref[...], preferred_element_type=F32)
    q_scale = QK_HEAD_DIM ** -0.5 * LOG2_E
    for h in range(MLA_HEADS):
        qh = qf[:, h * LANES:(h + 1) * LANES]
        ss = head_sum_sq(qh) * inv_head
        qn = qh * lax.rsqrt(ss + EPS) * qg_ref[...]
        q_ref[0, h] = (_rope(qn, cos_t, sin_t) * q_scale).astype(q_ref.dtype)

    ckv = z_ref[:, Z_CKV:Z_ROPE]
    ckvn = (ckv * lax.rsqrt(jnp.mean(ckv * ckv, axis=-1, keepdims=True) + EPS) * ckvg_ref[...]).astype(BF16)
    kf = jnp.dot(ckvn, wuk_ref[...], preferred_element_type=F32)
    vf = jnp.dot(ckvn, wuv_ref[...], preferred_element_type=F32)
    k_rope = z_ref[:, Z_ROPE:Z_WIDTH]
    ones_lane = lax.broadcasted_iota(jnp.int32, (tm, LANES), 1) == V_HEAD_DIM
    for h in range(MLA_HEADS):
        kh = kf[:, h * LANES:(h + 1) * LANES] + k_rope
        ss = head_sum_sq(kh) * inv_head
        kn = kh * lax.rsqrt(ss + EPS) * kg_ref[...]
        k_ref[0, h] = _rope(kn, cos_t, sin_t).astype(k_ref.dtype)
        v_ref[0, h] = jnp.where(ones_lane, 1.0, vf[:, h * LANES:(h + 1) * LANES]).astype(v_ref.dtype)


def mla_prep(z, pool_w, pool_scale, cq_g, wuq_pad, ckv_g, wuk_pad, wuv_pad, qg_pad, kg_pad,
             cos_t, sin_t, batch, seq, tm=512):
    n = z.shape[0]
    per = seq // tm
    hb = tm // POOL_HALO
    full = lambda a: pl.BlockSpec(a.shape, lambda i: (0,) * a.ndim)
    tab = pl.BlockSpec((QK_ROPE_DIM // 2, tm), lambda i: (0, i))
    head_out = pl.BlockSpec((1, MLA_HEADS, tm, LANES), lambda i: (i // per, 0, i % per, 0))
    head_shape = jax.ShapeDtypeStruct((batch, MLA_HEADS, seq, LANES), BF16)
    return pl.pallas_call(
        functools.partial(_mla_prep_kernel, tm=tm, per=per),
        grid=(n // tm,),
        in_specs=[pl.BlockSpec((tm, Z_WIDTH), lambda i: (i, 0)),
                  pl.BlockSpec((POOL_HALO, POOL_WIDTH), lambda i: (jnp.maximum(i * hb - 1, 0), 0)),
                  full(pool_w), full(pool_scale), full(cq_g), full(wuq_pad), full(ckv_g), full(wuk_pad),
                  full(wuv_pad), full(qg_pad), full(kg_pad), tab, tab],
        out_specs=[pl.BlockSpec((tm, POOL_WIDTH), lambda i: (i, 0)), head_out, head_out, head_out],
        out_shape=[jax.ShapeDtypeStruct((n, POOL_WIDTH), BF16), head_shape, head_shape, head_shape],
        scratch_shapes=[pltpu.VMEM((tm + POOL_HALO, POOL_WIDTH), F32)],
        compiler_params=_params("parallel"),
        name="mla_prep",
    )(z, z, pool_w, pool_scale, cq_g, wuq_pad, ckv_g, wuk_pad, wuv_pad, qg_pad, kg_pad, cos_t, sin_t)


def _flash_kernel(q_ref, k_ref, v_ref, o_ref, m_ref, acc_ref, *, tq, tk, td, heads):
    qi = pl.program_id(2)
    q_chunk = lax.broadcasted_iota(jnp.int32, (tq, td), 0) // CHUNK
    k_chunk = lax.broadcasted_iota(jnp.int32, (tq, td), 1) // CHUNK
    m_ref[...] = jnp.full(m_ref.shape, MASK_VALUE, F32)
    acc_ref[...] = jnp.zeros(acc_ref.shape, F32)

    def step(start, width, mask, row0=0):
        for hh in range(heads):
            k = k_ref[0, hh, pl.ds(start, width), :]
            v = v_ref[0, hh, pl.ds(start, width), :]
            s = lax.dot_general(q_ref[0, hh, row0:, :], k, (((1,), (1,)), ((), ())), preferred_element_type=F32)
            if mask is not None:
                s = jnp.where(mask[row0:], s, MASK_VALUE)
            cols_s = [s[:, c * LANES:(c + 1) * LANES] for c in range(width // LANES)]
            s_max = cols_s[0]
            for sc in cols_s[1:]:
                s_max = jnp.maximum(s_max, sc)
            m_prev = m_ref[hh, row0:, :]
            m_new = jnp.maximum(m_prev, jnp.max(s_max, axis=-1, keepdims=True))
            alpha = jnp.exp2(m_prev - m_new)
            p = jnp.concatenate([jnp.exp2(sc - m_new) for sc in cols_s], axis=1).astype(v.dtype)
            acc_ref[hh, row0:, :] = alpha * acc_ref[hh, row0:, :] + jnp.dot(p, v, preferred_element_type=F32)
            m_ref[hh, row0:, :] = m_new

    def full_step(j, carry):
        step(pl.multiple_of(j * tk, tk), tk, None)
        return carry

    lax.fori_loop(0, (tq // tk) * qi, full_step, 0)
    for dd in range(tq // td):
        step(pl.multiple_of(qi * tq + dd * td, td), td, k_chunk + dd * (td // CHUNK) <= q_chunk, row0=dd * td)
    outs = []
    for hh in range(heads):
        acc = acc_ref[hh]
        outs.append((acc / acc[:, V_HEAD_DIM:V_HEAD_DIM + 1])[:, :V_HEAD_DIM])
    o_ref[0] = jnp.concatenate(outs, axis=-1).astype(o_ref.dtype)


def flash_attention(q, k, v, tq=1024, tk=1024, td=512):
    b, h, s, _ = q.shape
    heads = LANES // V_HEAD_DIM
    return pl.pallas_call(
        functools.partial(_flash_kernel, tq=tq, tk=tk, td=td, heads=heads),
        grid=(b, h // heads, s // tq),
        in_specs=[pl.BlockSpec((1, heads, tq, LANES), lambda bi, hi, qi: (bi, hi, qi, 0)),
                  pl.BlockSpec((1, heads, s, LANES), lambda bi, hi, qi: (bi, hi, 0, 0)),
                  pl.BlockSpec((1, heads, s, LANES), lambda bi, hi, qi: (bi, hi, 0, 0))],
        out_specs=pl.BlockSpec((1, tq, LANES), lambda bi, hi, qi: (bi, qi, hi)),
        out_shape=jax.ShapeDtypeStruct((b, s, h * V_HEAD_DIM), BF16),
        scratch_shapes=[pltpu.VMEM((heads, tq, LANES), F32) for _ in range(2)],
        compiler_params=_params("parallel", "parallel", "parallel"),
        name="flash_attention",
    )(q, k, v)


def _glu_conv_kernel(h0_ref, hn_ref, wv_ref, wg_ref, bv_ref, bg_ref, w_ref, b_ref, g_ref, beta_ref, o_ref,
                     ext_ref, sh_ref, conv_ref, nxt_ref, *, tm, per):
    i = pl.program_id(0)
    d = o_ref.shape[1]

    def glu(h):
        val = jnp.dot(h, wv_ref[...], preferred_element_type=F32) + bv_ref[...]
        gt = jnp.dot(h, wg_ref[...], preferred_element_type=F32) + bg_ref[...]
        return val * jax.nn.sigmoid(gt)

    @pl.when(i == 0)
    def _():
        ext_ref[CONV_HALO:, :] = glu(h0_ref[...])

    @pl.when(i > 0)
    def _():
        ext_ref[0:CONV_HALO, :] = ext_ref[tm:tm + CONV_HALO, :]
        ext_ref[CONV_HALO:, :] = nxt_ref[...]

    @pl.when(i % per == 0)
    def _():
        ext_ref[0:CONV_HALO, :] = jnp.zeros((CONV_HALO, d), F32)

    nxt_ref[...] = glu(hn_ref[...])
    span = sh_ref.shape[1]
    for b in range(1, SUBLANES):
        sh_ref[b - 1] = ext_ref[b:b + span, :]
    first = CONV_HALO - (CONV_WIDTH - 1)
    for r0 in range(0, tm, CONV_PATCH_ROWS):
        for c0 in range(0, d, CONV_PATCH_COLS):
            cols = slice(c0, c0 + CONV_PATCH_COLS)
            view = (CONV_PATCH_ROWS // SUBLANES, SUBLANES, CONV_PATCH_COLS)
            patch = jnp.broadcast_to(b_ref[:, cols], view)
            for j in range(CONV_WIDTH):
                a, b = divmod(first + j, SUBLANES)
                rows = slice(SUBLANES * a + r0, SUBLANES * a + r0 + CONV_PATCH_ROWS)
                win = ext_ref[rows, cols] if b == 0 else sh_ref[b - 1, rows, cols]
                tap = jnp.broadcast_to(w_ref[j:j + 1, cols], view[1:])
                patch = patch + tap[None] * win.reshape(view)
            conv_ref[r0:r0 + CONV_PATCH_ROWS, cols] = patch.reshape(CONV_PATCH_ROWS, CONV_PATCH_COLS)
    acc = conv_ref[...]
    mu = jnp.mean(acc, axis=-1, keepdims=True)
    cen = acc - mu
    var = jnp.mean(cen * cen, axis=-1, keepdims=True)
    y = cen * lax.rsqrt(var + EPS) * g_ref[...] + beta_ref[...]
    o_ref[...] = (y * jax.nn.sigmoid(y)).astype(o_ref.dtype)


def glu_dwconv_ln_silu(h, wv, wg, bv, bg, w_pad, b, g, beta, seq, tm=256):
    n, k = h.shape
    d = wv.shape[1]
    per = seq // tm
    last = n // tm - 1
    full = lambda a: pl.BlockSpec(a.shape, lambda i: (0, 0))
    return pl.pallas_call(
        functools.partial(_glu_conv_kernel, tm=tm, per=per),
        grid=(n // tm,),
        in_specs=[pl.BlockSpec((tm, k), lambda i: (0, 0)),
                  pl.BlockSpec((tm, k), lambda i: (jnp.minimum(i + 1, last), 0)),
                  full(wv), full(wg), full(bv), full(bg), full(w_pad), full(b), full(g), full(beta)],
        out_specs=pl.BlockSpec((tm, d), lambda i: (i, 0)),
        out_shape=jax.ShapeDtypeStruct((n, d), BF16),
        scratch_shapes=[pltpu.VMEM((tm + CONV_HALO, d), F32),
                        pltpu.VMEM((SUBLANES - 1, tm + CONV_HALO - SUBLANES, d), F32),
                        pltpu.VMEM((tm, d), F32), pltpu.VMEM((tm, d), F32)],
        compiler_params=_params("arbitrary"),
        name="glu_dwconv_ln_silu",
    )(h, h, wv, wg, bv, bg, w_pad, b, g, beta)


def _route_tile(logits, before_tile, tm):
    row_f = lax.broadcasted_iota(jnp.int32, (N_EXPERTS, tm), 0).astype(F32)
    neg = -jnp.inf
    picks, vals, ids = [], [], []
    for _ in range(TOP_K):
        mx = jnp.max(logits, axis=0, keepdims=True)
        idx = jnp.min(jnp.where(logits == mx, row_f, float(N_EXPERTS)), axis=0, keepdims=True)
        pick = row_f == idx
        picks.append(pick)
        vals.append(mx)
        ids.append(idx)
        logits = jnp.where(pick, neg, logits)
    exps = [jnp.exp(v - vals[0]) for v in vals]
    den = exps[0]
    for e in exps[1:]:
        den = den + e

    chosen = jnp.zeros((N_EXPERTS, tm), F32)
    for pick in picks:
        chosen = chosen + pick.astype(F32)
    r_io = lax.broadcasted_iota(jnp.int32, (tm, tm), 0)
    c_io = lax.broadcasted_iota(jnp.int32, (tm, tm), 1)
    later = (r_io < c_io).astype(BF16)
    before = jnp.dot(chosen.astype(BF16), later, preferred_element_type=F32) + before_tile
    ranks = [jnp.sum(jnp.where(pick, before, 0.0), axis=0, keepdims=True) for pick in picks]
    idx_out = jnp.concatenate(ids + ranks, axis=0)
    gate_out = jnp.concatenate([e / den for e in exps], axis=0)
    total = before_tile + jnp.sum(chosen, axis=1, keepdims=True)
    return idx_out, gate_out, total


def _sc_worker_base(per_worker):
    return (lax.axis_index("s") * SC_CORES + lax.axis_index("c")) * per_worker


def sc_scatter_rows(src, dest_flat, n_rows):
    n, d = src.shape
    per_w = n // SC_WORKERS
    n_chunks = per_w // SC_CHUNK
    mesh = plsc.VectorSubcoreMesh(core_axis_name="c", subcore_axis_name="s")

    @functools.partial(
        pl.kernel, out_type=jax.ShapeDtypeStruct((n_rows, d), src.dtype), mesh=mesh,
        scratch_types=[pltpu.VMEM((per_w,), jnp.int32) for _ in range(TOP_K)]
        + [pltpu.VMEM((SC_CHUNK, d), src.dtype) for _ in range(2)] + [pltpu.SemaphoreType.DMA] * 4,
        name="sc_scatter_rows")
    def scatter(src_hbm, dest_hbm, out_hbm, *scratch):
        idx_refs = scratch[:TOP_K]
        bufs = scratch[TOP_K:TOP_K + 2]
        in_sems, out_sems = scratch[TOP_K + 2:TOP_K + 4], scratch[TOP_K + 4:TOP_K + 6]
        base = _sc_worker_base(per_w)
        for k, idx_ref in enumerate(idx_refs):
            pltpu.sync_copy(dest_hbm.at[pl.ds(k * n + base, per_w)], idx_ref)

        def load(j, slot):
            return pltpu.make_async_copy(src_hbm.at[pl.ds(base + j * SC_CHUNK, SC_CHUNK)], bufs[slot], in_sems[slot])

        def store_all(j, slot):
            copies = [pltpu.make_async_copy(bufs[slot], out_hbm.at[idx_ref.at[pl.ds(j * SC_CHUNK, SC_CHUNK)]],
                                            out_sems[slot]) for idx_ref in idx_refs]
            for cp in copies:
                cp.start()
            for cp in copies:
                cp.wait()

        load(0, 0).start()

        @pl.loop(0, n_chunks // 2)
        def _(p):
            j = 2 * p
            load(j + 1, 1).start()
            load(j, 0).wait()
            store_all(j, 0)

            @pl.when(j + 2 < n_chunks)
            def _():
                load(j + 2, 0).start()

            load(j + 1, 1).wait()
            store_all(j + 1, 1)

    return scatter(src, dest_flat)


def sc_gather_rows(table, idx):
    b = idx.shape[0]
    d = table.shape[1]
    per_w = b // SC_WORKERS
    n_chunks = per_w // SC_CHUNK
    mesh = plsc.VectorSubcoreMesh(core_axis_name="c", subcore_axis_name="s")

    @functools.partial(
        pl.kernel, out_type=jax.ShapeDtypeStruct((b, d), table.dtype), mesh=mesh,
        scratch_types=[pltpu.VMEM((per_w,), jnp.int32)] + [pltpu.VMEM((SC_CHUNK, d), table.dtype) for _ in range(2)]
        + [pltpu.SemaphoreType.DMA] * 4,
        name="sc_gather_rows")
    def gather(table_hbm, idx_hbm, out_hbm, idx_ref, buf0, buf1, gsem0, gsem1, osem0, osem1):
        bufs, in_sems, out_sems = (buf0, buf1), (gsem0, gsem1), (osem0, osem1)
        base = _sc_worker_base(per_w)
        pltpu.sync_copy(idx_hbm.at[pl.ds(base, per_w)], idx_ref)

        def fetch(j, slot):
            return pltpu.make_async_copy(table_hbm.at[idx_ref.at[pl.ds(j * SC_CHUNK, SC_CHUNK)]], bufs[slot],
                                         in_sems[slot])

        def store(j, slot):
            return pltpu.make_async_copy(bufs[slot], out_hbm.at[pl.ds(base + j * SC_CHUNK, SC_CHUNK)],
                                         out_sems[slot])

        fetch(0, 0).start()

        @pl.loop(0, n_chunks // 2)
        def _(p):
            j = 2 * p

            @pl.when(p > 0)
            def _():
                store(j - 1, 1).wait()

            fetch(j + 1, 1).start()
            fetch(j, 0).wait()
            store(j, 0).start()
            fetch(j + 1, 1).wait()
            store(j, 0).wait()

            @pl.when(j + 2 < n_chunks)
            def _():
                fetch(j + 2, 0).start()

            store(j + 1, 1).start()

        store(n_chunks - 1, 1).wait()

    return gather(table, idx)


def _ffn_kernel(be_ref, nv_ref, nu_ref, xs_ref, w1_ref, b1g_ref, b1u_ref, w2_ref, b2_ref, y_ref,
                wg_ref, wu_ref, w2s_ref):
    i = pl.program_id(0)
    pair = 2 * LANES

    @pl.when(i < nu_ref[0])
    def _():
        e = be_ref[i]
        prev = be_ref[jnp.maximum(i - 1, 0)]

        @pl.when((i == 0) | (e != prev))
        def _():
            r_io = lax.broadcasted_iota(jnp.int32, (pair, pair), 0)
            c_io = lax.broadcasted_iota(jnp.int32, (pair, pair), 1)
            want = jnp.where(c_io < LANES, 2 * c_io, 2 * (c_io - LANES) + 1)
            sel = (r_io == want).astype(BF16)
            for c in range(wg_ref.shape[1] // LANES):
                slab = w1_ref[0, 0, :, c * pair:(c + 1) * pair].astype(BF16)
                split = jnp.dot(slab, sel, preferred_element_type=F32)
                wg_ref[:, c * LANES:(c + 1) * LANES] = split[:, :LANES].astype(BF16)
                wu_ref[:, c * LANES:(c + 1) * LANES] = split[:, LANES:].astype(BF16)
            w2s_ref[...] = w2_ref[0, 0].astype(BF16)

        nvalid = nv_ref[i]

        def ffn_rows(r0, rows):
            row = r0 + lax.broadcasted_iota(jnp.int32, (rows, xs_ref.shape[1]), 0)
            lo, hi = _unpack_halves(jnp.where(row < nvalid, xs_ref[r0:r0 + rows, :], 0))
            x = jnp.concatenate([lo, hi], axis=1).astype(BF16)
            hg = jnp.dot(x, wg_ref[...], preferred_element_type=F32) + b1g_ref[0]
            hu = jnp.dot(x, wu_ref[...], preferred_element_type=F32) + b1u_ref[0]
            gate = jnp.minimum(hg, SWIGLU_LIMIT)
            up = jnp.clip(hu, -SWIGLU_LIMIT, SWIGLU_LIMIT)
            act = gate * jax.nn.sigmoid(SWIGLU_ALPHA * gate) * (up + 1.0)
            y = jnp.dot(act.astype(BF16), w2s_ref[...], preferred_element_type=F32) + b2_ref[0]
            y_ref[r0:r0 + rows, :] = _pack_halves(y)

        def zero_rows(r0, rows):
            y_ref[r0:r0 + rows, :] = jnp.zeros((rows, y_ref.shape[1]), y_ref.dtype)

        for r0 in range(0, y_ref.shape[0], MOE_CHUNK):
            left = nvalid - r0

            @pl.when(left > MOE_BLOCK)
            def _(r0=r0):
                ffn_rows(r0, MOE_CHUNK)

            @pl.when((left > 0) & (left <= MOE_BLOCK))
            def _(r0=r0):
                ffn_rows(r0, MOE_BLOCK)
                zero_rows(r0 + MOE_BLOCK, MOE_CHUNK - MOE_BLOCK)

            @pl.when(left <= 0)
            def _(r0=r0):
                zero_rows(r0, MOE_CHUNK)

    @pl.when(i >= nu_ref[0])
    def _():
        y_ref[...] = jnp.zeros(y_ref.shape, y_ref.dtype)


def moe_ffn(xs, block_e, block_valid, n_used, layer, w1, b1g, b1u, w2, b2):
    n_rows, half = xs.shape
    d = 2 * half
    n_blocks = n_rows // MOE_STEP
    n_exp, f2 = w1.shape[1], w1.shape[3]
    f = f2 // 2
    rows = lambda i, be, nv, nu: (jnp.minimum(i, nu[0] - 1), 0)
    vec = lambda width: pl.BlockSpec((1, 1, width), lambda i, be, nv, nu: (be[i], 0, 0))
    grid_spec = pltpu.PrefetchScalarGridSpec(
        num_scalar_prefetch=3,
        grid=(n_blocks,),
        in_specs=[
            pl.BlockSpec((MOE_STEP, half), rows),
            pl.BlockSpec((1, 1, d, f2), lambda i, be, nv, nu: (layer, be[i], 0, 0)),
            vec(f), vec(f),
            pl.BlockSpec((1, 1, f, d), lambda i, be, nv, nu: (layer, be[i], 0, 0)),
            vec(d),
        ],
        out_specs=pl.BlockSpec((MOE_STEP, half), lambda i, be, nv, nu: (i, 0)),
        scratch_shapes=[pltpu.VMEM((d, f), BF16), pltpu.VMEM((d, f), BF16), pltpu.VMEM((f, d), BF16)],
    )
    return pl.pallas_call(
        _ffn_kernel,
        grid_spec=grid_spec,
        out_shape=jax.ShapeDtypeStruct((n_rows, half), jnp.int32),
        compiler_params=_params("arbitrary"),
        name="moe_ffn",
    )(block_e, block_valid, n_used, xs, w1, b1g.reshape(n_exp, 1, f), b1u.reshape(n_exp, 1, f), w2,
      b2.reshape(n_exp, 1, d))


def _combine_kernel(x_ref, y_ref, gates_ref, gate_ref, *rest, with_norm):
    g = gates_ref[...]
    half = y_ref.shape[2]
    acc_lo = acc_hi = None
    for k in range(TOP_K):
        lo, hi = _unpack_halves(y_ref[k])
        gk = g[:, k:k + 1]
        acc_lo = gk * lo if acc_lo is None else acc_lo + gk * lo
        acc_hi = gk * hi if acc_hi is None else acc_hi + gk * hi
    x_lo = x_ref[:, :half] + gate_ref[0, :, :half] * acc_lo
    x_hi = x_ref[:, half:] + gate_ref[0, :, half:] * acc_hi
    if with_norm:
        ng_ref, sc_ref, sh_ref = rest[:3]
        o_ref, h_ref = rest[-2:]
        x_new = jnp.concatenate([x_lo, x_hi], axis=1)
        o_ref[...] = x_new
        h_ref[...] = _modulated_norm(x_new, ng_ref[...], sc_ref[0], sh_ref[0]).astype(h_ref.dtype)
    else:
        o_ref = rest[-1]
        o_ref[:, :half] = x_lo
        o_ref[:, half:] = x_hi


def moe_combine(x, y_part, gates, gate_mod, seq, next_norm=None, part=0, n_parts=1, prev=(), tm=512):
    n, d = x.shape
    per = seq // tm
    steps = n // (tm * n_parts)
    first = part * steps
    rows = lambda width: pl.BlockSpec((tm, width), lambda i: (i + first, 0))
    vec = pl.BlockSpec((1, 1, d), lambda i: ((i + first) // per, 0, 0))
    in_specs = [rows(d), pl.BlockSpec((TOP_K, tm, d // 2), lambda i: (0, i, 0)), rows(LANES), vec]
    args = [x, y_part, gates, gate_mod]
    out_specs, out_shape = [rows(d)], [jax.ShapeDtypeStruct((n, d), F32)]
    if next_norm is not None:
        in_specs += [pl.BlockSpec((1, d), lambda i: (0, 0)), vec, vec]
        args += list(next_norm)
        out_specs, out_shape = out_specs + [rows(d)], out_shape + [jax.ShapeDtypeStruct((n, d), BF16)]
    aliases = {len(args) + j: j for j in range(len(prev))}
    in_specs += [pl.BlockSpec(memory_space=pl.ANY) for _ in prev]
    args += list(prev)
    return pl.pallas_call(
        functools.partial(_combine_kernel, with_norm=next_norm is not None),
        grid=(steps,),
        in_specs=in_specs,
        out_specs=out_specs,
        out_shape=out_shape,
        input_output_aliases=aliases,
        compiler_params=_params("parallel"),
        name="moe_combine",
    )(*args)


def moe_layer(x, mixer, mix_gate, mods, norm_g, router_w, router_b, layer, w1, b1, w2, b2, seq, next_norm):
    n, d = x.shape
    shift, scale, gate = mods
    rw_rows = jnp.pad(router_w.T, ((0, LANES - N_EXPERTS), (0, 0)))
    rw_hi = rw_rows.astype(BF16)
    rw_lo = (rw_rows - rw_hi.astype(F32)).astype(BF16)
    x, h, idx, gates, counts = proj_residual_router(
        *mixer, x, mix_gate, norm_g.reshape(1, d), scale, shift, jnp.concatenate([rw_hi, rw_lo], axis=0), rw_hi,
        router_b.reshape(N_EXPERTS, 1), seq)

    top_i = idx[:TOP_K]
    rank = idx[TOP_K:]
    counts = counts[:, 0]
    experts = jnp.arange(N_EXPERTS, dtype=jnp.int32)
    padded = (counts + MOE_STEP - 1) // MOE_STEP * MOE_STEP
    pad_ends = jnp.sum(jnp.where(experts[:, None] >= experts[None, :], padded[None, :], 0), axis=1)
    pad_starts = pad_ends - padded
    dest = jnp.sum(jnp.where(top_i[..., None] == experts, pad_starts, 0), axis=-1) + rank
    n_blocks = -(-n * TOP_K // MOE_STEP) + N_EXPERTS
    block_start = jnp.arange(n_blocks, dtype=jnp.int32) * MOE_STEP
    block_e = jnp.minimum(jnp.sum((pad_ends[None, :] <= block_start[:, None]).astype(jnp.int32), axis=1),
                          N_EXPERTS - 1)
    n_used = (pad_ends[N_EXPERTS - 1:] // MOE_STEP).astype(jnp.int32)
    seg_end = jnp.sum(jnp.where(block_e[:, None] == experts, pad_starts + counts, 0), axis=1)
    block_valid = jnp.clip(seg_end - block_start, 0, MOE_STEP).astype(jnp.int32)
    dest_flat = dest.reshape(-1)

    xs = sc_scatter_rows(h, dest_flat, n_blocks * MOE_STEP)
    ys = moe_ffn(xs, block_e, block_valid, n_used, layer, w1, b1[layer][:, 0::2], b1[layer][:, 1::2], w2, b2[layer])
    outs = ()
    per_part = n // COMBINE_PARTS
    for part in range(COMBINE_PARTS):
        idx_part = dest[:, part * per_part:(part + 1) * per_part].reshape(-1)
        y = sc_gather_rows(ys, idx_part).reshape(TOP_K, per_part, d // 2)
        outs = moe_combine(x, y, gates, gate, seq, next_norm, part, COMBINE_PARTS, tuple(outs))
    return (outs[0], outs[1]) if next_norm is not None else (outs[0], None)


def _pad_heads(w, width):
    k = w.shape[0]
    w = w.reshape(k, MLA_HEADS, width)
    return jnp.pad(w, ((0, 0), (0, 0), (0, LANES - width))).reshape(k, MLA_HEADS * LANES)


def _head_lane_source():
    half = QK_ROPE_DIM // 2
    first_nope = LANES // 2 - half
    lanes = (list(range(QK_NOPE_DIM, QK_NOPE_DIM + half)) + list(range(first_nope))
             + list(range(QK_NOPE_DIM + half, QK_HEAD_DIM)) + list(range(first_nope, QK_NOPE_DIM)))
    return jnp.array(lanes + [QK_HEAD_DIM] * (LANES - QK_HEAD_DIM), jnp.int32)


def _to_head_lanes(w):
    w = jnp.concatenate([w, jnp.zeros(w.shape[:-1] + (1,), w.dtype)], axis=-1)
    return jnp.take(w, _head_lane_source(), axis=-1)


def _rope_tables(positions):
    inv = 1.0 / (ROPE_THETA ** (jnp.arange(0, QK_ROPE_DIM, 2, dtype=F32) / QK_ROPE_DIM))
    ang = inv[:, None] * positions.reshape(-1).astype(F32)[None, :]
    return jnp.cos(ang), jnp.sin(ang)


def _split_mods(m, batch):
    d = m.shape[-1] // 3
    m = m[:batch]
    return tuple(m[:, None, j * d:(j + 1) * d] for j in range(3))


def pool_mla_mixer(x, h, mods, norm_g, tables, w_in, pool_w, pool_scale, cq_norm_g, w_uq, ckv_norm_g, w_ukv,
                   q_norm_g, k_norm_g, w_out, batch, seq):
    n, d = x.shape
    shift, scale, _ = mods
    rope_cols = _to_head_lanes(jnp.pad(w_in[:, Z_ROPE:], ((0, 0), (QK_NOPE_DIM, 0))))
    w_in_pad = jnp.concatenate([w_in[:, :Z_ROPE], rope_cols], axis=1).astype(BF16)
    if h is None:
        z = norm_matmul(x, norm_g.reshape(1, d), scale, shift, w_in_pad, seq)
    else:
        z = matmul(h, w_in_pad)
    w_ukv_h = w_ukv.reshape(KV_LORA_RANK, MLA_HEADS, QK_NOPE_DIM + V_HEAD_DIM)
    k_nope = jnp.pad(w_ukv_h[:, :, :QK_NOPE_DIM], ((0, 0), (0, 0), (0, QK_ROPE_DIM)))
    wuk_pad = _to_head_lanes(k_nope).reshape(KV_LORA_RANK, MLA_HEADS * LANES).astype(BF16)
    wuv_pad = _pad_heads(w_ukv_h[:, :, QK_NOPE_DIM:].reshape(KV_LORA_RANK, -1), V_HEAD_DIM).astype(BF16)
    wuq_pad = _to_head_lanes(w_uq.reshape(Q_LORA_RANK, MLA_HEADS, QK_HEAD_DIM))
    wuq_pad = wuq_pad.reshape(Q_LORA_RANK, MLA_HEADS * LANES).astype(BF16)
    pad_g = lambda g: _to_head_lanes(g).reshape(1, LANES)
    y_pool, q, k, v = mla_prep(z, pool_w.astype(BF16), pool_scale.reshape(1, -1), cq_norm_g.reshape(1, -1), wuq_pad,
                               ckv_norm_g.reshape(1, -1), wuk_pad, wuv_pad, pad_g(q_norm_g), pad_g(k_norm_g),
                               *tables, batch, seq)
    y_att = flash_attention(q, k, v).reshape(n, MLA_HEADS * V_HEAD_DIM)
    w_out_b = w_out.astype(BF16)
    return [y_pool, y_att], [w_out_b[:POOL_WIDTH], w_out_b[POOL_WIDTH:]], jnp.zeros((1, d), F32)


def conformer_mixer(h, pw1_w, pw1_b, dw_w, dw_b, ln_g, ln_b, pw2_w, pw2_b, seq):
    d = pw2_w.shape[1]
    cd = pw1_w.shape[1] // 2
    pw1 = pw1_w.astype(BF16)
    w_pad = jnp.pad(dw_w, ((0, CONV_HALO - CONV_WIDTH), (0, 0)))
    u = glu_dwconv_ln_silu(h, pw1[:, :cd], pw1[:, cd:], pw1_b[:cd].reshape(1, cd), pw1_b[cd:].reshape(1, cd),
                           w_pad, dw_b.reshape(1, cd), ln_g.reshape(1, cd), ln_b.reshape(1, cd), seq)
    return [u], [pw2_w.astype(BF16)], pw2_b.reshape(1, d)


def kernel(x, c, positions, ada_mix_w, ada_mix_b, norm_mix_g, w_in, pool_w, pool_scale, cq_norm_g, w_uq,
           ckv_norm_g, w_ukv, q_norm_g, k_norm_g, w_out, conv_pw1_w, conv_pw1_b, conv_dw_w, conv_dw_b,
           conv_ln_g, conv_ln_b, conv_pw2_w, conv_pw2_b, ada_ffn_w, ada_ffn_b, norm_ffn_g, router_w,
           router_b, moe_w1, moe_b1, moe_w2, moe_b2):
    batch, seq, d = x.shape
    depth = ada_mix_w.shape[0]
    c_pad = jnp.pad(c, ((0, 8 - batch), (0, 0)))
    mix_mods = adaln(c_pad, ada_mix_w, ada_mix_b)
    ffn_mods = adaln(c_pad, ada_ffn_w, ada_ffn_b)
    tables = _rope_tables(positions)
    xf = x.reshape(batch * seq, d)
    h = None
    for layer in range(depth):
        i = layer // 2
        mods = _split_mods(mix_mods[layer], batch)
        if layer % 2 == 0:
            mixer = pool_mla_mixer(xf, h, mods, norm_mix_g[layer], tables, w_in[i], pool_w[i], pool_scale[i],
                                   cq_norm_g[i], w_uq[i], ckv_norm_g[i], w_ukv[i], q_norm_g[i], k_norm_g[i],
                                   w_out[i], batch, seq)
        else:
            mixer = conformer_mixer(h, conv_pw1_w[i], conv_pw1_b[i], conv_dw_w[i], conv_dw_b[i],
                                    conv_ln_g[i], conv_ln_b[i], conv_pw2_w[i], conv_pw2_b[i], seq)
        next_norm = None
        if layer + 1 < depth:
            n_shift, n_scale, _ = _split_mods(mix_mods[layer + 1], batch)
            next_norm = (norm_mix_g[layer + 1].reshape(1, d), n_scale, n_shift)
        xf, h = moe_layer(xf, mixer, mods[2], _split_mods(ffn_mods[layer], batch), norm_ffn_g[layer],
                          router_w[layer], router_b[layer], layer, moe_w1, moe_b1, moe_w2, moe_b2, seq, next_norm)
    return xf.reshape(batch, seq, d)
```
